```python
import math
import jax, jax.numpy as jnp
from jax import lax
import numpy as np

D_MODEL = 1024
BATCH = 4
SEQ = 8192
DEPTH = 2

GDN_HEADS = 4
GDN_DK = 64
GDN_DV = 64
GDN_QK_W = GDN_HEADS * GDN_DK
GDN_V_W = GDN_HEADS * GDN_DV
GDN_QKV_W = 2 * GDN_QK_W + GDN_V_W
GDN_CONV = 4
GDN_CHUNK = 64
S5_WIDTH = 256
S5_GROUP_CH = 16
S5_GROUPS = S5_WIDTH // S5_GROUP_CH
S5_STATE = 64
DIFF_HEADS = 4
DIFF_DQK = 64
DIFF_DV = 2 * DIFF_DQK
DIFF_QK_W = DIFF_HEADS * 2 * DIFF_DQK
DIFF_V_W = DIFF_HEADS * DIFF_DV
Q_BLOCK = 128
MIX_WIDTH = GDN_V_W + S5_WIDTH + DIFF_V_W
IN_SIZES = (GDN_QKV_W, GDN_V_W, GDN_HEADS, GDN_HEADS, S5_WIDTH, DIFF_QK_W, DIFF_QK_W, DIFF_V_W)
IN_WIDTH = sum(IN_SIZES)
PEER_HEADS = 8
PEER_DQ = 128
PEER_DHALF = PEER_DQ // 2
N_KEYS = 128
N_EXPERTS = N_KEYS * N_KEYS
PEER_TOPK = 16
PEER_BLOCK = 128
NORM_EPS = 1e-6

kernel_name = "hybrid_gdn_s5_diffattn_peer"


def rms_norm(x, gain):
    xf = x.astype(jnp.float32)
    y = xf * lax.rsqrt(jnp.mean(xf * xf, axis=-1, keepdims=True) + NORM_EPS)
    return (y * gain.astype(jnp.float32)).astype(x.dtype)


def l2_normalize(t):
    return t * lax.rsqrt(jnp.sum(t * t, axis=-1, keepdims=True) + 1e-6)


def chunk_gated_delta(q, k, v, g, beta):
    bn, seq, heads, dk = q.shape
    dv = v.shape[-1]
    nc = seq // GDN_CHUNK

    def chunks(t):
        t = t.reshape((bn, nc, GDN_CHUNK, heads) + t.shape[3:])
        return jnp.moveaxis(t, (1, 3), (0, 2))

    q = chunks(q * dk ** -0.5)
    k = chunks(k)
    v = chunks(v)
    beta = chunks(beta)
    g = jnp.cumsum(chunks(g), axis=-1)
    incl = jnp.tril(jnp.ones((GDN_CHUNK, GDN_CHUNK), bool))
    strict = jnp.tril(jnp.ones((GDN_CHUNK, GDN_CHUNK), bool), -1)
    decay = jnp.exp(jnp.where(incl, g[..., :, None] - g[..., None, :], -jnp.inf))
    kk = jnp.einsum('nbhid,nbhjd->nbhij', k, k)
    lower = jnp.where(strict, beta[..., :, None] * kk * decay, 0.0)
    eye = jnp.eye(GDN_CHUNK, dtype=jnp.float32)
    t_inv = lax.linalg.triangular_solve(eye + lower, jnp.broadcast_to(eye, lower.shape),
                                        left_side=True, lower=True)
    u = t_inv @ (v * beta[..., None])
    w = t_inv @ (k * (beta * jnp.exp(g))[..., None])
    qk = jnp.where(incl, jnp.einsum('nbhid,nbhjd->nbhij', q, k) * decay, 0.0)

    def step(state, inp):
        q_c, k_c, u_c, w_c, g_c, qk_c = inp
        v_new = u_c - w_c @ state
        out = (q_c * jnp.exp(g_c)[..., None]) @ state + qk_c @ v_new
        g_last = g_c[..., -1:]
        state = state * jnp.exp(g_last)[..., None] + jnp.einsum(
            'bhck,bhcv->bhkv', k_c * jnp.exp(g_last - g_c)[..., None], v_new)
        return state, out

    state0 = jnp.zeros((bn, heads, dk, dv), jnp.float32)
    _, out = lax.scan(step, state0, (q, k, u, w, g, qk))
    return jnp.moveaxis(out, (0, 2), (1, 3)).reshape(bn, seq, heads, dv)


def gdn_mixer(qkv, z, a, b, conv_w, a_log, dt_bias, norm_g):
    bn, seq, ch = qkv.shape
    qkv = jax.nn.silu(lax.conv_general_dilated(
        qkv, conv_w.astype(qkv.dtype)[:, None, :], (1,), [(GDN_CONV - 1, 0)],
        dimension_numbers=('NWC', 'WIO', 'NWC'), feature_group_count=ch))
    q, k, v = jnp.split(qkv.astype(jnp.float32), [GDN_QK_W, 2 * GDN_QK_W], axis=-1)
    q = l2_normalize(q.reshape(bn, seq, GDN_HEADS, GDN_DK))
    k = l2_normalize(k.reshape(bn, seq, GDN_HEADS, GDN_DK))
    v = v.reshape(bn, seq, GDN_HEADS, GDN_DV)
    beta = jax.nn.sigmoid(b.astype(jnp.float32))
    g = -jnp.exp(a_log.astype(jnp.float32)) * jax.nn.softplus(a.astype(jnp.float32) + dt_bias.astype(jnp.float32))
    o = chunk_gated_delta(q, k, v, g, beta)
    o = rms_norm(o, norm_g) * jax.nn.silu(z.astype(jnp.float32).reshape(bn, seq, GDN_HEADS, GDN_DV))
    return o.reshape(bn, seq, GDN_V_W).astype(z.dtype)


def _lin_combine(e1, e2):
    a1, b1 = e1
    a2, b2 = e2
    return a1 * a2, a2 * b1 + b2


def s5_mixer(u, lam_re, lam_im, b_re, b_im, c_re, c_im, d_skip, log_step, glu_w, glu_b):
    bn, seq, _ = u.shape
    uf = u.astype(jnp.float32)
    lam = lax.complex(lam_re.astype(jnp.float32), lam_im.astype(jnp.float32))
    step = jnp.exp(log_step.astype(jnp.float32))[:, None]
    a_bar = jnp.exp(lam * step)
    b_bar = ((a_bar - 1.0) / lam)[:, :, None] * lax.complex(b_re.astype(jnp.float32), b_im.astype(jnp.float32))
    bu = jnp.einsum('blgc,gnc->blgn', uf.reshape(bn, seq, S5_GROUPS, S5_GROUP_CH), b_bar)
    _, states = lax.associative_scan(_lin_combine, (jnp.broadcast_to(a_bar, bu.shape), bu), axis=1)
    c = lax.complex(c_re.astype(jnp.float32), c_im.astype(jnp.float32))
    y = jnp.einsum('blgn,gcn->blgc', states, c).real.reshape(bn, seq, S5_WIDTH) + d_skip.astype(jnp.float32) * uf
    y = jax.nn.gelu(y).astype(u.dtype)
    zg = y @ glu_w + glu_b
    z_val, z_gate = jnp.split(zg, 2, axis=-1)
    return z_val * jax.nn.sigmoid(z_gate)


def diff_attn_mixer(q, k, v, lq1, lk1, lq2, lk2, norm_g, lam_init):
    bn, seq, _ = q.shape
    q = q.reshape(bn, seq, DIFF_HEADS, 2, DIFF_DQK)
    k = k.reshape(bn, seq, DIFF_HEADS, 2, DIFF_DQK)
    vf = v.reshape(bn, seq, DIFF_HEADS, DIFF_DV).astype(jnp.float32)
    f32 = jnp.float32
    lam = (jnp.exp(jnp.sum(lq1.astype(f32) * lk1.astype(f32)))
           - jnp.exp(jnp.sum(lq2.astype(f32) * lk2.astype(f32))) + lam_init)
    nb = seq // Q_BLOCK
    q_blocks = q.reshape(bn, nb, Q_BLOCK, DIFF_HEADS, 2, DIFF_DQK).swapaxes(0, 1)
    scale = DIFF_DQK ** -0.5
    k_pos = jnp.arange(seq)

    def block(args):
        blk, qb = args
        s = jnp.einsum('bqhmd,bkhmd->bhmqk', qb, k).astype(f32) * scale
        q_pos = blk * Q_BLOCK + jnp.arange(Q_BLOCK)
        s = jnp.where(k_pos[None, :] <= q_pos[:, None], s, -jnp.inf)
        p = jax.nn.softmax(s, axis=-1)
        p_diff = p[:, :, 0] - lam * p[:, :, 1]
        return jnp.einsum('bhqk,bkhe->bqhe', p_diff, vf)

    o = lax.map(block, (jnp.arange(nb), q_blocks))
    o = o.swapaxes(0, 1).reshape(bn, seq, DIFF_HEADS, DIFF_DV)
    o = rms_norm(o, norm_g) * (1.0 - lam_init)
    return o.reshape(bn, seq, DIFF_V_W).astype(v.dtype)


def peer_ffn(x, wq, keys, u_tab, v_tab):
    bn, seq, dm = x.shape
    nb = seq // PEER_BLOCK
    xb = x.reshape(bn, nb, PEER_BLOCK, dm).swapaxes(0, 1)

    def block(xt):
        q = (xt @ wq).reshape(bn, PEER_BLOCK, PEER_HEADS, 2, PEER_DHALF)
        s = jnp.einsum('bthpd,hpnd->bthpn', q, keys).astype(jnp.float32)
        sv, si = lax.top_k(s, PEER_TOPK)
        cand_s = (sv[..., 0, :, None] + sv[..., 1, None, :]).reshape(bn, PEER_BLOCK, PEER_HEADS, PEER_TOPK * PEER_TOPK)
        cand_e = (si[..., 0, :, None] * N_KEYS + si[..., 1, None, :]).reshape(bn, PEER_BLOCK, PEER_HEADS, PEER_TOPK * PEER_TOPK)
        top_s, top_i = lax.top_k(cand_s, PEER_TOPK)
        expert = jnp.take_along_axis(cand_e, top_i, axis=-1)
        gate = jax.nn.softmax(top_s, axis=-1)
        act = jax.nn.gelu(jnp.einsum('bthkd,btd->bthk', u_tab[expert], xt).astype(jnp.float32))
        return jnp.einsum('bthk,bthkd->btd', (gate * act).astype(xt.dtype), v_tab[expert])

    y = lax.map(block, xb)
    return y.swapaxes(0, 1).reshape(bn, seq, dm)


def setup_inputs(seed: int = 0) -> dict:
    key = jax.random.key(seed)
    ks = iter(jax.random.split(key, 32))
    f32 = jnp.float32
    nrm = lambda shape, scale: scale * jax.random.normal(next(ks), shape, f32)
    unif = lambda shape, lo, hi: jax.random.uniform(next(ks), shape, f32, lo, hi)
    x = nrm((BATCH, SEQ, D_MODEL), 1.0)
    norm1_g = 1.0 + nrm((DEPTH, D_MODEL), 0.02)
    w_in = nrm((DEPTH, D_MODEL, IN_WIDTH), D_MODEL ** -0.5)
    gdn_conv_w = nrm((DEPTH, GDN_CONV, GDN_QKV_W), GDN_CONV ** -0.5)
    gdn_a_log = jnp.log(unif((DEPTH, GDN_HEADS), 1.0, 16.0))
    dt = jnp.exp(unif((DEPTH, GDN_HEADS), math.log(1e-3), math.log(1e-1)))
    gdn_dt_bias = dt + jnp.log(-jnp.expm1(-dt))
    gdn_norm_g = 1.0 + nrm((DEPTH, GDN_DV), 0.02)
    s5_lambda_re = -0.5 + nrm((DEPTH, S5_GROUPS, S5_STATE), 0.01)
    s5_lambda_im = math.pi * jnp.arange(S5_STATE, dtype=f32) + nrm((DEPTH, S5_GROUPS, S5_STATE), 0.01)
    s5_b_re = nrm((DEPTH, S5_GROUPS, S5_STATE, S5_GROUP_CH), (2 * S5_GROUP_CH) ** -0.5)
    s5_b_im = nrm((DEPTH, S5_GROUPS, S5_STATE, S5_GROUP_CH), (2 * S5_GROUP_CH) ** -0.5)
    s5_c_re = nrm((DEPTH, S5_GROUPS, S5_GROUP_CH, S5_STATE), (2 * S5_STATE) ** -0.5)
    s5_c_im = nrm((DEPTH, S5_GROUPS, S5_GROUP_CH, S5_STATE), (2 * S5_STATE) ** -0.5)
    s5_d = nrm((DEPTH, S5_WIDTH), 1.0)
    s5_log_step = unif((DEPTH, S5_GROUPS), math.log(1e-3), math.log(1e-1))
    s5_glu_w = nrm((DEPTH, S5_WIDTH, 2 * S5_WIDTH), S5_WIDTH ** -0.5)
    s5_glu_b = nrm((DEPTH, 2 * S5_WIDTH), 0.01)
    diff_lq1 = nrm((DEPTH, DIFF_DQK), 0.1)
    diff_lk1 = nrm((DEPTH, DIFF_DQK), 0.1)
    diff_lq2 = nrm((DEPTH, DIFF_DQK), 0.1)
    diff_lk2 = nrm((DEPTH, DIFF_DQK), 0.1)
    diff_norm_g = 1.0 + nrm((DEPTH, DIFF_DV), 0.02)
    w_out = nrm((DEPTH, MIX_WIDTH, D_MODEL), MIX_WIDTH ** -0.5)
    norm2_g = 1.0 + nrm((DEPTH, D_MODEL), 0.02)
    peer_wq = nrm((DEPTH, D_MODEL, PEER_HEADS * PEER_DQ), D_MODEL ** -0.5)
    peer_keys = nrm((DEPTH, PEER_HEADS, 2, N_KEYS, PEER_DHALF), PEER_DHALF ** -0.5)
    peer_u = nrm((DEPTH, N_EXPERTS, D_MODEL), D_MODEL ** -0.5)
    peer_v = nrm((DEPTH, N_EXPERTS, D_MODEL), PEER_HEADS ** -0.5)
    final_g = 1.0 + nrm((D_MODEL,), 0.02)
    return {"x": x, "norm1_g": norm1_g, "w_in": w_in, "gdn_conv_w": gdn_conv_w, "gdn_a_log": gdn_a_log,
            "gdn_dt_bias": gdn_dt_bias, "gdn_norm_g": gdn_norm_g, "s5_lambda_re": s5_lambda_re,
            "s5_lambda_im": s5_lambda_im, "s5_b_re": s5_b_re, "s5_b_im": s5_b_im, "s5_c_re": s5_c_re,
            "s5_c_im": s5_c_im, "s5_d": s5_d, "s5_log_step": s5_log_step, "s5_glu_w": s5_glu_w,
            "s5_glu_b": s5_glu_b, "diff_lq1": diff_lq1, "diff_lk1": diff_lk1, "diff_lq2": diff_lq2,
            "diff_lk2": diff_lk2, "diff_norm_g": diff_norm_g, "w_out": w_out, "norm2_g": norm2_g,
            "peer_wq": peer_wq, "peer_keys": peer_keys, "peer_u": peer_u, "peer_v": peer_v, "final_g": final_g}


def reference(x, norm1_g, w_in, gdn_conv_w, gdn_a_log, gdn_dt_bias, gdn_norm_g, s5_lambda_re, s5_lambda_im,
              s5_b_re, s5_b_im, s5_c_re, s5_c_im, s5_d, s5_log_step, s5_glu_w, s5_glu_b, diff_lq1, diff_lk1,
              diff_lq2, diff_lk2, diff_norm_g, w_out, norm2_g, peer_wq, peer_keys, peer_u, peer_v, final_g):
    split_at = np.cumsum(IN_SIZES)[:-1].tolist()
    h = x
    for layer in range(DEPTH):
        lam_init = 0.8 - 0.6 * math.exp(-0.3 * layer)
        hn = rms_norm(h, norm1_g[layer])
        proj = hn @ w_in[layer]
        qkv_a, z_a, a_a, b_a, u_s, q_c, k_c, v_c = jnp.split(proj, split_at, axis=-1)
        o_a = gdn_mixer(qkv_a, z_a, a_a, b_a, gdn_conv_w[layer], gdn_a_log[layer], gdn_dt_bias[layer],
                        gdn_norm_g[layer])
        o_b = s5_mixer(u_s, s5_lambda_re[layer], s5_lambda_im[layer], s5_b_re[layer], s5_b_im[layer],
                       s5_c_re[layer], s5_c_im[layer], s5_d[layer], s5_log_step[layer], s5_glu_w[layer],
                       s5_glu_b[layer])
        o_c = diff_attn_mixer(q_c, k_c, v_c, diff_lq1[layer], diff_lk1[layer], diff_lq2[layer], diff_lk2[layer],
                              diff_norm_g[layer], lam_init)
        h = h + jnp.concatenate([o_a, o_b, o_c], axis=-1) @ w_out[layer]
        h = h + peer_ffn(rms_norm(h, norm2_g[layer]), peer_wq[layer], peer_keys[layer], peer_u[layer], peer_v[layer])
    return rms_norm(h, final_g)
```

```python
import functools
import math

import jax
import jax.numpy as jnp
from jax import lax
from jax.experimental import pallas as pl
from jax.experimental.pallas import tpu as pltpu

F32 = jnp.float32
BF16 = jnp.bfloat16
I32 = jnp.int32
U32 = jnp.uint32

D_MODEL = 1024
DEPTH = 2
GDN_HEADS = 4
GDN_DK = 64
GDN_CHUNK = 64
GDN_W = GDN_HEADS * GDN_DK
GDN_QKV_W = 3 * GDN_W
GDN_CONV = 4
S5_WIDTH = 256
S5_GROUPS = 16
S5_GROUP_CH = 16
S5_STATE = 64
S5_NS = S5_GROUPS * S5_STATE
S5_SUB = 16
DIFF_HEADS = 4
DIFF_DQK = 64
DIFF_DV = 128
DIFF_W = 512
PEER_HEADS = 8
PEER_DHALF = 64
N_KEYS = 128
N_EXPERTS = N_KEYS * N_KEYS
PEER_TOPK = 16
PEER_PAIRS = PEER_HEADS * PEER_TOPK
NORM_EPS = 1e-6
NEG_INF = float("-inf")

VMEM_LIMIT_BYTES = 56 * 1024 * 1024


def _cparams(sem, vmem=VMEM_LIMIT_BYTES):
    return pltpu.CompilerParams(dimension_semantics=sem, vmem_limit_bytes=vmem)


def _dot(a, b):
    return jnp.dot(a, b, preferred_element_type=F32)


def _dot_nt(a, b):
    return lax.dot_general(a, b, (((1,), (1,)), ((), ())), preferred_element_type=F32)


def _dot_tn(a, b):
    return lax.dot_general(a, b, (((0,), (0,)), ((), ())), preferred_element_type=F32)


def _split(x):
    hi = x.astype(BF16)
    lo = (x - hi.astype(F32)).astype(BF16)
    return hi, lo


def _dot_sel_r(x, sel):
    hi, lo = _split(x)
    return _dot(hi, sel) + _dot(lo, sel)


def _dot_sel_l(sel, x):
    hi, lo = _split(x)
    return _dot(sel, hi) + _dot(sel, lo)


def _mm3(a, b):
    ah, al = _split(a)
    bh, bl = _split(b)
    return _dot(ah, bh) + (_dot(ah, bl) + _dot(al, bh))


def _sigmoid(x):
    return 1.0 / (1.0 + jnp.exp(-x))


def _softplus(x):
    return jnp.maximum(x, 0.0) + jnp.log1p(jnp.exp(-jnp.abs(x)))


def _gelu_tanh(x):
    c = math.sqrt(2.0 / math.pi)
    return 0.5 * x * (1.0 + jnp.tanh(c * (x + 0.044715 * (x * x * x))))


INPROJ_GDN_W = GDN_QKV_W + 3 * GDN_W


def _inproj_kernel(x_ref, g_ref, w_ref, gdn_ref, us_ref, q_ref, k_ref, v_ref):
    x = x_ref[...]
    ms = jnp.mean(x * x, axis=-1, keepdims=True)
    xn = (x * lax.rsqrt(ms + NORM_EPS) * g_ref[...]).astype(BF16)
    o = INPROJ_GDN_W
    gdn_ref[...] = _dot(xn, w_ref[:, 0:o])
    us_ref[...] = _dot(xn, w_ref[:, o:o + S5_WIDTH])
    o += S5_WIDTH
    q_ref[...] = (_dot(xn, w_ref[:, o:o + DIFF_W]) * (DIFF_DQK ** -0.5)).astype(BF16)
    k_ref[...] = _dot(xn, w_ref[:, o + DIFF_W:o + 2 * DIFF_W]).astype(BF16)
    v_ref[...] = _dot(xn, w_ref[:, o + 2 * DIFF_W:o + 3 * DIFF_W]).astype(BF16)


def _inproj(x2, gain, w, tm=512):
    t = x2.shape[0]
    nw = w.shape[1]
    row = lambda i: (i, 0)
    fixed = lambda i: (0, 0)
    return pl.pallas_call(
        _inproj_kernel,
        grid=(t // tm,),
        in_specs=[pl.BlockSpec((tm, D_MODEL), row), pl.BlockSpec((1, D_MODEL), fixed),
                  pl.BlockSpec((D_MODEL, nw), fixed)],
        out_specs=[pl.BlockSpec((tm, INPROJ_GDN_W), row), pl.BlockSpec((tm, S5_WIDTH), row),
                   pl.BlockSpec((tm, DIFF_W), row), pl.BlockSpec((tm, DIFF_W), row),
                   pl.BlockSpec((tm, DIFF_W), row)],
        out_shape=[jax.ShapeDtypeStruct((t, INPROJ_GDN_W), F32), jax.ShapeDtypeStruct((t, S5_WIDTH), F32),
                   jax.ShapeDtypeStruct((t, DIFF_W), BF16), jax.ShapeDtypeStruct((t, DIFF_W), BF16),
                   jax.ShapeDtypeStruct((t, DIFF_W), BF16)],
        compiler_params=_cparams(("parallel",)),
        name="inproj",
    )(x2, gain, w)


def _gdn_kernel(blk_ref, convw_ref, alog_ref, dtb_ref, ng_ref, out_ref,
                s_ref, tail_ref, xp_ref, q_s, k_s, v_s, b_s, g_s, *, ct):
    c64 = GDN_CHUNK

    @pl.when(pl.program_id(1) == 0)
    def _():
        s_ref[...] = jnp.zeros_like(s_ref)
        tail_ref[...] = jnp.zeros_like(tail_ref)

    qkv = blk_ref[0, :, 0:GDN_QKV_W]
    xp_ref[0:8, :] = tail_ref[...]
    xp_ref[8:8 + ct, :] = qkv
    tail_ref[...] = qkv[ct - 8:ct, :]
    cw = convw_ref[...]
    y = cw[0:1, :] * xp_ref[5:5 + ct, :]
    for j in range(1, GDN_CONV):
        y = y + cw[j:j + 1, :] * xp_ref[5 + j:5 + j + ct, :]
    y = y * _sigmoid(y)

    ri = lax.broadcasted_iota(I32, (GDN_W, GDN_W), 0)
    ci = lax.broadcasted_iota(I32, (GDN_W, GDN_W), 1)
    head_ones = jnp.where((ri // c64) == (ci // c64), 1.0, 0.0).astype(BF16)

    q = y[:, 0:GDN_W]
    k = y[:, GDN_W:2 * GDN_W]
    q_s[...] = q * lax.rsqrt(_dot_sel_r(q * q, head_ones) + 1e-6) * (GDN_DK ** -0.5)
    k_s[...] = k * lax.rsqrt(_dot_sel_r(k * k, head_ones) + 1e-6)
    v_s[...] = y[:, 2 * GDN_W:3 * GDN_W]
    a_rep = blk_ref[0, :, GDN_QKV_W + GDN_W:GDN_QKV_W + 2 * GDN_W]
    b_rep = blk_ref[0, :, GDN_QKV_W + 2 * GDN_W:GDN_QKV_W + 3 * GDN_W]
    b_s[...] = _sigmoid(b_rep)
    g_raw = -jnp.exp(alog_ref[...]) * _softplus(a_rep + dtb_ref[...])
    rt = lax.broadcasted_iota(I32, (ct, ct), 0)
    ctk = lax.broadcasted_iota(I32, (ct, ct), 1)
    tri_bd = jnp.where(((rt // c64) == (ctk // c64)) & (ctk <= rt), 1.0, 0.0).astype(BF16)
    g_s[...] = _dot_sel_l(tri_bd, g_raw)

    r64 = lax.broadcasted_iota(I32, (c64, c64), 0)
    col64 = lax.broadcasted_iota(I32, (c64, c64), 1)
    incl = col64 <= r64
    strict = col64 < r64
    eye_b = col64 == r64
    eye_f = jnp.where(eye_b, 1.0, 0.0)
    ones64 = jnp.ones((c64, c64), BF16)
    ng = ng_ref[...]

    def chunk_body(c, carry):
        r0 = pl.multiple_of(c * c64, c64)
        rows = pl.ds(r0, c64)
        outs = []
        for h in range(GDN_HEADS):
            lanes = slice(h * c64, (h + 1) * c64)
            gi = g_s[rows, lanes]
            gj = _dot_sel_l(ones64, jnp.where(eye_b, gi, 0.0))
            dec = jnp.exp(jnp.where(incl, gi - gj, NEG_INF))
            kh = k_s[rows, lanes]
            qh = q_s[rows, lanes]
            vh = v_s[rows, lanes]
            bi = b_s[rows, lanes]
            kb = kh.astype(BF16)
            kk = _dot_nt(kb, kb)
            lm = jnp.where(strict, bi * kk * dec, 0.0)
            p = eye_f - lm
            m = lm
            for _ in range(5):
                m = _mm3(m, m)
                p = p + _mm3(p, m)
            eg = jnp.exp(gi)
            pb = p.astype(BF16)
            u = _dot(pb, (vh * bi).astype(BF16))
            w = _dot(pb, (kh * (bi * eg)).astype(BF16))
            qk = jnp.where(incl, _dot_nt(qh.astype(BF16), kb) * dec, 0.0)
            g_last = gi[c64 - 1:c64, :]
            kg = kh * jnp.exp(g_last - gi)
            s_old = s_ref[h]
            sb = s_old.astype(BF16)
            v_new = u - _dot(w.astype(BF16), sb)
            vnb = v_new.astype(BF16)
            o = _dot((qh * eg).astype(BF16), sb) + _dot(qk.astype(BF16), vnb)
            s_ref[h] = s_old * jnp.exp(g_last) + _dot_tn(kg.astype(BF16), vnb)
            ms = _dot_sel_r(o * o, ones64) * (1.0 / c64)
            outs.append(o * lax.rsqrt(ms + NORM_EPS) * ng)
        z = blk_ref[0, rows, GDN_QKV_W:GDN_QKV_W + GDN_W]
        o_all = jnp.concatenate(outs, axis=1) * (z * _sigmoid(z))
        out_ref[0, rows, :] = o_all.astype(out_ref.dtype)
        return carry

    lax.fori_loop(0, ct // c64, chunk_body, 0)


def _gdn(gdn_in, conv_w, a_log_rep, dtb_rep, ng, ct=256):
    bn, seq, _ = gdn_in.shape
    fixed = lambda b, l: (0, 0)
    return pl.pallas_call(
        functools.partial(_gdn_kernel, ct=ct),
        grid=(bn, seq // ct),
        in_specs=[pl.BlockSpec((1, ct, INPROJ_GDN_W), lambda b, l: (b, l, 0)),
                  pl.BlockSpec((GDN_CONV, GDN_QKV_W), fixed), pl.BlockSpec((1, GDN_W), fixed),
                  pl.BlockSpec((1, GDN_W), fixed), pl.BlockSpec((1, GDN_DK), fixed)],
        out_specs=pl.BlockSpec((1, ct, GDN_W), lambda b, l: (b, l, 0)),
        out_shape=jax.ShapeDtypeStruct((bn, seq, GDN_W), BF16),
        scratch_shapes=[pltpu.VMEM((GDN_HEADS, GDN_DK, GDN_DK), F32), pltpu.VMEM((8, GDN_QKV_W), F32),
                        pltpu.VMEM((ct + 8, GDN_QKV_W), F32)] + [pltpu.VMEM((ct, GDN_W), F32)] * 5,
        compiler_params=_cparams(("arbitrary", "arbitrary")),
        name="gdn",
    )(gdn_in, conv_w, a_log_rep, dtb_rep, ng)


def _s5_kernel(u_ref, lre_ref, lim_ref, lstep_ref, bre_ref, bim_ref, cre_ref, cim_ref, d_ref, y_ref,
               bmat, cmat, avec, carry, zr_s, zi_s, xr_s, xi_s, *, rb):
    ns = S5_NS
    cw = S5_WIDTH

    @pl.when(pl.program_id(1) == 0)
    def _():
        step = jnp.exp(lstep_ref[...])
        lr = lre_ref[...]
        li = lim_ref[...]
        mag = jnp.exp(lr * step)
        ar = mag * jnp.cos(li * step)
        ai = mag * jnp.sin(li * step)
        den = lr * lr + li * li
        mr = ((ar - 1.0) * lr + ai * li) / den
        mi = (ai * lr - (ar - 1.0) * li) / den
        bre = bre_ref[...]
        bim = bim_ref[...]
        bmat[:, 0:ns] = (mr * bre - mi * bim).astype(BF16)
        bmat[:, ns:2 * ns] = (mr * bim + mi * bre).astype(BF16)
        cmat[0:ns, :] = cre_ref[...].astype(BF16)
        cmat[ns:2 * ns, :] = (-cim_ref[...]).astype(BF16)
        avec[0:1, :] = ar
        avec[1:2, :] = ai
        pr, pi = ar, ai
        for _ in range(4):
            pr, pi = pr * pr - pi * pi, 2.0 * pr * pi
        avec[2:3, :] = pr
        avec[3:4, :] = pi
        carry[...] = jnp.zeros_like(carry)

    ar = avec[0:1, :]
    ai = avec[1:2, :]
    a16r = avec[2:3, :]
    a16i = avec[3:4, :]

    def inject(s):
        ub = u_ref[0, :, s * cw:(s + 1) * cw].astype(BF16)
        return _dot(ub, bmat[...])

    xr = jnp.zeros((rb, ns), F32)
    xi = jnp.zeros((rb, ns), F32)
    for s in range(S5_SUB):
        p = inject(s)
        xr, xi = ar * xr - ai * xi + p[:, 0:ns], ar * xi + ai * xr + p[:, ns:2 * ns]
    zr_s[...] = xr
    zi_s[...] = xi

    def row_step(kk, c):
        cr, ci_ = c
        xr_s[pl.ds(kk, 1), :] = cr
        xi_s[pl.ds(kk, 1), :] = ci_
        zr = zr_s[pl.ds(kk, 1), :]
        zi = zi_s[pl.ds(kk, 1), :]
        return (a16r * cr - a16i * ci_ + zr, a16r * ci_ + a16i * cr + zi)

    cr, ci_ = lax.fori_loop(0, rb, row_step, (carry[0:1, :], carry[1:2, :]))
    carry[0:1, :] = cr
    carry[1:2, :] = ci_

    xr = xr_s[...]
    xi = xi_s[...]
    dsk = d_ref[...]
    for s in range(S5_SUB):
        p = inject(s)
        xr, xi = ar * xr - ai * xi + p[:, 0:ns], ar * xi + ai * xr + p[:, ns:2 * ns]
        yv = _dot(xr.astype(BF16), cmat[0:ns, :]) + _dot(xi.astype(BF16), cmat[ns:2 * ns, :])
        yv = yv + dsk * u_ref[0, :, s * cw:(s + 1) * cw]
        y_ref[0, :, s * cw:(s + 1) * cw] = _gelu_tanh(yv).astype(y_ref.dtype)


def _s5(u_rows, lre, lim, lstep, bre_bd, bim_bd, cre_bd, cim_bd, dskip, rb=128):
    bn, nrows, rw = u_rows.shape
    fixed = lambda b, r: (0, 0)
    ns = S5_NS
    return pl.pallas_call(
        functools.partial(_s5_kernel, rb=rb),
        grid=(bn, nrows // rb),
        in_specs=[pl.BlockSpec((1, rb, rw), lambda b, r: (b, r, 0)),
                  pl.BlockSpec((1, ns), fixed), pl.BlockSpec((1, ns), fixed), pl.BlockSpec((1, ns), fixed),
                  pl.BlockSpec((S5_WIDTH, ns), fixed), pl.BlockSpec((S5_WIDTH, ns), fixed),
                  pl.BlockSpec((ns, S5_WIDTH), fixed), pl.BlockSpec((ns, S5_WIDTH), fixed),
                  pl.BlockSpec((1, S5_WIDTH), fixed)],
        out_specs=pl.BlockSpec((1, rb, rw), lambda b, r: (b, r, 0)),
        out_shape=jax.ShapeDtypeStruct((bn, nrows, rw), BF16),
        scratch_shapes=[pltpu.VMEM((S5_WIDTH, 2 * ns), BF16), pltpu.VMEM((2 * ns, S5_WIDTH), BF16),
                        pltpu.VMEM((8, ns), F32), pltpu.VMEM((8, ns), F32)]
                       + [pltpu.VMEM((rb, ns), F32)] * 4,
        compiler_params=_cparams(("arbitrary", "arbitrary")),
        name="s5",
    )(u_rows, lre, lim, lstep, bre_bd, bim_bd, cre_bd, cim_bd, dskip)


def _attn_kernel(q_ref, k_ref, v_ref, lq1_ref, lk1_ref, lq2_ref, lk2_ref, ng_ref, o_ref,
                 qa_s, qb_s, m_s, l_s, acc_s, *, bq, bk, lam_init):
    i = pl.program_id(2)
    j = pl.program_id(3)

    @pl.when(j == 0)
    def _():
        q = q_ref[0]
        lane = lax.broadcasted_iota(I32, q.shape, 1)
        zero = jnp.zeros_like(q)
        qa_s[...] = jnp.where(lane < DIFF_DQK, q, zero)
        qb_s[...] = jnp.where(lane >= DIFF_DQK, q, zero)
        m_s[...] = jnp.full_like(m_s, NEG_INF)
        l_s[...] = jnp.zeros_like(l_s)
        acc_s[...] = jnp.zeros_like(acc_s)

    def update(masked):
        k = k_ref[0]
        v = v_ref[0]
        ps = []
        for mi, q_s in enumerate((qa_s, qb_s)):
            s = _dot_nt(q_s[...], k)
            if masked:
                row = lax.broadcasted_iota(I32, s.shape, 0)
                col = lax.broadcasted_iota(I32, s.shape, 1)
                s = jnp.where(col <= row, s, NEG_INF)
            m_old = m_s[mi]
            m_new = jnp.maximum(m_old, jnp.max(s, axis=-1, keepdims=True))
            p = jnp.exp(s - m_new[:, 0:1])
            alpha = jnp.exp(m_old - m_new)
            l_s[mi] = alpha * l_s[mi] + jnp.sum(p, axis=-1, keepdims=True)
            m_s[mi] = m_new
            acc_s[mi] = alpha * acc_s[mi] + _dot(p.astype(BF16), v)

    @pl.when(j < i)
    def _():
        update(False)

    @pl.when(j == i)
    def _():
        update(True)
        lam = (jnp.exp(jnp.sum(lq1_ref[...] * lk1_ref[...], axis=-1, keepdims=True))
               - jnp.exp(jnp.sum(lq2_ref[...] * lk2_ref[...], axis=-1, keepdims=True)) + lam_init)
        o = acc_s[0] / l_s[0] - lam * (acc_s[1] / l_s[1])
        ms = jnp.mean(o * o, axis=-1, keepdims=True)
        o = o * lax.rsqrt(ms + NORM_EPS) * ng_ref[...] * (1.0 - lam_init)
        o_ref[0] = o.astype(o_ref.dtype)


def _attn(q, k, v, lq1, lk1, lq2, lk2, ng, lam_init, blk=512):
    bn, seq, _ = q.shape
    nb = seq // blk
    fixed = lambda b, h, i, j: (0, 0)
    kv_map = lambda b, h, i, j: (b, jnp.minimum(j, i), h)
    return pl.pallas_call(
        functools.partial(_attn_kernel, bq=blk, bk=blk, lam_init=lam_init),
        grid=(bn, DIFF_HEADS, nb, nb),
        in_specs=[pl.BlockSpec((1, blk, DIFF_DV), lambda b, h, i, j: (b, i, h)),
                  pl.BlockSpec((1, blk, DIFF_DV), kv_map), pl.BlockSpec((1, blk, DIFF_DV), kv_map)]
                 + [pl.BlockSpec((1, DIFF_DQK), fixed)] * 4 + [pl.BlockSpec((1, DIFF_DV), fixed)],
        out_specs=pl.BlockSpec((1, blk, DIFF_DV), lambda b, h, i, j: (b, i, h)),
        out_shape=jax.ShapeDtypeStruct((bn, seq, DIFF_W), BF16),
        scratch_shapes=[pltpu.VMEM((blk, DIFF_DV), BF16), pltpu.VMEM((blk, DIFF_DV), BF16),
                        pltpu.VMEM((2, blk, DIFF_DV), F32), pltpu.VMEM((2, blk, DIFF_DV), F32),
                        pltpu.VMEM((2, blk, DIFF_DV), F32)],
        compiler_params=_cparams(("parallel", "parallel", "arbitrary", "arbitrary")),
        name="diff_attn",
    )(q, k, v, lq1, lk1, lq2, lk2, ng)


def _outproj_kernel(h_ref, oa_ref, ys_ref, oc_ref, gw_ref, gb_ref, wo_ref, out_ref):
    zg = _dot(ys_ref[...], gw_ref[...]) + gb_ref[...]
    ob = (zg[:, 0:S5_WIDTH] * _sigmoid(zg[:, S5_WIDTH:2 * S5_WIDTH])).astype(BF16)
    acc = _dot(oa_ref[...], wo_ref[0:GDN_W, :])
    acc = acc + _dot(ob, wo_ref[GDN_W:GDN_W + S5_WIDTH, :])
    acc = acc + _dot(oc_ref[...], wo_ref[GDN_W + S5_WIDTH:, :])
    out_ref[...] = h_ref[...] + acc


def _outproj(h2, oa, ys, oc, glu_w, glu_b, w_out, tm=512):
    t = h2.shape[0]
    row = lambda i: (i, 0)
    fixed = lambda i: (0, 0)
    return pl.pallas_call(
        _outproj_kernel,
        grid=(t // tm,),
        in_specs=[pl.BlockSpec((tm, D_MODEL), row), pl.BlockSpec((tm, GDN_W), row),
                  pl.BlockSpec((tm, S5_WIDTH), row), pl.BlockSpec((tm, DIFF_W), row),
                  pl.BlockSpec((S5_WIDTH, 2 * S5_WIDTH), fixed), pl.BlockSpec((1, 2 * S5_WIDTH), fixed),
                  pl.BlockSpec((D_MODEL, D_MODEL), fixed)],
        out_specs=pl.BlockSpec((tm, D_MODEL), row),
        out_shape=jax.ShapeDtypeStruct((t, D_MODEL), F32),
        compiler_params=_cparams(("parallel",)),
        name="outproj",
    )(h2, oa, ys, oc, glu_w, glu_b, w_out)


_BIG_ID = 1.0e9


def _top16(x, ids, payload):
    n = x.shape[1]
    r16 = lax.broadcasted_iota(I32, (PEER_TOPK, n), 0)
    vals = jnp.zeros((PEER_TOPK, n), F32)
    pays = jnp.zeros((PEER_TOPK, n), F32)
    for kk in range(PEER_TOPK):
        m = jnp.max(x, axis=0, keepdims=True)
        first = jnp.min(jnp.where(x == m, ids, _BIG_ID), axis=0, keepdims=True)
        hit = ids == first
        pay = jnp.max(jnp.where(hit, payload, -1.0), axis=0, keepdims=True)
        x = jnp.where(hit, NEG_INF, x)
        vals = jnp.where(r16 == kk, m, vals)
        pays = jnp.where(r16 == kk, pay, pays)
    return vals, pays


def _peer_route_kernel(h_ref, g_ref, wqt_ref, keys_ref, xn_ref, idx_ref, gate_ref, qt_s, sv_s, si_s, *, tm):
    x = h_ref[...]
    ms = jnp.mean(x * x, axis=-1, keepdims=True)
    xn = x * lax.rsqrt(ms + NORM_EPS) * g_ref[...]
    xn_ref[...] = xn
    qt_s[...] = _dot_nt(wqt_ref[...], xn.astype(BF16)).astype(BF16)

    key_id = lax.broadcasted_iota(I32, (N_KEYS, tm), 0).astype(F32)

    def half_body(hp, carry):
        r0 = pl.multiple_of(hp * PEER_DHALF, PEER_DHALF)
        s = _dot(keys_ref[hp], qt_s[pl.ds(r0, PEER_DHALF), :])
        vals, ids = _top16(s, key_id, key_id)
        sv_s[hp] = vals
        si_s[hp] = ids
        return carry

    lax.fori_loop(0, 2 * PEER_HEADS, half_body, 0)

    i8 = lax.broadcasted_iota(I32, (8, tm), 0).astype(F32)

    def head_body(hd, carry):
        a0 = sv_s[2 * hd]
        a1 = sv_s[2 * hd + 1]
        e0 = si_s[2 * hd] * float(N_KEYS)
        e1 = si_s[2 * hd + 1]
        cs, es, fs = [], [], []
        for i in range(8):
            cs.append(a0[i:i + 1, :] + a1[0:8, :])
            es.append(e0[i:i + 1, :] + e1[0:8, :])
            fs.append(i8 + float(i * PEER_TOPK))
        cs.append(a0[0:1, :] + a1[8:16, :])
        es.append(e0[0:1, :] + e1[8:16, :])
        fs.append(i8 + 8.0)
        cs.append(a0[8:16, :] + a1[0:1, :])
        es.append(e0[8:16, :] + e1[0:1, :])
        fs.append((i8 + 8.0) * float(PEER_TOPK))
        top_s, experts = _top16(jnp.concatenate(cs, axis=0), jnp.concatenate(fs, axis=0),
                                jnp.concatenate(es, axis=0))
        ex = jnp.exp(top_s - jnp.max(top_s, axis=0, keepdims=True))
        rows = pl.ds(pl.multiple_of(hd * PEER_TOPK, PEER_TOPK), PEER_TOPK)
        gate_ref[rows, :] = ex / jnp.sum(ex, axis=0, keepdims=True)
        idx_ref[rows, :] = experts.astype(I32)
        return carry

    lax.fori_loop(0, PEER_HEADS, head_body, 0)


def _peer_route(h2, gain, wqt, keys, tm=256):
    t = h2.shape[0]
    row = lambda i: (i, 0)
    col = lambda i: (0, i)
    return pl.pallas_call(
        functools.partial(_peer_route_kernel, tm=tm),
        grid=(t // tm,),
        in_specs=[pl.BlockSpec((tm, D_MODEL), row), pl.BlockSpec((1, D_MODEL), lambda i: (0, 0)),
                  pl.BlockSpec((D_MODEL, D_MODEL), lambda i: (0, 0)),
                  pl.BlockSpec((2 * PEER_HEADS, N_KEYS, PEER_DHALF), lambda i: (0, 0, 0))],
        out_specs=[pl.BlockSpec((tm, D_MODEL), row), pl.BlockSpec((PEER_PAIRS, tm), col),
                   pl.BlockSpec((PEER_PAIRS, tm), col)],
        out_shape=[jax.ShapeDtypeStruct((t, D_MODEL), F32), jax.ShapeDtypeStruct((PEER_PAIRS, t), I32),
                   jax.ShapeDtypeStruct((PEER_PAIRS, t), F32)],
        scratch_shapes=[pltpu.VMEM((D_MODEL, tm), BF16), pltpu.VMEM((2 * PEER_HEADS, PEER_TOPK, tm), F32),
                        pltpu.VMEM((2 * PEER_HEADS, PEER_TOPK, tm), F32)],
        compiler_params=_cparams(("parallel",)),
        name="peer_route",
    )(h2, gain, wqt, keys)


TAB_ROWS = N_EXPERTS // 2


def _load_expert(tab_ref, e):
    f = tab_ref[lax.shift_right_logical(e, 1)].astype(F32)
    return f[0:8, :], f[8:16, :]


def _peer_u_kernel(idx_ref, x_ref, tab_ref, d_ref, *, tb):
    sub = lax.broadcasted_iota(I32, (8, 128), 0)
    lane = lax.broadcasted_iota(I32, (8, 128), 1)
    half_id = lax.shift_right_logical(sub, 2)
    upper = sub >= 4

    def tok8_body(t8, carry):
        dacc = jnp.zeros((8, 128), F32)
        for tl in range(8):
            t = t8 * 8 + tl
            xv = x_ref[t]
            xr = pltpu.roll(xv, 4, 0)
            xa = jnp.where(upper, xr, xv)
            xb = jnp.where(upper, xv, xr)

            def grp_body(g, dacc):
                racc = jnp.zeros((8, 128), F32)
                for i in range(8):
                    e = idx_ref[g * 8 + i, t]
                    lo, hi = _load_expert(tab_ref, e)
                    s = jnp.where(half_id == (e & 1), lo * xa + hi * xb, 0.0)
                    s = s + pltpu.roll(s, 4, 0)
                    s = s + pltpu.roll(s, 2, 0)
                    s = s + pltpu.roll(s, 1, 0)
                    racc = jnp.where(sub == i, s, racc)
                d = jnp.sum(racc, axis=1, keepdims=True)
                return jnp.where(lane == tl * PEER_TOPK + g, d, dacc)

            dacc = lax.fori_loop(0, PEER_PAIRS // 8, grp_body, dacc)
        d_ref[t8] = dacc
        return carry

    lax.fori_loop(0, tb // 8, tok8_body, 0)


def _peer_u(idx, x3, tab, tb=128):
    t = x3.shape[0]
    return pl.pallas_call(
        functools.partial(_peer_u_kernel, tb=tb),
        grid=(t // tb,),
        in_specs=[pl.BlockSpec((PEER_PAIRS, tb), lambda i: (0, i), memory_space=pltpu.SMEM),
                  pl.BlockSpec((tb, 8, 128), lambda i: (i, 0, 0)),
                  pl.BlockSpec((TAB_ROWS, 16, 128), lambda i: (0, 0, 0), pipeline_mode=pl.Buffered(1))],
        out_specs=pl.BlockSpec((tb // 8, 8, 128), lambda i: (i, 0, 0)),
        out_shape=jax.ShapeDtypeStruct((t // 8, 8, 128), F32),
        compiler_params=_cparams(("arbitrary",)),
        name="peer_u",
    )(idx, x3, tab)


def _peer_coef_kernel(d_ref, gate_ref, c_ref):
    c_ref[...] = gate_ref[...] * _gelu_tanh(d_ref[...])


def _peer_coef(d_pt, gate, tm=2048):
    t = d_pt.shape[1]
    tm = min(tm, t)
    col = lambda i: (0, i)
    return pl.pallas_call(
        _peer_coef_kernel,
        grid=(t // tm,),
        in_specs=[pl.BlockSpec((PEER_PAIRS, tm), col), pl.BlockSpec((PEER_PAIRS, tm), col)],
        out_specs=pl.BlockSpec((PEER_PAIRS, tm), col),
        out_shape=jax.ShapeDtypeStruct((PEER_PAIRS, t), F32),
        compiler_params=_cparams(("parallel",)),
        name="peer_coef",
    )(d_pt, gate)


def _peer_v_kernel(idx_ref, coef_ref, h_ref, fg_ref, tab_ref, out_ref, *, tb, final_norm):
    sub = lax.broadcasted_iota(I32, (8, 128), 0)
    half_id = lax.shift_right_logical(sub, 2)
    upper = sub >= 4
    unroll = 8

    def tok_body(t, carry):
        def pair_body(pp, accs):
            acc_lo, acc_hi = accs
            for i in range(unroll):
                p = pp * unroll + i
                e = idx_ref[p, t]
                c = coef_ref[p, t]
                lo, hi = _load_expert(tab_ref, e)
                cv = jnp.where(half_id == (e & 1), c, 0.0)
                acc_lo = acc_lo + cv * lo
                acc_hi = acc_hi + cv * hi
            return acc_lo, acc_hi

        z = jnp.zeros((8, 128), F32)
        acc_lo, acc_hi = lax.fori_loop(0, PEER_PAIRS // unroll, pair_body, (z, z))
        y_lo = acc_lo + pltpu.roll(acc_lo, 4, 0)
        y_hi = acc_hi + pltpu.roll(acc_hi, 4, 0)
        y = h_ref[t] + jnp.where(upper, y_hi, y_lo)
        if final_norm:
            ms = jnp.sum(jnp.sum(y * y, axis=1, keepdims=True), axis=0, keepdims=True) * (1.0 / D_MODEL)
            y = y * lax.rsqrt(ms + NORM_EPS) * fg_ref[...]
        out_ref[t] = y
        return carry

    lax.fori_loop(0, tb, tok_body, 0)


def _peer_v(idx, coef, h3, fg3, tab, final_norm, tb=128):
    t = h3.shape[0]
    smem = lambda: pl.BlockSpec((PEER_PAIRS, tb), lambda i: (0, i), memory_space=pltpu.SMEM)
    return pl.pallas_call(
        functools.partial(_peer_v_kernel, tb=tb, final_norm=final_norm),
        grid=(t // tb,),
        in_specs=[smem(), smem(), pl.BlockSpec((tb, 8, 128), lambda i: (i, 0, 0)),
                  pl.BlockSpec((8, 128), lambda i: (0, 0)),
                  pl.BlockSpec((TAB_ROWS, 16, 128), lambda i: (0, 0, 0), pipeline_mode=pl.Buffered(1))],
        out_specs=pl.BlockSpec((tb, 8, 128), lambda i: (i, 0, 0)),
        out_shape=jax.ShapeDtypeStruct((t, 8, 128), F32),
        compiler_params=_cparams(("arbitrary",)),
        name="peer_v",
    )(idx, coef, h3, fg3, tab)


def _pack_table(tab):
    tb = tab.astype(BF16).reshape(TAB_ROWS, 2, 2, 4, 128)
    return jnp.swapaxes(tb, 1, 2).reshape(TAB_ROWS, 16, 128)


def _peer(h2, layer, p, final_norm):
    t = h2.shape[0]
    keys = p["peer_keys"][layer].reshape(2 * PEER_HEADS, N_KEYS, PEER_DHALF).astype(BF16)
    xn, idx, gate = _peer_route(h2, p["norm2_g"][layer][None, :], p["peer_wq"][layer].T.astype(BF16), keys)
    d = _peer_u(idx, xn.reshape(t, 8, 128), _pack_table(p["peer_u"][layer]))
    d_pt = d.reshape(t // 8, 8, 8, PEER_TOPK).transpose(3, 1, 0, 2).reshape(PEER_PAIRS, t)
    coef = _peer_coef(d_pt, gate)
    out = _peer_v(idx, coef, h2.reshape(t, 8, 128), p["final_g"].reshape(8, 128),
                  _pack_table(p["peer_v"][layer]), final_norm)
    return out.reshape(t, D_MODEL)
def _rep(x, n):
    return jnp.repeat(x, n, axis=-1)


def _inproj_weight(w_in_l):
    o = 0
    qkv = w_in_l[:, o:o + GDN_QKV_W]; o += GDN_QKV_W
    z = w_in_l[:, o:o + GDN_W]; o += GDN_W
    a = w_in_l[:, o:o + GDN_HEADS]; o += GDN_HEADS
    b = w_in_l[:, o:o + GDN_HEADS]; o += GDN_HEADS
    rest = w_in_l[:, o:]
    return jnp.concatenate([qkv, z, _rep(a, GDN_DK), _rep(b, GDN_DK), rest], axis=1).astype(BF16)


def _block_diag_gc(b_gnc):
    g = b_gnc.shape[0]
    eye = jnp.eye(g, dtype=b_gnc.dtype)
    t = jnp.swapaxes(b_gnc, 1, 2)
    return (t[:, :, None, :] * eye[:, None, :, None]).reshape(g * t.shape[1], g * t.shape[2])


def _mixers(h, layer, p):
    bn, seq, _ = h.shape
    t = bn * seq
    lam_init = 0.8 - 0.6 * math.exp(-0.3 * layer)
    gdn_in, us, qc, kc, vc = _inproj(h.reshape(t, D_MODEL), p["norm1_g"][layer][None, :],
                                     _inproj_weight(p["w_in"][layer]))
    a_log = _rep(p["gdn_a_log"][layer].astype(F32), GDN_DK)[None, :]
    dtb = _rep(p["gdn_dt_bias"][layer].astype(F32), GDN_DK)[None, :]
    o_a = _gdn(gdn_in.reshape(bn, seq, INPROJ_GDN_W), p["gdn_conv_w"][layer], a_log, dtb,
               p["gdn_norm_g"][layer][None, :])
    flat = lambda x: x.reshape(1, S5_NS)
    bre_bd = _block_diag_gc(p["s5_b_re"][layer])
    bim_bd = _block_diag_gc(p["s5_b_im"][layer])
    cre_bd = _block_diag_gc(p["s5_c_re"][layer])
    cim_bd = _block_diag_gc(p["s5_c_im"][layer])
    ys = _s5(us.reshape(bn, seq // S5_SUB, S5_SUB * S5_WIDTH),
             flat(p["s5_lambda_re"][layer]), flat(p["s5_lambda_im"][layer]),
             flat(_rep(p["s5_log_step"][layer][:, None], S5_STATE)),
             bre_bd, bim_bd, cre_bd, cim_bd, p["s5_d"][layer][None, :])
    row = lambda x: x[None, :]
    o_c = _attn(qc.reshape(bn, seq, DIFF_W), kc.reshape(bn, seq, DIFF_W), vc.reshape(bn, seq, DIFF_W),
                row(p["diff_lq1"][layer]), row(p["diff_lk1"][layer]), row(p["diff_lq2"][layer]),
                row(p["diff_lk2"][layer]), row(p["diff_norm_g"][layer]), lam_init)
    h2 = _outproj(h.reshape(t, D_MODEL), o_a.reshape(t, GDN_W), ys.reshape(t, S5_WIDTH),
                  o_c.reshape(t, DIFF_W), p["s5_glu_w"][layer].astype(BF16), p["s5_glu_b"][layer][None, :],
                  p["w_out"][layer].astype(BF16))
    return h2


def kernel(x, norm1_g, w_in, gdn_conv_w, gdn_a_log, gdn_dt_bias, gdn_norm_g, s5_lambda_re, s5_lambda_im, s5_b_re, s5_b_im, s5_c_re, s5_c_im, s5_d, s5_log_step, s5_glu_w, s5_glu_b, diff_lq1, diff_lk1, diff_lq2, diff_lk2, diff_norm_g, w_out, norm2_g, peer_wq, peer_keys, peer_u, peer_v, final_g):
    p = dict(norm1_g=norm1_g, w_in=w_in, gdn_conv_w=gdn_conv_w, gdn_a_log=gdn_a_log, gdn_dt_bias=gdn_dt_bias,
             gdn_norm_g=gdn_norm_g, s5_lambda_re=s5_lambda_re, s5_lambda_im=s5_lambda_im, s5_b_re=s5_b_re,
             s5_b_im=s5_b_im, s5_c_re=s5_c_re, s5_c_im=s5_c_im, s5_d=s5_d, s5_log_step=s5_log_step,
             s5_glu_w=s5_glu_w, s5_glu_b=s5_glu_b, diff_lq1=diff_lq1, diff_lk1=diff_lk1, diff_lq2=diff_lq2,
             diff_lk2=diff_lk2, diff_norm_g=diff_norm_g, w_out=w_out, norm2_g=norm2_g, peer_wq=peer_wq,
             peer_keys=peer_keys, peer_u=peer_u, peer_v=peer_v, final_g=final_g)
    h = x
    for layer in range(DEPTH):
        h2 = _mixers(h, layer, p)
        h = _peer(h2, layer, p, final_norm=(layer == DEPTH - 1)).reshape(x.shape)
    return h
```

```python
import functools
import math

import jax
import jax.numpy as jnp
from jax import lax
from jax.experimental import pallas as pl
from jax.experimental.pallas import tpu as pltpu

F32 = jnp.float32
BF16 = jnp.bfloat16
I32 = jnp.int32
U32 = jnp.uint32

D_MODEL = 1024
DEPTH = 2
GDN_HEADS = 4
GDN_DK = 64
GDN_CHUNK = 64
GDN_W = GDN_HEADS * GDN_DK
GDN_QKV_W = 3 * GDN_W
GDN_CONV = 4
S5_WIDTH = 256
S5_GROUPS = 16
S5_GROUP_CH = 16
S5_STATE = 64
S5_NS = S5_GROUPS * S5_STATE
S5_SUB = 16
DIFF_HEADS = 4
DIFF_DQK = 64
DIFF_DV = 128
DIFF_W = 512
PEER_HEADS = 8
PEER_DHALF = 64
N_KEYS = 128
N_EXPERTS = N_KEYS * N_KEYS
PEER_TOPK = 16
PEER_PAIRS = PEER_HEADS * PEER_TOPK
NORM_EPS = 1e-6
NEG_INF = float("-inf")

VMEM_LIMIT_BYTES = 56 * 1024 * 1024


def _cparams(sem, vmem=VMEM_LIMIT_BYTES):
    return pltpu.CompilerParams(dimension_semantics=sem, vmem_limit_bytes=vmem)


def _dot(a, b):
    return jnp.dot(a, b, preferred_element_type=F32)


def _dot_nt(a, b):
    return lax.dot_general(a, b, (((1,), (1,)), ((), ())), preferred_element_type=F32)


def _dot_tn(a, b):
    return lax.dot_general(a, b, (((0,), (0,)), ((), ())), preferred_element_type=F32)


def _split(x):
    hi = x.astype(BF16)
    lo = (x - hi.astype(F32)).astype(BF16)
    return hi, lo


def _dot_sel_r(x, sel):
    hi, lo = _split(x)
    return _dot(hi, sel) + _dot(lo, sel)


def _dot_sel_l(sel, x):
    hi, lo = _split(x)
    return _dot(sel, hi) + _dot(sel, lo)


def _mm3(a, b):
    ah, al = _split(a)
    bh, bl = _split(b)
    return _dot(ah, bh) + (_dot(ah, bl) + _dot(al, bh))


def _sigmoid(x):
    return 1.0 / (1.0 + jnp.exp(-x))


def _softplus(x):
    return jnp.maximum(x, 0.0) + jnp.log1p(jnp.exp(-jnp.abs(x)))


def _gelu_tanh(x):
    c = math.sqrt(2.0 / math.pi)
    return 0.5 * x * (1.0 + jnp.tanh(c * (x + 0.044715 * (x * x * x))))


INPROJ_GDN_W = GDN_QKV_W + 3 * GDN_W


def _inproj_kernel(x_ref, g_ref, w_ref, gdn_ref, us_ref, q_ref, k_ref, v_ref):
    x = x_ref[...]
    ms = jnp.mean(x * x, axis=-1, keepdims=True)
    xn = (x * lax.rsqrt(ms + NORM_EPS) * g_ref[...]).astype(BF16)
    o = INPROJ_GDN_W
    gdn_ref[...] = _dot(xn, w_ref[:, 0:o])
    us_ref[...] = _dot(xn, w_ref[:, o:o + S5_WIDTH])
    o += S5_WIDTH
    q_ref[...] = (_dot(xn, w_ref[:, o:o + DIFF_W]) * (DIFF_DQK ** -0.5)).astype(BF16)
    k_ref[...] = _dot(xn, w_ref[:, o + DIFF_W:o + 2 * DIFF_W]).astype(BF16)
    v_ref[...] = _dot(xn, w_ref[:, o + 2 * DIFF_W:o + 3 * DIFF_W]).astype(BF16)


def _inproj(x2, gain, w, tm=512):
    t = x2.shape[0]
    nw = w.shape[1]
    row = lambda i: (i, 0)
    fixed = lambda i: (0, 0)
    return pl.pallas_call(
        _inproj_kernel,
        grid=(t // tm,),
        in_specs=[pl.BlockSpec((tm, D_MODEL), row), pl.BlockSpec((1, D_MODEL), fixed),
                  pl.BlockSpec((D_MODEL, nw), fixed)],
        out_specs=[pl.BlockSpec((tm, INPROJ_GDN_W), row), pl.BlockSpec((tm, S5_WIDTH), row),
                   pl.BlockSpec((tm, DIFF_W), row), pl.BlockSpec((tm, DIFF_W), row),
                   pl.BlockSpec((tm, DIFF_W), row)],
        out_shape=[jax.ShapeDtypeStruct((t, INPROJ_GDN_W), F32), jax.ShapeDtypeStruct((t, S5_WIDTH), F32),
                   jax.ShapeDtypeStruct((t, DIFF_W), BF16), jax.ShapeDtypeStruct((t, DIFF_W), BF16),
                   jax.ShapeDtypeStruct((t, DIFF_W), BF16)],
        compiler_params=_cparams(("parallel",)),
        name="inproj",
    )(x2, gain, w)


def _gdn_kernel(blk_ref, convw_ref, alog_ref, dtb_ref, ng_ref, out_ref,
                s_ref, tail_ref, xp_ref, q_s, k_s, v_s, b_s, g_s, *, ct):
    c64 = GDN_CHUNK

    @pl.when(pl.program_id(1) == 0)
    def _():
        s_ref[...] = jnp.zeros_like(s_ref)
        tail_ref[...] = jnp.zeros_like(tail_ref)

    qkv = blk_ref[0, :, 0:GDN_QKV_W]
    xp_ref[0:8, :] = tail_ref[...]
    xp_ref[8:8 + ct, :] = qkv
    tail_ref[...] = qkv[ct - 8:ct, :]
    cw = convw_ref[...]
    y = cw[0:1, :] * xp_ref[5:5 + ct, :]
    for j in range(1, GDN_CONV):
        y = y + cw[j:j + 1, :] * xp_ref[5 + j:5 + j + ct, :]
    y = y * _sigmoid(y)

    ri = lax.broadcasted_iota(I32, (GDN_W, GDN_W), 0)
    ci = lax.broadcasted_iota(I32, (GDN_W, GDN_W), 1)
    head_ones = jnp.where((ri // c64) == (ci // c64), 1.0, 0.0).astype(BF16)

    q = y[:, 0:GDN_W]
    k = y[:, GDN_W:2 * GDN_W]
    q_s[...] = q * lax.rsqrt(_dot_sel_r(q * q, head_ones) + 1e-6) * (GDN_DK ** -0.5)
    k_s[...] = k * lax.rsqrt(_dot_sel_r(k * k, head_ones) + 1e-6)
    v_s[...] = y[:, 2 * GDN_W:3 * GDN_W]
    a_rep = blk_ref[0, :, GDN_QKV_W + GDN_W:GDN_QKV_W + 2 * GDN_W]
    b_rep = blk_ref[0, :, GDN_QKV_W + 2 * GDN_W:GDN_QKV_W + 3 * GDN_W]
    b_s[...] = _sigmoid(b_rep)
    g_raw = -jnp.exp(alog_ref[...]) * _softplus(a_rep + dtb_ref[...])
    rt = lax.broadcasted_iota(I32, (ct, ct), 0)
    ctk = lax.broadcasted_iota(I32, (ct, ct), 1)
    tri_bd = jnp.where(((rt // c64) == (ctk // c64)) & (ctk <= rt), 1.0, 0.0).astype(BF16)
    g_s[...] = _dot_sel_l(tri_bd, g_raw)

    r64 = lax.broadcasted_iota(I32, (c64, c64), 0)
    col64 = lax.broadcasted_iota(I32, (c64, c64), 1)
    incl = col64 <= r64
    strict = col64 < r64
    eye_b = col64 == r64
    eye_f = jnp.where(eye_b, 1.0, 0.0)
    ones64 = jnp.ones((c64, c64), BF16)
    ng = ng_ref[...]

    def chunk_body(c, carry):
        r0 = pl.multiple_of(c * c64, c64)
        rows = pl.ds(r0, c64)
        outs = []
        for h in range(GDN_HEADS):
            lanes = slice(h * c64, (h + 1) * c64)
            gi = g_s[rows, lanes]
            gj = _dot_sel_l(ones64, jnp.where(eye_b, gi, 0.0))
            dec = jnp.exp(jnp.where(incl, gi - gj, NEG_INF))
            kh = k_s[rows, lanes]
            qh = q_s[rows, lanes]
            vh = v_s[rows, lanes]
            bi = b_s[rows, lanes]
            kb = kh.astype(BF16)
            kk = _dot_nt(kb, kb)
            lm = jnp.where(strict, bi * kk * dec, 0.0)
            p = eye_f - lm
            m = lm
            for _ in range(5):
                m = _mm3(m, m)
                p = p + _mm3(p, m)
            eg = jnp.exp(gi)
            pb = p.astype(BF16)
            u = _dot(pb, (vh * bi).astype(BF16))
            w = _dot(pb, (kh * (bi * eg)).astype(BF16))
            qk = jnp.where(incl, _dot_nt(qh.astype(BF16), kb) * dec, 0.0)
            g_last = gi[c64 - 1:c64, :]
            kg = kh * jnp.exp(g_last - gi)
            s_old = s_ref[h]
            sb = s_old.astype(BF16)
            v_new = u - _dot(w.astype(BF16), sb)
            vnb = v_new.astype(BF16)
            o = _dot((qh * eg).astype(BF16), sb) + _dot(qk.astype(BF16), vnb)
            s_ref[h] = s_old * jnp.exp(g_last) + _dot_tn(kg.astype(BF16), vnb)
            ms = _dot_sel_r(o * o, ones64) * (1.0 / c64)
            outs.append(o * lax.rsqrt(ms + NORM_EPS) * ng)
        z = blk_ref[0, rows, GDN_QKV_W:GDN_QKV_W + GDN_W]
        o_all = jnp.concatenate(outs, axis=1) * (z * _sigmoid(z))
        out_ref[0, rows, :] = o_all.astype(out_ref.dtype)
        return carry

    lax.fori_loop(0, ct // c64, chunk_body, 0)


def _gdn(gdn_in, conv_w, a_log_rep, dtb_rep, ng, ct=256):
    bn, seq, _ = gdn_in.shape
    fixed = lambda b, l: (0, 0)
    return pl.pallas_call(
        functools.partial(_gdn_kernel, ct=ct),
        grid=(bn, seq // ct),
        in_specs=[pl.BlockSpec((1, ct, INPROJ_GDN_W), lambda b, l: (b, l, 0)),
                  pl.BlockSpec((GDN_CONV, GDN_QKV_W), fixed), pl.BlockSpec((1, GDN_W), fixed),
                  pl.BlockSpec((1, GDN_W), fixed), pl.BlockSpec((1, GDN_DK), fixed)],
        out_specs=pl.BlockSpec((1, ct, GDN_W), lambda b, l: (b, l, 0)),
        out_shape=jax.ShapeDtypeStruct((bn, seq, GDN_W), BF16),
        scratch_shapes=[pltpu.VMEM((GDN_HEADS, GDN_DK, GDN_DK), F32), pltpu.VMEM((8, GDN_QKV_W), F32),
                        pltpu.VMEM((ct + 8, GDN_QKV_W), F32)] + [pltpu.VMEM((ct, GDN_W), F32)] * 5,
        compiler_params=_cparams(("arbitrary", "arbitrary")),
        name="gdn",
    )(gdn_in, conv_w, a_log_rep, dtb_rep, ng)


def _s5_kernel(u_ref, lre_ref, lim_ref, lstep_ref, bre_ref, bim_ref, cre_ref, cim_ref, d_ref, y_ref,
               bmat, cmat, avec, carry, zr_s, zi_s, xr_s, xi_s, *, rb):
    ns = S5_NS
    cw = S5_WIDTH

    @pl.when(pl.program_id(1) == 0)
    def _():
        step = jnp.exp(lstep_ref[...])
        lr = lre_ref[...]
        li = lim_ref[...]
        mag = jnp.exp(lr * step)
        ar = mag * jnp.cos(li * step)
        ai = mag * jnp.sin(li * step)
        den = lr * lr + li * li
        mr = ((ar - 1.0) * lr + ai * li) / den
        mi = (ai * lr - (ar - 1.0) * li) / den
        bre = bre_ref[...]
        bim = bim_ref[...]
        bmat[:, 0:ns] = (mr * bre - mi * bim).astype(BF16)
        bmat[:, ns:2 * ns] = (mr * bim + mi * bre).astype(BF16)
        cmat[0:ns, :] = cre_ref[...].astype(BF16)
        cmat[ns:2 * ns, :] = (-cim_ref[...]).astype(BF16)
        avec[0:1, :] = ar
        avec[1:2, :] = ai
        pr, pi = ar, ai
        for _ in range(4):
            pr, pi = pr * pr - pi * pi, 2.0 * pr * pi
        avec[2:3, :] = pr
        avec[3:4, :] = pi
        carry[...] = jnp.zeros_like(carry)

    ar = avec[0:1, :]
    ai = avec[1:2, :]
    a16r = avec[2:3, :]
    a16i = avec[3:4, :]

    def inject(s):
        ub = u_ref[0, :, s * cw:(s + 1) * cw].astype(BF16)
        return _dot(ub, bmat[...])

    xr = jnp.zeros((rb, ns), F32)
    xi = jnp.zeros((rb, ns), F32)
    for s in range(S5_SUB):
        p = inject(s)
        xr, xi = ar * xr - ai * xi + p[:, 0:ns], ar * xi + ai * xr + p[:, ns:2 * ns]
    zr_s[...] = xr
    zi_s[...] = xi

    def row_step(kk, c):
        cr, ci_ = c
        xr_s[pl.ds(kk, 1), :] = cr
        xi_s[pl.ds(kk, 1), :] = ci_
        zr = zr_s[pl.ds(kk, 1), :]
        zi = zi_s[pl.ds(kk, 1), :]
        return (a16r * cr - a16i * ci_ + zr, a16r * ci_ + a16i * cr + zi)

    cr, ci_ = lax.fori_loop(0, rb, row_step, (carry[0:1, :], carry[1:2, :]))
    carry[0:1, :] = cr
    carry[1:2, :] = ci_

    xr = xr_s[...]
    xi = xi_s[...]
    dsk = d_ref[...]
    for s in range(S5_SUB):
        p = inject(s)
        xr, xi = ar * xr - ai * xi + p[:, 0:ns], ar * xi + ai * xr + p[:, ns:2 * ns]
        yv = _dot(xr.astype(BF16), cmat[0:ns, :]) + _dot(xi.astype(BF16), cmat[ns:2 * ns, :])
        yv = yv + dsk * u_ref[0, :, s * cw:(s + 1) * cw]
        y_ref[0, :, s * cw:(s + 1) * cw] = _gelu_tanh(yv).astype(y_ref.dtype)


def _s5(u_rows, lre, lim, lstep, bre_bd, bim_bd, cre_bd, cim_bd, dskip, rb=128):
    bn, nrows, rw = u_rows.shape
    fixed = lambda b, r: (0, 0)
    ns = S5_NS
    return pl.pallas_call(
        functools.partial(_s5_kernel, rb=rb),
        grid=(bn, nrows // rb),
        in_specs=[pl.BlockSpec((1, rb, rw), lambda b, r: (b, r, 0)),
                  pl.BlockSpec((1, ns), fixed), pl.BlockSpec((1, ns), fixed), pl.BlockSpec((1, ns), fixed),
                  pl.BlockSpec((S5_WIDTH, ns), fixed), pl.BlockSpec((S5_WIDTH, ns), fixed),
                  pl.BlockSpec((ns, S5_WIDTH), fixed), pl.BlockSpec((ns, S5_WIDTH), fixed),
                  pl.BlockSpec((1, S5_WIDTH), fixed)],
        out_specs=pl.BlockSpec((1, rb, rw), lambda b, r: (b, r, 0)),
        out_shape=jax.ShapeDtypeStruct((bn, nrows, rw), BF16),
        scratch_shapes=[pltpu.VMEM((S5_WIDTH, 2 * ns), BF16), pltpu.VMEM((2 * ns, S5_WIDTH), BF16),
                        pltpu.VMEM((8, ns), F32), pltpu.VMEM((8, ns), F32)]
                       + [pltpu.VMEM((rb, ns), F32)] * 4,
        compiler_params=_cparams(("arbitrary", "arbitrary")),
        name="s5",
    )(u_rows, lre, lim, lstep, bre_bd, bim_bd, cre_bd, cim_bd, dskip)


def _attn_kernel(q_ref, k_ref, v_ref, lq1_ref, lk1_ref, lq2_ref, lk2_ref, ng_ref, o_ref,
                 qa_s, qb_s, m_s, l_s, acc_s, *, bq, bk, lam_init):
    i = pl.program_id(2)
    j = pl.program_id(3)

    @pl.when(j == 0)
    def _():
        q = q_ref[0]
        lane = lax.broadcasted_iota(I32, q.shape, 1)
        zero = jnp.zeros_like(q)
        qa_s[...] = jnp.where(lane < DIFF_DQK, q, zero)
        qb_s[...] = jnp.where(lane >= DIFF_DQK, q, zero)
        m_s[...] = jnp.full_like(m_s, NEG_INF)
        l_s[...] = jnp.zeros_like(l_s)
        acc_s[...] = jnp.zeros_like(acc_s)

    def update(masked):
        k = k_ref[0]
        v = v_ref[0]
        ps = []
        for mi, q_s in enumerate((qa_s, qb_s)):
            s = _dot_nt(q_s[...], k)
            if masked:
                row = lax.broadcasted_iota(I32, s.shape, 0)
                col = lax.broadcasted_iota(I32, s.shape, 1)
                s = jnp.where(col <= row, s, NEG_INF)
            m_old = m_s[mi]
            m_new = jnp.maximum(m_old, jnp.max(s, axis=-1, keepdims=True))
            p = jnp.exp(s - m_new[:, 0:1])
            alpha = jnp.exp(m_old - m_new)
            l_s[mi] = alpha * l_s[mi] + jnp.sum(p, axis=-1, keepdims=True)
            m_s[mi] = m_new
            acc_s[mi] = alpha * acc_s[mi] + _dot(p.astype(BF16), v)

    @pl.when(j < i)
    def _():
        update(False)

    @pl.when(j == i)
    def _():
        update(True)
        lam = (jnp.exp(jnp.sum(lq1_ref[...] * lk1_ref[...], axis=-1, keepdims=True))
               - jnp.exp(jnp.sum(lq2_ref[...] * lk2_ref[...], axis=-1, keepdims=True)) + lam_init)
        o = acc_s[0] / l_s[0] - lam * (acc_s[1] / l_s[1])
        ms = jnp.mean(o * o, axis=-1, keepdims=True)
        o = o * lax.rsqrt(ms + NORM_EPS) * ng_ref[...] * (1.0 - lam_init)
        o_ref[0] = o.astype(o_ref.dtype)


def _attn(q, k, v, lq1, lk1, lq2, lk2, ng, lam_init, blk=512):
    bn, seq, _ = q.shape
    nb = seq // blk
    fixed = lambda b, h, i, j: (0, 0)
    kv_map = lambda b, h, i, j: (b, jnp.minimum(j, i), h)
    return pl.pallas_call(
        functools.partial(_attn_kernel, bq=blk, bk=blk, lam_init=lam_init),
        grid=(bn, DIFF_HEADS, nb, nb),
        in_specs=[pl.BlockSpec((1, blk, DIFF_DV), lambda b, h, i, j: (b, i, h)),
                  pl.BlockSpec((1, blk, DIFF_DV), kv_map), pl.BlockSpec((1, blk, DIFF_DV), kv_map)]
                 + [pl.BlockSpec((1, DIFF_DQK), fixed)] * 4 + [pl.BlockSpec((1, DIFF_DV), fixed)],
        out_specs=pl.BlockSpec((1, blk, DIFF_DV), lambda b, h, i, j: (b, i, h)),
        out_shape=jax.ShapeDtypeStruct((bn, seq, DIFF_W), BF16),
        scratch_shapes=[pltpu.VMEM((blk, DIFF_DV), BF16), pltpu.VMEM((blk, DIFF_DV), BF16),
                        pltpu.VMEM((2, blk, DIFF_DV), F32), pltpu.VMEM((2, blk, DIFF_DV), F32),
                        pltpu.VMEM((2, blk, DIFF_DV), F32)],
        compiler_params=_cparams(("parallel", "parallel", "arbitrary", "arbitrary")),
        name="diff_attn",
    )(q, k, v, lq1, lk1, lq2, lk2, ng)


def _outproj_kernel(h_ref, oa_ref, ys_ref, oc_ref, gw_ref, gb_ref, wo_ref, out_ref):
    zg = _dot(ys_ref[...], gw_ref[...]) + gb_ref[...]
    ob = (zg[:, 0:S5_WIDTH] * _sigmoid(zg[:, S5_WIDTH:2 * S5_WIDTH])).astype(BF16)
    acc = _dot(oa_ref[...], wo_ref[0:GDN_W, :])
    acc = acc + _dot(ob, wo_ref[GDN_W:GDN_W + S5_WIDTH, :])
    acc = acc + _dot(oc_ref[...], wo_ref[GDN_W + S5_WIDTH:, :])
    out_ref[...] = h_ref[...] + acc


def _outproj(h2, oa, ys, oc, glu_w, glu_b, w_out, tm=512):
    t = h2.shape[0]
    row = lambda i: (i, 0)
    fixed = lambda i: (0, 0)
    return pl.pallas_call(
        _outproj_kernel,
        grid=(t // tm,),
        in_specs=[pl.BlockSpec((tm, D_MODEL), row), pl.BlockSpec((tm, GDN_W), row),
                  pl.BlockSpec((tm, S5_WIDTH), row), pl.BlockSpec((tm, DIFF_W), row),
                  pl.BlockSpec((S5_WIDTH, 2 * S5_WIDTH), fixed), pl.BlockSpec((1, 2 * S5_WIDTH), fixed),
                  pl.BlockSpec((D_MODEL, D_MODEL), fixed)],
        out_specs=pl.BlockSpec((tm, D_MODEL), row),
        out_shape=jax.ShapeDtypeStruct((t, D_MODEL), F32),
        compiler_params=_cparams(("parallel",)),
        name="outproj",
    )(h2, oa, ys, oc, glu_w, glu_b, w_out)


_BIG_ID = 1.0e9


def _top16(x, ids, payload):
    n = x.shape[1]
    r16 = lax.broadcasted_iota(I32, (PEER_TOPK, n), 0)
    vals = jnp.zeros((PEER_TOPK, n), F32)
    pays = jnp.zeros((PEER_TOPK, n), F32)
    for kk in range(PEER_TOPK):
        m = jnp.max(x, axis=0, keepdims=True)
        first = jnp.min(jnp.where(x == m, ids, _BIG_ID), axis=0, keepdims=True)
        hit = ids == first
        pay = jnp.max(jnp.where(hit, payload, -1.0), axis=0, keepdims=True)
        x = jnp.where(hit, NEG_INF, x)
        vals = jnp.where(r16 == kk, m, vals)
        pays = jnp.where(r16 == kk, pay, pays)
    return vals, pays


def _peer_route_kernel(h_ref, g_ref, wqt_ref, keys_ref, xn_ref, row_ref, par_ref, gate_ref,
                       qt_s, sv_s, si_s, *, tm):
    x = h_ref[...]
    ms = jnp.mean(x * x, axis=-1, keepdims=True)
    xn = x * lax.rsqrt(ms + NORM_EPS) * g_ref[...]
    xn_ref[...] = xn
    qt_s[...] = _dot_nt(wqt_ref[...], xn.astype(BF16)).astype(BF16)

    key_id = lax.broadcasted_iota(I32, (N_KEYS, tm), 0).astype(F32)

    def half_body(hp, carry):
        r0 = pl.multiple_of(hp * PEER_DHALF, PEER_DHALF)
        s = _dot(keys_ref[hp], qt_s[pl.ds(r0, PEER_DHALF), :])
        vals, ids = _top16(s, key_id, key_id)
        sv_s[hp] = vals
        si_s[hp] = ids
        return carry

    lax.fori_loop(0, 2 * PEER_HEADS, half_body, 0)

    i8 = lax.broadcasted_iota(I32, (8, tm), 0).astype(F32)

    def head_body(hd, carry):
        a0 = sv_s[2 * hd]
        a1 = sv_s[2 * hd + 1]
        e0 = si_s[2 * hd] * float(N_KEYS)
        e1 = si_s[2 * hd + 1]
        cs, es, fs = [], [], []
        for i in range(8):
            cs.append(a0[i:i + 1, :] + a1[0:8, :])
            es.append(e0[i:i + 1, :] + e1[0:8, :])
            fs.append(i8 + float(i * PEER_TOPK))
        cs.append(a0[0:1, :] + a1[8:16, :])
        es.append(e0[0:1, :] + e1[8:16, :])
        fs.append(i8 + 8.0)
        cs.append(a0[8:16, :] + a1[0:1, :])
        es.append(e0[8:16, :] + e1[0:1, :])
        fs.append((i8 + 8.0) * float(PEER_TOPK))
        top_s, experts = _top16(jnp.concatenate(cs, axis=0), jnp.concatenate(fs, axis=0),
                                jnp.concatenate(es, axis=0))
        ex = jnp.exp(top_s - jnp.max(top_s, axis=0, keepdims=True))
        rows = pl.ds(pl.multiple_of(hd * PEER_TOPK, PEER_TOPK), PEER_TOPK)
        gate_ref[rows, :] = ex / jnp.sum(ex, axis=0, keepdims=True)
        expert = experts.astype(I32)
        row_ref[rows, :] = lax.shift_right_logical(expert, 1) * 8
        par_ref[rows, :] = (expert & 1).astype(F32)
        return carry

    lax.fori_loop(0, PEER_HEADS, head_body, 0)


def _peer_route(h2, gain, wqt, keys, tm=256):
    t = h2.shape[0]
    row = lambda i: (i, 0)
    col = lambda i: (0, i)
    return pl.pallas_call(
        functools.partial(_peer_route_kernel, tm=tm),
        grid=(t // tm,),
        in_specs=[pl.BlockSpec((tm, D_MODEL), row), pl.BlockSpec((1, D_MODEL), lambda i: (0, 0)),
                  pl.BlockSpec((D_MODEL, D_MODEL), lambda i: (0, 0)),
                  pl.BlockSpec((2 * PEER_HEADS, N_KEYS, PEER_DHALF), lambda i: (0, 0, 0))],
        out_specs=[pl.BlockSpec((tm, D_MODEL), row), pl.BlockSpec((PEER_PAIRS, tm), col),
                   pl.BlockSpec((PEER_PAIRS, tm), col), pl.BlockSpec((PEER_PAIRS, tm), col)],
        out_shape=[jax.ShapeDtypeStruct((t, D_MODEL), F32), jax.ShapeDtypeStruct((PEER_PAIRS, t), I32),
                   jax.ShapeDtypeStruct((PEER_PAIRS, t), F32), jax.ShapeDtypeStruct((PEER_PAIRS, t), F32)],
        scratch_shapes=[pltpu.VMEM((D_MODEL, tm), BF16), pltpu.VMEM((2 * PEER_HEADS, PEER_TOPK, tm), F32),
                        pltpu.VMEM((2 * PEER_HEADS, PEER_TOPK, tm), F32)],
        compiler_params=_cparams(("parallel",)),
        name="peer_route",
    )(h2, gain, wqt, keys)


TAB_ROWS = N_EXPERTS // 2
HIGH_HALF = 0xFFFF0000


def _splat_rows(row):
    r = lax.broadcasted_iota(I32, (128, 128), 0)
    c = lax.broadcasted_iota(I32, (128, 128), 1)
    diag = jnp.where(r == c, jnp.broadcast_to(row, (128, 128)), 0.0).astype(BF16)
    return _dot(diag, jnp.ones((128, 128), BF16))


def _stage_shifts(par_ref, t, srep_s):
    shift = 16.0 - 16.0 * par_ref[pl.ds(t, 1), :]
    srep_s[...] = pltpu.bitcast(_splat_rows(shift).astype(I32), U32)


def _load_words(tab_ref, row_ref, t, p):
    return tab_ref[pl.ds(pl.multiple_of(row_ref[t, p], 8), 8), :]


def _load_expert(tab_ref, row_ref, srep_s, t, p):
    w = lax.shift_left(_load_words(tab_ref, row_ref, t, p), jnp.broadcast_to(srep_s[p:p + 1, :], (8, 128)))
    return pltpu.bitcast(w & jnp.uint32(HIGH_HALF), F32)


_TREE8_SUBLANE = (3, 7, 1, 5, 2, 6, 0, 4)


def _tree8(v, upper, bit2, bit1):
    c = []
    for a, b in ((v[0], v[1]), (v[2], v[3]), (v[4], v[5]), (v[6], v[7])):
        c.append(jnp.where(upper, b + pltpu.roll(b, 4, 0), a + pltpu.roll(a, 4, 0)))
    e = []
    for c1, c2 in ((c[0], c[1]), (c[2], c[3])):
        e.append(jnp.where(bit2, c1 + pltpu.roll(c1, 2, 0), c2 + pltpu.roll(c2, 6, 0)))
    return jnp.where(bit1, e[0] + pltpu.roll(e[0], 1, 0), e[1] + pltpu.roll(e[1], 7, 0))


PEER_U_TOKENS_PER_STEP = 2


def _peer_u_kernel(row_ref, par_ref, x_ref, tab_ref, d_ref, r_s, srep_s, *, tb):
    sub = lax.broadcasted_iota(I32, (8, 128), 0)
    lane = lax.broadcasted_iota(I32, (8, 128), 1)
    upper = sub >= 4
    bit2 = (sub & 2) != 0
    bit1 = (sub & 1) != 0
    groups = PEER_PAIRS // 8
    nt = PEER_U_TOKENS_PER_STEP

    def tok_body(tt, carry):
        for j in range(nt):
            _stage_shifts(par_ref, tt * nt + j, srep_s.at[j])
        for j in range(nt):
            t = tt * nt + j
            xv = x_ref[t]
            for g in range(groups):
                prods = []
                for m in range(8):
                    p = g * 8 + _TREE8_SUBLANE[m]
                    prods.append(_load_expert(tab_ref, row_ref, srep_s.at[j], t, p) * xv)
                r_s[t * groups + g] = _tree8(prods, upper, bit2, bit1)
        return carry

    lax.fori_loop(0, tb // nt, tok_body, 0)

    def red_body(t8, carry):
        dacc = jnp.zeros((8, 128), F32)
        for tl in range(8):
            for g in range(groups):
                d = jnp.sum(r_s[(t8 * 8 + tl) * groups + g], axis=1, keepdims=True)
                dacc = jnp.where(lane == tl * groups + g, d, dacc)
        d_ref[t8] = dacc
        return carry

    lax.fori_loop(0, tb // 8, red_body, 0)


def _tok_spec(tb, space=None):
    return pl.BlockSpec((tb, PEER_PAIRS), lambda i: (i, 0), memory_space=space)


def _table_spec():
    return pl.BlockSpec((TAB_ROWS * 8, 128), lambda i: (0, 0), pipeline_mode=pl.Buffered(1))


def _peer_u(rows, par, x3, tab, tb=128):
    t = x3.shape[0]
    return pl.pallas_call(
        functools.partial(_peer_u_kernel, tb=tb),
        grid=(t // tb,),
        in_specs=[_tok_spec(tb, pltpu.SMEM), _tok_spec(tb), pl.BlockSpec((tb, 8, 128), lambda i: (i, 0, 0)),
                  _table_spec()],
        out_specs=pl.BlockSpec((tb // 8, 8, 128), lambda i: (i, 0, 0)),
        out_shape=jax.ShapeDtypeStruct((t // 8, 8, 128), F32),
        scratch_shapes=[pltpu.VMEM((tb * (PEER_PAIRS // 8), 8, 128), F32),
                        pltpu.VMEM((PEER_U_TOKENS_PER_STEP, 128, 128), U32)],
        compiler_params=_cparams(("arbitrary",)),
        name="peer_u",
    )(rows, par, x3, tab)


def _peer_coef_kernel(d_ref, gate_ref, c_ref):
    c_ref[...] = gate_ref[...] * _gelu_tanh(d_ref[...])


def _peer_coef(d_tp, gate_tp, tm=2048):
    t = d_tp.shape[0]
    tm = min(tm, t)
    return pl.pallas_call(
        _peer_coef_kernel,
        grid=(t // tm,),
        in_specs=[_tok_spec(tm), _tok_spec(tm)],
        out_specs=_tok_spec(tm),
        out_shape=jax.ShapeDtypeStruct((t, PEER_PAIRS), F32),
        compiler_params=_cparams(("parallel",)),
        name="peer_coef",
    )(d_tp, gate_tp)


PEER_V_TOKENS_PER_STEP = 2
PEER_V_KDIM = (PEER_PAIRS // 2) * 16


def _peer_v_kernel(row_ref, par_ref, coef_ref, h_ref, fg_ref, tab_ref, out_ref, cz_s, *, tb, final_norm):
    half = PEER_PAIRS // 2
    kdim = PEER_V_KDIM
    nt = PEER_V_TOKENS_PER_STEP
    pk = lax.broadcasted_iota(I32, (PEER_PAIRS, kdim), 0)
    qk = lax.shift_right_logical(lax.broadcasted_iota(I32, (PEER_PAIRS, kdim), 1), 4)
    hk = (lax.broadcasted_iota(I32, (tb, kdim), 1) & 1).astype(F32)
    coef = coef_ref[...].astype(BF16)
    par = par_ref[...].astype(BF16)
    for hf in range(2):
        expand = jnp.where(pk == qk + hf * half, 1.0, 0.0).astype(BF16)
        cz_s[hf] = jnp.where(hk == _dot(par, expand), _dot(coef, expand), 0.0)
    ks = lax.broadcasted_iota(I32, (8, kdim), 0)
    k8 = lax.broadcasted_iota(I32, (8, kdim), 1)
    own_sublane = lax.shift_right_logical(k8 & 15, 1) == ks

    def tok_body(tt, carry):
        for j in range(nt):
            t = tt * nt + j
            pieces = []
            for q in range(half):
                wa = pltpu.bitcast(_load_words(tab_ref, row_ref, t, q), BF16)
                wb = pltpu.bitcast(_load_words(tab_ref, row_ref, t, half + q), BF16)
                pieces.append(jnp.concatenate([wa, wb], axis=1))
            g = jnp.concatenate(pieces, axis=0)
            ck = jnp.concatenate(
                [jnp.where(own_sublane, jnp.broadcast_to(cz_s[hf, pl.ds(t, 1), :], (8, kdim)), 0.0)
                 for hf in range(2)], axis=0).astype(BF16)
            o = _dot(ck, g)
            y = h_ref[t] + (o[0:8, 0:128] + o[8:16, 128:256])
            if final_norm:
                ms = jnp.sum(jnp.sum(y * y, axis=1, keepdims=True), axis=0, keepdims=True) * (1.0 / D_MODEL)
                y = y * lax.rsqrt(ms + NORM_EPS) * fg_ref[...]
            out_ref[t] = y
        return carry

    lax.fori_loop(0, tb // nt, tok_body, 0)


def _peer_v(rows, par, coef, h3, fg3, tab, final_norm, tb=128):
    t = h3.shape[0]
    return pl.pallas_call(
        functools.partial(_peer_v_kernel, tb=tb, final_norm=final_norm),
        grid=(t // tb,),
        in_specs=[_tok_spec(tb, pltpu.SMEM), _tok_spec(tb), _tok_spec(tb),
                  pl.BlockSpec((tb, 8, 128), lambda i: (i, 0, 0)), pl.BlockSpec((8, 128), lambda i: (0, 0)),
                  _table_spec()],
        out_specs=pl.BlockSpec((tb, 8, 128), lambda i: (i, 0, 0)),
        out_shape=jax.ShapeDtypeStruct((t, 8, 128), F32),
        scratch_shapes=[pltpu.VMEM((2, tb, PEER_V_KDIM), F32)],
        compiler_params=_cparams(("arbitrary",)),
        name="peer_v",
    )(rows, par, coef, h3, fg3, tab)


def _pack_table(tab):
    bits = lax.bitcast_convert_type(tab.astype(BF16), jnp.uint16).astype(U32).reshape(TAB_ROWS, 2, 8, 128)
    return (bits[:, 0] | (bits[:, 1] << 16)).reshape(TAB_ROWS * 8, 128)


def _peer(h2, layer, p, final_norm):
    t = h2.shape[0]
    keys = p["peer_keys"][layer].reshape(2 * PEER_HEADS, N_KEYS, PEER_DHALF).astype(BF16)
    xn, rows, par, gate = _peer_route(h2, p["norm2_g"][layer][None, :],
                                      p["peer_wq"][layer].T.astype(BF16), keys)
    rows, par, gate = rows.T, par.T, gate.T
    d = _peer_u(rows, par, xn.reshape(t, 8, 128), _pack_table(p["peer_u"][layer]))
    d_tp = d.reshape(t // 8, 8, 8, PEER_TOPK).transpose(0, 2, 3, 1).reshape(t, PEER_PAIRS)
    coef = _peer_coef(d_tp, gate)
    out = _peer_v(rows, par, coef, h2.reshape(t, 8, 128), p["final_g"].reshape(8, 128),
                  _pack_table(p["peer_v"][layer]), final_norm)
    return out.reshape(t, D_MODEL)


def _rep(x, n):
    return jnp.repeat(x, n, axis=-1)


def _inproj_weight(w_in_l):
    o = 0
    qkv = w_in_l[:, o:o + GDN_QKV_W]; o += GDN_QKV_W
    z = w_in_l[:, o:o + GDN_W]; o += GDN_W
    a = w_in_l[:, o:o + GDN_HEADS]; o += GDN_HEADS
    b = w_in_l[:, o:o + GDN_HEADS]; o += GDN_HEADS
    rest = w_in_l[:, o:]
    return jnp.concatenate([qkv, z, _rep(a, GDN_DK), _rep(b, GDN_DK), rest], axis=1).astype(BF16)


def _block_diag_gc(b_gnc):
    g = b_gnc.shape[0]
    eye = jnp.eye(g, dtype=b_gnc.dtype)
    t = jnp.swapaxes(b_gnc, 1, 2)
    return (t[:, :, None, :] * eye[:, None, :, None]).reshape(g * t.shape[1], g * t.shape[2])


def _mixers(h, layer, p):
    bn, seq, _ = h.shape
    t = bn * seq
    lam_init = 0.8 - 0.6 * math.exp(-0.3 * layer)
    gdn_in, us, qc, kc, vc = _inproj(h.reshape(t, D_MODEL), p["norm1_g"][layer][None, :],
                                     _inproj_weight(p["w_in"][layer]))
    a_log = _rep(p["gdn_a_log"][layer].astype(F32), GDN_DK)[None, :]
    dtb = _rep(p["gdn_dt_bias"][layer].astype(F32), GDN_DK)[None, :]
    o_a = _gdn(gdn_in.reshape(bn, seq, INPROJ_GDN_W), p["gdn_conv_w"][layer], a_log, dtb,
               p["gdn_norm_g"][layer][None, :])
    flat = lambda x: x.reshape(1, S5_NS)
    bre_bd = _block_diag_gc(p["s5_b_re"][layer])
    bim_bd = _block_diag_gc(p["s5_b_im"][layer])
    cre_bd = _block_diag_gc(p["s5_c_re"][layer])
    cim_bd = _block_diag_gc(p["s5_c_im"][layer])
    ys = _s5(us.reshape(bn, seq // S5_SUB, S5_SUB * S5_WIDTH),
             flat(p["s5_lambda_re"][layer]), flat(p["s5_lambda_im"][layer]),
             flat(_rep(p["s5_log_step"][layer][:, None], S5_STATE)),
             bre_bd, bim_bd, cre_bd, cim_bd, p["s5_d"][layer][None, :])
    row = lambda x: x[None, :]
    o_c = _attn(qc.reshape(bn, seq, DIFF_W), kc.reshape(bn, seq, DIFF_W), vc.reshape(bn, seq, DIFF_W),
                row(p["diff_lq1"][layer]), row(p["diff_lk1"][layer]), row(p["diff_lq2"][layer]),
                row(p["diff_lk2"][layer]), row(p["diff_norm_g"][layer]), lam_init)
    h2 = _outproj(h.reshape(t, D_MODEL), o_a.reshape(t, GDN_W), ys.reshape(t, S5_WIDTH),
                  o_c.reshape(t, DIFF_W), p["s5_glu_w"][layer].astype(BF16), p["s5_glu_b"][layer][None, :],
                  p["w_out"][layer].astype(BF16))
    return h2


def kernel(x, norm1_g, w_in, gdn_conv_w, gdn_a_log, gdn_dt_bias, gdn_norm_g, s5_lambda_re, s5_lambda_im, s5_b_re, s5_b_im, s5_c_re, s5_c_im, s5_d, s5_log_step, s5_glu_w, s5_glu_b, diff_lq1, diff_lk1, diff_lq2, diff_lk2, diff_norm_g, w_out, norm2_g, peer_wq, peer_keys, peer_u, peer_v, final_g):
    p = dict(norm1_g=norm1_g, w_in=w_in, gdn_conv_w=gdn_conv_w, gdn_a_log=gdn_a_log, gdn_dt_bias=gdn_dt_bias,
             gdn_norm_g=gdn_norm_g, s5_lambda_re=s5_lambda_re, s5_lambda_im=s5_lambda_im, s5_b_re=s5_b_re,
             s5_b_im=s5_b_im, s5_c_re=s5_c_re, s5_c_im=s5_c_im, s5_d=s5_d, s5_log_step=s5_log_step,
             s5_glu_w=s5_glu_w, s5_glu_b=s5_glu_b, diff_lq1=diff_lq1, diff_lk1=diff_lk1, diff_lq2=diff_lq2,
             diff_lk2=diff_lk2, diff_norm_g=diff_norm_g, w_out=w_out, norm2_g=norm2_g, peer_wq=peer_wq,
             peer_keys=peer_keys, peer_u=peer_u, peer_v=peer_v, final_g=final_g)
    h = x
    for layer in range(DEPTH):
        h2 = _mixers(h, layer, p)
        h = _peer(h2, layer, p, final_norm=(layer == DEPTH - 1)).reshape(x.shape)
    return h
```

```python
import functools
import math

import jax
import jax.numpy as jnp
from jax import lax
from jax.experimental import pallas as pl
from jax.experimental.pallas import tpu as pltpu

F32 = jnp.float32
BF16 = jnp.bfloat16
I32 = jnp.int32
U32 = jnp.uint32

D_MODEL = 1024
DEPTH = 2
GDN_HEADS = 4
GDN_DK = 64
GDN_CHUNK = 64
GDN_W = GDN_HEADS * GDN_DK
GDN_QKV_W = 3 * GDN_W
GDN_CONV = 4
S5_WIDTH = 256
S5_GROUPS = 16
S5_GROUP_CH = 16
S5_STATE = 64
S5_NS = S5_GROUPS * S5_STATE
S5_SUB = 16
DIFF_HEADS = 4
DIFF_DQK = 64
DIFF_DV = 128
DIFF_W = 512
PEER_HEADS = 8
PEER_DHALF = 64
N_KEYS = 128
N_EXPERTS = N_KEYS * N_KEYS
PEER_TOPK = 16
PEER_PAIRS = PEER_HEADS * PEER_TOPK
NORM_EPS = 1e-6
NEG_INF = float("-inf")

VMEM_LIMIT_BYTES = 56 * 1024 * 1024


def _cparams(sem, vmem=VMEM_LIMIT_BYTES):
    return pltpu.CompilerParams(dimension_semantics=sem, vmem_limit_bytes=vmem)


def _dot(a, b):
    return jnp.dot(a, b, preferred_element_type=F32)


def _dot_nt(a, b):
    return lax.dot_general(a, b, (((1,), (1,)), ((), ())), preferred_element_type=F32)


def _dot_tn(a, b):
    return lax.dot_general(a, b, (((0,), (0,)), ((), ())), preferred_element_type=F32)


def _split(x):
    hi = x.astype(BF16)
    lo = (x - hi.astype(F32)).astype(BF16)
    return hi, lo


def _dot_sel_r(x, sel):
    hi, lo = _split(x)
    return _dot(hi, sel) + _dot(lo, sel)


def _dot_sel_l(sel, x):
    hi, lo = _split(x)
    return _dot(sel, hi) + _dot(sel, lo)


def _mm3(a, b):
    ah, al = _split(a)
    bh, bl = _split(b)
    return _dot(ah, bh) + (_dot(ah, bl) + _dot(al, bh))


def _sigmoid(x):
    return 1.0 / (1.0 + jnp.exp(-x))


def _softplus(x):
    return jnp.maximum(x, 0.0) + jnp.log1p(jnp.exp(-jnp.abs(x)))


def _gelu_tanh(x):
    c = math.sqrt(2.0 / math.pi)
    return 0.5 * x * (1.0 + jnp.tanh(c * (x + 0.044715 * (x * x * x))))


INPROJ_GDN_W = GDN_QKV_W + 3 * GDN_W


def _inproj_kernel(x_ref, g_ref, w_ref, gdn_ref, us_ref, q_ref, k_ref, v_ref):
    x = x_ref[...]
    ms = jnp.mean(x * x, axis=-1, keepdims=True)
    xn = (x * lax.rsqrt(ms + NORM_EPS) * g_ref[...]).astype(BF16)
    o = INPROJ_GDN_W
    gdn_ref[...] = _dot(xn, w_ref[:, 0:o])
    us_ref[...] = _dot(xn, w_ref[:, o:o + S5_WIDTH])
    o += S5_WIDTH
    q_ref[...] = (_dot(xn, w_ref[:, o:o + DIFF_W]) * (DIFF_DQK ** -0.5)).astype(BF16)
    k_ref[...] = _dot(xn, w_ref[:, o + DIFF_W:o + 2 * DIFF_W]).astype(BF16)
    v_ref[...] = _dot(xn, w_ref[:, o + 2 * DIFF_W:o + 3 * DIFF_W]).astype(BF16)


def _inproj(x2, gain, w, tm=512):
    t = x2.shape[0]
    nw = w.shape[1]
    row = lambda i: (i, 0)
    fixed = lambda i: (0, 0)
    return pl.pallas_call(
        _inproj_kernel,
        grid=(t // tm,),
        in_specs=[pl.BlockSpec((tm, D_MODEL), row), pl.BlockSpec((1, D_MODEL), fixed),
                  pl.BlockSpec((D_MODEL, nw), fixed)],
        out_specs=[pl.BlockSpec((tm, INPROJ_GDN_W), row), pl.BlockSpec((tm, S5_WIDTH), row),
                   pl.BlockSpec((tm, DIFF_W), row), pl.BlockSpec((tm, DIFF_W), row),
                   pl.BlockSpec((tm, DIFF_W), row)],
        out_shape=[jax.ShapeDtypeStruct((t, INPROJ_GDN_W), F32), jax.ShapeDtypeStruct((t, S5_WIDTH), F32),
                   jax.ShapeDtypeStruct((t, DIFF_W), BF16), jax.ShapeDtypeStruct((t, DIFF_W), BF16),
                   jax.ShapeDtypeStruct((t, DIFF_W), BF16)],
        compiler_params=_cparams(("parallel",)),
        name="inproj",
    )(x2, gain, w)


def _gdn_kernel(blk_ref, convw_ref, alog_ref, dtb_ref, ng_ref, out_ref,
                s_ref, tail_ref, xp_ref, q_s, k_s, v_s, b_s, g_s, *, ct):
    c64 = GDN_CHUNK

    @pl.when(pl.program_id(1) == 0)
    def _():
        s_ref[...] = jnp.zeros_like(s_ref)
        tail_ref[...] = jnp.zeros_like(tail_ref)

    qkv = blk_ref[0, :, 0:GDN_QKV_W]
    xp_ref[0:8, :] = tail_ref[...]
    xp_ref[8:8 + ct, :] = qkv
    tail_ref[...] = qkv[ct - 8:ct, :]
    cw = convw_ref[...]
    y = cw[0:1, :] * xp_ref[5:5 + ct, :]
    for j in range(1, GDN_CONV):
        y = y + cw[j:j + 1, :] * xp_ref[5 + j:5 + j + ct, :]
    y = y * _sigmoid(y)

    ri = lax.broadcasted_iota(I32, (GDN_W, GDN_W), 0)
    ci = lax.broadcasted_iota(I32, (GDN_W, GDN_W), 1)
    head_ones = jnp.where((ri // c64) == (ci // c64), 1.0, 0.0).astype(BF16)

    q = y[:, 0:GDN_W]
    k = y[:, GDN_W:2 * GDN_W]
    q_s[...] = q * lax.rsqrt(_dot_sel_r(q * q, head_ones) + 1e-6) * (GDN_DK ** -0.5)
    k_s[...] = k * lax.rsqrt(_dot_sel_r(k * k, head_ones) + 1e-6)
    v_s[...] = y[:, 2 * GDN_W:3 * GDN_W]
    a_rep = blk_ref[0, :, GDN_QKV_W + GDN_W:GDN_QKV_W + 2 * GDN_W]
    b_rep = blk_ref[0, :, GDN_QKV_W + 2 * GDN_W:GDN_QKV_W + 3 * GDN_W]
    b_s[...] = _sigmoid(b_rep)
    g_raw = -jnp.exp(alog_ref[...]) * _softplus(a_rep + dtb_ref[...])
    rt = lax.broadcasted_iota(I32, (ct, ct), 0)
    ctk = lax.broadcasted_iota(I32, (ct, ct), 1)
    tri_bd = jnp.where(((rt // c64) == (ctk // c64)) & (ctk <= rt), 1.0, 0.0).astype(BF16)
    g_s[...] = _dot_sel_l(tri_bd, g_raw)

    r64 = lax.broadcasted_iota(I32, (c64, c64), 0)
    col64 = lax.broadcasted_iota(I32, (c64, c64), 1)
    incl = col64 <= r64
    strict = col64 < r64
    eye_b = col64 == r64
    eye_f = jnp.where(eye_b, 1.0, 0.0)
    ones64 = jnp.ones((c64, c64), BF16)
    ng = ng_ref[...]

    n_chunks = ct // c64
    heads = range(GDN_HEADS)
    chains = [(slice(c * c64, (c + 1) * c64), slice(h * c64, (h + 1) * c64))
              for c in range(n_chunks) for h in heads]
    each = lambda f, *ls: [f(*xs) for xs in zip(*ls)]
    gi = [g_s[r, l] for r, l in chains]
    gj = each(lambda g: _dot_sel_l(ones64, jnp.where(eye_b, g, 0.0)), gi)
    dec = each(lambda a, b: jnp.exp(jnp.where(incl, a - b, NEG_INF)), gi, gj)
    kh = [k_s[r, l] for r, l in chains]
    bi = [b_s[r, l] for r, l in chains]
    kb = each(lambda x: x.astype(BF16), kh)
    kk = each(_dot_nt, kb, kb)
    lm = each(lambda b, x, d: jnp.where(strict, b * x * d, 0.0), bi, kk, dec)
    tinv = each(lambda x: eye_f - x, lm)
    pw = lm
    for _ in range(5):
        pw = each(_mm3, pw, pw)
        tinv = each(lambda t, m: t + _mm3(t, m), tinv, pw)
    tb = each(lambda x: x.astype(BF16), tinv)
    eg = each(jnp.exp, gi)
    u = each(lambda t, rl, b: _dot(t, (v_s[rl[0], rl[1]] * b).astype(BF16)), tb, chains, bi)
    w = each(lambda t, k, b, e: _dot(t, (k * (b * e)).astype(BF16)).astype(BF16), tb, kh, bi, eg)
    qh = [q_s[r, l] for r, l in chains]
    qk = each(lambda q, k, d: jnp.where(incl, _dot_nt(q.astype(BF16), k) * d, 0.0).astype(BF16), qh, kb, dec)
    qg = each(lambda q, e: (q * e).astype(BF16), qh, eg)
    g_last = each(lambda g: g[c64 - 1:c64, :], gi)
    kg = each(lambda k, gl, g: (k * jnp.exp(gl - g)).astype(BF16), kh, g_last, gi)
    decay_last = each(jnp.exp, g_last)

    states = [s_ref[h] for h in heads]
    for c in range(n_chunks):
        ids = [c * GDN_HEADS + h for h in heads]
        sb = [s.astype(BF16) for s in states]
        vnb = [(u[n] - _dot(w[n], sb[h])).astype(BF16) for h, n in zip(heads, ids)]
        o = [_dot(qg[n], sb[h]) + _dot(qk[n], vnb[h]) for h, n in zip(heads, ids)]
        states = [states[h] * decay_last[n] + _dot_tn(kg[n], vnb[h]) for h, n in zip(heads, ids)]
        ms = [_dot_sel_r(x * x, ones64) * (1.0 / c64) for x in o]
        outs = [x * lax.rsqrt(m + NORM_EPS) * ng for x, m in zip(o, ms)]
        rows = slice(c * c64, (c + 1) * c64)
        z = blk_ref[0, rows, GDN_QKV_W:GDN_QKV_W + GDN_W]
        out_ref[0, rows, :] = (jnp.concatenate(outs, axis=1) * (z * _sigmoid(z))).astype(out_ref.dtype)
    for h in heads:
        s_ref[h] = states[h]


def _gdn(gdn_in, conv_w, a_log_rep, dtb_rep, ng, ct=256):
    bn, seq, _ = gdn_in.shape
    fixed = lambda b, l: (0, 0)
    return pl.pallas_call(
        functools.partial(_gdn_kernel, ct=ct),
        grid=(bn, seq // ct),
        in_specs=[pl.BlockSpec((1, ct, INPROJ_GDN_W), lambda b, l: (b, l, 0)),
                  pl.BlockSpec((GDN_CONV, GDN_QKV_W), fixed), pl.BlockSpec((1, GDN_W), fixed),
                  pl.BlockSpec((1, GDN_W), fixed), pl.BlockSpec((1, GDN_DK), fixed)],
        out_specs=pl.BlockSpec((1, ct, GDN_W), lambda b, l: (b, l, 0)),
        out_shape=jax.ShapeDtypeStruct((bn, seq, GDN_W), BF16),
        scratch_shapes=[pltpu.VMEM((GDN_HEADS, GDN_DK, GDN_DK), F32), pltpu.VMEM((8, GDN_QKV_W), F32),
                        pltpu.VMEM((ct + 8, GDN_QKV_W), F32)] + [pltpu.VMEM((ct, GDN_W), F32)] * 5,
        compiler_params=_cparams(("arbitrary", "arbitrary")),
        name="gdn",
    )(gdn_in, conv_w, a_log_rep, dtb_rep, ng)


def _s5_kernel(u_ref, lre_ref, lim_ref, lstep_ref, bre_ref, bim_ref, cre_ref, cim_ref, d_ref, y_ref,
               bmat, cmat, avec, carry, zr_s, zi_s, xr_s, xi_s, *, rb):
    ns = S5_NS
    cw = S5_WIDTH

    @pl.when(pl.program_id(1) == 0)
    def _():
        step = jnp.exp(lstep_ref[...])
        lr = lre_ref[...]
        li = lim_ref[...]
        mag = jnp.exp(lr * step)
        ar = mag * jnp.cos(li * step)
        ai = mag * jnp.sin(li * step)
        den = lr * lr + li * li
        mr = ((ar - 1.0) * lr + ai * li) / den
        mi = (ai * lr - (ar - 1.0) * li) / den
        bre = bre_ref[...]
        bim = bim_ref[...]
        bmat[:, 0:ns] = (mr * bre - mi * bim).astype(BF16)
        bmat[:, ns:2 * ns] = (mr * bim + mi * bre).astype(BF16)
        cmat[0:ns, :] = cre_ref[...].astype(BF16)
        cmat[ns:2 * ns, :] = (-cim_ref[...]).astype(BF16)
        avec[0:1, :] = ar
        avec[1:2, :] = ai
        pr, pi = ar, ai
        for _ in range(4):
            pr, pi = pr * pr - pi * pi, 2.0 * pr * pi
        avec[2:3, :] = pr
        avec[3:4, :] = pi
        carry[...] = jnp.zeros_like(carry)

    ar = avec[0:1, :]
    ai = avec[1:2, :]
    a16r = avec[2:3, :]
    a16i = avec[3:4, :]

    def inject(s):
        ub = u_ref[0, :, s * cw:(s + 1) * cw].astype(BF16)
        return _dot(ub, bmat[...])

    xr = jnp.zeros((rb, ns), F32)
    xi = jnp.zeros((rb, ns), F32)
    for s in range(S5_SUB):
        p = inject(s)
        xr, xi = ar * xr - ai * xi + p[:, 0:ns], ar * xi + ai * xr + p[:, ns:2 * ns]
    zr_s[...] = xr
    zi_s[...] = xi

    def row_step(kk, c):
        cr, ci_ = c
        xr_s[pl.ds(kk, 1), :] = cr
        xi_s[pl.ds(kk, 1), :] = ci_
        zr = zr_s[pl.ds(kk, 1), :]
        zi = zi_s[pl.ds(kk, 1), :]
        return (a16r * cr - a16i * ci_ + zr, a16r * ci_ + a16i * cr + zi)

    cr, ci_ = lax.fori_loop(0, rb, row_step, (carry[0:1, :], carry[1:2, :]))
    carry[0:1, :] = cr
    carry[1:2, :] = ci_

    xr = xr_s[...]
    xi = xi_s[...]
    dsk = d_ref[...]
    for s in range(S5_SUB):
        p = inject(s)
        xr, xi = ar * xr - ai * xi + p[:, 0:ns], ar * xi + ai * xr + p[:, ns:2 * ns]
        yv = _dot(xr.astype(BF16), cmat[0:ns, :]) + _dot(xi.astype(BF16), cmat[ns:2 * ns, :])
        yv = yv + dsk * u_ref[0, :, s * cw:(s + 1) * cw]
        y_ref[0, :, s * cw:(s + 1) * cw] = _gelu_tanh(yv).astype(y_ref.dtype)


def _s5(u_rows, lre, lim, lstep, bre_bd, bim_bd, cre_bd, cim_bd, dskip, rb=128):
    bn, nrows, rw = u_rows.shape
    fixed = lambda b, r: (0, 0)
    ns = S5_NS
    return pl.pallas_call(
        functools.partial(_s5_kernel, rb=rb),
        grid=(bn, nrows // rb),
        in_specs=[pl.BlockSpec((1, rb, rw), lambda b, r: (b, r, 0)),
                  pl.BlockSpec((1, ns), fixed), pl.BlockSpec((1, ns), fixed), pl.BlockSpec((1, ns), fixed),
                  pl.BlockSpec((S5_WIDTH, ns), fixed), pl.BlockSpec((S5_WIDTH, ns), fixed),
                  pl.BlockSpec((ns, S5_WIDTH), fixed), pl.BlockSpec((ns, S5_WIDTH), fixed),
                  pl.BlockSpec((1, S5_WIDTH), fixed)],
        out_specs=pl.BlockSpec((1, rb, rw), lambda b, r: (b, r, 0)),
        out_shape=jax.ShapeDtypeStruct((bn, nrows, rw), BF16),
        scratch_shapes=[pltpu.VMEM((S5_WIDTH, 2 * ns), BF16), pltpu.VMEM((2 * ns, S5_WIDTH), BF16),
                        pltpu.VMEM((8, ns), F32), pltpu.VMEM((8, ns), F32)]
                       + [pltpu.VMEM((rb, ns), F32)] * 4,
        compiler_params=_cparams(("arbitrary", "arbitrary")),
        name="s5",
    )(u_rows, lre, lim, lstep, bre_bd, bim_bd, cre_bd, cim_bd, dskip)


def _attn_kernel(q_ref, k_ref, v_ref, lq1_ref, lk1_ref, lq2_ref, lk2_ref, ng_ref, o_ref,
                 qs_s, m_s, l_s, acc_s, *, blk, lam_init):
    i = pl.program_id(2)
    q = q_ref[0]
    lane = lax.broadcasted_iota(I32, q.shape, 1)
    zero = jnp.zeros_like(q)
    qs_s[0:blk, :] = jnp.where(lane < DIFF_DQK, q, zero)
    qs_s[blk:2 * blk, :] = jnp.where(lane >= DIFF_DQK, q, zero)
    m_s[...] = jnp.full_like(m_s, NEG_INF)
    l_s[...] = jnp.zeros_like(l_s)
    acc_s[...] = jnp.zeros_like(acc_s)
    ones = jnp.ones((blk, DIFF_DV), BF16)

    def block_rows(j):
        return pl.ds(pl.multiple_of(j * blk, blk), blk)

    def scores(j):
        return _dot_nt(qs_s[...], k_ref[0, block_rows(j), :])

    def update(j, s):
        m_old = m_s[...]
        m_new = jnp.maximum(m_old, jnp.max(s, axis=-1, keepdims=True))
        p = jnp.exp(s - m_new[:, 0:1]).astype(BF16)
        alpha = jnp.exp(m_old - m_new)
        pv = _dot(p, jnp.concatenate([v_ref[0, block_rows(j), :], ones], axis=1))
        acc_s[...] = alpha * acc_s[...] + pv[:, 0:DIFF_DV]
        l_s[...] = alpha * l_s[...] + pv[:, DIFF_DV:2 * DIFF_DV]
        m_s[...] = m_new

    def body(j, s):
        s_next = scores(j + 1)
        update(j, s)
        return s_next

    s = lax.fori_loop(0, i, body, scores(0))
    row = lax.broadcasted_iota(I32, s.shape, 0) & (blk - 1)
    col = lax.broadcasted_iota(I32, s.shape, 1)
    update(i, jnp.where(col <= row, s, NEG_INF))
    lam = (jnp.exp(jnp.sum(lq1_ref[...] * lk1_ref[...], axis=-1, keepdims=True))
           - jnp.exp(jnp.sum(lq2_ref[...] * lk2_ref[...], axis=-1, keepdims=True)) + lam_init)
    o = acc_s[0:blk, :] / l_s[0:blk, :] - lam * (acc_s[blk:2 * blk, :] / l_s[blk:2 * blk, :])
    ms = jnp.mean(o * o, axis=-1, keepdims=True)
    o = o * lax.rsqrt(ms + NORM_EPS) * ng_ref[...] * (1.0 - lam_init)
    o_ref[0] = o.astype(o_ref.dtype)


def _attn(q, k, v, lq1, lk1, lq2, lk2, ng, lam_init, blk=512):
    bn, seq, _ = q.shape
    blk = min(blk, seq)
    fixed = lambda b, h, i: (0, 0)
    kv_spec = pl.BlockSpec((1, seq, DIFF_DV), lambda b, h, i: (b, 0, h))
    return pl.pallas_call(
        functools.partial(_attn_kernel, blk=blk, lam_init=lam_init),
        grid=(bn, DIFF_HEADS, seq // blk),
        in_specs=[pl.BlockSpec((1, blk, DIFF_DV), lambda b, h, i: (b, i, h)), kv_spec, kv_spec]
                 + [pl.BlockSpec((1, DIFF_DQK), fixed)] * 4 + [pl.BlockSpec((1, DIFF_DV), fixed)],
        out_specs=pl.BlockSpec((1, blk, DIFF_DV), lambda b, h, i: (b, i, h)),
        out_shape=jax.ShapeDtypeStruct((bn, seq, DIFF_W), BF16),
        scratch_shapes=[pltpu.VMEM((2 * blk, DIFF_DV), BF16), pltpu.VMEM((2 * blk, DIFF_DV), F32),
                        pltpu.VMEM((2 * blk, DIFF_DV), F32), pltpu.VMEM((2 * blk, DIFF_DV), F32)],
        compiler_params=_cparams(("parallel", "parallel", "arbitrary")),
        name="diff_attn",
    )(q, k, v, lq1, lk1, lq2, lk2, ng)


def _outproj_kernel(h_ref, oa_ref, ys_ref, oc_ref, gw_ref, gb_ref, wo_ref, out_ref):
    zg = _dot(ys_ref[...], gw_ref[...]) + gb_ref[...]
    ob = (zg[:, 0:S5_WIDTH] * _sigmoid(zg[:, S5_WIDTH:2 * S5_WIDTH])).astype(BF16)
    acc = _dot(oa_ref[...], wo_ref[0:GDN_W, :])
    acc = acc + _dot(ob, wo_ref[GDN_W:GDN_W + S5_WIDTH, :])
    acc = acc + _dot(oc_ref[...], wo_ref[GDN_W + S5_WIDTH:, :])
    out_ref[...] = h_ref[...] + acc


def _outproj(h2, oa, ys, oc, glu_w, glu_b, w_out, tm=512):
    t = h2.shape[0]
    row = lambda i: (i, 0)
    fixed = lambda i: (0, 0)
    return pl.pallas_call(
        _outproj_kernel,
        grid=(t // tm,),
        in_specs=[pl.BlockSpec((tm, D_MODEL), row), pl.BlockSpec((tm, GDN_W), row),
                  pl.BlockSpec((tm, S5_WIDTH), row), pl.BlockSpec((tm, DIFF_W), row),
                  pl.BlockSpec((S5_WIDTH, 2 * S5_WIDTH), fixed), pl.BlockSpec((1, 2 * S5_WIDTH), fixed),
                  pl.BlockSpec((D_MODEL, D_MODEL), fixed)],
        out_specs=pl.BlockSpec((tm, D_MODEL), row),
        out_shape=jax.ShapeDtypeStruct((t, D_MODEL), F32),
        compiler_params=_cparams(("parallel",)),
        name="outproj",
    )(h2, oa, ys, oc, glu_w, glu_b, w_out)


_BIG_ID = 1.0e9


def _top16(x, ids, payload):
    n = x.shape[1]
    r16 = lax.broadcasted_iota(I32, (PEER_TOPK, n), 0)
    vals = jnp.zeros((PEER_TOPK, n), F32)
    pays = jnp.zeros((PEER_TOPK, n), F32)
    for kk in range(PEER_TOPK):
        m = jnp.max(x, axis=0, keepdims=True)
        first = jnp.min(jnp.where(x == m, ids, _BIG_ID), axis=0, keepdims=True)
        hit = ids == first
        pay = jnp.max(jnp.where(hit, payload, -1.0), axis=0, keepdims=True)
        x = jnp.where(hit, NEG_INF, x)
        vals = jnp.where(r16 == kk, m, vals)
        pays = jnp.where(r16 == kk, pay, pays)
    return vals, pays


def _peer_route_kernel(h_ref, g_ref, wqt_ref, keys_ref, xn_ref, row_ref, par_ref, gate_ref,
                       qt_s, sv_s, si_s, *, tm):
    x = h_ref[...]
    ms = jnp.mean(x * x, axis=-1, keepdims=True)
    xn = x * lax.rsqrt(ms + NORM_EPS) * g_ref[...]
    xn_ref[...] = xn
    qt_s[...] = _dot_nt(wqt_ref[...], xn.astype(BF16)).astype(BF16)

    key_id = lax.broadcasted_iota(I32, (N_KEYS, tm), 0).astype(F32)

    def half_body(hp, carry):
        r0 = pl.multiple_of(hp * PEER_DHALF, PEER_DHALF)
        s = _dot(keys_ref[hp], qt_s[pl.ds(r0, PEER_DHALF), :])
        vals, ids = _top16(s, key_id, key_id)
        sv_s[hp] = vals
        si_s[hp] = ids
        return carry

    lax.fori_loop(0, 2 * PEER_HEADS, half_body, 0)

    i8 = lax.broadcasted_iota(I32, (8, tm), 0).astype(F32)

    def head_body(hd, carry):
        a0 = sv_s[2 * hd]
        a1 = sv_s[2 * hd + 1]
        e0 = si_s[2 * hd] * float(N_KEYS)
        e1 = si_s[2 * hd + 1]
        cs, es, fs = [], [], []
        for i in range(8):
            cs.append(a0[i:i + 1, :] + a1[0:8, :])
            es.append(e0[i:i + 1, :] + e1[0:8, :])
            fs.append(i8 + float(i * PEER_TOPK))
        cs.append(a0[0:1, :] + a1[8:16, :])
        es.append(e0[0:1, :] + e1[8:16, :])
        fs.append(i8 + 8.0)
        cs.append(a0[8:16, :] + a1[0:1, :])
        es.append(e0[8:16, :] + e1[0:1, :])
        fs.append((i8 + 8.0) * float(PEER_TOPK))
        top_s, experts = _top16(jnp.concatenate(cs, axis=0), jnp.concatenate(fs, axis=0),
                                jnp.concatenate(es, axis=0))
        ex = jnp.exp(top_s - jnp.max(top_s, axis=0, keepdims=True))
        rows = pl.ds(pl.multiple_of(hd * PEER_TOPK, PEER_TOPK), PEER_TOPK)
        gate_ref[rows, :] = ex / jnp.sum(ex, axis=0, keepdims=True)
        expert = experts.astype(I32)
        row_ref[rows, :] = lax.shift_right_logical(expert, 1) * 8
        par_ref[rows, :] = (expert & 1).astype(F32)
        return carry

    lax.fori_loop(0, PEER_HEADS, head_body, 0)


def _peer_route(h2, gain, wqt, keys, tm=256):
    t = h2.shape[0]
    row = lambda i: (i, 0)
    col = lambda i: (0, i)
    return pl.pallas_call(
        functools.partial(_peer_route_kernel, tm=tm),
        grid=(t // tm,),
        in_specs=[pl.BlockSpec((tm, D_MODEL), row), pl.BlockSpec((1, D_MODEL), lambda i: (0, 0)),
                  pl.BlockSpec((D_MODEL, D_MODEL), lambda i: (0, 0)),
                  pl.BlockSpec((2 * PEER_HEADS, N_KEYS, PEER_DHALF), lambda i: (0, 0, 0))],
        out_specs=[pl.BlockSpec((tm, D_MODEL), row), pl.BlockSpec((PEER_PAIRS, tm), col),
                   pl.BlockSpec((PEER_PAIRS, tm), col), pl.BlockSpec((PEER_PAIRS, tm), col)],
        out_shape=[jax.ShapeDtypeStruct((t, D_MODEL), F32), jax.ShapeDtypeStruct((PEER_PAIRS, t), I32),
                   jax.ShapeDtypeStruct((PEER_PAIRS, t), F32), jax.ShapeDtypeStruct((PEER_PAIRS, t), F32)],
        scratch_shapes=[pltpu.VMEM((D_MODEL, tm), BF16), pltpu.VMEM((2 * PEER_HEADS, PEER_TOPK, tm), F32),
                        pltpu.VMEM((2 * PEER_HEADS, PEER_TOPK, tm), F32)],
        compiler_params=_cparams(("parallel",)),
        name="peer_route",
    )(h2, gain, wqt, keys)


TAB_ROWS = N_EXPERTS // 2
HIGH_HALF = 0xFFFF0000


def _splat_rows(row):
    r = lax.broadcasted_iota(I32, (128, 128), 0)
    c = lax.broadcasted_iota(I32, (128, 128), 1)
    diag = jnp.where(r == c, jnp.broadcast_to(row, (128, 128)), 0.0).astype(BF16)
    return _dot(diag, jnp.ones((128, 128), BF16))


def _stage_shifts(par_ref, t, srep_s):
    shift = 16.0 - 16.0 * par_ref[pl.ds(t, 1), :]
    srep_s[...] = pltpu.bitcast(_splat_rows(shift).astype(I32), U32)


def _load_words(tab_ref, row_ref, t, p):
    return tab_ref[pl.ds(pl.multiple_of(row_ref[t, p], 8), 8), :]


def _load_expert(tab_ref, row_ref, srep_s, t, p):
    w = lax.shift_left(_load_words(tab_ref, row_ref, t, p), jnp.broadcast_to(srep_s[p:p + 1, :], (8, 128)))
    return pltpu.bitcast(w & jnp.uint32(HIGH_HALF), F32)


_TREE8_SUBLANE = (3, 7, 1, 5, 2, 6, 0, 4)


def _tree8(v, upper, bit2, bit1):
    c = []
    for a, b in ((v[0], v[1]), (v[2], v[3]), (v[4], v[5]), (v[6], v[7])):
        c.append(jnp.where(upper, b + pltpu.roll(b, 4, 0), a + pltpu.roll(a, 4, 0)))
    e = []
    for c1, c2 in ((c[0], c[1]), (c[2], c[3])):
        e.append(jnp.where(bit2, c1 + pltpu.roll(c1, 2, 0), c2 + pltpu.roll(c2, 6, 0)))
    return jnp.where(bit1, e[0] + pltpu.roll(e[0], 1, 0), e[1] + pltpu.roll(e[1], 7, 0))


PEER_U_TOKENS_PER_STEP = 2


def _peer_u_kernel(row_ref, par_ref, x_ref, tab_ref, d_ref, r_s, srep_s, *, tb):
    sub = lax.broadcasted_iota(I32, (8, 128), 0)
    lane = lax.broadcasted_iota(I32, (8, 128), 1)
    upper = sub >= 4
    bit2 = (sub & 2) != 0
    bit1 = (sub & 1) != 0
    groups = PEER_PAIRS // 8
    nt = PEER_U_TOKENS_PER_STEP

    def tok_body(tt, carry):
        for j in range(nt):
            _stage_shifts(par_ref, tt * nt + j, srep_s.at[j])
        for j in range(nt):
            t = tt * nt + j
            xv = x_ref[t]
            for g in range(groups):
                prods = []
                for m in range(8):
                    p = g * 8 + _TREE8_SUBLANE[m]
                    prods.append(_load_expert(tab_ref, row_ref, srep_s.at[j], t, p) * xv)
                r_s[t * groups + g] = _tree8(prods, upper, bit2, bit1)
        return carry

    lax.fori_loop(0, tb // nt, tok_body, 0)

    def red_body(t8, carry):
        dacc = jnp.zeros((8, 128), F32)
        for tl in range(8):
            for g in range(groups):
                d = jnp.sum(r_s[(t8 * 8 + tl) * groups + g], axis=1, keepdims=True)
                dacc = jnp.where(lane == tl * groups + g, d, dacc)
        d_ref[t8] = dacc
        return carry

    lax.fori_loop(0, tb // 8, red_body, 0)


def _tok_spec(tb, space=None):
    return pl.BlockSpec((tb, PEER_PAIRS), lambda i: (i, 0), memory_space=space)


def _table_spec():
    return pl.BlockSpec((TAB_ROWS * 8, 128), lambda i: (0, 0), pipeline_mode=pl.Buffered(1))


def _peer_u(rows, par, x3, tab, tb=128):
    t = x3.shape[0]
    return pl.pallas_call(
        functools.partial(_peer_u_kernel, tb=tb),
        grid=(t // tb,),
        in_specs=[_tok_spec(tb, pltpu.SMEM), _tok_spec(tb), pl.BlockSpec((tb, 8, 128), lambda i: (i, 0, 0)),
                  _table_spec()],
        out_specs=pl.BlockSpec((tb // 8, 8, 128), lambda i: (i, 0, 0)),
        out_shape=jax.ShapeDtypeStruct((t // 8, 8, 128), F32),
        scratch_shapes=[pltpu.VMEM((tb * (PEER_PAIRS // 8), 8, 128), F32),
                        pltpu.VMEM((PEER_U_TOKENS_PER_STEP, 128, 128), U32)],
        compiler_params=_cparams(("arbitrary",)),
        name="peer_u",
    )(rows, par, x3, tab)


def _peer_coef_kernel(d_ref, gate_ref, c_ref):
    c_ref[...] = gate_ref[...] * _gelu_tanh(d_ref[...])


def _peer_coef(d_tp, gate_tp, tm=2048):
    t = d_tp.shape[0]
    tm = min(tm, t)
    return pl.pallas_call(
        _peer_coef_kernel,
        grid=(t // tm,),
        in_specs=[_tok_spec(tm), _tok_spec(tm)],
        out_specs=_tok_spec(tm),
        out_shape=jax.ShapeDtypeStruct((t, PEER_PAIRS), F32),
        compiler_params=_cparams(("parallel",)),
        name="peer_coef",
    )(d_tp, gate_tp)


PEER_V_TOKENS_PER_STEP = 2
PEER_V_KDIM = (PEER_PAIRS // 2) * 16


def _peer_v_kernel(row_ref, par_ref, coef_ref, h_ref, tab_ref, out_ref, cz_s, *, tb):
    half = PEER_PAIRS // 2
    kdim = PEER_V_KDIM
    nt = PEER_V_TOKENS_PER_STEP
    pk = lax.broadcasted_iota(I32, (PEER_PAIRS, kdim), 0)
    qk = lax.shift_right_logical(lax.broadcasted_iota(I32, (PEER_PAIRS, kdim), 1), 4)
    hk = (lax.broadcasted_iota(I32, (tb, kdim), 1) & 1).astype(F32)
    coef = coef_ref[...].astype(BF16)
    par = par_ref[...].astype(BF16)
    for hf in range(2):
        expand = jnp.where(pk == qk + hf * half, 1.0, 0.0).astype(BF16)
        cz_s[hf] = jnp.where(hk == _dot(par, expand), _dot(coef, expand), 0.0)
    ks = lax.broadcasted_iota(I32, (8, kdim), 0)
    k8 = lax.broadcasted_iota(I32, (8, kdim), 1)
    own_sublane = lax.shift_right_logical(k8 & 15, 1) == ks

    def tok_body(tt, carry):
        for j in range(nt):
            t = tt * nt + j
            pieces = []
            for q in range(half):
                wa = pltpu.bitcast(_load_words(tab_ref, row_ref, t, q), BF16)
                wb = pltpu.bitcast(_load_words(tab_ref, row_ref, t, half + q), BF16)
                pieces.append(jnp.concatenate([wa, wb], axis=1))
            g = jnp.concatenate(pieces, axis=0)
            ck = jnp.concatenate(
                [jnp.where(own_sublane, jnp.broadcast_to(cz_s[hf, pl.ds(t, 1), :], (8, kdim)), 0.0)
                 for hf in range(2)], axis=0).astype(BF16)
            o = _dot(ck, g)
            out_ref[t] = h_ref[t] + (o[0:8, 0:128] + o[8:16, 128:256])
        return carry

    lax.fori_loop(0, tb // nt, tok_body, 0)


def _peer_v(rows, par, coef, h3, tab, tb=128):
    t = h3.shape[0]
    return pl.pallas_call(
        functools.partial(_peer_v_kernel, tb=tb),
        grid=(t // tb,),
        in_specs=[_tok_spec(tb, pltpu.SMEM), _tok_spec(tb), _tok_spec(tb),
                  pl.BlockSpec((tb, 8, 128), lambda i: (i, 0, 0)), _table_spec()],
        out_specs=pl.BlockSpec((tb, 8, 128), lambda i: (i, 0, 0)),
        out_shape=jax.ShapeDtypeStruct((t, 8, 128), F32),
        scratch_shapes=[pltpu.VMEM((2, tb, PEER_V_KDIM), F32)],
        compiler_params=_cparams(("arbitrary",)),
        name="peer_v",
    )(rows, par, coef, h3, tab)


def _final_norm_kernel(x_ref, g_ref, o_ref):
    x = x_ref[...]
    ms = jnp.mean(x * x, axis=-1, keepdims=True)
    o_ref[...] = x * lax.rsqrt(ms + NORM_EPS) * g_ref[...]


def _final_norm(x2, gain, tm=1024):
    t = x2.shape[0]
    tm = min(tm, t)
    row = lambda i: (i, 0)
    return pl.pallas_call(
        _final_norm_kernel,
        grid=(t // tm,),
        in_specs=[pl.BlockSpec((tm, D_MODEL), row), pl.BlockSpec((1, D_MODEL), lambda i: (0, 0))],
        out_specs=pl.BlockSpec((tm, D_MODEL), row),
        out_shape=jax.ShapeDtypeStruct((t, D_MODEL), F32),
        compiler_params=_cparams(("parallel",)),
        name="final_norm",
    )(x2, gain)


def _pack_table(tab):
    bits = lax.bitcast_convert_type(tab.astype(BF16), jnp.uint16).astype(U32).reshape(TAB_ROWS, 2, 8, 128)
    return (bits[:, 0] | (bits[:, 1] << 16)).reshape(TAB_ROWS * 8, 128)


def _peer(h2, layer, p):
    t = h2.shape[0]
    keys = p["peer_keys"][layer].reshape(2 * PEER_HEADS, N_KEYS, PEER_DHALF).astype(BF16)
    xn, rows, par, gate = _peer_route(h2, p["norm2_g"][layer][None, :],
                                      p["peer_wq"][layer].T.astype(BF16), keys)
    rows, par, gate = rows.T, par.T, gate.T
    d = _peer_u(rows, par, xn.reshape(t, 8, 128), _pack_table(p["peer_u"][layer]))
    d_tp = d.reshape(t // 8, 8, 8, PEER_TOPK).transpose(0, 2, 3, 1).reshape(t, PEER_PAIRS)
    coef = _peer_coef(d_tp, gate)
    out = _peer_v(rows, par, coef, h2.reshape(t, 8, 128), _pack_table(p["peer_v"][layer]))
    return out.reshape(t, D_MODEL)


def _rep(x, n):
    return jnp.repeat(x, n, axis=-1)


def _inproj_weight(w_in_l):
    o = 0
    qkv = w_in_l[:, o:o + GDN_QKV_W]; o += GDN_QKV_W
    z = w_in_l[:, o:o + GDN_W]; o += GDN_W
    a = w_in_l[:, o:o + GDN_HEADS]; o += GDN_HEADS
    b = w_in_l[:, o:o + GDN_HEADS]; o += GDN_HEADS
    rest = w_in_l[:, o:]
    return jnp.concatenate([qkv, z, _rep(a, GDN_DK), _rep(b, GDN_DK), rest], axis=1).astype(BF16)


def _block_diag_gc(b_gnc):
    g = b_gnc.shape[0]
    eye = jnp.eye(g, dtype=b_gnc.dtype)
    t = jnp.swapaxes(b_gnc, 1, 2)
    return (t[:, :, None, :] * eye[:, None, :, None]).reshape(g * t.shape[1], g * t.shape[2])


def _mixers(h, layer, p):
    bn, seq, _ = h.shape
    t = bn * seq
    lam_init = 0.8 - 0.6 * math.exp(-0.3 * layer)
    gdn_in, us, qc, kc, vc = _inproj(h.reshape(t, D_MODEL), p["norm1_g"][layer][None, :],
                                     _inproj_weight(p["w_in"][layer]))
    a_log = _rep(p["gdn_a_log"][layer].astype(F32), GDN_DK)[None, :]
    dtb = _rep(p["gdn_dt_bias"][layer].astype(F32), GDN_DK)[None, :]
    o_a = _gdn(gdn_in.reshape(bn, seq, INPROJ_GDN_W), p["gdn_conv_w"][layer], a_log, dtb,
               p["gdn_norm_g"][layer][None, :])
    flat = lambda x: x.reshape(1, S5_NS)
    bre_bd = _block_diag_gc(p["s5_b_re"][layer])
    bim_bd = _block_diag_gc(p["s5_b_im"][layer])
    cre_bd = _block_diag_gc(p["s5_c_re"][layer])
    cim_bd = _block_diag_gc(p["s5_c_im"][layer])
    ys = _s5(us.reshape(bn, seq // S5_SUB, S5_SUB * S5_WIDTH),
             flat(p["s5_lambda_re"][layer]), flat(p["s5_lambda_im"][layer]),
             flat(_rep(p["s5_log_step"][layer][:, None], S5_STATE)),
             bre_bd, bim_bd, cre_bd, cim_bd, p["s5_d"][layer][None, :])
    row = lambda x: x[None, :]
    o_c = _attn(qc.reshape(bn, seq, DIFF_W), kc.reshape(bn, seq, DIFF_W), vc.reshape(bn, seq, DIFF_W),
                row(p["diff_lq1"][layer]), row(p["diff_lk1"][layer]), row(p["diff_lq2"][layer]),
                row(p["diff_lk2"][layer]), row(p["diff_norm_g"][layer]), lam_init)
    h2 = _outproj(h.reshape(t, D_MODEL), o_a.reshape(t, GDN_W), ys.reshape(t, S5_WIDTH),
                  o_c.reshape(t, DIFF_W), p["s5_glu_w"][layer].astype(BF16), p["s5_glu_b"][layer][None, :],
                  p["w_out"][layer].astype(BF16))
    return h2


def kernel(x, norm1_g, w_in, gdn_conv_w, gdn_a_log, gdn_dt_bias, gdn_norm_g, s5_lambda_re, s5_lambda_im, s5_b_re, s5_b_im, s5_c_re, s5_c_im, s5_d, s5_log_step, s5_glu_w, s5_glu_b, diff_lq1, diff_lk1, diff_lq2, diff_lk2, diff_norm_g, w_out, norm2_g, peer_wq, peer_keys, peer_u, peer_v, final_g):
    p = dict(norm1_g=norm1_g, w_in=w_in, gdn_conv_w=gdn_conv_w, gdn_a_log=gdn_a_log, gdn_dt_bias=gdn_dt_bias,
             gdn_norm_g=gdn_norm_g, s5_lambda_re=s5_lambda_re, s5_lambda_im=s5_lambda_im, s5_b_re=s5_b_re,
             s5_b_im=s5_b_im, s5_c_re=s5_c_re, s5_c_im=s5_c_im, s5_d=s5_d, s5_log_step=s5_log_step,
             s5_glu_w=s5_glu_w, s5_glu_b=s5_glu_b, diff_lq1=diff_lq1, diff_lk1=diff_lk1, diff_lq2=diff_lq2,
             diff_lk2=diff_lk2, diff_norm_g=diff_norm_g, w_out=w_out, norm2_g=norm2_g, peer_wq=peer_wq,
             peer_keys=peer_keys, peer_u=peer_u, peer_v=peer_v, final_g=final_g)
    h = x
    for layer in range(DEPTH):
        h2 = _mixers(h, layer, p)
        h = _peer(h2, layer, p).reshape(x.shape)
    return _final_norm(h.reshape(-1, D_MODEL), final_g[None, :]).reshape(x.shape)
```

```python
import functools
import math

import jax
import jax.numpy as jnp
from jax import lax
from jax.experimental import pallas as pl
from jax.experimental.pallas import tpu as pltpu

F32 = jnp.float32
BF16 = jnp.bfloat16
I32 = jnp.int32
U32 = jnp.uint32

D_MODEL = 1024
DEPTH = 2
GDN_HEADS = 4
GDN_DK = 64
GDN_CHUNK = 64
GDN_W = GDN_HEADS * GDN_DK
GDN_QKV_W = 3 * GDN_W
GDN_CONV = 4
S5_WIDTH = 256
S5_GROUPS = 16
S5_GROUP_CH = 16
S5_STATE = 64
S5_NS = S5_GROUPS * S5_STATE
S5_SUB = 16
DIFF_HEADS = 4
DIFF_DQK = 64
DIFF_DV = 128
DIFF_W = 512
PEER_HEADS = 8
PEER_DHALF = 64
N_KEYS = 128
N_EXPERTS = N_KEYS * N_KEYS
PEER_TOPK = 16
PEER_PAIRS = PEER_HEADS * PEER_TOPK
NORM_EPS = 1e-6
NEG_INF = float("-inf")

VMEM_LIMIT_BYTES = 56 * 1024 * 1024


def _cparams(sem, vmem=VMEM_LIMIT_BYTES):
    return pltpu.CompilerParams(dimension_semantics=sem, vmem_limit_bytes=vmem)


def _dot(a, b):
    return jnp.dot(a, b, preferred_element_type=F32)


def _dot_nt(a, b):
    return lax.dot_general(a, b, (((1,), (1,)), ((), ())), preferred_element_type=F32)


def _dot_tn(a, b):
    return lax.dot_general(a, b, (((0,), (0,)), ((), ())), preferred_element_type=F32)


def _split(x):
    hi = x.astype(BF16)
    lo = (x - hi.astype(F32)).astype(BF16)
    return hi, lo


def _dot_sel_r(x, sel):
    hi, lo = _split(x)
    return _dot(hi, sel) + _dot(lo, sel)


def _dot_sel_l(sel, x):
    hi, lo = _split(x)
    return _dot(sel, hi) + _dot(sel, lo)


def _mm3(a, b):
    ah, al = _split(a)
    bh, bl = _split(b)
    return _dot(ah, bh) + (_dot(ah, bl) + _dot(al, bh))


def _sigmoid(x):
    return 1.0 / (1.0 + jnp.exp(-x))


def _softplus(x):
    return jnp.maximum(x, 0.0) + jnp.log1p(jnp.exp(-jnp.abs(x)))


def _gelu_tanh(x):
    c = math.sqrt(2.0 / math.pi)
    return 0.5 * x * (1.0 + jnp.tanh(c * (x + 0.044715 * (x * x * x))))


INPROJ_GDN_W = GDN_QKV_W + 3 * GDN_W


def _inproj_kernel(x_ref, g_ref, w_ref, gdn_ref, us_ref, q_ref, k_ref, v_ref):
    x = x_ref[...]
    ms = jnp.mean(x * x, axis=-1, keepdims=True)
    xn = (x * lax.rsqrt(ms + NORM_EPS) * g_ref[...]).astype(BF16)
    o = INPROJ_GDN_W
    gdn_ref[...] = _dot(xn, w_ref[:, 0:o])
    us_ref[...] = _dot(xn, w_ref[:, o:o + S5_WIDTH])
    o += S5_WIDTH
    q_ref[...] = (_dot(xn, w_ref[:, o:o + DIFF_W]) * (DIFF_DQK ** -0.5)).astype(BF16)
    k_ref[...] = _dot(xn, w_ref[:, o + DIFF_W:o + 2 * DIFF_W]).astype(BF16)
    v_ref[...] = _dot(xn, w_ref[:, o + 2 * DIFF_W:o + 3 * DIFF_W]).astype(BF16)


def _inproj(x2, gain, w, tm=512):
    t = x2.shape[0]
    nw = w.shape[1]
    row = lambda i: (i, 0)
    fixed = lambda i: (0, 0)
    return pl.pallas_call(
        _inproj_kernel,
        grid=(t // tm,),
        in_specs=[pl.BlockSpec((tm, D_MODEL), row), pl.BlockSpec((1, D_MODEL), fixed),
                  pl.BlockSpec((D_MODEL, nw), fixed)],
        out_specs=[pl.BlockSpec((tm, INPROJ_GDN_W), row), pl.BlockSpec((tm, S5_WIDTH), row),
                   pl.BlockSpec((tm, DIFF_W), row), pl.BlockSpec((tm, DIFF_W), row),
                   pl.BlockSpec((tm, DIFF_W), row)],
        out_shape=[jax.ShapeDtypeStruct((t, INPROJ_GDN_W), F32), jax.ShapeDtypeStruct((t, S5_WIDTH), F32),
                   jax.ShapeDtypeStruct((t, DIFF_W), BF16), jax.ShapeDtypeStruct((t, DIFF_W), BF16),
                   jax.ShapeDtypeStruct((t, DIFF_W), BF16)],
        compiler_params=_cparams(("parallel",)),
        name="inproj",
    )(x2, gain, w)


def _gdn_kernel(blk_ref, convw_ref, alog_ref, dtb_ref, ng_ref, out_ref,
                s_ref, tail_ref, xp_ref, q_s, k_s, v_s, b_s, g_s, *, ct):
    c64 = GDN_CHUNK

    @pl.when(pl.program_id(1) == 0)
    def _():
        s_ref[...] = jnp.zeros_like(s_ref)
        tail_ref[...] = jnp.zeros_like(tail_ref)

    qkv = blk_ref[0, :, 0:GDN_QKV_W]
    xp_ref[0:8, :] = tail_ref[...]
    xp_ref[8:8 + ct, :] = qkv
    tail_ref[...] = qkv[ct - 8:ct, :]
    cw = convw_ref[...]
    y = cw[0:1, :] * xp_ref[5:5 + ct, :]
    for j in range(1, GDN_CONV):
        y = y + cw[j:j + 1, :] * xp_ref[5 + j:5 + j + ct, :]
    y = y * _sigmoid(y)

    ri = lax.broadcasted_iota(I32, (GDN_W, GDN_W), 0)
    ci = lax.broadcasted_iota(I32, (GDN_W, GDN_W), 1)
    head_ones = jnp.where((ri // c64) == (ci // c64), 1.0, 0.0).astype(BF16)

    q = y[:, 0:GDN_W]
    k = y[:, GDN_W:2 * GDN_W]
    q_s[...] = q * lax.rsqrt(_dot_sel_r(q * q, head_ones) + 1e-6) * (GDN_DK ** -0.5)
    k_s[...] = k * lax.rsqrt(_dot_sel_r(k * k, head_ones) + 1e-6)
    v_s[...] = y[:, 2 * GDN_W:3 * GDN_W]
    a_rep = blk_ref[0, :, GDN_QKV_W + GDN_W:GDN_QKV_W + 2 * GDN_W]
    b_rep = blk_ref[0, :, GDN_QKV_W + 2 * GDN_W:GDN_QKV_W + 3 * GDN_W]
    b_s[...] = _sigmoid(b_rep)
    g_raw = -jnp.exp(alog_ref[...]) * _softplus(a_rep + dtb_ref[...])
    rt = lax.broadcasted_iota(I32, (ct, ct), 0)
    ctk = lax.broadcasted_iota(I32, (ct, ct), 1)
    tri_bd = jnp.where(((rt // c64) == (ctk // c64)) & (ctk <= rt), 1.0, 0.0).astype(BF16)
    g_s[...] = _dot_sel_l(tri_bd, g_raw)

    r64 = lax.broadcasted_iota(I32, (c64, c64), 0)
    col64 = lax.broadcasted_iota(I32, (c64, c64), 1)
    incl = col64 <= r64
    strict = col64 < r64
    eye_b = col64 == r64
    eye_f = jnp.where(eye_b, 1.0, 0.0)
    ones64 = jnp.ones((c64, c64), BF16)
    ng = ng_ref[...]

    n_chunks = ct // c64
    heads = range(GDN_HEADS)
    chains = [(slice(c * c64, (c + 1) * c64), slice(h * c64, (h + 1) * c64))
              for c in range(n_chunks) for h in heads]
    each = lambda f, *ls: [f(*xs) for xs in zip(*ls)]
    gi = [g_s[r, l] for r, l in chains]
    gj = each(lambda g: _dot_sel_l(ones64, jnp.where(eye_b, g, 0.0)), gi)
    dec = each(lambda a, b: jnp.exp(jnp.where(incl, a - b, NEG_INF)), gi, gj)
    kh = [k_s[r, l] for r, l in chains]
    bi = [b_s[r, l] for r, l in chains]
    kb = each(lambda x: x.astype(BF16), kh)
    kk = each(_dot_nt, kb, kb)
    lm = each(lambda b, x, d: jnp.where(strict, b * x * d, 0.0), bi, kk, dec)
    tinv = each(lambda x: eye_f - x, lm)
    pw = lm
    for _ in range(5):
        pw = each(_mm3, pw, pw)
        tinv = each(lambda t, m: t + _mm3(t, m), tinv, pw)
    tb = each(lambda x: x.astype(BF16), tinv)
    eg = each(jnp.exp, gi)
    u = each(lambda t, rl, b: _dot(t, (v_s[rl[0], rl[1]] * b).astype(BF16)), tb, chains, bi)
    w = each(lambda t, k, b, e: _dot(t, (k * (b * e)).astype(BF16)).astype(BF16), tb, kh, bi, eg)
    qh = [q_s[r, l] for r, l in chains]
    qk = each(lambda q, k, d: jnp.where(incl, _dot_nt(q.astype(BF16), k) * d, 0.0).astype(BF16), qh, kb, dec)
    qg = each(lambda q, e: (q * e).astype(BF16), qh, eg)
    g_last = each(lambda g: g[c64 - 1:c64, :], gi)
    kg = each(lambda k, gl, g: (k * jnp.exp(gl - g)).astype(BF16), kh, g_last, gi)
    decay_last = each(jnp.exp, g_last)

    states = [s_ref[h] for h in heads]
    for c in range(n_chunks):
        ids = [c * GDN_HEADS + h for h in heads]
        sb = [s.astype(BF16) for s in states]
        vnb = [(u[n] - _dot(w[n], sb[h])).astype(BF16) for h, n in zip(heads, ids)]
        o = [_dot(qg[n], sb[h]) + _dot(qk[n], vnb[h]) for h, n in zip(heads, ids)]
        states = [states[h] * decay_last[n] + _dot_tn(kg[n], vnb[h]) for h, n in zip(heads, ids)]
        ms = [_dot_sel_r(x * x, ones64) * (1.0 / c64) for x in o]
        outs = [x * lax.rsqrt(m + NORM_EPS) * ng for x, m in zip(o, ms)]
        rows = slice(c * c64, (c + 1) * c64)
        z = blk_ref[0, rows, GDN_QKV_W:GDN_QKV_W + GDN_W]
        out_ref[0, rows, :] = (jnp.concatenate(outs, axis=1) * (z * _sigmoid(z))).astype(out_ref.dtype)
    for h in heads:
        s_ref[h] = states[h]


def _gdn(gdn_in, conv_w, a_log_rep, dtb_rep, ng, ct=256):
    bn, seq, _ = gdn_in.shape
    fixed = lambda b, l: (0, 0)
    return pl.pallas_call(
        functools.partial(_gdn_kernel, ct=ct),
        grid=(bn, seq // ct),
        in_specs=[pl.BlockSpec((1, ct, INPROJ_GDN_W), lambda b, l: (b, l, 0)),
                  pl.BlockSpec((GDN_CONV, GDN_QKV_W), fixed), pl.BlockSpec((1, GDN_W), fixed),
                  pl.BlockSpec((1, GDN_W), fixed), pl.BlockSpec((1, GDN_DK), fixed)],
        out_specs=pl.BlockSpec((1, ct, GDN_W), lambda b, l: (b, l, 0)),
        out_shape=jax.ShapeDtypeStruct((bn, seq, GDN_W), BF16),
        scratch_shapes=[pltpu.VMEM((GDN_HEADS, GDN_DK, GDN_DK), F32), pltpu.VMEM((8, GDN_QKV_W), F32),
                        pltpu.VMEM((ct + 8, GDN_QKV_W), F32)] + [pltpu.VMEM((ct, GDN_W), F32)] * 5,
        compiler_params=_cparams(("arbitrary", "arbitrary")),
        name="gdn",
    )(gdn_in, conv_w, a_log_rep, dtb_rep, ng)


def _s5_kernel(u_ref, lre_ref, lim_ref, lstep_ref, bre_ref, bim_ref, cre_ref, cim_ref, d_ref, y_ref,
               bmat, cmat, avec, carry, zr_s, zi_s, xr_s, xi_s, *, rb):
    ns = S5_NS
    cw = S5_WIDTH

    @pl.when(pl.program_id(1) == 0)
    def _():
        step = jnp.exp(lstep_ref[...])
        lr = lre_ref[...]
        li = lim_ref[...]
        mag = jnp.exp(lr * step)
        ar = mag * jnp.cos(li * step)
        ai = mag * jnp.sin(li * step)
        den = lr * lr + li * li
        mr = ((ar - 1.0) * lr + ai * li) / den
        mi = (ai * lr - (ar - 1.0) * li) / den
        bre = bre_ref[...]
        bim = bim_ref[...]
        bmat[:, 0:ns] = (mr * bre - mi * bim).astype(BF16)
        bmat[:, ns:2 * ns] = (mr * bim + mi * bre).astype(BF16)
        cmat[0:ns, :] = cre_ref[...].astype(BF16)
        cmat[ns:2 * ns, :] = (-cim_ref[...]).astype(BF16)
        avec[0:1, :] = ar
        avec[1:2, :] = ai
        pr, pi = ar, ai
        for _ in range(4):
            pr, pi = pr * pr - pi * pi, 2.0 * pr * pi
        avec[2:3, :] = pr
        avec[3:4, :] = pi
        carry[...] = jnp.zeros_like(carry)

    ar = avec[0:1, :]
    ai = avec[1:2, :]
    a16r = avec[2:3, :]
    a16i = avec[3:4, :]

    def inject(s):
        ub = u_ref[0, :, s * cw:(s + 1) * cw].astype(BF16)
        return _dot(ub, bmat[...])

    xr = jnp.zeros((rb, ns), F32)
    xi = jnp.zeros((rb, ns), F32)
    for s in range(S5_SUB):
        p = inject(s)
        xr, xi = ar * xr - ai * xi + p[:, 0:ns], ar * xi + ai * xr + p[:, ns:2 * ns]
    zr_s[...] = xr
    zi_s[...] = xi

    def row_step(kk, c):
        cr, ci_ = c
        xr_s[pl.ds(kk, 1), :] = cr
        xi_s[pl.ds(kk, 1), :] = ci_
        zr = zr_s[pl.ds(kk, 1), :]
        zi = zi_s[pl.ds(kk, 1), :]
        return (a16r * cr - a16i * ci_ + zr, a16r * ci_ + a16i * cr + zi)

    cr, ci_ = lax.fori_loop(0, rb, row_step, (carry[0:1, :], carry[1:2, :]))
    carry[0:1, :] = cr
    carry[1:2, :] = ci_

    xr = xr_s[...]
    xi = xi_s[...]
    dsk = d_ref[...]
    for s in range(S5_SUB):
        p = inject(s)
        xr, xi = ar * xr - ai * xi + p[:, 0:ns], ar * xi + ai * xr + p[:, ns:2 * ns]
        yv = _dot(xr.astype(BF16), cmat[0:ns, :]) + _dot(xi.astype(BF16), cmat[ns:2 * ns, :])
        yv = yv + dsk * u_ref[0, :, s * cw:(s + 1) * cw]
        y_ref[0, :, s * cw:(s + 1) * cw] = _gelu_tanh(yv).astype(y_ref.dtype)


def _s5(u_rows, lre, lim, lstep, bre_bd, bim_bd, cre_bd, cim_bd, dskip, rb=128):
    bn, nrows, rw = u_rows.shape
    fixed = lambda b, r: (0, 0)
    ns = S5_NS
    return pl.pallas_call(
        functools.partial(_s5_kernel, rb=rb),
        grid=(bn, nrows // rb),
        in_specs=[pl.BlockSpec((1, rb, rw), lambda b, r: (b, r, 0)),
                  pl.BlockSpec((1, ns), fixed), pl.BlockSpec((1, ns), fixed), pl.BlockSpec((1, ns), fixed),
                  pl.BlockSpec((S5_WIDTH, ns), fixed), pl.BlockSpec((S5_WIDTH, ns), fixed),
                  pl.BlockSpec((ns, S5_WIDTH), fixed), pl.BlockSpec((ns, S5_WIDTH), fixed),
                  pl.BlockSpec((1, S5_WIDTH), fixed)],
        out_specs=pl.BlockSpec((1, rb, rw), lambda b, r: (b, r, 0)),
        out_shape=jax.ShapeDtypeStruct((bn, nrows, rw), BF16),
        scratch_shapes=[pltpu.VMEM((S5_WIDTH, 2 * ns), BF16), pltpu.VMEM((2 * ns, S5_WIDTH), BF16),
                        pltpu.VMEM((8, ns), F32), pltpu.VMEM((8, ns), F32)]
                       + [pltpu.VMEM((rb, ns), F32)] * 4,
        compiler_params=_cparams(("arbitrary", "arbitrary")),
        name="s5",
    )(u_rows, lre, lim, lstep, bre_bd, bim_bd, cre_bd, cim_bd, dskip)


def _attn_kernel(q_ref, k_ref, v_ref, lq1_ref, lk1_ref, lq2_ref, lk2_ref, ng_ref, o_ref,
                 qs_s, m_s, l_s, acc_s, *, blk, lam_init):
    i = pl.program_id(2)
    q = q_ref[0]
    lane = lax.broadcasted_iota(I32, q.shape, 1)
    zero = jnp.zeros_like(q)
    qs_s[0:blk, :] = jnp.where(lane < DIFF_DQK, q, zero)
    qs_s[blk:2 * blk, :] = jnp.where(lane >= DIFF_DQK, q, zero)
    m_s[...] = jnp.full_like(m_s, NEG_INF)
    l_s[...] = jnp.zeros_like(l_s)
    acc_s[...] = jnp.zeros_like(acc_s)
    ones = jnp.ones((blk, DIFF_DV), BF16)

    def block_rows(j):
        return pl.ds(pl.multiple_of(j * blk, blk), blk)

    def scores(j):
        return _dot_nt(qs_s[...], k_ref[0, block_rows(j), :])

    def update(j, s):
        m_old = m_s[...]
        m_new = jnp.maximum(m_old, jnp.max(s, axis=-1, keepdims=True))
        p = jnp.exp(s - m_new[:, 0:1]).astype(BF16)
        alpha = jnp.exp(m_old - m_new)
        pv = _dot(p, jnp.concatenate([v_ref[0, block_rows(j), :], ones], axis=1))
        acc_s[...] = alpha * acc_s[...] + pv[:, 0:DIFF_DV]
        l_s[...] = alpha * l_s[...] + pv[:, DIFF_DV:2 * DIFF_DV]
        m_s[...] = m_new

    def body(j, s):
        s_next = scores(j + 1)
        update(j, s)
        return s_next

    s = lax.fori_loop(0, i, body, scores(0))
    row = lax.broadcasted_iota(I32, s.shape, 0) & (blk - 1)
    col = lax.broadcasted_iota(I32, s.shape, 1)
    update(i, jnp.where(col <= row, s, NEG_INF))
    lam = (jnp.exp(jnp.sum(lq1_ref[...] * lk1_ref[...], axis=-1, keepdims=True))
           - jnp.exp(jnp.sum(lq2_ref[...] * lk2_ref[...], axis=-1, keepdims=True)) + lam_init)
    o = acc_s[0:blk, :] / l_s[0:blk, :] - lam * (acc_s[blk:2 * blk, :] / l_s[blk:2 * blk, :])
    ms = jnp.mean(o * o, axis=-1, keepdims=True)
    o = o * lax.rsqrt(ms + NORM_EPS) * ng_ref[...] * (1.0 - lam_init)
    o_ref[0] = o.astype(o_ref.dtype)


def _attn(q, k, v, lq1, lk1, lq2, lk2, ng, lam_init, blk=512):
    bn, seq, _ = q.shape
    blk = min(blk, seq)
    fixed = lambda b, h, i: (0, 0)
    kv_spec = pl.BlockSpec((1, seq, DIFF_DV), lambda b, h, i: (b, 0, h))
    return pl.pallas_call(
        functools.partial(_attn_kernel, blk=blk, lam_init=lam_init),
        grid=(bn, DIFF_HEADS, seq // blk),
        in_specs=[pl.BlockSpec((1, blk, DIFF_DV), lambda b, h, i: (b, i, h)), kv_spec, kv_spec]
                 + [pl.BlockSpec((1, DIFF_DQK), fixed)] * 4 + [pl.BlockSpec((1, DIFF_DV), fixed)],
        out_specs=pl.BlockSpec((1, blk, DIFF_DV), lambda b, h, i: (b, i, h)),
        out_shape=jax.ShapeDtypeStruct((bn, seq, DIFF_W), BF16),
        scratch_shapes=[pltpu.VMEM((2 * blk, DIFF_DV), BF16), pltpu.VMEM((2 * blk, DIFF_DV), F32),
                        pltpu.VMEM((2 * blk, DIFF_DV), F32), pltpu.VMEM((2 * blk, DIFF_DV), F32)],
        compiler_params=_cparams(("parallel", "parallel", "arbitrary")),
        name="diff_attn",
    )(q, k, v, lq1, lk1, lq2, lk2, ng)


def _outproj_kernel(h_ref, oa_ref, ys_ref, oc_ref, gw_ref, gb_ref, wo_ref, out_ref):
    zg = _dot(ys_ref[...], gw_ref[...]) + gb_ref[...]
    ob = (zg[:, 0:S5_WIDTH] * _sigmoid(zg[:, S5_WIDTH:2 * S5_WIDTH])).astype(BF16)
    acc = _dot(oa_ref[...], wo_ref[0:GDN_W, :])
    acc = acc + _dot(ob, wo_ref[GDN_W:GDN_W + S5_WIDTH, :])
    acc = acc + _dot(oc_ref[...], wo_ref[GDN_W + S5_WIDTH:, :])
    out_ref[...] = h_ref[...] + acc


def _outproj(h2, oa, ys, oc, glu_w, glu_b, w_out, tm=512):
    t = h2.shape[0]
    row = lambda i: (i, 0)
    fixed = lambda i: (0, 0)
    return pl.pallas_call(
        _outproj_kernel,
        grid=(t // tm,),
        in_specs=[pl.BlockSpec((tm, D_MODEL), row), pl.BlockSpec((tm, GDN_W), row),
                  pl.BlockSpec((tm, S5_WIDTH), row), pl.BlockSpec((tm, DIFF_W), row),
                  pl.BlockSpec((S5_WIDTH, 2 * S5_WIDTH), fixed), pl.BlockSpec((1, 2 * S5_WIDTH), fixed),
                  pl.BlockSpec((D_MODEL, D_MODEL), fixed)],
        out_specs=pl.BlockSpec((tm, D_MODEL), row),
        out_shape=jax.ShapeDtypeStruct((t, D_MODEL), F32),
        compiler_params=_cparams(("parallel",)),
        name="outproj",
    )(h2, oa, ys, oc, glu_w, glu_b, w_out)


_BIG_ID = 1.0e9


def _top16(x, ids, payload):
    n = x.shape[1]
    r16 = lax.broadcasted_iota(I32, (PEER_TOPK, n), 0)
    vals = jnp.zeros((PEER_TOPK, n), F32)
    pays = jnp.zeros((PEER_TOPK, n), F32)
    for kk in range(PEER_TOPK):
        m = jnp.max(x, axis=0, keepdims=True)
        first = jnp.min(jnp.where(x == m, ids, _BIG_ID), axis=0, keepdims=True)
        hit = ids == first
        pay = first if payload is None else jnp.max(jnp.where(hit, payload, -1.0), axis=0, keepdims=True)
        x = jnp.where(hit, NEG_INF, x)
        vals = jnp.where(r16 == kk, m, vals)
        pays = jnp.where(r16 == kk, pay, pays)
    return vals, pays


def _peer_route_kernel(h_ref, g_ref, wqt_ref, keys_ref, xn_ref, row_ref, par_ref, gate_ref,
                       qt_s, sv_s, si_s, *, tm):
    x = h_ref[...]
    ms = jnp.mean(x * x, axis=-1, keepdims=True)
    xn = x * lax.rsqrt(ms + NORM_EPS) * g_ref[...]
    xn_ref[...] = xn
    qt_s[...] = _dot_nt(wqt_ref[...], xn.astype(BF16)).astype(BF16)

    key_id = lax.broadcasted_iota(I32, (N_KEYS, tm), 0).astype(F32)

    def half_body(hp, carry):
        r0 = pl.multiple_of(hp * PEER_DHALF, PEER_DHALF)
        s = _dot(keys_ref[hp], qt_s[pl.ds(r0, PEER_DHALF), :])
        vals, ids = _top16(s, key_id, None)
        sv_s[hp] = vals
        si_s[hp] = ids
        return carry

    lax.fori_loop(0, 2 * PEER_HEADS, half_body, 0)

    i8 = lax.broadcasted_iota(I32, (8, tm), 0).astype(F32)

    def head_body(hd, carry):
        a0 = sv_s[2 * hd]
        a1 = sv_s[2 * hd + 1]
        e0 = si_s[2 * hd] * float(N_KEYS)
        e1 = si_s[2 * hd + 1]
        cs, es, fs = [], [], []
        for i in range(8):
            cs.append(a0[i:i + 1, :] + a1[0:8, :])
            es.append(e0[i:i + 1, :] + e1[0:8, :])
            fs.append(i8 + float(i * PEER_TOPK))
        cs.append(a0[0:1, :] + a1[8:16, :])
        es.append(e0[0:1, :] + e1[8:16, :])
        fs.append(i8 + 8.0)
        cs.append(a0[8:16, :] + a1[0:1, :])
        es.append(e0[8:16, :] + e1[0:1, :])
        fs.append((i8 + 8.0) * float(PEER_TOPK))
        top_s, experts = _top16(jnp.concatenate(cs, axis=0), jnp.concatenate(fs, axis=0),
                                jnp.concatenate(es, axis=0))
        ex = jnp.exp(top_s - jnp.max(top_s, axis=0, keepdims=True))
        rows = pl.ds(pl.multiple_of(hd * PEER_TOPK, PEER_TOPK), PEER_TOPK)
        gate_ref[rows, :] = ex / jnp.sum(ex, axis=0, keepdims=True)
        expert = experts.astype(I32)
        row_ref[rows, :] = lax.shift_right_logical(expert, 1) * 8
        par_ref[rows, :] = (expert & 1).astype(F32)
        return carry

    lax.fori_loop(0, PEER_HEADS, head_body, 0)


def _peer_route(h2, gain, wqt, keys, tm=256):
    t = h2.shape[0]
    row = lambda i: (i, 0)
    col = lambda i: (0, i)
    return pl.pallas_call(
        functools.partial(_peer_route_kernel, tm=tm),
        grid=(t // tm,),
        in_specs=[pl.BlockSpec((tm, D_MODEL), row), pl.BlockSpec((1, D_MODEL), lambda i: (0, 0)),
                  pl.BlockSpec((D_MODEL, D_MODEL), lambda i: (0, 0)),
                  pl.BlockSpec((2 * PEER_HEADS, N_KEYS, PEER_DHALF), lambda i: (0, 0, 0))],
        out_specs=[pl.BlockSpec((tm, D_MODEL), row), pl.BlockSpec((PEER_PAIRS, tm), col),
                   pl.BlockSpec((PEER_PAIRS, tm), col), pl.BlockSpec((PEER_PAIRS, tm), col)],
        out_shape=[jax.ShapeDtypeStruct((t, D_MODEL), F32), jax.ShapeDtypeStruct((PEER_PAIRS, t), I32),
                   jax.ShapeDtypeStruct((PEER_PAIRS, t), F32), jax.ShapeDtypeStruct((PEER_PAIRS, t), F32)],
        scratch_shapes=[pltpu.VMEM((D_MODEL, tm), BF16), pltpu.VMEM((2 * PEER_HEADS, PEER_TOPK, tm), F32),
                        pltpu.VMEM((2 * PEER_HEADS, PEER_TOPK, tm), F32)],
        compiler_params=_cparams(("parallel",)),
        name="peer_route",
    )(h2, gain, wqt, keys)


TAB_ROWS = N_EXPERTS // 2
HIGH_HALF = 0xFFFF0000


def _splat_rows(row):
    r = lax.broadcasted_iota(I32, (128, 128), 0)
    c = lax.broadcasted_iota(I32, (128, 128), 1)
    diag = jnp.where(r == c, jnp.broadcast_to(row, (128, 128)), 0.0).astype(BF16)
    return _dot(diag, jnp.ones((128, 128), BF16))


def _stage_shifts(par_ref, t, srep_s):
    shift = 16.0 - 16.0 * par_ref[pl.ds(t, 1), :]
    srep_s[...] = pltpu.bitcast(_splat_rows(shift).astype(I32), U32)


def _load_words(tab_ref, row_ref, t, p):
    return tab_ref[pl.ds(pl.multiple_of(row_ref[t, p], 8), 8), :]


def _load_expert(tab_ref, row_ref, srep_s, t, p):
    w = lax.shift_left(_load_words(tab_ref, row_ref, t, p), jnp.broadcast_to(srep_s[p:p + 1, :], (8, 128)))
    return pltpu.bitcast(w & jnp.uint32(HIGH_HALF), F32)


_TREE8_SUBLANE = (3, 7, 1, 5, 2, 6, 0, 4)


def _tree8(v, upper, bit2, bit1):
    c = []
    for a, b in ((v[0], v[1]), (v[2], v[3]), (v[4], v[5]), (v[6], v[7])):
        c.append(jnp.where(upper, b + pltpu.roll(b, 4, 0), a + pltpu.roll(a, 4, 0)))
    e = []
    for c1, c2 in ((c[0], c[1]), (c[2], c[3])):
        e.append(jnp.where(bit2, c1 + pltpu.roll(c1, 2, 0), c2 + pltpu.roll(c2, 6, 0)))
    return jnp.where(bit1, e[0] + pltpu.roll(e[0], 1, 0), e[1] + pltpu.roll(e[1], 7, 0))


PEER_U_TOKENS_PER_STEP = 2


def _peer_u_kernel(row_ref, par_ref, x_ref, tab_ref, d_ref, r_s, srep_a, srep_b, *, tb):
    sub = lax.broadcasted_iota(I32, (8, 128), 0)
    lane = lax.broadcasted_iota(I32, (8, 128), 1)
    upper = sub >= 4
    bit2 = (sub & 2) != 0
    bit1 = (sub & 1) != 0
    groups = PEER_PAIRS // 8
    nt = PEER_U_TOKENS_PER_STEP
    steps = tb // nt

    def stage(tt, srep_s):
        for j in range(nt):
            _stage_shifts(par_ref, tt * nt + j, srep_s.at[j])

    def run(tt, srep_s):
        for j in range(nt):
            t = tt * nt + j
            xv = x_ref[t]
            prod = lambda p: _load_expert(tab_ref, row_ref, srep_s.at[j], t, p) * xv
            for g in range(groups):
                r_s[t * groups + g] = _tree8([prod(g * 8 + _TREE8_SUBLANE[m]) for m in range(8)],
                                             upper, bit2, bit1)

    stage(0, srep_a)

    def two_steps(k, carry):
        stage(2 * k + 1, srep_b)
        run(2 * k, srep_a)
        stage(jnp.minimum(2 * k + 2, steps - 1), srep_a)
        run(2 * k + 1, srep_b)
        return carry

    lax.fori_loop(0, steps // 2, two_steps, 0)

    def red_body(t8, carry):
        dacc = jnp.zeros((8, 128), F32)
        for tl in range(8):
            for g in range(groups):
                d = jnp.sum(r_s[(t8 * 8 + tl) * groups + g], axis=1, keepdims=True)
                dacc = jnp.where(lane == tl * PEER_TOPK + g, d, dacc)
        d_ref[t8] = dacc
        return carry

    lax.fori_loop(0, tb // 8, red_body, 0)


def _tok_spec(tb, space=None):
    return pl.BlockSpec((tb, PEER_PAIRS), lambda i: (i, 0), memory_space=space)


def _table_spec():
    return pl.BlockSpec((TAB_ROWS * 8, 128), lambda i: (0, 0), pipeline_mode=pl.Buffered(1))


def _peer_u(rows, par, x3, tab, tb=128):
    t = x3.shape[0]
    return pl.pallas_call(
        functools.partial(_peer_u_kernel, tb=tb),
        grid=(t // tb,),
        in_specs=[_tok_spec(tb, pltpu.SMEM), _tok_spec(tb), pl.BlockSpec((tb, 8, 128), lambda i: (i, 0, 0)),
                  _table_spec()],
        out_specs=pl.BlockSpec((tb // 8, 8, 128), lambda i: (i, 0, 0)),
        out_shape=jax.ShapeDtypeStruct((t // 8, 8, 128), F32),
        scratch_shapes=[pltpu.VMEM((tb * (PEER_PAIRS // 8), 8, 128), F32),
                        pltpu.VMEM((PEER_U_TOKENS_PER_STEP, 128, 128), U32),
                        pltpu.VMEM((PEER_U_TOKENS_PER_STEP, 128, 128), U32)],
        compiler_params=_cparams(("arbitrary",)),
        name="peer_u",
    )(rows, par, x3, tab)


def _peer_coef_kernel(d_ref, gate_ref, c_ref):
    c_ref[...] = gate_ref[...] * _gelu_tanh(d_ref[...])


def _peer_coef(d_tp, gate_tp, tm=2048):
    t = d_tp.shape[0]
    tm = min(tm, t)
    return pl.pallas_call(
        _peer_coef_kernel,
        grid=(t // tm,),
        in_specs=[_tok_spec(tm), _tok_spec(tm)],
        out_specs=_tok_spec(tm),
        out_shape=jax.ShapeDtypeStruct((t, PEER_PAIRS), F32),
        compiler_params=_cparams(("parallel",)),
        name="peer_coef",
    )(d_tp, gate_tp)


PEER_V_TOKENS_PER_STEP = 4
PEER_V_KDIM = (PEER_PAIRS // 2) * 16


def _peer_v_kernel(row_ref, par_ref, coef_ref, h_ref, tab_ref, out_ref, cz_s, *, tb):
    half = PEER_PAIRS // 2
    kdim = PEER_V_KDIM
    nt = PEER_V_TOKENS_PER_STEP
    pk = lax.broadcasted_iota(I32, (PEER_PAIRS, kdim), 0)
    qk = lax.shift_right_logical(lax.broadcasted_iota(I32, (PEER_PAIRS, kdim), 1), 4)
    hk = (lax.broadcasted_iota(I32, (tb, kdim), 1) & 1).astype(F32)
    coef = coef_ref[...].astype(BF16)
    par = par_ref[...].astype(BF16)
    for hf in range(2):
        expand = jnp.where(pk == qk + hf * half, 1.0, 0.0).astype(BF16)
        cz_s[hf] = jnp.where(hk == _dot(par, expand), _dot(coef, expand), 0.0)
    ks = lax.broadcasted_iota(I32, (8, kdim), 0)
    k8 = lax.broadcasted_iota(I32, (8, kdim), 1)
    own_sublane = lax.shift_right_logical(k8 & 15, 1) == ks

    def tok_body(tt, carry):
        for j in range(nt):
            t = tt * nt + j
            pieces = []
            for q in range(half):
                wa = pltpu.bitcast(_load_words(tab_ref, row_ref, t, q), BF16)
                wb = pltpu.bitcast(_load_words(tab_ref, row_ref, t, half + q), BF16)
                pieces.append(jnp.concatenate([wa, wb], axis=1))
            g = jnp.concatenate(pieces, axis=0)
            ck = jnp.concatenate(
                [jnp.where(own_sublane, jnp.broadcast_to(cz_s[hf, pl.ds(t, 1), :], (8, kdim)), 0.0)
                 for hf in range(2)], axis=0).astype(BF16)
            o = _dot(ck, g)
            out_ref[t] = h_ref[t] + (o[0:8, 0:128] + o[8:16, 128:256])
        return carry

    lax.fori_loop(0, tb // nt, tok_body, 0)


def _peer_v(rows, par, coef, h3, tab, tb=128):
    t = h3.shape[0]
    return pl.pallas_call(
        functools.partial(_peer_v_kernel, tb=tb),
        grid=(t // tb,),
        in_specs=[_tok_spec(tb, pltpu.SMEM), _tok_spec(tb), _tok_spec(tb),
                  pl.BlockSpec((tb, 8, 128), lambda i: (i, 0, 0)), _table_spec()],
        out_specs=pl.BlockSpec((tb, 8, 128), lambda i: (i, 0, 0)),
        out_shape=jax.ShapeDtypeStruct((t, 8, 128), F32),
        scratch_shapes=[pltpu.VMEM((2, tb, PEER_V_KDIM), F32)],
        compiler_params=_cparams(("arbitrary",)),
        name="peer_v",
    )(rows, par, coef, h3, tab)


def _final_norm_kernel(x_ref, g_ref, o_ref):
    x = x_ref[...]
    ms = jnp.mean(x * x, axis=-1, keepdims=True)
    o_ref[...] = x * lax.rsqrt(ms + NORM_EPS) * g_ref[...]


def _final_norm(x2, gain, tm=1024):
    t = x2.shape[0]
    tm = min(tm, t)
    row = lambda i: (i, 0)
    return pl.pallas_call(
        _final_norm_kernel,
        grid=(t // tm,),
        in_specs=[pl.BlockSpec((tm, D_MODEL), row), pl.BlockSpec((1, D_MODEL), lambda i: (0, 0))],
        out_specs=pl.BlockSpec((tm, D_MODEL), row),
        out_shape=jax.ShapeDtypeStruct((t, D_MODEL), F32),
        compiler_params=_cparams(("parallel",)),
        name="final_norm",
    )(x2, gain)


def _pack_table(tab):
    bits = lax.bitcast_convert_type(tab.astype(BF16), jnp.uint16).astype(U32).reshape(TAB_ROWS, 2, 8, 128)
    return (bits[:, 0] | (bits[:, 1] << 16)).reshape(TAB_ROWS * 8, 128)


def _peer(h2, layer, p):
    t = h2.shape[0]
    keys = p["peer_keys"][layer].reshape(2 * PEER_HEADS, N_KEYS, PEER_DHALF).astype(BF16)
    xn, rows, par, gate = _peer_route(h2, p["norm2_g"][layer][None, :],
                                      p["peer_wq"][layer].T.astype(BF16), keys)
    rows, par, gate = rows.T, par.T, gate.T
    d = _peer_u(rows, par, xn.reshape(t, 8, 128), _pack_table(p["peer_u"][layer]))
    d_tp = d.reshape(t // 8, 8, 8, PEER_TOPK).transpose(0, 2, 3, 1).reshape(t, PEER_PAIRS)
    coef = _peer_coef(d_tp, gate)
    out = _peer_v(rows, par, coef, h2.reshape(t, 8, 128), _pack_table(p["peer_v"][layer]))
    return out.reshape(t, D_MODEL)


def _rep(x, n):
    return jnp.repeat(x, n, axis=-1)


def _inproj_weight(w_in_l):
    o = 0
    qkv = w_in_l[:, o:o + GDN_QKV_W]; o += GDN_QKV_W
    z = w_in_l[:, o:o + GDN_W]; o += GDN_W
    a = w_in_l[:, o:o + GDN_HEADS]; o += GDN_HEADS
    b = w_in_l[:, o:o + GDN_HEADS]; o += GDN_HEADS
    rest = w_in_l[:, o:]
    return jnp.concatenate([qkv, z, _rep(a, GDN_DK), _rep(b, GDN_DK), rest], axis=1).astype(BF16)


def _block_diag_gc(b_gnc):
    g = b_gnc.shape[0]
    eye = jnp.eye(g, dtype=b_gnc.dtype)
    t = jnp.swapaxes(b_gnc, 1, 2)
    return (t[:, :, None, :] * eye[:, None, :, None]).reshape(g * t.shape[1], g * t.shape[2])


def _mixers(h, layer, p):
    bn, seq, _ = h.shape
    t = bn * seq
    lam_init = 0.8 - 0.6 * math.exp(-0.3 * layer)
    gdn_in, us, qc, kc, vc = _inproj(h.reshape(t, D_MODEL), p["norm1_g"][layer][None, :],
                                     _inproj_weight(p["w_in"][layer]))
    a_log = _rep(p["gdn_a_log"][layer].astype(F32), GDN_DK)[None, :]
    dtb = _rep(p["gdn_dt_bias"][layer].astype(F32), GDN_DK)[None, :]
    o_a = _gdn(gdn_in.reshape(bn, seq, INPROJ_GDN_W), p["gdn_conv_w"][layer], a_log, dtb,
               p["gdn_norm_g"][layer][None, :])
    flat = lambda x: x.reshape(1, S5_NS)
    bre_bd = _block_diag_gc(p["s5_b_re"][layer])
    bim_bd = _block_diag_gc(p["s5_b_im"][layer])
    cre_bd = _block_diag_gc(p["s5_c_re"][layer])
    cim_bd = _block_diag_gc(p["s5_c_im"][layer])
    ys = _s5(us.reshape(bn, seq // S5_SUB, S5_SUB * S5_WIDTH),
             flat(p["s5_lambda_re"][layer]), flat(p["s5_lambda_im"][layer]),
             flat(_rep(p["s5_log_step"][layer][:, None], S5_STATE)),
             bre_bd, bim_bd, cre_bd, cim_bd, p["s5_d"][layer][None, :])
    row = lambda x: x[None, :]
    o_c = _attn(qc.reshape(bn, seq, DIFF_W), kc.reshape(bn, seq, DIFF_W), vc.reshape(bn, seq, DIFF_W),
                row(p["diff_lq1"][layer]), row(p["diff_lk1"][layer]), row(p["diff_lq2"][layer]),
                row(p["diff_lk2"][layer]), row(p["diff_norm_g"][layer]), lam_init)
    h2 = _outproj(h.reshape(t, D_MODEL), o_a.reshape(t, GDN_W), ys.reshape(t, S5_WIDTH),
                  o_c.reshape(t, DIFF_W), p["s5_glu_w"][layer].astype(BF16), p["s5_glu_b"][layer][None, :],
                  p["w_out"][layer].astype(BF16))
    return h2


def kernel(x, norm1_g, w_in, gdn_conv_w, gdn_a_log, gdn_dt_bias, gdn_norm_g, s5_lambda_re, s5_lambda_im, s5_b_re, s5_b_im, s5_c_re, s5_c_im, s5_d, s5_log_step, s5_glu_w, s5_glu_b, diff_lq1, diff_lk1, diff_lq2, diff_lk2, diff_norm_g, w_out, norm2_g, peer_wq, peer_keys, peer_u, peer_v, final_g):
    p = dict(norm1_g=norm1_g, w_in=w_in, gdn_conv_w=gdn_conv_w, gdn_a_log=gdn_a_log, gdn_dt_bias=gdn_dt_bias,
             gdn_norm_g=gdn_norm_g, s5_lambda_re=s5_lambda_re, s5_lambda_im=s5_lambda_im, s5_b_re=s5_b_re,
             s5_b_im=s5_b_im, s5_c_re=s5_c_re, s5_c_im=s5_c_im, s5_d=s5_d, s5_log_step=s5_log_step,
             s5_glu_w=s5_glu_w, s5_glu_b=s5_glu_b, diff_lq1=diff_lq1, diff_lk1=diff_lk1, diff_lq2=diff_lq2,
             diff_lk2=diff_lk2, diff_norm_g=diff_norm_g, w_out=w_out, norm2_g=norm2_g, peer_wq=peer_wq,
             peer_keys=peer_keys, peer_u=peer_u, peer_v=peer_v, final_g=final_g)
    h = x
    for layer in range(DEPTH):
        h2 = _mixers(h, layer, p)
        h = _peer(h2, layer, p).reshape(x.shape)
    return _final_norm(h.reshape(-1, D_MODEL), final_g[None, :]).reshape(x.shape)
```

```python
import functools
import math

import jax
import jax.numpy as jnp
from jax import lax
from jax.experimental import pallas as pl
from jax.experimental.pallas import tpu as pltpu
from jax.experimental.pallas import tpu_sc as plsc

F32 = jnp.float32
BF16 = jnp.bfloat16
I32 = jnp.int32
U32 = jnp.uint32

D_MODEL = 1024
DEPTH = 2
GDN_HEADS = 4
GDN_DK = 64
GDN_CHUNK = 64
GDN_W = GDN_HEADS * GDN_DK
GDN_QKV_W = 3 * GDN_W
GDN_CONV = 4
S5_WIDTH = 256
S5_GROUPS = 16
S5_GROUP_CH = 16
S5_STATE = 64
S5_NS = S5_GROUPS * S5_STATE
S5_SUB = 16
DIFF_HEADS = 4
DIFF_DQK = 64
DIFF_DV = 128
DIFF_W = 512
PEER_HEADS = 8
PEER_DHALF = 64
N_KEYS = 128
N_EXPERTS = N_KEYS * N_KEYS
PEER_TOPK = 16
PEER_PAIRS = PEER_HEADS * PEER_TOPK
NORM_EPS = 1e-6
NEG_INF = float("-inf")

VMEM_LIMIT_BYTES = 56 * 1024 * 1024


def _cparams(sem, vmem=VMEM_LIMIT_BYTES):
    return pltpu.CompilerParams(dimension_semantics=sem, vmem_limit_bytes=vmem)


def _dot(a, b):
    return jnp.dot(a, b, preferred_element_type=F32)


def _dot_nt(a, b):
    return lax.dot_general(a, b, (((1,), (1,)), ((), ())), preferred_element_type=F32)


def _dot_tn(a, b):
    return lax.dot_general(a, b, (((0,), (0,)), ((), ())), preferred_element_type=F32)


def _split(x):
    hi = x.astype(BF16)
    lo = (x - hi.astype(F32)).astype(BF16)
    return hi, lo


def _dot_sel_r(x, sel):
    hi, lo = _split(x)
    return _dot(hi, sel) + _dot(lo, sel)


def _dot_sel_l(sel, x):
    hi, lo = _split(x)
    return _dot(sel, hi) + _dot(sel, lo)


def _mm3(a, b):
    ah, al = _split(a)
    bh, bl = _split(b)
    return _dot(ah, bh) + (_dot(ah, bl) + _dot(al, bh))


def _sigmoid(x):
    return 1.0 / (1.0 + jnp.exp(-x))


def _softplus(x):
    return jnp.maximum(x, 0.0) + jnp.log1p(jnp.exp(-jnp.abs(x)))


def _gelu_tanh(x):
    c = math.sqrt(2.0 / math.pi)
    return 0.5 * x * (1.0 + jnp.tanh(c * (x + 0.044715 * (x * x * x))))


INPROJ_GDN_W = GDN_QKV_W + 3 * GDN_W


def _inproj_kernel(x_ref, g_ref, w_ref, gdn_ref, us_ref, q_ref, k_ref, v_ref):
    x = x_ref[...]
    ms = jnp.mean(x * x, axis=-1, keepdims=True)
    xn = (x * lax.rsqrt(ms + NORM_EPS) * g_ref[...]).astype(BF16)
    o = INPROJ_GDN_W
    gdn_ref[...] = _dot(xn, w_ref[:, 0:o])
    us_ref[...] = _dot(xn, w_ref[:, o:o + S5_WIDTH])
    o += S5_WIDTH
    q_ref[...] = (_dot(xn, w_ref[:, o:o + DIFF_W]) * (DIFF_DQK ** -0.5)).astype(BF16)
    k_ref[...] = _dot(xn, w_ref[:, o + DIFF_W:o + 2 * DIFF_W]).astype(BF16)
    v_ref[...] = _dot(xn, w_ref[:, o + 2 * DIFF_W:o + 3 * DIFF_W]).astype(BF16)


def _inproj(x2, gain, w, tm=512):
    t = x2.shape[0]
    nw = w.shape[1]
    row = lambda i: (i, 0)
    fixed = lambda i: (0, 0)
    return pl.pallas_call(
        _inproj_kernel,
        grid=(t // tm,),
        in_specs=[pl.BlockSpec((tm, D_MODEL), row), pl.BlockSpec((1, D_MODEL), fixed),
                  pl.BlockSpec((D_MODEL, nw), fixed)],
        out_specs=[pl.BlockSpec((tm, INPROJ_GDN_W), row), pl.BlockSpec((tm, S5_WIDTH), row),
                   pl.BlockSpec((tm, DIFF_W), row), pl.BlockSpec((tm, DIFF_W), row),
                   pl.BlockSpec((tm, DIFF_W), row)],
        out_shape=[jax.ShapeDtypeStruct((t, INPROJ_GDN_W), F32), jax.ShapeDtypeStruct((t, S5_WIDTH), F32),
                   jax.ShapeDtypeStruct((t, DIFF_W), BF16), jax.ShapeDtypeStruct((t, DIFF_W), BF16),
                   jax.ShapeDtypeStruct((t, DIFF_W), BF16)],
        compiler_params=_cparams(("parallel",)),
        name="inproj",
    )(x2, gain, w)


def _gdn_kernel(blk_ref, convw_ref, alog_ref, dtb_ref, ng_ref, out_ref,
                s_ref, tail_ref, xp_ref, q_s, k_s, v_s, b_s, g_s, *, ct):
    c64 = GDN_CHUNK

    @pl.when(pl.program_id(1) == 0)
    def _():
        s_ref[...] = jnp.zeros_like(s_ref)
        tail_ref[...] = jnp.zeros_like(tail_ref)

    qkv = blk_ref[0, :, 0:GDN_QKV_W]
    xp_ref[0:8, :] = tail_ref[...]
    xp_ref[8:8 + ct, :] = qkv
    tail_ref[...] = qkv[ct - 8:ct, :]
    cw = convw_ref[...]
    y = cw[0:1, :] * xp_ref[5:5 + ct, :]
    for j in range(1, GDN_CONV):
        y = y + cw[j:j + 1, :] * xp_ref[5 + j:5 + j + ct, :]
    y = y * _sigmoid(y)

    ri = lax.broadcasted_iota(I32, (GDN_W, GDN_W), 0)
    ci = lax.broadcasted_iota(I32, (GDN_W, GDN_W), 1)
    head_ones = jnp.where((ri // c64) == (ci // c64), 1.0, 0.0).astype(BF16)

    q = y[:, 0:GDN_W]
    k = y[:, GDN_W:2 * GDN_W]
    q_s[...] = q * lax.rsqrt(_dot_sel_r(q * q, head_ones) + 1e-6) * (GDN_DK ** -0.5)
    k_s[...] = k * lax.rsqrt(_dot_sel_r(k * k, head_ones) + 1e-6)
    v_s[...] = y[:, 2 * GDN_W:3 * GDN_W]
    a_rep = blk_ref[0, :, GDN_QKV_W + GDN_W:GDN_QKV_W + 2 * GDN_W]
    b_rep = blk_ref[0, :, GDN_QKV_W + 2 * GDN_W:GDN_QKV_W + 3 * GDN_W]
    b_s[...] = _sigmoid(b_rep)
    g_raw = -jnp.exp(alog_ref[...]) * _softplus(a_rep + dtb_ref[...])
    rt = lax.broadcasted_iota(I32, (ct, ct), 0)
    ctk = lax.broadcasted_iota(I32, (ct, ct), 1)
    tri_bd = jnp.where(((rt // c64) == (ctk // c64)) & (ctk <= rt), 1.0, 0.0).astype(BF16)
    g_s[...] = _dot_sel_l(tri_bd, g_raw)

    r64 = lax.broadcasted_iota(I32, (c64, c64), 0)
    col64 = lax.broadcasted_iota(I32, (c64, c64), 1)
    incl = col64 <= r64
    strict = col64 < r64
    eye_b = col64 == r64
    eye_f = jnp.where(eye_b, 1.0, 0.0)
    ones64 = jnp.ones((c64, c64), BF16)
    ng = ng_ref[...]

    n_chunks = ct // c64
    heads = range(GDN_HEADS)
    chains = [(slice(c * c64, (c + 1) * c64), slice(h * c64, (h + 1) * c64))
              for c in range(n_chunks) for h in heads]
    each = lambda f, *ls: [f(*xs) for xs in zip(*ls)]
    gi = [g_s[r, l] for r, l in chains]
    gj = each(lambda g: _dot_sel_l(ones64, jnp.where(eye_b, g, 0.0)), gi)
    dec = each(lambda a, b: jnp.exp(jnp.where(incl, a - b, NEG_INF)), gi, gj)
    kh = [k_s[r, l] for r, l in chains]
    bi = [b_s[r, l] for r, l in chains]
    kb = each(lambda x: x.astype(BF16), kh)
    kk = each(_dot_nt, kb, kb)
    lm = each(lambda b, x, d: jnp.where(strict, b * x * d, 0.0), bi, kk, dec)
    tinv = each(lambda x: eye_f - x, lm)
    pw = lm
    for _ in range(5):
        pw = each(_mm3, pw, pw)
        tinv = each(lambda t, m: t + _mm3(t, m), tinv, pw)
    tb = each(lambda x: x.astype(BF16), tinv)
    eg = each(jnp.exp, gi)
    u = each(lambda t, rl, b: _dot(t, (v_s[rl[0], rl[1]] * b).astype(BF16)), tb, chains, bi)
    w = each(lambda t, k, b, e: _dot(t, (k * (b * e)).astype(BF16)).astype(BF16), tb, kh, bi, eg)
    qh = [q_s[r, l] for r, l in chains]
    qk = each(lambda q, k, d: jnp.where(incl, _dot_nt(q.astype(BF16), k) * d, 0.0).astype(BF16), qh, kb, dec)
    qg = each(lambda q, e: (q * e).astype(BF16), qh, eg)
    g_last = each(lambda g: g[c64 - 1:c64, :], gi)
    kg = each(lambda k, gl, g: (k * jnp.exp(gl - g)).astype(BF16), kh, g_last, gi)
    decay_last = each(jnp.exp, g_last)

    states = [s_ref[h] for h in heads]
    for c in range(n_chunks):
        ids = [c * GDN_HEADS + h for h in heads]
        sb = [s.astype(BF16) for s in states]
        vnb = [(u[n] - _dot(w[n], sb[h])).astype(BF16) for h, n in zip(heads, ids)]
        o = [_dot(qg[n], sb[h]) + _dot(qk[n], vnb[h]) for h, n in zip(heads, ids)]
        states = [states[h] * decay_last[n] + _dot_tn(kg[n], vnb[h]) for h, n in zip(heads, ids)]
        ms = [_dot_sel_r(x * x, ones64) * (1.0 / c64) for x in o]
        outs = [x * lax.rsqrt(m + NORM_EPS) * ng for x, m in zip(o, ms)]
        rows = slice(c * c64, (c + 1) * c64)
        z = blk_ref[0, rows, GDN_QKV_W:GDN_QKV_W + GDN_W]
        out_ref[0, rows, :] = (jnp.concatenate(outs, axis=1) * (z * _sigmoid(z))).astype(out_ref.dtype)
    for h in heads:
        s_ref[h] = states[h]


def _gdn(gdn_in, conv_w, a_log_rep, dtb_rep, ng, ct=256):
    bn, seq, _ = gdn_in.shape
    fixed = lambda b, l: (0, 0)
    return pl.pallas_call(
        functools.partial(_gdn_kernel, ct=ct),
        grid=(bn, seq // ct),
        in_specs=[pl.BlockSpec((1, ct, INPROJ_GDN_W), lambda b, l: (b, l, 0)),
                  pl.BlockSpec((GDN_CONV, GDN_QKV_W), fixed), pl.BlockSpec((1, GDN_W), fixed),
                  pl.BlockSpec((1, GDN_W), fixed), pl.BlockSpec((1, GDN_DK), fixed)],
        out_specs=pl.BlockSpec((1, ct, GDN_W), lambda b, l: (b, l, 0)),
        out_shape=jax.ShapeDtypeStruct((bn, seq, GDN_W), BF16),
        scratch_shapes=[pltpu.VMEM((GDN_HEADS, GDN_DK, GDN_DK), F32), pltpu.VMEM((8, GDN_QKV_W), F32),
                        pltpu.VMEM((ct + 8, GDN_QKV_W), F32)] + [pltpu.VMEM((ct, GDN_W), F32)] * 5,
        compiler_params=_cparams(("arbitrary", "arbitrary")),
        name="gdn",
    )(gdn_in, conv_w, a_log_rep, dtb_rep, ng)


def _s5_kernel(u_ref, lre_ref, lim_ref, lstep_ref, bre_ref, bim_ref, cre_ref, cim_ref, d_ref, y_ref,
               bmat, cmat, avec, carry, zr_s, zi_s, xr_s, xi_s, *, rb):
    ns = S5_NS
    cw = S5_WIDTH

    @pl.when(pl.program_id(1) == 0)
    def _():
        step = jnp.exp(lstep_ref[...])
        lr = lre_ref[...]
        li = lim_ref[...]
        mag = jnp.exp(lr * step)
        ar = mag * jnp.cos(li * step)
        ai = mag * jnp.sin(li * step)
        den = lr * lr + li * li
        mr = ((ar - 1.0) * lr + ai * li) / den
        mi = (ai * lr - (ar - 1.0) * li) / den
        bre = bre_ref[...]
        bim = bim_ref[...]
        bmat[:, 0:ns] = (mr * bre - mi * bim).astype(BF16)
        bmat[:, ns:2 * ns] = (mr * bim + mi * bre).astype(BF16)
        cmat[0:ns, :] = cre_ref[...].astype(BF16)
        cmat[ns:2 * ns, :] = (-cim_ref[...]).astype(BF16)
        avec[0:1, :] = ar
        avec[1:2, :] = ai
        pr, pi = ar, ai
        for _ in range(4):
            pr, pi = pr * pr - pi * pi, 2.0 * pr * pi
        avec[2:3, :] = pr
        avec[3:4, :] = pi
        carry[...] = jnp.zeros_like(carry)

    ar = avec[0:1, :]
    ai = avec[1:2, :]
    a16r = avec[2:3, :]
    a16i = avec[3:4, :]

    def inject(s):
        ub = u_ref[0, :, s * cw:(s + 1) * cw].astype(BF16)
        return _dot(ub, bmat[...])

    xr = jnp.zeros((rb, ns), F32)
    xi = jnp.zeros((rb, ns), F32)
    for s in range(S5_SUB):
        p = inject(s)
        xr, xi = ar * xr - ai * xi + p[:, 0:ns], ar * xi + ai * xr + p[:, ns:2 * ns]
    zr_s[...] = xr
    zi_s[...] = xi

    def row_step(kk, c):
        cr, ci_ = c
        xr_s[pl.ds(kk, 1), :] = cr
        xi_s[pl.ds(kk, 1), :] = ci_
        zr = zr_s[pl.ds(kk, 1), :]
        zi = zi_s[pl.ds(kk, 1), :]
        return (a16r * cr - a16i * ci_ + zr, a16r * ci_ + a16i * cr + zi)

    cr, ci_ = lax.fori_loop(0, rb, row_step, (carry[0:1, :], carry[1:2, :]))
    carry[0:1, :] = cr
    carry[1:2, :] = ci_

    xr = xr_s[...]
    xi = xi_s[...]
    dsk = d_ref[...]
    for s in range(S5_SUB):
        p = inject(s)
        xr, xi = ar * xr - ai * xi + p[:, 0:ns], ar * xi + ai * xr + p[:, ns:2 * ns]
        yv = _dot(xr.astype(BF16), cmat[0:ns, :]) + _dot(xi.astype(BF16), cmat[ns:2 * ns, :])
        yv = yv + dsk * u_ref[0, :, s * cw:(s + 1) * cw]
        y_ref[0, :, s * cw:(s + 1) * cw] = _gelu_tanh(yv).astype(y_ref.dtype)


def _s5(u_rows, lre, lim, lstep, bre_bd, bim_bd, cre_bd, cim_bd, dskip, rb=128):
    bn, nrows, rw = u_rows.shape
    fixed = lambda b, r: (0, 0)
    ns = S5_NS
    return pl.pallas_call(
        functools.partial(_s5_kernel, rb=rb),
        grid=(bn, nrows // rb),
        in_specs=[pl.BlockSpec((1, rb, rw), lambda b, r: (b, r, 0)),
                  pl.BlockSpec((1, ns), fixed), pl.BlockSpec((1, ns), fixed), pl.BlockSpec((1, ns), fixed),
                  pl.BlockSpec((S5_WIDTH, ns), fixed), pl.BlockSpec((S5_WIDTH, ns), fixed),
                  pl.BlockSpec((ns, S5_WIDTH), fixed), pl.BlockSpec((ns, S5_WIDTH), fixed),
                  pl.BlockSpec((1, S5_WIDTH), fixed)],
        out_specs=pl.BlockSpec((1, rb, rw), lambda b, r: (b, r, 0)),
        out_shape=jax.ShapeDtypeStruct((bn, nrows, rw), BF16),
        scratch_shapes=[pltpu.VMEM((S5_WIDTH, 2 * ns), BF16), pltpu.VMEM((2 * ns, S5_WIDTH), BF16),
                        pltpu.VMEM((8, ns), F32), pltpu.VMEM((8, ns), F32)]
                       + [pltpu.VMEM((rb, ns), F32)] * 4,
        compiler_params=_cparams(("arbitrary", "arbitrary")),
        name="s5",
    )(u_rows, lre, lim, lstep, bre_bd, bim_bd, cre_bd, cim_bd, dskip)


def _attn_kernel(q_ref, k_ref, v_ref, lq1_ref, lk1_ref, lq2_ref, lk2_ref, ng_ref, o_ref,
                 qs_s, m_s, l_s, acc_s, *, blk, lam_init):
    i = pl.program_id(2)
    q = q_ref[0]
    lane = lax.broadcasted_iota(I32, q.shape, 1)
    zero = jnp.zeros_like(q)
    qs_s[0:blk, :] = jnp.where(lane < DIFF_DQK, q, zero)
    qs_s[blk:2 * blk, :] = jnp.where(lane >= DIFF_DQK, q, zero)
    m_s[...] = jnp.full_like(m_s, NEG_INF)
    l_s[...] = jnp.zeros_like(l_s)
    acc_s[...] = jnp.zeros_like(acc_s)
    ones = jnp.ones((blk, DIFF_DV), BF16)

    def block_rows(j):
        return pl.ds(pl.multiple_of(j * blk, blk), blk)

    def scores(j):
        return _dot_nt(qs_s[...], k_ref[0, block_rows(j), :])

    def update(j, s):
        m_old = m_s[...]
        m_new = jnp.maximum(m_old, jnp.max(s, axis=-1, keepdims=True))
        p = jnp.exp(s - jnp.concatenate([m_new] * (blk // DIFF_DV), axis=1)).astype(BF16)
        alpha = jnp.exp(m_old - m_new)
        pv = _dot(p, jnp.concatenate([v_ref[0, block_rows(j), :], ones], axis=1))
        acc_s[...] = alpha * acc_s[...] + pv[:, 0:DIFF_DV]
        l_s[...] = alpha * l_s[...] + pv[:, DIFF_DV:2 * DIFF_DV]
        m_s[...] = m_new

    def body(j, s):
        s_next = scores(j + 1)
        update(j, s)
        return s_next

    s = lax.fori_loop(0, i, body, scores(0))
    row = lax.broadcasted_iota(I32, s.shape, 0) & (blk - 1)
    col = lax.broadcasted_iota(I32, s.shape, 1)
    update(i, jnp.where(col <= row, s, NEG_INF))
    lam = (jnp.exp(jnp.sum(lq1_ref[...] * lk1_ref[...], axis=-1, keepdims=True))
           - jnp.exp(jnp.sum(lq2_ref[...] * lk2_ref[...], axis=-1, keepdims=True)) + lam_init)
    o = acc_s[0:blk, :] / l_s[0:blk, :] - lam * (acc_s[blk:2 * blk, :] / l_s[blk:2 * blk, :])
    ms = jnp.mean(o * o, axis=-1, keepdims=True)
    o = o * lax.rsqrt(ms + NORM_EPS) * ng_ref[...] * (1.0 - lam_init)
    o_ref[0] = o.astype(o_ref.dtype)


def _attn(q, k, v, lq1, lk1, lq2, lk2, ng, lam_init, blk=512):
    bn, seq, _ = q.shape
    blk = min(blk, seq)
    fixed = lambda b, h, i: (0, 0)
    kv_spec = pl.BlockSpec((1, seq, DIFF_DV), lambda b, h, i: (b, 0, h))
    return pl.pallas_call(
        functools.partial(_attn_kernel, blk=blk, lam_init=lam_init),
        grid=(bn, DIFF_HEADS, seq // blk),
        in_specs=[pl.BlockSpec((1, blk, DIFF_DV), lambda b, h, i: (b, i, h)), kv_spec, kv_spec]
                 + [pl.BlockSpec((1, DIFF_DQK), fixed)] * 4 + [pl.BlockSpec((1, DIFF_DV), fixed)],
        out_specs=pl.BlockSpec((1, blk, DIFF_DV), lambda b, h, i: (b, i, h)),
        out_shape=jax.ShapeDtypeStruct((bn, seq, DIFF_W), BF16),
        scratch_shapes=[pltpu.VMEM((2 * blk, DIFF_DV), BF16), pltpu.VMEM((2 * blk, DIFF_DV), F32),
                        pltpu.VMEM((2 * blk, DIFF_DV), F32), pltpu.VMEM((2 * blk, DIFF_DV), F32)],
        compiler_params=_cparams(("parallel", "parallel", "arbitrary")),
        name="diff_attn",
    )(q, k, v, lq1, lk1, lq2, lk2, ng)


def _outproj_kernel(h_ref, oa_ref, ys_ref, oc_ref, gw_ref, gb_ref, wo_ref, out_ref):
    zg = _dot(ys_ref[...], gw_ref[...]) + gb_ref[...]
    ob = (zg[:, 0:S5_WIDTH] * _sigmoid(zg[:, S5_WIDTH:2 * S5_WIDTH])).astype(BF16)
    acc = _dot(oa_ref[...], wo_ref[0:GDN_W, :])
    acc = acc + _dot(ob, wo_ref[GDN_W:GDN_W + S5_WIDTH, :])
    acc = acc + _dot(oc_ref[...], wo_ref[GDN_W + S5_WIDTH:, :])
    out_ref[...] = h_ref[...] + acc


def _outproj(h2, oa, ys, oc, glu_w, glu_b, w_out, tm=512):
    t = h2.shape[0]
    row = lambda i: (i, 0)
    fixed = lambda i: (0, 0)
    return pl.pallas_call(
        _outproj_kernel,
        grid=(t // tm,),
        in_specs=[pl.BlockSpec((tm, D_MODEL), row), pl.BlockSpec((tm, GDN_W), row),
                  pl.BlockSpec((tm, S5_WIDTH), row), pl.BlockSpec((tm, DIFF_W), row),
                  pl.BlockSpec((S5_WIDTH, 2 * S5_WIDTH), fixed), pl.BlockSpec((1, 2 * S5_WIDTH), fixed),
                  pl.BlockSpec((D_MODEL, D_MODEL), fixed)],
        out_specs=pl.BlockSpec((tm, D_MODEL), row),
        out_shape=jax.ShapeDtypeStruct((t, D_MODEL), F32),
        compiler_params=_cparams(("parallel",)),
        name="outproj",
    )(h2, oa, ys, oc, glu_w, glu_b, w_out)


_BIG_ID = 1.0e9


def _top16(x, ids, payload):
    n = x.shape[1]
    r16 = lax.broadcasted_iota(I32, (PEER_TOPK, n), 0)
    vals = jnp.zeros((PEER_TOPK, n), F32)
    pays = jnp.zeros((PEER_TOPK, n), F32)
    for kk in range(PEER_TOPK):
        m = jnp.max(x, axis=0, keepdims=True)
        first = jnp.min(jnp.where(x == m, ids, _BIG_ID), axis=0, keepdims=True)
        hit = ids == first
        pay = first if payload is None else jnp.max(jnp.where(hit, payload, -1.0), axis=0, keepdims=True)
        x = jnp.where(hit, NEG_INF, x)
        vals = jnp.where(r16 == kk, m, vals)
        pays = jnp.where(r16 == kk, pay, pays)
    return vals, pays


def _peer_route_kernel(h_ref, g_ref, wqt_ref, keys_ref, xn_ref, exp_ref, row_ref, par_ref, gate_ref,
                       qt_s, sv_s, si_s, *, tm):
    x = h_ref[...]
    ms = jnp.mean(x * x, axis=-1, keepdims=True)
    xn = x * lax.rsqrt(ms + NORM_EPS) * g_ref[...]
    xn_ref[...] = xn
    qt_s[...] = _dot_nt(wqt_ref[...], xn.astype(BF16)).astype(BF16)

    key_id = lax.broadcasted_iota(I32, (N_KEYS, tm), 0).astype(F32)

    def half_body(hp, carry):
        r0 = pl.multiple_of(hp * PEER_DHALF, PEER_DHALF)
        s = _dot(keys_ref[hp], qt_s[pl.ds(r0, PEER_DHALF), :])
        vals, ids = _top16(s, key_id, None)
        sv_s[hp] = vals
        si_s[hp] = ids
        return carry

    lax.fori_loop(0, 2 * PEER_HEADS, half_body, 0)

    i8 = lax.broadcasted_iota(I32, (8, tm), 0).astype(F32)

    def head_body(hd, carry):
        a0 = sv_s[2 * hd]
        a1 = sv_s[2 * hd + 1]
        e0 = si_s[2 * hd] * float(N_KEYS)
        e1 = si_s[2 * hd + 1]
        cs, es, fs = [], [], []
        for i in range(8):
            cs.append(a0[i:i + 1, :] + a1[0:8, :])
            es.append(e0[i:i + 1, :] + e1[0:8, :])
            fs.append(i8 + float(i * PEER_TOPK))
        cs.append(a0[0:1, :] + a1[8:16, :])
        es.append(e0[0:1, :] + e1[8:16, :])
        fs.append(i8 + 8.0)
        cs.append(a0[8:16, :] + a1[0:1, :])
        es.append(e0[8:16, :] + e1[0:1, :])
        fs.append((i8 + 8.0) * float(PEER_TOPK))
        top_s, experts = _top16(jnp.concatenate(cs, axis=0), jnp.concatenate(fs, axis=0),
                                jnp.concatenate(es, axis=0))
        ex = jnp.exp(top_s - jnp.max(top_s, axis=0, keepdims=True))
        rows = pl.ds(pl.multiple_of(hd * PEER_TOPK, PEER_TOPK), PEER_TOPK)
        gate_ref[rows, :] = ex / jnp.sum(ex, axis=0, keepdims=True)
        expert = experts.astype(I32)
        exp_ref[rows, :] = expert
        row_ref[rows, :] = lax.shift_right_logical(expert, 1) * 8
        par_ref[rows, :] = (expert & 1).astype(F32)
        return carry

    lax.fori_loop(0, PEER_HEADS, head_body, 0)


def _peer_route(h2, gain, wqt, keys, tm=512):
    t = h2.shape[0]
    tm = min(tm, t)
    row = lambda i: (i, 0)
    col = lambda i: (0, i)
    return pl.pallas_call(
        functools.partial(_peer_route_kernel, tm=tm),
        grid=(t // tm,),
        in_specs=[pl.BlockSpec((tm, D_MODEL), row), pl.BlockSpec((1, D_MODEL), lambda i: (0, 0)),
                  pl.BlockSpec((D_MODEL, D_MODEL), lambda i: (0, 0)),
                  pl.BlockSpec((2 * PEER_HEADS, N_KEYS, PEER_DHALF), lambda i: (0, 0, 0))],
        out_specs=[pl.BlockSpec((tm, D_MODEL), row)] + [pl.BlockSpec((PEER_PAIRS, tm), col)] * 4,
        out_shape=[jax.ShapeDtypeStruct((t, D_MODEL), F32), jax.ShapeDtypeStruct((PEER_PAIRS, t), I32),
                   jax.ShapeDtypeStruct((PEER_PAIRS, t), I32), jax.ShapeDtypeStruct((PEER_PAIRS, t), F32),
                   jax.ShapeDtypeStruct((PEER_PAIRS, t), F32)],
        scratch_shapes=[pltpu.VMEM((D_MODEL, tm), BF16), pltpu.VMEM((2 * PEER_HEADS, PEER_TOPK, tm), F32),
                        pltpu.VMEM((2 * PEER_HEADS, PEER_TOPK, tm), F32)],
        compiler_params=_cparams(("parallel",)),
        name="peer_route",
    )(h2, gain, wqt, keys)


TAB_ROWS = N_EXPERTS // 2
HIGH_HALF = 0xFFFF0000


def _splat_rows(row):
    r = lax.broadcasted_iota(I32, (128, 128), 0)
    c = lax.broadcasted_iota(I32, (128, 128), 1)
    diag = jnp.where(r == c, jnp.broadcast_to(row, (128, 128)), 0.0).astype(BF16)
    return _dot(diag, jnp.ones((128, 128), BF16))


def _stage_shifts(par_ref, t, srep_s):
    shift = 16.0 - 16.0 * par_ref[pl.ds(t, 1), :]
    srep_s[...] = pltpu.bitcast(_splat_rows(shift).astype(I32), U32)


def _load_words(tab_ref, row_ref, t, p):
    return tab_ref[pl.ds(pl.multiple_of(row_ref[t, p], 8), 8), :]


def _load_expert(tab_ref, row_ref, srep_s, t, p):
    w = lax.shift_left(_load_words(tab_ref, row_ref, t, p), jnp.broadcast_to(srep_s[p:p + 1, :], (8, 128)))
    return pltpu.bitcast(w & jnp.uint32(HIGH_HALF), F32)


_TREE8_SUBLANE = (3, 7, 1, 5, 2, 6, 0, 4)


def _tree8(v, upper, bit2, bit1):
    c = []
    for a, b in ((v[0], v[1]), (v[2], v[3]), (v[4], v[5]), (v[6], v[7])):
        c.append(jnp.where(upper, b + pltpu.roll(b, 4, 0), a + pltpu.roll(a, 4, 0)))
    e = []
    for c1, c2 in ((c[0], c[1]), (c[2], c[3])):
        e.append(jnp.where(bit2, c1 + pltpu.roll(c1, 2, 0), c2 + pltpu.roll(c2, 6, 0)))
    return jnp.where(bit1, e[0] + pltpu.roll(e[0], 1, 0), e[1] + pltpu.roll(e[1], 7, 0))


PEER_U_TOKENS_PER_STEP = 2


def _peer_u_kernel(row_ref, par_ref, x_ref, tab_ref, d_ref, r_s, srep_a, srep_b, *, tb):
    sub = lax.broadcasted_iota(I32, (8, 128), 0)
    lane = lax.broadcasted_iota(I32, (8, 128), 1)
    upper = sub >= 4
    bit2 = (sub & 2) != 0
    bit1 = (sub & 1) != 0
    groups = PEER_PAIRS // 8
    nt = PEER_U_TOKENS_PER_STEP
    steps = tb // nt

    def stage(tt, srep_s):
        for j in range(nt):
            _stage_shifts(par_ref, tt * nt + j, srep_s.at[j])

    def run(tt, srep_s):
        for j in range(nt):
            t = tt * nt + j
            xv = x_ref[t]
            prod = lambda p: _load_expert(tab_ref, row_ref, srep_s.at[j], t, p) * xv
            for g in range(groups):
                r_s[t * groups + g] = _tree8([prod(g * 8 + _TREE8_SUBLANE[m]) for m in range(8)],
                                             upper, bit2, bit1)

    stage(0, srep_a)

    def two_steps(k, carry):
        stage(2 * k + 1, srep_b)
        run(2 * k, srep_a)
        stage(jnp.minimum(2 * k + 2, steps - 1), srep_a)
        run(2 * k + 1, srep_b)
        return carry

    lax.fori_loop(0, steps // 2, two_steps, 0)

    def red_body(t8, carry):
        dacc = jnp.zeros((8, 128), F32)
        for tl in range(8):
            for g in range(groups):
                d = jnp.sum(r_s[(t8 * 8 + tl) * groups + g], axis=1, keepdims=True)
                dacc = jnp.where(lane == tl * PEER_TOPK + g, d, dacc)
        d_ref[t8] = dacc
        return carry

    lax.fori_loop(0, tb // 8, red_body, 0)


def _tok_spec(tb, space=None):
    return pl.BlockSpec((tb, PEER_PAIRS), lambda i: (i, 0), memory_space=space)


def _table_spec():
    return pl.BlockSpec((TAB_ROWS * 8, 128), lambda i: (0, 0), pipeline_mode=pl.Buffered(1))


def _peer_u(rows, par, x3, tab, t, tb=128):
    return pl.pallas_call(
        functools.partial(_peer_u_kernel, tb=tb),
        grid=(t // tb,),
        in_specs=[_tok_spec(tb, pltpu.SMEM), _tok_spec(tb), pl.BlockSpec((tb, 8, 128), lambda i: (i, 0, 0)),
                  _table_spec()],
        out_specs=pl.BlockSpec((tb // 8, 8, 128), lambda i: (i, 0, 0)),
        out_shape=jax.ShapeDtypeStruct((t // 8, 8, 128), F32),
        scratch_shapes=[pltpu.VMEM((tb * (PEER_PAIRS // 8), 8, 128), F32),
                        pltpu.VMEM((PEER_U_TOKENS_PER_STEP, 128, 128), U32),
                        pltpu.VMEM((PEER_U_TOKENS_PER_STEP, 128, 128), U32)],
        compiler_params=_cparams(("arbitrary",)),
        name="peer_u",
    )(rows, par, x3, tab)


SC_CORES = 2
SC_SUBCORES = 16
SC_LANES = 16
SC_WORKERS = SC_CORES * SC_SUBCORES
PEER_U_SC_SHARE_NUM, PEER_U_SC_SHARE_DEN = 3, 8


def _peer_u_sc(tab, expert, xn, t0, t_sc):
    tw = t_sc // SC_WORKERS
    heads = PEER_PAIRS // PEER_TOPK
    chunks = D_MODEL // SC_LANES
    mesh = plsc.VectorSubcoreMesh(core_axis_name="c", subcore_axis_name="s")

    def body(tab_hbm, idx_hbm, x_hbm, d_hbm, idx_v, x_v, rows_a, rows_b, d_v, sem_a, sem_b):
        wid = lax.axis_index("s") * SC_CORES + lax.axis_index("c")
        lane = lax.iota(I32, SC_LANES)
        bufs = ((rows_a, sem_a), (rows_b, sem_b))

        def gather(h):
            buf, sem = bufs[h % 2]
            return pltpu.make_async_copy(tab_hbm.at[idx_v.at[pl.ds(h * PEER_TOPK, PEER_TOPK)]], buf, sem)

        def token(i, carry):
            t = t0 + wid * tw + i
            pltpu.sync_copy(idx_hbm.at[t], idx_v)
            pltpu.sync_copy(x_hbm.at[t], x_v)
            gather(0).start()
            for h in range(heads):
                if h + 1 < heads:
                    gather(h + 1).start()
                gather(h).wait()
                rows_v = bufs[h % 2][0]

                def chunk(j, accs):
                    xj = x_v[pl.ds(j * SC_LANES, SC_LANES)]
                    return tuple(a + rows_v[r, pl.ds(j * SC_LANES, SC_LANES)] * xj for r, a in enumerate(accs))

                accs = lax.fori_loop(0, chunks, chunk,
                                     tuple(jnp.zeros((SC_LANES,), F32) for _ in range(PEER_TOPK)))
                out = jnp.zeros((SC_LANES,), F32)
                for r in range(PEER_TOPK):
                    out = jnp.where(lane == r, jnp.sum(accs[r]), out)
                d_v[pl.ds(h * PEER_TOPK, PEER_TOPK)] = out
            pltpu.sync_copy(d_v, d_hbm.at[t - t0])
            return carry

        lax.fori_loop(0, tw, token, 0)

    return pl.kernel(
        body, mesh=mesh,
        out_type=jax.ShapeDtypeStruct((t_sc, PEER_PAIRS), F32),
        compiler_params=pltpu.CompilerParams(needs_layout_passes=False),
        scratch_types=[pltpu.VMEM((PEER_PAIRS,), I32), pltpu.VMEM((D_MODEL,), F32),
                       pltpu.VMEM((PEER_TOPK, D_MODEL), F32), pltpu.VMEM((PEER_TOPK, D_MODEL), F32),
                       pltpu.VMEM((PEER_PAIRS,), F32), pltpu.SemaphoreType.DMA, pltpu.SemaphoreType.DMA],
        name="peer_u_sc",
    )(tab, expert, xn)


def _peer_coef_kernel(d_ref, gate_ref, c_ref):
    c_ref[...] = gate_ref[...] * _gelu_tanh(d_ref[...])


def _peer_coef(d_tp, gate_tp, tm=2048):
    t = d_tp.shape[0]
    tm = min(tm, t)
    return pl.pallas_call(
        _peer_coef_kernel,
        grid=(t // tm,),
        in_specs=[_tok_spec(tm), _tok_spec(tm)],
        out_specs=_tok_spec(tm),
        out_shape=jax.ShapeDtypeStruct((t, PEER_PAIRS), F32),
        compiler_params=_cparams(("parallel",)),
        name="peer_coef",
    )(d_tp, gate_tp)


PEER_V_TOKENS_PER_STEP = 4
PEER_V_KDIM = (PEER_PAIRS // 2) * 16


def _peer_v_kernel(row_ref, par_ref, coef_ref, h_ref, tab_ref, out_ref, cz_s, *, tb):
    half = PEER_PAIRS // 2
    kdim = PEER_V_KDIM
    nt = PEER_V_TOKENS_PER_STEP
    pk = lax.broadcasted_iota(I32, (PEER_PAIRS, kdim), 0)
    qk = lax.shift_right_logical(lax.broadcasted_iota(I32, (PEER_PAIRS, kdim), 1), 4)
    hk = (lax.broadcasted_iota(I32, (tb, kdim), 1) & 1).astype(F32)
    coef = coef_ref[...].astype(BF16)
    par = par_ref[...].astype(BF16)
    for hf in range(2):
        expand = jnp.where(pk == qk + hf * half, 1.0, 0.0).astype(BF16)
        cz_s[hf] = jnp.where(hk == _dot(par, expand), _dot(coef, expand), 0.0)
    ks = lax.broadcasted_iota(I32, (8, kdim), 0)
    k8 = lax.broadcasted_iota(I32, (8, kdim), 1)
    own_sublane = lax.shift_right_logical(k8 & 15, 1) == ks

    def tok_body(tt, carry):
        for j in range(nt):
            t = tt * nt + j
            pieces = []
            for q in range(half):
                wa = pltpu.bitcast(_load_words(tab_ref, row_ref, t, q), BF16)
                wb = pltpu.bitcast(_load_words(tab_ref, row_ref, t, half + q), BF16)
                pieces.append(jnp.concatenate([wa, wb], axis=1))
            g = jnp.concatenate(pieces, axis=0)
            ck = jnp.concatenate(
                [jnp.where(own_sublane, jnp.broadcast_to(cz_s[hf, pl.ds(t, 1), :], (8, kdim)), 0.0)
                 for hf in range(2)], axis=0).astype(BF16)
            o = _dot(ck, g)
            out_ref[t] = h_ref[t] + (o[0:8, 0:128] + o[8:16, 128:256])
        return carry

    lax.fori_loop(0, tb // nt, tok_body, 0)


def _peer_v(rows, par, coef, h3, tab, tb=128):
    t = h3.shape[0]
    return pl.pallas_call(
        functools.partial(_peer_v_kernel, tb=tb),
        grid=(t // tb,),
        in_specs=[_tok_spec(tb, pltpu.SMEM), _tok_spec(tb), _tok_spec(tb),
                  pl.BlockSpec((tb, 8, 128), lambda i: (i, 0, 0)), _table_spec()],
        out_specs=pl.BlockSpec((tb, 8, 128), lambda i: (i, 0, 0)),
        out_shape=jax.ShapeDtypeStruct((t, 8, 128), F32),
        scratch_shapes=[pltpu.VMEM((2, tb, PEER_V_KDIM), F32)],
        compiler_params=_cparams(("arbitrary",)),
        name="peer_v",
    )(rows, par, coef, h3, tab)


def _final_norm_kernel(x_ref, g_ref, o_ref):
    x = x_ref[...]
    ms = jnp.mean(x * x, axis=-1, keepdims=True)
    o_ref[...] = x * lax.rsqrt(ms + NORM_EPS) * g_ref[...]


def _final_norm(x2, gain, tm=1024):
    t = x2.shape[0]
    tm = min(tm, t)
    row = lambda i: (i, 0)
    return pl.pallas_call(
        _final_norm_kernel,
        grid=(t // tm,),
        in_specs=[pl.BlockSpec((tm, D_MODEL), row), pl.BlockSpec((1, D_MODEL), lambda i: (0, 0))],
        out_specs=pl.BlockSpec((tm, D_MODEL), row),
        out_shape=jax.ShapeDtypeStruct((t, D_MODEL), F32),
        compiler_params=_cparams(("parallel",)),
        name="final_norm",
    )(x2, gain)


def _pack_table(tab):
    bits = lax.bitcast_convert_type(tab.astype(BF16), jnp.uint16).astype(U32).reshape(TAB_ROWS, 2, 8, 128)
    return (bits[:, 0] | (bits[:, 1] << 16)).reshape(TAB_ROWS * 8, 128)


def _peer(h2, layer, p):
    t = h2.shape[0]
    keys = p["peer_keys"][layer].reshape(2 * PEER_HEADS, N_KEYS, PEER_DHALF).astype(BF16)
    xn, expert, rows, par, gate = _peer_route(h2, p["norm2_g"][layer][None, :],
                                              p["peer_wq"][layer].T.astype(BF16), keys)
    expert, rows, par, gate = expert.T, rows.T, par.T, gate.T
    t_sc = (t * PEER_U_SC_SHARE_NUM // PEER_U_SC_SHARE_DEN) // (8 * SC_WORKERS) * (8 * SC_WORKERS)
    t_tc = t - t_sc
    d_sc = _peer_u_sc(p["peer_u"][layer], expert, xn, t_tc, t_sc)
    d = _peer_u(rows, par, xn.reshape(t, 8, 128), _pack_table(p["peer_u"][layer]), t_tc)
    d_tp = d.reshape(t_tc // 8, 8, 8, PEER_TOPK).transpose(0, 2, 3, 1).reshape(t_tc, PEER_PAIRS)
    coef = _peer_coef(jnp.concatenate([d_tp, d_sc], axis=0), gate)
    out = _peer_v(rows, par, coef, h2.reshape(t, 8, 128), _pack_table(p["peer_v"][layer]))
    return out.reshape(t, D_MODEL)


def _rep(x, n):
    return jnp.repeat(x, n, axis=-1)


def _inproj_weight(w_in_l):
    o = 0
    qkv = w_in_l[:, o:o + GDN_QKV_W]; o += GDN_QKV_W
    z = w_in_l[:, o:o + GDN_W]; o += GDN_W
    a = w_in_l[:, o:o + GDN_HEADS]; o += GDN_HEADS
    b = w_in_l[:, o:o + GDN_HEADS]; o += GDN_HEADS
    rest = w_in_l[:, o:]
    return jnp.concatenate([qkv, z, _rep(a, GDN_DK), _rep(b, GDN_DK), rest], axis=1).astype(BF16)


def _block_diag_gc(b_gnc):
    g = b_gnc.shape[0]
    eye = jnp.eye(g, dtype=b_gnc.dtype)
    t = jnp.swapaxes(b_gnc, 1, 2)
    return (t[:, :, None, :] * eye[:, None, :, None]).reshape(g * t.shape[1], g * t.shape[2])


def _mixers(h, layer, p):
    bn, seq, _ = h.shape
    t = bn * seq
    lam_init = 0.8 - 0.6 * math.exp(-0.3 * layer)
    gdn_in, us, qc, kc, vc = _inproj(h.reshape(t, D_MODEL), p["norm1_g"][layer][None, :],
                                     _inproj_weight(p["w_in"][layer]))
    a_log = _rep(p["gdn_a_log"][layer].astype(F32), GDN_DK)[None, :]
    dtb = _rep(p["gdn_dt_bias"][layer].astype(F32), GDN_DK)[None, :]
    o_a = _gdn(gdn_in.reshape(bn, seq, INPROJ_GDN_W), p["gdn_conv_w"][layer], a_log, dtb,
               p["gdn_norm_g"][layer][None, :])
    flat = lambda x: x.reshape(1, S5_NS)
    bre_bd = _block_diag_gc(p["s5_b_re"][layer])
    bim_bd = _block_diag_gc(p["s5_b_im"][layer])
    cre_bd = _block_diag_gc(p["s5_c_re"][layer])
    cim_bd = _block_diag_gc(p["s5_c_im"][layer])
    ys = _s5(us.reshape(bn, seq // S5_SUB, S5_SUB * S5_WIDTH),
             flat(p["s5_lambda_re"][layer]), flat(p["s5_lambda_im"][layer]),
             flat(_rep(p["s5_log_step"][layer][:, None], S5_STATE)),
             bre_bd, bim_bd, cre_bd, cim_bd, p["s5_d"][layer][None, :])
    row = lambda x: x[None, :]
    o_c = _attn(qc.reshape(bn, seq, DIFF_W), kc.reshape(bn, seq, DIFF_W), vc.reshape(bn, seq, DIFF_W),
                row(p["diff_lq1"][layer]), row(p["diff_lk1"][layer]), row(p["diff_lq2"][layer]),
                row(p["diff_lk2"][layer]), row(p["diff_norm_g"][layer]), lam_init)
    h2 = _outproj(h.reshape(t, D_MODEL), o_a.reshape(t, GDN_W), ys.reshape(t, S5_WIDTH),
                  o_c.reshape(t, DIFF_W), p["s5_glu_w"][layer].astype(BF16), p["s5_glu_b"][layer][None, :],
                  p["w_out"][layer].astype(BF16))
    return h2


def kernel(x, norm1_g, w_in, gdn_conv_w, gdn_a_log, gdn_dt_bias, gdn_norm_g, s5_lambda_re, s5_lambda_im, s5_b_re, s5_b_im, s5_c_re, s5_c_im, s5_d, s5_log_step, s5_glu_w, s5_glu_b, diff_lq1, diff_lk1, diff_lq2, diff_lk2, diff_norm_g, w_out, norm2_g, peer_wq, peer_keys, peer_u, peer_v, final_g):
    p = dict(norm1_g=norm1_g, w_in=w_in, gdn_conv_w=gdn_conv_w, gdn_a_log=gdn_a_log, gdn_dt_bias=gdn_dt_bias,
             gdn_norm_g=gdn_norm_g, s5_lambda_re=s5_lambda_re, s5_lambda_im=s5_lambda_im, s5_b_re=s5_b_re,
             s5_b_im=s5_b_im, s5_c_re=s5_c_re, s5_c_im=s5_c_im, s5_d=s5_d, s5_log_step=s5_log_step,
             s5_glu_w=s5_glu_w, s5_glu_b=s5_glu_b, diff_lq1=diff_lq1, diff_lk1=diff_lk1, diff_lq2=diff_lq2,
             diff_lk2=diff_lk2, diff_norm_g=diff_norm_g, w_out=w_out, norm2_g=norm2_g, peer_wq=peer_wq,
             peer_keys=peer_keys, peer_u=peer_u, peer_v=peer_v, final_g=final_g)
    h = x
    for layer in range(DEPTH):
        h2 = _mixers(h, layer, p)
        h = _peer(h2, layer, p).reshape(x.shape)
    return _final_norm(h.reshape(-1, D_MODEL), final_g[None, :]).reshape(x.shape)
```

```python
import functools
import math

import jax
import jax.numpy as jnp
from jax import lax
from jax.experimental import pallas as pl
from jax.experimental.pallas import tpu as pltpu
from jax.experimental.pallas import tpu_sc as plsc

F32 = jnp.float32
BF16 = jnp.bfloat16
I32 = jnp.int32
U32 = jnp.uint32

D_MODEL = 1024
DEPTH = 2
GDN_HEADS = 4
GDN_DK = 64
GDN_CHUNK = 64
GDN_W = GDN_HEADS * GDN_DK
GDN_QKV_W = 3 * GDN_W
GDN_CONV = 4
S5_WIDTH = 256
S5_GROUPS = 16
S5_GROUP_CH = 16
S5_STATE = 64
S5_NS = S5_GROUPS * S5_STATE
S5_SUB = 16
DIFF_HEADS = 4
DIFF_DQK = 64
DIFF_DV = 128
DIFF_W = 512
PEER_HEADS = 8
PEER_DHALF = 64
N_KEYS = 128
N_EXPERTS = N_KEYS * N_KEYS
PEER_TOPK = 16
PEER_PAIRS = PEER_HEADS * PEER_TOPK
NORM_EPS = 1e-6
NEG_INF = float("-inf")

VMEM_LIMIT_BYTES = 56 * 1024 * 1024


def _cparams(sem, vmem=VMEM_LIMIT_BYTES):
    return pltpu.CompilerParams(dimension_semantics=sem, vmem_limit_bytes=vmem)


def _dot(a, b):
    return jnp.dot(a, b, preferred_element_type=F32)


def _dot_nt(a, b):
    return lax.dot_general(a, b, (((1,), (1,)), ((), ())), preferred_element_type=F32)


def _dot_tn(a, b):
    return lax.dot_general(a, b, (((0,), (0,)), ((), ())), preferred_element_type=F32)


def _split(x):
    hi = x.astype(BF16)
    lo = (x - hi.astype(F32)).astype(BF16)
    return hi, lo


def _dot_sel_r(x, sel):
    hi, lo = _split(x)
    return _dot(hi, sel) + _dot(lo, sel)


def _dot_sel_l(sel, x):
    hi, lo = _split(x)
    return _dot(sel, hi) + _dot(sel, lo)


def _mm3(a, b):
    ah, al = _split(a)
    bh, bl = _split(b)
    return _dot(ah, bh) + (_dot(ah, bl) + _dot(al, bh))


def _sigmoid(x):
    return 1.0 / (1.0 + jnp.exp(-x))


def _softplus(x):
    return jnp.maximum(x, 0.0) + jnp.log1p(jnp.exp(-jnp.abs(x)))


def _gelu_tanh(x):
    c = math.sqrt(2.0 / math.pi)
    return 0.5 * x * (1.0 + jnp.tanh(c * (x + 0.044715 * (x * x * x))))


INPROJ_GDN_W = GDN_QKV_W + 3 * GDN_W


def _inproj_kernel(x_ref, g_ref, w_ref, gdn_ref, us_ref, q_ref, k_ref, v_ref):
    x = x_ref[...]
    ms = jnp.mean(x * x, axis=-1, keepdims=True)
    xn = (x * lax.rsqrt(ms + NORM_EPS) * g_ref[...]).astype(BF16)
    o = INPROJ_GDN_W
    gdn_ref[...] = _dot(xn, w_ref[:, 0:o])
    us_ref[...] = _dot(xn, w_ref[:, o:o + S5_WIDTH])
    o += S5_WIDTH
    q_ref[...] = (_dot(xn, w_ref[:, o:o + DIFF_W]) * (DIFF_DQK ** -0.5)).astype(BF16)
    k_ref[...] = _dot(xn, w_ref[:, o + DIFF_W:o + 2 * DIFF_W]).astype(BF16)
    v_ref[...] = _dot(xn, w_ref[:, o + 2 * DIFF_W:o + 3 * DIFF_W]).astype(BF16)


def _inproj(x2, gain, w, tm=512):
    t = x2.shape[0]
    nw = w.shape[1]
    row = lambda i: (i, 0)
    fixed = lambda i: (0, 0)
    return pl.pallas_call(
        _inproj_kernel,
        grid=(t // tm,),
        in_specs=[pl.BlockSpec((tm, D_MODEL), row), pl.BlockSpec((1, D_MODEL), fixed),
                  pl.BlockSpec((D_MODEL, nw), fixed)],
        out_specs=[pl.BlockSpec((tm, INPROJ_GDN_W), row), pl.BlockSpec((tm, S5_WIDTH), row),
                   pl.BlockSpec((tm, DIFF_W), row), pl.BlockSpec((tm, DIFF_W), row),
                   pl.BlockSpec((tm, DIFF_W), row)],
        out_shape=[jax.ShapeDtypeStruct((t, INPROJ_GDN_W), F32), jax.ShapeDtypeStruct((t, S5_WIDTH), F32),
                   jax.ShapeDtypeStruct((t, DIFF_W), BF16), jax.ShapeDtypeStruct((t, DIFF_W), BF16),
                   jax.ShapeDtypeStruct((t, DIFF_W), BF16)],
        compiler_params=_cparams(("parallel",)),
        name="inproj",
    )(x2, gain, w)


def _gdn_kernel(blk_ref, convw_ref, alog_ref, dtb_ref, ng_ref, out_ref,
                s_ref, tail_ref, xp_ref, q_s, k_s, v_s, b_s, g_s, *, ct):
    c64 = GDN_CHUNK

    @pl.when(pl.program_id(1) == 0)
    def _():
        s_ref[...] = jnp.zeros_like(s_ref)
        tail_ref[...] = jnp.zeros_like(tail_ref)

    qkv = blk_ref[0, :, 0:GDN_QKV_W]
    xp_ref[0:8, :] = tail_ref[...]
    xp_ref[8:8 + ct, :] = qkv
    tail_ref[...] = qkv[ct - 8:ct, :]
    cw = convw_ref[...]
    y = cw[0:1, :] * xp_ref[5:5 + ct, :]
    for j in range(1, GDN_CONV):
        y = y + cw[j:j + 1, :] * xp_ref[5 + j:5 + j + ct, :]
    y = y * _sigmoid(y)

    ri = lax.broadcasted_iota(I32, (GDN_W, GDN_W), 0)
    ci = lax.broadcasted_iota(I32, (GDN_W, GDN_W), 1)
    head_ones = jnp.where((ri // c64) == (ci // c64), 1.0, 0.0).astype(BF16)

    q = y[:, 0:GDN_W]
    k = y[:, GDN_W:2 * GDN_W]
    q_s[...] = q * lax.rsqrt(_dot_sel_r(q * q, head_ones) + 1e-6) * (GDN_DK ** -0.5)
    k_s[...] = k * lax.rsqrt(_dot_sel_r(k * k, head_ones) + 1e-6)
    v_s[...] = y[:, 2 * GDN_W:3 * GDN_W]
    a_rep = blk_ref[0, :, GDN_QKV_W + GDN_W:GDN_QKV_W + 2 * GDN_W]
    b_rep = blk_ref[0, :, GDN_QKV_W + 2 * GDN_W:GDN_QKV_W + 3 * GDN_W]
    b_s[...] = _sigmoid(b_rep)
    g_raw = -jnp.exp(alog_ref[...]) * _softplus(a_rep + dtb_ref[...])
    rt = lax.broadcasted_iota(I32, (ct, ct), 0)
    ctk = lax.broadcasted_iota(I32, (ct, ct), 1)
    tri_bd = jnp.where(((rt // c64) == (ctk // c64)) & (ctk <= rt), 1.0, 0.0).astype(BF16)
    g_s[...] = _dot_sel_l(tri_bd, g_raw)

    r64 = lax.broadcasted_iota(I32, (c64, c64), 0)
    col64 = lax.broadcasted_iota(I32, (c64, c64), 1)
    incl = col64 <= r64
    strict = col64 < r64
    eye_b = col64 == r64
    eye_f = jnp.where(eye_b, 1.0, 0.0)
    ones64 = jnp.ones((c64, c64), BF16)
    ng = ng_ref[...]

    n_chunks = ct // c64
    heads = range(GDN_HEADS)
    chains = [(slice(c * c64, (c + 1) * c64), slice(h * c64, (h + 1) * c64))
              for c in range(n_chunks) for h in heads]
    each = lambda f, *ls: [f(*xs) for xs in zip(*ls)]
    gi = [g_s[r, l] for r, l in chains]
    gj = each(lambda g: _dot_sel_l(ones64, jnp.where(eye_b, g, 0.0)), gi)
    dec = each(lambda a, b: jnp.exp(jnp.where(incl, a - b, NEG_INF)), gi, gj)
    kh = [k_s[r, l] for r, l in chains]
    bi = [b_s[r, l] for r, l in chains]
    kb = each(lambda x: x.astype(BF16), kh)
    kk = each(_dot_nt, kb, kb)
    lm = each(lambda b, x, d: jnp.where(strict, b * x * d, 0.0), bi, kk, dec)
    tinv = each(lambda x: eye_f - x, lm)
    pw = lm
    for _ in range(5):
        pw = each(_mm3, pw, pw)
        tinv = each(lambda t, m: t + _mm3(t, m), tinv, pw)
    tb = each(lambda x: x.astype(BF16), tinv)
    eg = each(jnp.exp, gi)
    u = each(lambda t, rl, b: _dot(t, (v_s[rl[0], rl[1]] * b).astype(BF16)), tb, chains, bi)
    w = each(lambda t, k, b, e: _dot(t, (k * (b * e)).astype(BF16)).astype(BF16), tb, kh, bi, eg)
    qh = [q_s[r, l] for r, l in chains]
    qk = each(lambda q, k, d: jnp.where(incl, _dot_nt(q.astype(BF16), k) * d, 0.0).astype(BF16), qh, kb, dec)
    qg = each(lambda q, e: (q * e).astype(BF16), qh, eg)
    g_last = each(lambda g: g[c64 - 1:c64, :], gi)
    kg = each(lambda k, gl, g: (k * jnp.exp(gl - g)).astype(BF16), kh, g_last, gi)
    decay_last = each(jnp.exp, g_last)

    states = [s_ref[h] for h in heads]
    for c in range(n_chunks):
        ids = [c * GDN_HEADS + h for h in heads]
        sb = [s.astype(BF16) for s in states]
        vnb = [(u[n] - _dot(w[n], sb[h])).astype(BF16) for h, n in zip(heads, ids)]
        o = [_dot(qg[n], sb[h]) + _dot(qk[n], vnb[h]) for h, n in zip(heads, ids)]
        states = [states[h] * decay_last[n] + _dot_tn(kg[n], vnb[h]) for h, n in zip(heads, ids)]
        ms = [_dot_sel_r(x * x, ones64) * (1.0 / c64) for x in o]
        outs = [x * lax.rsqrt(m + NORM_EPS) * ng for x, m in zip(o, ms)]
        rows = slice(c * c64, (c + 1) * c64)
        z = blk_ref[0, rows, GDN_QKV_W:GDN_QKV_W + GDN_W]
        out_ref[0, rows, :] = (jnp.concatenate(outs, axis=1) * (z * _sigmoid(z))).astype(out_ref.dtype)
    for h in heads:
        s_ref[h] = states[h]


def _gdn(gdn_in, conv_w, a_log_rep, dtb_rep, ng, ct=256):
    bn, seq, _ = gdn_in.shape
    fixed = lambda b, l: (0, 0)
    return pl.pallas_call(
        functools.partial(_gdn_kernel, ct=ct),
        grid=(bn, seq // ct),
        in_specs=[pl.BlockSpec((1, ct, INPROJ_GDN_W), lambda b, l: (b, l, 0)),
                  pl.BlockSpec((GDN_CONV, GDN_QKV_W), fixed), pl.BlockSpec((1, GDN_W), fixed),
                  pl.BlockSpec((1, GDN_W), fixed), pl.BlockSpec((1, GDN_DK), fixed)],
        out_specs=pl.BlockSpec((1, ct, GDN_W), lambda b, l: (b, l, 0)),
        out_shape=jax.ShapeDtypeStruct((bn, seq, GDN_W), BF16),
        scratch_shapes=[pltpu.VMEM((GDN_HEADS, GDN_DK, GDN_DK), F32), pltpu.VMEM((8, GDN_QKV_W), F32),
                        pltpu.VMEM((ct + 8, GDN_QKV_W), F32)] + [pltpu.VMEM((ct, GDN_W), F32)] * 5,
        compiler_params=_cparams(("arbitrary", "arbitrary")),
        name="gdn",
    )(gdn_in, conv_w, a_log_rep, dtb_rep, ng)


def _s5_kernel(u_ref, lre_ref, lim_ref, lstep_ref, bre_ref, bim_ref, cre_ref, cim_ref, d_ref, y_ref,
               bmat, cmat, avec, carry, zr_s, zi_s, xr_s, xi_s, *, rb):
    ns = S5_NS
    cw = S5_WIDTH

    @pl.when(pl.program_id(1) == 0)
    def _():
        step = jnp.exp(lstep_ref[...])
        lr = lre_ref[...]
        li = lim_ref[...]
        mag = jnp.exp(lr * step)
        ar = mag * jnp.cos(li * step)
        ai = mag * jnp.sin(li * step)
        den = lr * lr + li * li
        mr = ((ar - 1.0) * lr + ai * li) / den
        mi = (ai * lr - (ar - 1.0) * li) / den
        bre = bre_ref[...]
        bim = bim_ref[...]
        bmat[:, 0:ns] = (mr * bre - mi * bim).astype(BF16)
        bmat[:, ns:2 * ns] = (mr * bim + mi * bre).astype(BF16)
        cmat[0:ns, :] = cre_ref[...].astype(BF16)
        cmat[ns:2 * ns, :] = (-cim_ref[...]).astype(BF16)
        avec[0:1, :] = ar
        avec[1:2, :] = ai
        pr, pi = ar, ai
        for _ in range(4):
            pr, pi = pr * pr - pi * pi, 2.0 * pr * pi
        avec[2:3, :] = pr
        avec[3:4, :] = pi
        carry[...] = jnp.zeros_like(carry)

    ar = avec[0:1, :]
    ai = avec[1:2, :]
    a16r = avec[2:3, :]
    a16i = avec[3:4, :]

    def inject(s):
        ub = u_ref[0, :, s * cw:(s + 1) * cw].astype(BF16)
        return _dot(ub, bmat[...])

    xr = jnp.zeros((rb, ns), F32)
    xi = jnp.zeros((rb, ns), F32)
    for s in range(S5_SUB):
        p = inject(s)
        xr, xi = ar * xr - ai * xi + p[:, 0:ns], ar * xi + ai * xr + p[:, ns:2 * ns]
    zr_s[...] = xr
    zi_s[...] = xi

    def row_step(kk, c):
        cr, ci_ = c
        xr_s[pl.ds(kk, 1), :] = cr
        xi_s[pl.ds(kk, 1), :] = ci_
        zr = zr_s[pl.ds(kk, 1), :]
        zi = zi_s[pl.ds(kk, 1), :]
        return (a16r * cr - a16i * ci_ + zr, a16r * ci_ + a16i * cr + zi)

    cr, ci_ = lax.fori_loop(0, rb, row_step, (carry[0:1, :], carry[1:2, :]))
    carry[0:1, :] = cr
    carry[1:2, :] = ci_

    xr = xr_s[...]
    xi = xi_s[...]
    dsk = d_ref[...]
    for s in range(S5_SUB):
        p = inject(s)
        xr, xi = ar * xr - ai * xi + p[:, 0:ns], ar * xi + ai * xr + p[:, ns:2 * ns]
        yv = _dot(xr.astype(BF16), cmat[0:ns, :]) + _dot(xi.astype(BF16), cmat[ns:2 * ns, :])
        yv = yv + dsk * u_ref[0, :, s * cw:(s + 1) * cw]
        y_ref[0, :, s * cw:(s + 1) * cw] = _gelu_tanh(yv).astype(y_ref.dtype)


def _s5(u_rows, lre, lim, lstep, bre_bd, bim_bd, cre_bd, cim_bd, dskip, rb=128):
    bn, nrows, rw = u_rows.shape
    fixed = lambda b, r: (0, 0)
    ns = S5_NS
    return pl.pallas_call(
        functools.partial(_s5_kernel, rb=rb),
        grid=(bn, nrows // rb),
        in_specs=[pl.BlockSpec((1, rb, rw), lambda b, r: (b, r, 0)),
                  pl.BlockSpec((1, ns), fixed), pl.BlockSpec((1, ns), fixed), pl.BlockSpec((1, ns), fixed),
                  pl.BlockSpec((S5_WIDTH, ns), fixed), pl.BlockSpec((S5_WIDTH, ns), fixed),
                  pl.BlockSpec((ns, S5_WIDTH), fixed), pl.BlockSpec((ns, S5_WIDTH), fixed),
                  pl.BlockSpec((1, S5_WIDTH), fixed)],
        out_specs=pl.BlockSpec((1, rb, rw), lambda b, r: (b, r, 0)),
        out_shape=jax.ShapeDtypeStruct((bn, nrows, rw), BF16),
        scratch_shapes=[pltpu.VMEM((S5_WIDTH, 2 * ns), BF16), pltpu.VMEM((2 * ns, S5_WIDTH), BF16),
                        pltpu.VMEM((8, ns), F32), pltpu.VMEM((8, ns), F32)]
                       + [pltpu.VMEM((rb, ns), F32)] * 4,
        compiler_params=_cparams(("arbitrary", "arbitrary")),
        name="s5",
    )(u_rows, lre, lim, lstep, bre_bd, bim_bd, cre_bd, cim_bd, dskip)


def _attn_kernel(q_ref, k_ref, v_ref, lq1_ref, lk1_ref, lq2_ref, lk2_ref, ng_ref, o_ref,
                 qs_s, m_s, l_s, acc_s, *, blk, lam_init):
    i = pl.program_id(2)
    q = q_ref[0]
    lane = lax.broadcasted_iota(I32, q.shape, 1)
    zero = jnp.zeros_like(q)
    qs_s[0:blk, :] = jnp.where(lane < DIFF_DQK, q, zero)
    qs_s[blk:2 * blk, :] = jnp.where(lane >= DIFF_DQK, q, zero)
    m_s[...] = jnp.full_like(m_s, NEG_INF)
    l_s[...] = jnp.zeros_like(l_s)
    acc_s[...] = jnp.zeros_like(acc_s)
    ones = jnp.ones((blk, DIFF_DV), BF16)

    def block_rows(j):
        return pl.ds(pl.multiple_of(j * blk, blk), blk)

    def scores(j):
        return _dot_nt(qs_s[...], k_ref[0, block_rows(j), :])

    def update(j, s):
        m_old = m_s[...]
        m_new = jnp.maximum(m_old, jnp.max(s, axis=-1, keepdims=True))
        p = jnp.exp(s - jnp.concatenate([m_new] * (blk // DIFF_DV), axis=1)).astype(BF16)
        alpha = jnp.exp(m_old - m_new)
        pv = _dot(p, jnp.concatenate([v_ref[0, block_rows(j), :], ones], axis=1))
        acc_s[...] = alpha * acc_s[...] + pv[:, 0:DIFF_DV]
        l_s[...] = alpha * l_s[...] + pv[:, DIFF_DV:2 * DIFF_DV]
        m_s[...] = m_new

    def body(j, s):
        s_next = scores(j + 1)
        update(j, s)
        return s_next

    s = lax.fori_loop(0, i, body, scores(0))
    row = lax.broadcasted_iota(I32, s.shape, 0) & (blk - 1)
    col = lax.broadcasted_iota(I32, s.shape, 1)
    update(i, jnp.where(col <= row, s, NEG_INF))
    lam = (jnp.exp(jnp.sum(lq1_ref[...] * lk1_ref[...], axis=-1, keepdims=True))
           - jnp.exp(jnp.sum(lq2_ref[...] * lk2_ref[...], axis=-1, keepdims=True)) + lam_init)
    o = acc_s[0:blk, :] / l_s[0:blk, :] - lam * (acc_s[blk:2 * blk, :] / l_s[blk:2 * blk, :])
    ms = jnp.mean(o * o, axis=-1, keepdims=True)
    o = o * lax.rsqrt(ms + NORM_EPS) * ng_ref[...] * (1.0 - lam_init)
    o_ref[0] = o.astype(o_ref.dtype)


def _attn(q, k, v, lq1, lk1, lq2, lk2, ng, lam_init, blk=512):
    bn, seq, _ = q.shape
    blk = min(blk, seq)
    fixed = lambda b, h, i: (0, 0)
    kv_spec = pl.BlockSpec((1, seq, DIFF_DV), lambda b, h, i: (b, 0, h))
    return pl.pallas_call(
        functools.partial(_attn_kernel, blk=blk, lam_init=lam_init),
        grid=(bn, DIFF_HEADS, seq // blk),
        in_specs=[pl.BlockSpec((1, blk, DIFF_DV), lambda b, h, i: (b, i, h)), kv_spec, kv_spec]
                 + [pl.BlockSpec((1, DIFF_DQK), fixed)] * 4 + [pl.BlockSpec((1, DIFF_DV), fixed)],
        out_specs=pl.BlockSpec((1, blk, DIFF_DV), lambda b, h, i: (b, i, h)),
        out_shape=jax.ShapeDtypeStruct((bn, seq, DIFF_W), BF16),
        scratch_shapes=[pltpu.VMEM((2 * blk, DIFF_DV), BF16), pltpu.VMEM((2 * blk, DIFF_DV), F32),
                        pltpu.VMEM((2 * blk, DIFF_DV), F32), pltpu.VMEM((2 * blk, DIFF_DV), F32)],
        compiler_params=_cparams(("parallel", "parallel", "arbitrary")),
        name="diff_attn",
    )(q, k, v, lq1, lk1, lq2, lk2, ng)


def _outproj_kernel(h_ref, oa_ref, ys_ref, oc_ref, gw_ref, gb_ref, wo_ref, out_ref):
    zg = _dot(ys_ref[...], gw_ref[...]) + gb_ref[...]
    ob = (zg[:, 0:S5_WIDTH] * _sigmoid(zg[:, S5_WIDTH:2 * S5_WIDTH])).astype(BF16)
    acc = _dot(oa_ref[...], wo_ref[0:GDN_W, :])
    acc = acc + _dot(ob, wo_ref[GDN_W:GDN_W + S5_WIDTH, :])
    acc = acc + _dot(oc_ref[...], wo_ref[GDN_W + S5_WIDTH:, :])
    out_ref[...] = h_ref[...] + acc


def _outproj(h2, oa, ys, oc, glu_w, glu_b, w_out, tm=512):
    t = h2.shape[0]
    row = lambda i: (i, 0)
    fixed = lambda i: (0, 0)
    return pl.pallas_call(
        _outproj_kernel,
        grid=(t // tm,),
        in_specs=[pl.BlockSpec((tm, D_MODEL), row), pl.BlockSpec((tm, GDN_W), row),
                  pl.BlockSpec((tm, S5_WIDTH), row), pl.BlockSpec((tm, DIFF_W), row),
                  pl.BlockSpec((S5_WIDTH, 2 * S5_WIDTH), fixed), pl.BlockSpec((1, 2 * S5_WIDTH), fixed),
                  pl.BlockSpec((D_MODEL, D_MODEL), fixed)],
        out_specs=pl.BlockSpec((tm, D_MODEL), row),
        out_shape=jax.ShapeDtypeStruct((t, D_MODEL), F32),
        compiler_params=_cparams(("parallel",)),
        name="outproj",
    )(h2, oa, ys, oc, glu_w, glu_b, w_out)


_BIG_ID = 1.0e9


def _top16(x, ids, payload):
    n = x.shape[1]
    r16 = lax.broadcasted_iota(I32, (PEER_TOPK, n), 0)
    vals = jnp.zeros((PEER_TOPK, n), F32)
    pays = jnp.zeros((PEER_TOPK, n), F32)
    for kk in range(PEER_TOPK):
        m = jnp.max(x, axis=0, keepdims=True)
        first = jnp.min(jnp.where(x == m, ids, _BIG_ID), axis=0, keepdims=True)
        hit = ids == first
        pay = first if payload is None else jnp.max(jnp.where(hit, payload, -1.0), axis=0, keepdims=True)
        x = jnp.where(hit, NEG_INF, x)
        vals = jnp.where(r16 == kk, m, vals)
        pays = jnp.where(r16 == kk, pay, pays)
    return vals, pays


def _peer_route_kernel(h_ref, g_ref, wqt_ref, keys_ref, xn_ref, exp_ref, row_ref, par_ref, gate_ref,
                       qt_s, sv_s, si_s, *, tm):
    x = h_ref[...]
    ms = jnp.mean(x * x, axis=-1, keepdims=True)
    xn = x * lax.rsqrt(ms + NORM_EPS) * g_ref[...]
    xn_ref[...] = xn
    qt_s[...] = _dot_nt(wqt_ref[...], xn.astype(BF16)).astype(BF16)

    key_id = lax.broadcasted_iota(I32, (N_KEYS, tm), 0).astype(F32)

    def half_body(hp, carry):
        r0 = pl.multiple_of(hp * PEER_DHALF, PEER_DHALF)
        s = _dot(keys_ref[hp], qt_s[pl.ds(r0, PEER_DHALF), :])
        vals, ids = _top16(s, key_id, None)
        sv_s[hp] = vals
        si_s[hp] = ids
        return carry

    lax.fori_loop(0, 2 * PEER_HEADS, half_body, 0)

    i8 = lax.broadcasted_iota(I32, (8, tm), 0).astype(F32)

    def head_body(hd, carry):
        a0 = sv_s[2 * hd]
        a1 = sv_s[2 * hd + 1]
        e0 = si_s[2 * hd] * float(N_KEYS)
        e1 = si_s[2 * hd + 1]
        cs, es, fs = [], [], []
        for i in range(8):
            cs.append(a0[i:i + 1, :] + a1[0:8, :])
            es.append(e0[i:i + 1, :] + e1[0:8, :])
            fs.append(i8 + float(i * PEER_TOPK))
        cs.append(a0[0:1, :] + a1[8:16, :])
        es.append(e0[0:1, :] + e1[8:16, :])
        fs.append(i8 + 8.0)
        cs.append(a0[8:16, :] + a1[0:1, :])
        es.append(e0[8:16, :] + e1[0:1, :])
        fs.append((i8 + 8.0) * float(PEER_TOPK))
        top_s, experts = _top16(jnp.concatenate(cs, axis=0), jnp.concatenate(fs, axis=0),
                                jnp.concatenate(es, axis=0))
        ex = jnp.exp(top_s - jnp.max(top_s, axis=0, keepdims=True))
        rows = pl.ds(pl.multiple_of(hd * PEER_TOPK, PEER_TOPK), PEER_TOPK)
        gate_ref[rows, :] = ex / jnp.sum(ex, axis=0, keepdims=True)
        expert = experts.astype(I32)
        exp_ref[rows, :] = expert
        row_ref[rows, :] = lax.shift_right_logical(expert, 1) * 8
        par_ref[rows, :] = (expert & 1).astype(F32)
        return carry

    lax.fori_loop(0, PEER_HEADS, head_body, 0)


def _peer_route(h2, gain, wqt, keys, tm=512):
    t = h2.shape[0]
    tm = min(tm, t)
    row = lambda i: (i, 0)
    col = lambda i: (0, i)
    return pl.pallas_call(
        functools.partial(_peer_route_kernel, tm=tm),
        grid=(t // tm,),
        in_specs=[pl.BlockSpec((tm, D_MODEL), row), pl.BlockSpec((1, D_MODEL), lambda i: (0, 0)),
                  pl.BlockSpec((D_MODEL, D_MODEL), lambda i: (0, 0)),
                  pl.BlockSpec((2 * PEER_HEADS, N_KEYS, PEER_DHALF), lambda i: (0, 0, 0))],
        out_specs=[pl.BlockSpec((tm, D_MODEL), row)] + [pl.BlockSpec((PEER_PAIRS, tm), col)] * 4,
        out_shape=[jax.ShapeDtypeStruct((t, D_MODEL), F32), jax.ShapeDtypeStruct((PEER_PAIRS, t), I32),
                   jax.ShapeDtypeStruct((PEER_PAIRS, t), I32), jax.ShapeDtypeStruct((PEER_PAIRS, t), F32),
                   jax.ShapeDtypeStruct((PEER_PAIRS, t), F32)],
        scratch_shapes=[pltpu.VMEM((D_MODEL, tm), BF16), pltpu.VMEM((2 * PEER_HEADS, PEER_TOPK, tm), F32),
                        pltpu.VMEM((2 * PEER_HEADS, PEER_TOPK, tm), F32)],
        compiler_params=_cparams(("parallel",)),
        name="peer_route",
    )(h2, gain, wqt, keys)


TAB_ROWS = N_EXPERTS // 2
HIGH_HALF = 0xFFFF0000


def _splat_rows(row):
    r = lax.broadcasted_iota(I32, (128, 128), 0)
    c = lax.broadcasted_iota(I32, (128, 128), 1)
    diag = jnp.where(r == c, jnp.broadcast_to(row, (128, 128)), 0.0).astype(BF16)
    return _dot(diag, jnp.ones((128, 128), BF16))


def _stage_shifts(par_ref, t, srep_s):
    shift = 16.0 - 16.0 * par_ref[pl.ds(t, 1), :]
    srep_s[...] = pltpu.bitcast(_splat_rows(shift).astype(I32), U32)


def _load_words(tab_ref, row_ref, t, p):
    return tab_ref[pl.ds(pl.multiple_of(row_ref[t, p], 8), 8), :]


def _load_expert(tab_ref, row_ref, srep_s, t, p):
    w = lax.shift_left(_load_words(tab_ref, row_ref, t, p), jnp.broadcast_to(srep_s[p:p + 1, :], (8, 128)))
    return pltpu.bitcast(w & jnp.uint32(HIGH_HALF), F32)


_TREE8_SUBLANE = (3, 7, 1, 5, 2, 6, 0, 4)


def _tree8(v, upper, bit2, bit1):
    c = []
    for a, b in ((v[0], v[1]), (v[2], v[3]), (v[4], v[5]), (v[6], v[7])):
        c.append(jnp.where(upper, b + pltpu.roll(b, 4, 0), a + pltpu.roll(a, 4, 0)))
    e = []
    for c1, c2 in ((c[0], c[1]), (c[2], c[3])):
        e.append(jnp.where(bit2, c1 + pltpu.roll(c1, 2, 0), c2 + pltpu.roll(c2, 6, 0)))
    return jnp.where(bit1, e[0] + pltpu.roll(e[0], 1, 0), e[1] + pltpu.roll(e[1], 7, 0))


PEER_U_TOKENS_PER_STEP = 2


def _peer_u_kernel(row_ref, par_ref, x_ref, tab_ref, d_ref, r_s, srep_a, srep_b, *, tb):
    sub = lax.broadcasted_iota(I32, (8, 128), 0)
    lane = lax.broadcasted_iota(I32, (8, 128), 1)
    upper = sub >= 4
    bit2 = (sub & 2) != 0
    bit1 = (sub & 1) != 0
    groups = PEER_PAIRS // 8
    nt = PEER_U_TOKENS_PER_STEP
    steps = tb // nt

    def stage(tt, srep_s):
        for j in range(nt):
            _stage_shifts(par_ref, tt * nt + j, srep_s.at[j])

    def run(tt, srep_s):
        for j in range(nt):
            t = tt * nt + j
            xv = x_ref[t]
            prod = lambda p: _load_expert(tab_ref, row_ref, srep_s.at[j], t, p) * xv
            for g in range(groups):
                r_s[t * groups + g] = _tree8([prod(g * 8 + _TREE8_SUBLANE[m]) for m in range(8)],
                                             upper, bit2, bit1)

    stage(0, srep_a)

    def two_steps(k, carry):
        stage(2 * k + 1, srep_b)
        run(2 * k, srep_a)
        stage(jnp.minimum(2 * k + 2, steps - 1), srep_a)
        run(2 * k + 1, srep_b)
        return carry

    lax.fori_loop(0, steps // 2, two_steps, 0)

    def red_body(t8, carry):
        dacc = jnp.zeros((8, 128), F32)
        for tl in range(8):
            for g in range(groups):
                d = jnp.sum(r_s[(t8 * 8 + tl) * groups + g], axis=1, keepdims=True)
                dacc = jnp.where(lane == tl * PEER_TOPK + g, d, dacc)
        d_ref[t8] = dacc
        return carry

    lax.fori_loop(0, tb // 8, red_body, 0)


def _tok_spec(tb, space=None):
    return pl.BlockSpec((tb, PEER_PAIRS), lambda i: (i, 0), memory_space=space)


def _table_spec():
    return pl.BlockSpec((TAB_ROWS * 8, 128), lambda i: (0, 0), pipeline_mode=pl.Buffered(1))


def _peer_u(rows, par, x3, tab, t, tb=128):
    return pl.pallas_call(
        functools.partial(_peer_u_kernel, tb=tb),
        grid=(t // tb,),
        in_specs=[_tok_spec(tb, pltpu.SMEM), _tok_spec(tb), pl.BlockSpec((tb, 8, 128), lambda i: (i, 0, 0)),
                  _table_spec()],
        out_specs=pl.BlockSpec((tb // 8, 8, 128), lambda i: (i, 0, 0)),
        out_shape=jax.ShapeDtypeStruct((t // 8, 8, 128), F32),
        scratch_shapes=[pltpu.VMEM((tb * (PEER_PAIRS // 8), 8, 128), F32),
                        pltpu.VMEM((PEER_U_TOKENS_PER_STEP, 128, 128), U32),
                        pltpu.VMEM((PEER_U_TOKENS_PER_STEP, 128, 128), U32)],
        compiler_params=_cparams(("arbitrary",)),
        name="peer_u",
    )(rows, par, x3, tab)


SC_CORES = 2
SC_SUBCORES = 16
SC_LANES = 16
SC_WORKERS = SC_CORES * SC_SUBCORES
PEER_U_SC_SHARE_NUM, PEER_U_SC_SHARE_DEN = 3, 8
PEER_V_SC_SHARE_NUM, PEER_V_SC_SHARE_DEN = 9, 32


def _peer_u_sc(tab, expert, xn, t0, t_sc):
    tw = t_sc // SC_WORKERS
    heads = PEER_PAIRS // PEER_TOPK
    chunks = D_MODEL // SC_LANES
    mesh = plsc.VectorSubcoreMesh(core_axis_name="c", subcore_axis_name="s")

    def body(tab_hbm, idx_hbm, x_hbm, d_hbm, idx_v, x_v, rows_a, rows_b, d_v, sem_a, sem_b):
        wid = lax.axis_index("s") * SC_CORES + lax.axis_index("c")
        lane = lax.iota(I32, SC_LANES)
        bufs = ((rows_a, sem_a), (rows_b, sem_b))

        def gather(h):
            buf, sem = bufs[h % 2]
            return pltpu.make_async_copy(tab_hbm.at[idx_v.at[pl.ds(h * PEER_TOPK, PEER_TOPK)]], buf, sem)

        def token(i, carry):
            t = t0 + wid * tw + i
            pltpu.sync_copy(idx_hbm.at[t], idx_v)
            pltpu.sync_copy(x_hbm.at[t], x_v)
            gather(0).start()
            for h in range(heads):
                if h + 1 < heads:
                    gather(h + 1).start()
                gather(h).wait()
                rows_v = bufs[h % 2][0]

                def chunk(j, accs):
                    xj = x_v[pl.ds(j * SC_LANES, SC_LANES)]
                    return tuple(a + rows_v[r, pl.ds(j * SC_LANES, SC_LANES)] * xj for r, a in enumerate(accs))

                accs = lax.fori_loop(0, chunks, chunk,
                                     tuple(jnp.zeros((SC_LANES,), F32) for _ in range(PEER_TOPK)))
                out = jnp.zeros((SC_LANES,), F32)
                for r in range(PEER_TOPK):
                    out = jnp.where(lane == r, jnp.sum(accs[r]), out)
                d_v[pl.ds(h * PEER_TOPK, PEER_TOPK)] = out
            pltpu.sync_copy(d_v, d_hbm.at[t - t0])
            return carry

        lax.fori_loop(0, tw, token, 0)

    return pl.kernel(
        body, mesh=mesh,
        out_type=jax.ShapeDtypeStruct((t_sc, PEER_PAIRS), F32),
        compiler_params=pltpu.CompilerParams(needs_layout_passes=False),
        scratch_types=[pltpu.VMEM((PEER_PAIRS,), I32), pltpu.VMEM((D_MODEL,), F32),
                       pltpu.VMEM((PEER_TOPK, D_MODEL), F32), pltpu.VMEM((PEER_TOPK, D_MODEL), F32),
                       pltpu.VMEM((PEER_PAIRS,), F32), pltpu.SemaphoreType.DMA, pltpu.SemaphoreType.DMA],
        name="peer_u_sc",
    )(tab, expert, xn)


def _peer_coef_kernel(d_ref, gate_ref, c_ref):
    c_ref[...] = gate_ref[...] * _gelu_tanh(d_ref[...])


def _peer_coef(d_tp, gate_tp, tm=2048):
    t = d_tp.shape[0]
    tm = min(tm, t)
    return pl.pallas_call(
        _peer_coef_kernel,
        grid=(t // tm,),
        in_specs=[_tok_spec(tm), _tok_spec(tm)],
        out_specs=_tok_spec(tm),
        out_shape=jax.ShapeDtypeStruct((t, PEER_PAIRS), F32),
        compiler_params=_cparams(("parallel",)),
        name="peer_coef",
    )(d_tp, gate_tp)


PEER_V_TOKENS_PER_STEP = 4
PEER_V_KDIM = (PEER_PAIRS // 2) * 16


def _peer_v_kernel(row_ref, par_ref, coef_ref, h_ref, tab_ref, out_ref, cz_s, *, tb):
    half = PEER_PAIRS // 2
    kdim = PEER_V_KDIM
    nt = PEER_V_TOKENS_PER_STEP
    pk = lax.broadcasted_iota(I32, (PEER_PAIRS, kdim), 0)
    qk = lax.shift_right_logical(lax.broadcasted_iota(I32, (PEER_PAIRS, kdim), 1), 4)
    hk = (lax.broadcasted_iota(I32, (tb, kdim), 1) & 1).astype(F32)
    coef = coef_ref[...].astype(BF16)
    par = par_ref[...].astype(BF16)
    for hf in range(2):
        expand = jnp.where(pk == qk + hf * half, 1.0, 0.0).astype(BF16)
        cz_s[hf] = jnp.where(hk == _dot(par, expand), _dot(coef, expand), 0.0)
    ks = lax.broadcasted_iota(I32, (8, kdim), 0)
    k8 = lax.broadcasted_iota(I32, (8, kdim), 1)
    own_sublane = lax.shift_right_logical(k8 & 15, 1) == ks

    def tok_body(tt, carry):
        for j in range(nt):
            t = tt * nt + j
            pieces = []
            for q in range(half):
                wa = pltpu.bitcast(_load_words(tab_ref, row_ref, t, q), BF16)
                wb = pltpu.bitcast(_load_words(tab_ref, row_ref, t, half + q), BF16)
                pieces.append(jnp.concatenate([wa, wb], axis=1))
            g = jnp.concatenate(pieces, axis=0)
            ck = jnp.concatenate(
                [jnp.where(own_sublane, jnp.broadcast_to(cz_s[hf, pl.ds(t, 1), :], (8, kdim)), 0.0)
                 for hf in range(2)], axis=0).astype(BF16)
            o = _dot(ck, g)
            out_ref[t] = h_ref[t] + (o[0:8, 0:128] + o[8:16, 128:256])
        return carry

    lax.fori_loop(0, tb // nt, tok_body, 0)


def _peer_v_sc(tab, expert, coef, h2, t0, t_sc):
    tw = t_sc // SC_WORKERS
    heads = PEER_PAIRS // PEER_TOPK
    chunks = D_MODEL // SC_LANES
    mesh = plsc.VectorSubcoreMesh(core_axis_name="c", subcore_axis_name="s")

    def body(tab_hbm, idx_hbm, coef_hbm, h_hbm, out_hbm, idx_v, c_v, acc_v, rows_a, rows_b, sem_a, sem_b):
        wid = lax.axis_index("s") * SC_CORES + lax.axis_index("c")
        lane = lax.iota(I32, SC_LANES)
        bufs = ((rows_a, sem_a), (rows_b, sem_b))

        def gather(h):
            buf, sem = bufs[h % 2]
            return pltpu.make_async_copy(tab_hbm.at[idx_v.at[pl.ds(h * PEER_TOPK, PEER_TOPK)]], buf, sem)

        def token(i, carry):
            t = t0 + wid * tw + i
            pltpu.sync_copy(idx_hbm.at[t], idx_v)
            gather(0).start()
            pltpu.sync_copy(coef_hbm.at[t], c_v)
            pltpu.sync_copy(h_hbm.at[t], acc_v)
            for h in range(heads):
                if h + 1 < heads:
                    gather(h + 1).start()
                gather(h).wait()
                rows_v = bufs[h % 2][0]
                cvec = c_v[pl.ds(h * PEER_TOPK, PEER_TOPK)]
                cs = [jnp.sum(jnp.where(lane == r, cvec, 0.0)) for r in range(PEER_TOPK)]

                def chunk(j, carry2):
                    cols = pl.ds(j * SC_LANES, SC_LANES)
                    a = acc_v[cols]
                    for r in range(PEER_TOPK):
                        a = a + cs[r] * rows_v[r, cols]
                    acc_v[cols] = a
                    return carry2

                lax.fori_loop(0, chunks, chunk, 0)
            pltpu.sync_copy(acc_v, out_hbm.at[t - t0])
            return carry

        lax.fori_loop(0, tw, token, 0)

    return pl.kernel(
        body, mesh=mesh,
        out_type=jax.ShapeDtypeStruct((t_sc, D_MODEL), F32),
        compiler_params=pltpu.CompilerParams(needs_layout_passes=False),
        scratch_types=[pltpu.VMEM((PEER_PAIRS,), I32), pltpu.VMEM((PEER_PAIRS,), F32), pltpu.VMEM((D_MODEL,), F32),
                       pltpu.VMEM((PEER_TOPK, D_MODEL), F32), pltpu.VMEM((PEER_TOPK, D_MODEL), F32),
                       pltpu.SemaphoreType.DMA, pltpu.SemaphoreType.DMA],
        name="peer_v_sc",
    )(tab, expert, coef, h2)


def _peer_v(rows, par, coef, h3, tab, t, tb=128):
    return pl.pallas_call(
        functools.partial(_peer_v_kernel, tb=tb),
        grid=(t // tb,),
        in_specs=[_tok_spec(tb, pltpu.SMEM), _tok_spec(tb), _tok_spec(tb),
                  pl.BlockSpec((tb, 8, 128), lambda i: (i, 0, 0)), _table_spec()],
        out_specs=pl.BlockSpec((tb, 8, 128), lambda i: (i, 0, 0)),
        out_shape=jax.ShapeDtypeStruct((t, 8, 128), F32),
        scratch_shapes=[pltpu.VMEM((2, tb, PEER_V_KDIM), F32)],
        compiler_params=_cparams(("arbitrary",)),
        name="peer_v",
    )(rows, par, coef, h3, tab)


def _final_norm_kernel(x_ref, g_ref, o_ref):
    x = x_ref[...]
    ms = jnp.mean(x * x, axis=-1, keepdims=True)
    o_ref[...] = x * lax.rsqrt(ms + NORM_EPS) * g_ref[...]


def _final_norm(x2, gain, tm=1024):
    t = x2.shape[0]
    tm = min(tm, t)
    row = lambda i: (i, 0)
    return pl.pallas_call(
        _final_norm_kernel,
        grid=(t // tm,),
        in_specs=[pl.BlockSpec((tm, D_MODEL), row), pl.BlockSpec((1, D_MODEL), lambda i: (0, 0))],
        out_specs=pl.BlockSpec((tm, D_MODEL), row),
        out_shape=jax.ShapeDtypeStruct((t, D_MODEL), F32),
        compiler_params=_cparams(("parallel",)),
        name="final_norm",
    )(x2, gain)


def _pack_table(tab):
    bits = lax.bitcast_convert_type(tab.astype(BF16), jnp.uint16).astype(U32).reshape(TAB_ROWS, 2, 8, 128)
    return (bits[:, 0] | (bits[:, 1] << 16)).reshape(TAB_ROWS * 8, 128)


def _peer(h2, layer, p):
    t = h2.shape[0]
    keys = p["peer_keys"][layer].reshape(2 * PEER_HEADS, N_KEYS, PEER_DHALF).astype(BF16)
    xn, expert, rows, par, gate = _peer_route(h2, p["norm2_g"][layer][None, :],
                                              p["peer_wq"][layer].T.astype(BF16), keys)
    expert, rows, par, gate = expert.T, rows.T, par.T, gate.T
    t_sc = (t * PEER_U_SC_SHARE_NUM // PEER_U_SC_SHARE_DEN) // (8 * SC_WORKERS) * (8 * SC_WORKERS)
    t_tc = t - t_sc
    d_sc = _peer_u_sc(p["peer_u"][layer], expert, xn, t_tc, t_sc)
    d = _peer_u(rows, par, xn.reshape(t, 8, 128), _pack_table(p["peer_u"][layer]), t_tc)
    d_tp = d.reshape(t_tc // 8, 8, 8, PEER_TOPK).transpose(0, 2, 3, 1).reshape(t_tc, PEER_PAIRS)
    coef = _peer_coef(jnp.concatenate([d_tp, d_sc], axis=0), gate)
    t_sc = (t * PEER_V_SC_SHARE_NUM // PEER_V_SC_SHARE_DEN) // (8 * SC_WORKERS) * (8 * SC_WORKERS)
    t_tc = t - t_sc
    out_sc = _peer_v_sc(p["peer_v"][layer], expert, coef, h2, t_tc, t_sc)
    out_tc = _peer_v(rows, par, coef, h2.reshape(t, 8, 128), _pack_table(p["peer_v"][layer]), t_tc)
    return jnp.concatenate([out_tc.reshape(t_tc, D_MODEL), out_sc], axis=0)


def _rep(x, n):
    return jnp.repeat(x, n, axis=-1)


def _inproj_weight(w_in_l):
    o = 0
    qkv = w_in_l[:, o:o + GDN_QKV_W]; o += GDN_QKV_W
    z = w_in_l[:, o:o + GDN_W]; o += GDN_W
    a = w_in_l[:, o:o + GDN_HEADS]; o += GDN_HEADS
    b = w_in_l[:, o:o + GDN_HEADS]; o += GDN_HEADS
    rest = w_in_l[:, o:]
    return jnp.concatenate([qkv, z, _rep(a, GDN_DK), _rep(b, GDN_DK), rest], axis=1).astype(BF16)


def _block_diag_gc(b_gnc):
    g = b_gnc.shape[0]
    eye = jnp.eye(g, dtype=b_gnc.dtype)
    t = jnp.swapaxes(b_gnc, 1, 2)
    return (t[:, :, None, :] * eye[:, None, :, None]).reshape(g * t.shape[1], g * t.shape[2])


def _mixers(h, layer, p):
    bn, seq, _ = h.shape
    t = bn * seq
    lam_init = 0.8 - 0.6 * math.exp(-0.3 * layer)
    gdn_in, us, qc, kc, vc = _inproj(h.reshape(t, D_MODEL), p["norm1_g"][layer][None, :],
                                     _inproj_weight(p["w_in"][layer]))
    a_log = _rep(p["gdn_a_log"][layer].astype(F32), GDN_DK)[None, :]
    dtb = _rep(p["gdn_dt_bias"][layer].astype(F32), GDN_DK)[None, :]
    o_a = _gdn(gdn_in.reshape(bn, seq, INPROJ_GDN_W), p["gdn_conv_w"][layer], a_log, dtb,
               p["gdn_norm_g"][layer][None, :])
    flat = lambda x: x.reshape(1, S5_NS)
    bre_bd = _block_diag_gc(p["s5_b_re"][layer])
    bim_bd = _block_diag_gc(p["s5_b_im"][layer])
    cre_bd = _block_diag_gc(p["s5_c_re"][layer])
    cim_bd = _block_diag_gc(p["s5_c_im"][layer])
    ys = _s5(us.reshape(bn, seq // S5_SUB, S5_SUB * S5_WIDTH),
             flat(p["s5_lambda_re"][layer]), flat(p["s5_lambda_im"][layer]),
             flat(_rep(p["s5_log_step"][layer][:, None], S5_STATE)),
             bre_bd, bim_bd, cre_bd, cim_bd, p["s5_d"][layer][None, :])
    row = lambda x: x[None, :]
    o_c = _attn(qc.reshape(bn, seq, DIFF_W), kc.reshape(bn, seq, DIFF_W), vc.reshape(bn, seq, DIFF_W),
                row(p["diff_lq1"][layer]), row(p["diff_lk1"][layer]), row(p["diff_lq2"][layer]),
                row(p["diff_lk2"][layer]), row(p["diff_norm_g"][layer]), lam_init)
    h2 = _outproj(h.reshape(t, D_MODEL), o_a.reshape(t, GDN_W), ys.reshape(t, S5_WIDTH),
                  o_c.reshape(t, DIFF_W), p["s5_glu_w"][layer].astype(BF16), p["s5_glu_b"][layer][None, :],
                  p["w_out"][layer].astype(BF16))
    return h2


def kernel(x, norm1_g, w_in, gdn_conv_w, gdn_a_log, gdn_dt_bias, gdn_norm_g, s5_lambda_re, s5_lambda_im, s5_b_re, s5_b_im, s5_c_re, s5_c_im, s5_d, s5_log_step, s5_glu_w, s5_glu_b, diff_lq1, diff_lk1, diff_lq2, diff_lk2, diff_norm_g, w_out, norm2_g, peer_wq, peer_keys, peer_u, peer_v, final_g):
    p = dict(norm1_g=norm1_g, w_in=w_in, gdn_conv_w=gdn_conv_w, gdn_a_log=gdn_a_log, gdn_dt_bias=gdn_dt_bias,
             gdn_norm_g=gdn_norm_g, s5_lambda_re=s5_lambda_re, s5_lambda_im=s5_lambda_im, s5_b_re=s5_b_re,
             s5_b_im=s5_b_im, s5_c_re=s5_c_re, s5_c_im=s5_c_im, s5_d=s5_d, s5_log_step=s5_log_step,
             s5_glu_w=s5_glu_w, s5_glu_b=s5_glu_b, diff_lq1=diff_lq1, diff_lk1=diff_lk1, diff_lq2=diff_lq2,
             diff_lk2=diff_lk2, diff_norm_g=diff_norm_g, w_out=w_out, norm2_g=norm2_g, peer_wq=peer_wq,
             peer_keys=peer_keys, peer_u=peer_u, peer_v=peer_v, final_g=final_g)
    h = x
    for layer in range(DEPTH):
        h2 = _mixers(h, layer, p)
        h = _peer(h2, layer, p).reshape(x.shape)
    return _final_norm(h.reshape(-1, D_MODEL), final_g[None, :]).reshape(x.shape)
```

```python
import functools
import math

import jax
import jax.numpy as jnp
from jax import lax
from jax.experimental import pallas as pl
from jax.experimental.pallas import tpu as pltpu
from jax.experimental.pallas import tpu_sc as plsc

F32 = jnp.float32
BF16 = jnp.bfloat16
I32 = jnp.int32
U32 = jnp.uint32

D_MODEL = 1024
DEPTH = 2
GDN_HEADS = 4
GDN_DK = 64
GDN_CHUNK = 64
GDN_W = GDN_HEADS * GDN_DK
GDN_QKV_W = 3 * GDN_W
GDN_CONV = 4
S5_WIDTH = 256
S5_GROUPS = 16
S5_GROUP_CH = 16
S5_STATE = 64
S5_NS = S5_GROUPS * S5_STATE
S5_SUB = 16
DIFF_HEADS = 4
DIFF_DQK = 64
DIFF_DV = 128
DIFF_W = 512
PEER_HEADS = 8
PEER_DHALF = 64
N_KEYS = 128
N_EXPERTS = N_KEYS * N_KEYS
PEER_TOPK = 16
PEER_PAIRS = PEER_HEADS * PEER_TOPK
NORM_EPS = 1e-6
NEG_INF = float("-inf")

VMEM_LIMIT_BYTES = 56 * 1024 * 1024


def _cparams(sem, vmem=VMEM_LIMIT_BYTES):
    return pltpu.CompilerParams(dimension_semantics=sem, vmem_limit_bytes=vmem)


def _dot(a, b):
    return jnp.dot(a, b, preferred_element_type=F32)


def _dot_nt(a, b):
    return lax.dot_general(a, b, (((1,), (1,)), ((), ())), preferred_element_type=F32)


def _dot_tn(a, b):
    return lax.dot_general(a, b, (((0,), (0,)), ((), ())), preferred_element_type=F32)


def _split(x):
    hi = x.astype(BF16)
    lo = (x - hi.astype(F32)).astype(BF16)
    return hi, lo


def _dot_sel_r(x, sel):
    hi, lo = _split(x)
    return _dot(hi, sel) + _dot(lo, sel)


def _dot_sel_l(sel, x):
    hi, lo = _split(x)
    return _dot(sel, hi) + _dot(sel, lo)


def _mm3(a, b):
    ah, al = _split(a)
    bh, bl = _split(b)
    return _dot(ah, bh) + (_dot(ah, bl) + _dot(al, bh))


def _sigmoid(x):
    return 1.0 / (1.0 + jnp.exp(-x))


def _softplus(x):
    return jnp.maximum(x, 0.0) + jnp.log1p(jnp.exp(-jnp.abs(x)))


def _gelu_tanh(x):
    c = math.sqrt(2.0 / math.pi)
    return 0.5 * x * (1.0 + jnp.tanh(c * (x + 0.044715 * (x * x * x))))


INPROJ_GDN_W = GDN_QKV_W + 3 * GDN_W


def _inproj_kernel(x_ref, g_ref, w_ref, gdn_ref, us_ref, q_ref, k_ref, v_ref):
    x = x_ref[...]
    ms = jnp.mean(x * x, axis=-1, keepdims=True)
    xn = (x * lax.rsqrt(ms + NORM_EPS) * g_ref[...]).astype(BF16)
    o = INPROJ_GDN_W
    gdn_ref[...] = _dot(xn, w_ref[:, 0:o])
    us_ref[...] = _dot(xn, w_ref[:, o:o + S5_WIDTH])
    o += S5_WIDTH
    q_ref[...] = (_dot(xn, w_ref[:, o:o + DIFF_W]) * (DIFF_DQK ** -0.5)).astype(BF16)
    k_ref[...] = _dot(xn, w_ref[:, o + DIFF_W:o + 2 * DIFF_W]).astype(BF16)
    v_ref[...] = _dot(xn, w_ref[:, o + 2 * DIFF_W:o + 3 * DIFF_W]).astype(BF16)


def _inproj(x2, gain, w, tm=512):
    t = x2.shape[0]
    nw = w.shape[1]
    row = lambda i: (i, 0)
    fixed = lambda i: (0, 0)
    return pl.pallas_call(
        _inproj_kernel,
        grid=(t // tm,),
        in_specs=[pl.BlockSpec((tm, D_MODEL), row), pl.BlockSpec((1, D_MODEL), fixed),
                  pl.BlockSpec((D_MODEL, nw), fixed)],
        out_specs=[pl.BlockSpec((tm, INPROJ_GDN_W), row), pl.BlockSpec((tm, S5_WIDTH), row),
                   pl.BlockSpec((tm, DIFF_W), row), pl.BlockSpec((tm, DIFF_W), row),
                   pl.BlockSpec((tm, DIFF_W), row)],
        out_shape=[jax.ShapeDtypeStruct((t, INPROJ_GDN_W), F32), jax.ShapeDtypeStruct((t, S5_WIDTH), F32),
                   jax.ShapeDtypeStruct((t, DIFF_W), BF16), jax.ShapeDtypeStruct((t, DIFF_W), BF16),
                   jax.ShapeDtypeStruct((t, DIFF_W), BF16)],
        compiler_params=_cparams(("parallel",)),
        name="inproj",
    )(x2, gain, w)


def _gdn_kernel(blk_ref, convw_ref, alog_ref, dtb_ref, ng_ref, out_ref,
                s_ref, tail_ref, xp_ref, q_s, k_s, v_s, b_s, g_s, *, ct):
    c64 = GDN_CHUNK

    @pl.when(pl.program_id(1) == 0)
    def _():
        s_ref[...] = jnp.zeros_like(s_ref)
        tail_ref[...] = jnp.zeros_like(tail_ref)

    qkv = blk_ref[0, :, 0:GDN_QKV_W]
    xp_ref[0:8, :] = tail_ref[...]
    xp_ref[8:8 + ct, :] = qkv
    tail_ref[...] = qkv[ct - 8:ct, :]
    cw = convw_ref[...]
    y = cw[0:1, :] * xp_ref[5:5 + ct, :]
    for j in range(1, GDN_CONV):
        y = y + cw[j:j + 1, :] * xp_ref[5 + j:5 + j + ct, :]
    y = y * _sigmoid(y)

    ri = lax.broadcasted_iota(I32, (GDN_W, GDN_W), 0)
    ci = lax.broadcasted_iota(I32, (GDN_W, GDN_W), 1)
    head_ones = jnp.where((ri // c64) == (ci // c64), 1.0, 0.0).astype(BF16)

    q = y[:, 0:GDN_W]
    k = y[:, GDN_W:2 * GDN_W]
    q_s[...] = q * lax.rsqrt(_dot_sel_r(q * q, head_ones) + 1e-6) * (GDN_DK ** -0.5)
    k_s[...] = k * lax.rsqrt(_dot_sel_r(k * k, head_ones) + 1e-6)
    v_s[...] = y[:, 2 * GDN_W:3 * GDN_W]
    a_rep = blk_ref[0, :, GDN_QKV_W + GDN_W:GDN_QKV_W + 2 * GDN_W]
    b_rep = blk_ref[0, :, GDN_QKV_W + 2 * GDN_W:GDN_QKV_W + 3 * GDN_W]
    b_s[...] = _sigmoid(b_rep)
    g_raw = -jnp.exp(alog_ref[...]) * _softplus(a_rep + dtb_ref[...])
    rt = lax.broadcasted_iota(I32, (ct, ct), 0)
    ctk = lax.broadcasted_iota(I32, (ct, ct), 1)
    tri_bd = jnp.where(((rt // c64) == (ctk // c64)) & (ctk <= rt), 1.0, 0.0).astype(BF16)
    g_s[...] = _dot_sel_l(tri_bd, g_raw)

    r64 = lax.broadcasted_iota(I32, (c64, c64), 0)
    col64 = lax.broadcasted_iota(I32, (c64, c64), 1)
    incl = col64 <= r64
    strict = col64 < r64
    eye_b = col64 == r64
    eye_f = jnp.where(eye_b, 1.0, 0.0)
    ones64 = jnp.ones((c64, c64), BF16)
    ng = ng_ref[...]

    n_chunks = ct // c64
    heads = range(GDN_HEADS)
    chains = [(slice(c * c64, (c + 1) * c64), slice(h * c64, (h + 1) * c64))
              for c in range(n_chunks) for h in heads]
    each = lambda f, *ls: [f(*xs) for xs in zip(*ls)]
    gi = [g_s[r, l] for r, l in chains]
    gj = each(lambda g: _dot_sel_l(ones64, jnp.where(eye_b, g, 0.0)), gi)
    dec = each(lambda a, b: jnp.exp(jnp.where(incl, a - b, NEG_INF)), gi, gj)
    kh = [k_s[r, l] for r, l in chains]
    bi = [b_s[r, l] for r, l in chains]
    kb = each(lambda x: x.astype(BF16), kh)
    kk = each(_dot_nt, kb, kb)
    lm = each(lambda b, x, d: jnp.where(strict, b * x * d, 0.0), bi, kk, dec)
    tinv = each(lambda x: eye_f - x, lm)
    pw = lm
    for _ in range(5):
        pw = each(_mm3, pw, pw)
        tinv = each(lambda t, m: t + _mm3(t, m), tinv, pw)
    tb = each(lambda x: x.astype(BF16), tinv)
    eg = each(jnp.exp, gi)
    u = each(lambda t, rl, b: _dot(t, (v_s[rl[0], rl[1]] * b).astype(BF16)), tb, chains, bi)
    w = each(lambda t, k, b, e: _dot(t, (k * (b * e)).astype(BF16)).astype(BF16), tb, kh, bi, eg)
    qh = [q_s[r, l] for r, l in chains]
    qk = each(lambda q, k, d: jnp.where(incl, _dot_nt(q.astype(BF16), k) * d, 0.0).astype(BF16), qh, kb, dec)
    qg = each(lambda q, e: (q * e).astype(BF16), qh, eg)
    g_last = each(lambda g: g[c64 - 1:c64, :], gi)
    kg = each(lambda k, gl, g: (k * jnp.exp(gl - g)).astype(BF16), kh, g_last, gi)
    decay_last = each(jnp.exp, g_last)

    states = [s_ref[h] for h in heads]
    for c in range(n_chunks):
        ids = [c * GDN_HEADS + h for h in heads]
        sb = [s.astype(BF16) for s in states]
        vnb = [(u[n] - _dot(w[n], sb[h])).astype(BF16) for h, n in zip(heads, ids)]
        o = [_dot(qg[n], sb[h]) + _dot(qk[n], vnb[h]) for h, n in zip(heads, ids)]
        states = [states[h] * decay_last[n] + _dot_tn(kg[n], vnb[h]) for h, n in zip(heads, ids)]
        ms = [_dot_sel_r(x * x, ones64) * (1.0 / c64) for x in o]
        outs = [x * lax.rsqrt(m + NORM_EPS) * ng for x, m in zip(o, ms)]
        rows = slice(c * c64, (c + 1) * c64)
        z = blk_ref[0, rows, GDN_QKV_W:GDN_QKV_W + GDN_W]
        out_ref[0, rows, :] = (jnp.concatenate(outs, axis=1) * (z * _sigmoid(z))).astype(out_ref.dtype)
    for h in heads:
        s_ref[h] = states[h]


def _gdn(gdn_in, conv_w, a_log_rep, dtb_rep, ng, ct=256):
    bn, seq, _ = gdn_in.shape
    fixed = lambda b, l: (0, 0)
    return pl.pallas_call(
        functools.partial(_gdn_kernel, ct=ct),
        grid=(bn, seq // ct),
        in_specs=[pl.BlockSpec((1, ct, INPROJ_GDN_W), lambda b, l: (b, l, 0)),
                  pl.BlockSpec((GDN_CONV, GDN_QKV_W), fixed), pl.BlockSpec((1, GDN_W), fixed),
                  pl.BlockSpec((1, GDN_W), fixed), pl.BlockSpec((1, GDN_DK), fixed)],
        out_specs=pl.BlockSpec((1, ct, GDN_W), lambda b, l: (b, l, 0)),
        out_shape=jax.ShapeDtypeStruct((bn, seq, GDN_W), BF16),
        scratch_shapes=[pltpu.VMEM((GDN_HEADS, GDN_DK, GDN_DK), F32), pltpu.VMEM((8, GDN_QKV_W), F32),
                        pltpu.VMEM((ct + 8, GDN_QKV_W), F32)] + [pltpu.VMEM((ct, GDN_W), F32)] * 5,
        compiler_params=_cparams(("arbitrary", "arbitrary")),
        name="gdn",
    )(gdn_in, conv_w, a_log_rep, dtb_rep, ng)


def _s5_kernel(u_ref, lre_ref, lim_ref, lstep_ref, bre_ref, bim_ref, cre_ref, cim_ref, d_ref, y_ref,
               bmat, cmat, avec, carry, zr_s, zi_s, xr_s, xi_s, *, rb):
    ns = S5_NS
    cw = S5_WIDTH

    @pl.when(pl.program_id(1) == 0)
    def _():
        step = jnp.exp(lstep_ref[...])
        lr = lre_ref[...]
        li = lim_ref[...]
        mag = jnp.exp(lr * step)
        ar = mag * jnp.cos(li * step)
        ai = mag * jnp.sin(li * step)
        den = lr * lr + li * li
        mr = ((ar - 1.0) * lr + ai * li) / den
        mi = (ai * lr - (ar - 1.0) * li) / den
        bre = bre_ref[...]
        bim = bim_ref[...]
        bmat[:, 0:ns] = (mr * bre - mi * bim).astype(BF16)
        bmat[:, ns:2 * ns] = (mr * bim + mi * bre).astype(BF16)
        cmat[0:ns, :] = cre_ref[...].astype(BF16)
        cmat[ns:2 * ns, :] = (-cim_ref[...]).astype(BF16)
        avec[0:1, :] = ar
        avec[1:2, :] = ai
        pr, pi = ar, ai
        for _ in range(4):
            pr, pi = pr * pr - pi * pi, 2.0 * pr * pi
        avec[2:3, :] = pr
        avec[3:4, :] = pi
        carry[...] = jnp.zeros_like(carry)

    ar = avec[0:1, :]
    ai = avec[1:2, :]
    a16r = avec[2:3, :]
    a16i = avec[3:4, :]

    def inject(s):
        ub = u_ref[0, :, s * cw:(s + 1) * cw].astype(BF16)
        return _dot(ub, bmat[...])

    xr = jnp.zeros((rb, ns), F32)
    xi = jnp.zeros((rb, ns), F32)
    for s in range(S5_SUB):
        p = inject(s)
        xr, xi = ar * xr - ai * xi + p[:, 0:ns], ar * xi + ai * xr + p[:, ns:2 * ns]
    zr_s[...] = xr
    zi_s[...] = xi

    def row_step(kk, c):
        cr, ci_ = c
        xr_s[pl.ds(kk, 1), :] = cr
        xi_s[pl.ds(kk, 1), :] = ci_
        zr = zr_s[pl.ds(kk, 1), :]
        zi = zi_s[pl.ds(kk, 1), :]
        return (a16r * cr - a16i * ci_ + zr, a16r * ci_ + a16i * cr + zi)

    cr, ci_ = lax.fori_loop(0, rb, row_step, (carry[0:1, :], carry[1:2, :]))
    carry[0:1, :] = cr
    carry[1:2, :] = ci_

    xr = xr_s[...]
    xi = xi_s[...]
    dsk = d_ref[...]
    for s in range(S5_SUB):
        p = inject(s)
        xr, xi = ar * xr - ai * xi + p[:, 0:ns], ar * xi + ai * xr + p[:, ns:2 * ns]
        yv = _dot(xr.astype(BF16), cmat[0:ns, :]) + _dot(xi.astype(BF16), cmat[ns:2 * ns, :])
        yv = yv + dsk * u_ref[0, :, s * cw:(s + 1) * cw]
        y_ref[0, :, s * cw:(s + 1) * cw] = _gelu_tanh(yv).astype(y_ref.dtype)


def _s5(u_rows, lre, lim, lstep, bre_bd, bim_bd, cre_bd, cim_bd, dskip, rb=128):
    bn, nrows, rw = u_rows.shape
    fixed = lambda b, r: (0, 0)
    ns = S5_NS
    return pl.pallas_call(
        functools.partial(_s5_kernel, rb=rb),
        grid=(bn, nrows // rb),
        in_specs=[pl.BlockSpec((1, rb, rw), lambda b, r: (b, r, 0)),
                  pl.BlockSpec((1, ns), fixed), pl.BlockSpec((1, ns), fixed), pl.BlockSpec((1, ns), fixed),
                  pl.BlockSpec((S5_WIDTH, ns), fixed), pl.BlockSpec((S5_WIDTH, ns), fixed),
                  pl.BlockSpec((ns, S5_WIDTH), fixed), pl.BlockSpec((ns, S5_WIDTH), fixed),
                  pl.BlockSpec((1, S5_WIDTH), fixed)],
        out_specs=pl.BlockSpec((1, rb, rw), lambda b, r: (b, r, 0)),
        out_shape=jax.ShapeDtypeStruct((bn, nrows, rw), BF16),
        scratch_shapes=[pltpu.VMEM((S5_WIDTH, 2 * ns), BF16), pltpu.VMEM((2 * ns, S5_WIDTH), BF16),
                        pltpu.VMEM((8, ns), F32), pltpu.VMEM((8, ns), F32)]
                       + [pltpu.VMEM((rb, ns), F32)] * 4,
        compiler_params=_cparams(("arbitrary", "arbitrary")),
        name="s5",
    )(u_rows, lre, lim, lstep, bre_bd, bim_bd, cre_bd, cim_bd, dskip)


def _attn_kernel(q_ref, k_ref, v_ref, lq1_ref, lk1_ref, lq2_ref, lk2_ref, ng_ref, o_ref,
                 qs_s, m_s, l_s, acc_s, *, blk, lam_init):
    i = pl.program_id(2)
    q = q_ref[0]
    lane = lax.broadcasted_iota(I32, q.shape, 1)
    zero = jnp.zeros_like(q)
    qs_s[0:blk, :] = jnp.where(lane < DIFF_DQK, q, zero)
    qs_s[blk:2 * blk, :] = jnp.where(lane >= DIFF_DQK, q, zero)
    m_s[...] = jnp.full_like(m_s, NEG_INF)
    l_s[...] = jnp.zeros_like(l_s)
    acc_s[...] = jnp.zeros_like(acc_s)
    ones = jnp.ones((blk, DIFF_DV), BF16)

    def block_rows(j):
        return pl.ds(pl.multiple_of(j * blk, blk), blk)

    def scores(j):
        return _dot_nt(qs_s[...], k_ref[0, block_rows(j), :])

    def update(j, s):
        m_old = m_s[...]
        m_new = jnp.maximum(m_old, jnp.max(s, axis=-1, keepdims=True))
        p = jnp.exp(s - jnp.concatenate([m_new] * (blk // DIFF_DV), axis=1)).astype(BF16)
        alpha = jnp.exp(m_old - m_new)
        pv = _dot(p, jnp.concatenate([v_ref[0, block_rows(j), :], ones], axis=1))
        acc_s[...] = alpha * acc_s[...] + pv[:, 0:DIFF_DV]
        l_s[...] = alpha * l_s[...] + pv[:, DIFF_DV:2 * DIFF_DV]
        m_s[...] = m_new

    def body(j, s):
        s_next = scores(j + 1)
        update(j, s)
        return s_next

    s = lax.fori_loop(0, i, body, scores(0))
    row = lax.broadcasted_iota(I32, s.shape, 0) & (blk - 1)
    col = lax.broadcasted_iota(I32, s.shape, 1)
    update(i, jnp.where(col <= row, s, NEG_INF))
    lam = (jnp.exp(jnp.sum(lq1_ref[...] * lk1_ref[...], axis=-1, keepdims=True))
           - jnp.exp(jnp.sum(lq2_ref[...] * lk2_ref[...], axis=-1, keepdims=True)) + lam_init)
    o = acc_s[0:blk, :] / l_s[0:blk, :] - lam * (acc_s[blk:2 * blk, :] / l_s[blk:2 * blk, :])
    ms = jnp.mean(o * o, axis=-1, keepdims=True)
    o = o * lax.rsqrt(ms + NORM_EPS) * ng_ref[...] * (1.0 - lam_init)
    o_ref[0] = o.astype(o_ref.dtype)


def _attn(q, k, v, lq1, lk1, lq2, lk2, ng, lam_init, blk=512):
    bn, seq, _ = q.shape
    blk = min(blk, seq)
    fixed = lambda b, h, i: (0, 0)
    kv_spec = pl.BlockSpec((1, seq, DIFF_DV), lambda b, h, i: (b, 0, h))
    return pl.pallas_call(
        functools.partial(_attn_kernel, blk=blk, lam_init=lam_init),
        grid=(bn, DIFF_HEADS, seq // blk),
        in_specs=[pl.BlockSpec((1, blk, DIFF_DV), lambda b, h, i: (b, i, h)), kv_spec, kv_spec]
                 + [pl.BlockSpec((1, DIFF_DQK), fixed)] * 4 + [pl.BlockSpec((1, DIFF_DV), fixed)],
        out_specs=pl.BlockSpec((1, blk, DIFF_DV), lambda b, h, i: (b, i, h)),
        out_shape=jax.ShapeDtypeStruct((bn, seq, DIFF_W), BF16),
        scratch_shapes=[pltpu.VMEM((2 * blk, DIFF_DV), BF16), pltpu.VMEM((2 * blk, DIFF_DV), F32),
                        pltpu.VMEM((2 * blk, DIFF_DV), F32), pltpu.VMEM((2 * blk, DIFF_DV), F32)],
        compiler_params=_cparams(("parallel", "parallel", "arbitrary")),
        name="diff_attn",
    )(q, k, v, lq1, lk1, lq2, lk2, ng)


def _outproj_kernel(h_ref, oa_ref, ys_ref, oc_ref, gw_ref, gb_ref, wo_ref, out_ref):
    zg = _dot(ys_ref[...], gw_ref[...]) + gb_ref[...]
    ob = (zg[:, 0:S5_WIDTH] * _sigmoid(zg[:, S5_WIDTH:2 * S5_WIDTH])).astype(BF16)
    acc = _dot(oa_ref[...], wo_ref[0:GDN_W, :])
    acc = acc + _dot(ob, wo_ref[GDN_W:GDN_W + S5_WIDTH, :])
    acc = acc + _dot(oc_ref[...], wo_ref[GDN_W + S5_WIDTH:, :])
    out_ref[...] = h_ref[...] + acc


def _outproj(h2, oa, ys, oc, glu_w, glu_b, w_out, tm=512):
    t = h2.shape[0]
    row = lambda i: (i, 0)
    fixed = lambda i: (0, 0)
    return pl.pallas_call(
        _outproj_kernel,
        grid=(t // tm,),
        in_specs=[pl.BlockSpec((tm, D_MODEL), row), pl.BlockSpec((tm, GDN_W), row),
                  pl.BlockSpec((tm, S5_WIDTH), row), pl.BlockSpec((tm, DIFF_W), row),
                  pl.BlockSpec((S5_WIDTH, 2 * S5_WIDTH), fixed), pl.BlockSpec((1, 2 * S5_WIDTH), fixed),
                  pl.BlockSpec((D_MODEL, D_MODEL), fixed)],
        out_specs=pl.BlockSpec((tm, D_MODEL), row),
        out_shape=jax.ShapeDtypeStruct((t, D_MODEL), F32),
        compiler_params=_cparams(("parallel",)),
        name="outproj",
    )(h2, oa, ys, oc, glu_w, glu_b, w_out)


_BIG_ID = 1.0e9


def _top16(x, ids, payload):
    n = x.shape[1]
    r16 = lax.broadcasted_iota(I32, (PEER_TOPK, n), 0)
    vals = jnp.zeros((PEER_TOPK, n), F32)
    pays = jnp.zeros((PEER_TOPK, n), F32)
    for kk in range(PEER_TOPK):
        m = jnp.max(x, axis=0, keepdims=True)
        first = jnp.min(jnp.where(x == m, ids, _BIG_ID), axis=0, keepdims=True)
        hit = ids == first
        pay = first if payload is None else jnp.max(jnp.where(hit, payload, -1.0), axis=0, keepdims=True)
        x = jnp.where(hit, NEG_INF, x)
        vals = jnp.where(r16 == kk, m, vals)
        pays = jnp.where(r16 == kk, pay, pays)
    return vals, pays


def _peer_route_kernel(h_ref, g_ref, wqt_ref, keys_ref, xn_ref, exp_ref, row_ref, par_ref, gate_ref,
                       qt_s, sv_s, si_s, *, tm):
    x = h_ref[...]
    ms = jnp.mean(x * x, axis=-1, keepdims=True)
    xn = x * lax.rsqrt(ms + NORM_EPS) * g_ref[...]
    xn_ref[...] = xn
    qt_s[...] = _dot_nt(wqt_ref[...], xn.astype(BF16)).astype(BF16)

    key_id = lax.broadcasted_iota(I32, (N_KEYS, tm), 0).astype(F32)

    def half_body(hp, carry):
        r0 = pl.multiple_of(hp * PEER_DHALF, PEER_DHALF)
        s = _dot(keys_ref[hp], qt_s[pl.ds(r0, PEER_DHALF), :])
        vals, ids = _top16(s, key_id, None)
        sv_s[hp] = vals
        si_s[hp] = ids
        return carry

    lax.fori_loop(0, 2 * PEER_HEADS, half_body, 0)

    i8 = lax.broadcasted_iota(I32, (8, tm), 0).astype(F32)

    def head_body(hd, carry):
        a0 = sv_s[2 * hd]
        a1 = sv_s[2 * hd + 1]
        e0 = si_s[2 * hd] * float(N_KEYS)
        e1 = si_s[2 * hd + 1]
        cs, es, fs = [], [], []
        for i in range(8):
            cs.append(a0[i:i + 1, :] + a1[0:8, :])
            es.append(e0[i:i + 1, :] + e1[0:8, :])
            fs.append(i8 + float(i * PEER_TOPK))
        cs.append(a0[0:1, :] + a1[8:16, :])
        es.append(e0[0:1, :] + e1[8:16, :])
        fs.append(i8 + 8.0)
        cs.append(a0[8:16, :] + a1[0:1, :])
        es.append(e0[8:16, :] + e1[0:1, :])
        fs.append((i8 + 8.0) * float(PEER_TOPK))
        top_s, experts = _top16(jnp.concatenate(cs, axis=0), jnp.concatenate(fs, axis=0),
                                jnp.concatenate(es, axis=0))
        ex = jnp.exp(top_s - jnp.max(top_s, axis=0, keepdims=True))
        rows = pl.ds(pl.multiple_of(hd * PEER_TOPK, PEER_TOPK), PEER_TOPK)
        gate_ref[rows, :] = ex / jnp.sum(ex, axis=0, keepdims=True)
        expert = experts.astype(I32)
        exp_ref[rows, :] = expert
        row_ref[rows, :] = lax.shift_right_logical(expert, 1) * 8
        par_ref[rows, :] = (expert & 1).astype(F32)
        return carry

    lax.fori_loop(0, PEER_HEADS, head_body, 0)


def _peer_route(h2, gain, wqt, keys, tm=512):
    t = h2.shape[0]
    tm = min(tm, t)
    row = lambda i: (i, 0)
    col = lambda i: (0, i)
    return pl.pallas_call(
        functools.partial(_peer_route_kernel, tm=tm),
        grid=(t // tm,),
        in_specs=[pl.BlockSpec((tm, D_MODEL), row), pl.BlockSpec((1, D_MODEL), lambda i: (0, 0)),
                  pl.BlockSpec((D_MODEL, D_MODEL), lambda i: (0, 0)),
                  pl.BlockSpec((2 * PEER_HEADS, N_KEYS, PEER_DHALF), lambda i: (0, 0, 0))],
        out_specs=[pl.BlockSpec((tm, D_MODEL), row)] + [pl.BlockSpec((PEER_PAIRS, tm), col)] * 4,
        out_shape=[jax.ShapeDtypeStruct((t, D_MODEL), F32), jax.ShapeDtypeStruct((PEER_PAIRS, t), I32),
                   jax.ShapeDtypeStruct((PEER_PAIRS, t), I32), jax.ShapeDtypeStruct((PEER_PAIRS, t), F32),
                   jax.ShapeDtypeStruct((PEER_PAIRS, t), F32)],
        scratch_shapes=[pltpu.VMEM((D_MODEL, tm), BF16), pltpu.VMEM((2 * PEER_HEADS, PEER_TOPK, tm), F32),
                        pltpu.VMEM((2 * PEER_HEADS, PEER_TOPK, tm), F32)],
        compiler_params=_cparams(("parallel",)),
        name="peer_route",
    )(h2, gain, wqt, keys)


TAB_ROWS = N_EXPERTS // 2
HIGH_HALF = 0xFFFF0000


def _splat_rows(row):
    r = lax.broadcasted_iota(I32, (128, 128), 0)
    c = lax.broadcasted_iota(I32, (128, 128), 1)
    diag = jnp.where(r == c, jnp.broadcast_to(row, (128, 128)), 0.0).astype(BF16)
    return _dot(diag, jnp.ones((128, 128), BF16))


def _stage_shifts(par_ref, t, srep_s):
    shift = 16.0 - 16.0 * par_ref[pl.ds(t, 1), :]
    srep_s[...] = pltpu.bitcast(_splat_rows(shift).astype(I32), U32)


def _token_tile(block, tl):
    return jnp.concatenate([block[tl:tl + 1, s * 128:(s + 1) * 128] for s in range(8)], axis=0)


def _tiles_to_rows(tiles):
    return jnp.concatenate([jnp.concatenate([tile[s:s + 1, :] for tile in tiles], axis=0) for s in range(8)], axis=1)


def _load_words(tab_ref, row_ref, t, p):
    return tab_ref[pl.ds(pl.multiple_of(row_ref[t, p], 8), 8), :]


def _load_expert(tab_ref, row_ref, srep_s, t, p):
    w = lax.shift_left(_load_words(tab_ref, row_ref, t, p), jnp.broadcast_to(srep_s[p:p + 1, :], (8, 128)))
    return pltpu.bitcast(w & jnp.uint32(HIGH_HALF), F32)


_TREE8_SUBLANE = (3, 7, 1, 5, 2, 6, 0, 4)


def _tree8(v, upper, bit2, bit1):
    c = []
    for a, b in ((v[0], v[1]), (v[2], v[3]), (v[4], v[5]), (v[6], v[7])):
        c.append(jnp.where(upper, b + pltpu.roll(b, 4, 0), a + pltpu.roll(a, 4, 0)))
    e = []
    for c1, c2 in ((c[0], c[1]), (c[2], c[3])):
        e.append(jnp.where(bit2, c1 + pltpu.roll(c1, 2, 0), c2 + pltpu.roll(c2, 6, 0)))
    return jnp.where(bit1, e[0] + pltpu.roll(e[0], 1, 0), e[1] + pltpu.roll(e[1], 7, 0))


PEER_U_TOKENS_PER_STEP = 2


def _peer_u_kernel(row_ref, par_ref, x_ref, tab_ref, d_ref, r_s, srep_a, srep_b, *, tb):
    sub = lax.broadcasted_iota(I32, (8, 128), 0)
    lane = lax.broadcasted_iota(I32, (8, 128), 1)
    upper = sub >= 4
    bit2 = (sub & 2) != 0
    bit1 = (sub & 1) != 0
    groups = PEER_PAIRS // 8
    nt = PEER_U_TOKENS_PER_STEP
    steps = tb // nt

    def stage(tt, srep_s):
        for j in range(nt):
            _stage_shifts(par_ref, tt * nt + j, srep_s.at[j])

    def run(tt, srep_s, xb, first):
        for j in range(nt):
            t = tt * nt + j
            xv = _token_tile(xb, first + j)
            prod = lambda p: _load_expert(tab_ref, row_ref, srep_s.at[j], t, p) * xv
            for g in range(groups):
                r_s[t * groups + g] = _tree8([prod(g * 8 + _TREE8_SUBLANE[m]) for m in range(8)],
                                             upper, bit2, bit1)

    stage(0, srep_a)
    steps_per_body = 8 // nt

    def eight_tokens(k, carry):
        xb = x_ref[pl.ds(pl.multiple_of(k * 8, 8), 8), :]
        for i in range(steps_per_body):
            tt = k * steps_per_body + i
            cur, nxt = (srep_a, srep_b) if i % 2 == 0 else (srep_b, srep_a)
            stage(jnp.minimum(tt + 1, steps - 1), nxt)
            run(tt, cur, xb, i * nt)
        return carry

    lax.fori_loop(0, tb // 8, eight_tokens, 0)

    def red_body(t8, carry):
        dacc = jnp.zeros((8, 128), F32)
        for tl in range(8):
            for g in range(groups):
                d = jnp.sum(r_s[(t8 * 8 + tl) * groups + g], axis=1, keepdims=True)
                dacc = jnp.where(lane == tl * PEER_TOPK + g, d, dacc)
        d_ref[t8] = dacc
        return carry

    lax.fori_loop(0, tb // 8, red_body, 0)


def _tok_spec(tb, space=None):
    return pl.BlockSpec((tb, PEER_PAIRS), lambda i: (i, 0), memory_space=space)


def _table_spec():
    return pl.BlockSpec((TAB_ROWS * 8, 128), lambda i: (0, 0), pipeline_mode=pl.Buffered(1))


def _peer_u(rows, par, x2, tab, t, tb=128):
    return pl.pallas_call(
        functools.partial(_peer_u_kernel, tb=tb),
        grid=(t // tb,),
        in_specs=[_tok_spec(tb, pltpu.SMEM), _tok_spec(tb), pl.BlockSpec((tb, D_MODEL), lambda i: (i, 0)),
                  _table_spec()],
        out_specs=pl.BlockSpec((tb // 8, 8, 128), lambda i: (i, 0, 0)),
        out_shape=jax.ShapeDtypeStruct((t // 8, 8, 128), F32),
        scratch_shapes=[pltpu.VMEM((tb * (PEER_PAIRS // 8), 8, 128), F32),
                        pltpu.VMEM((PEER_U_TOKENS_PER_STEP, 128, 128), U32),
                        pltpu.VMEM((PEER_U_TOKENS_PER_STEP, 128, 128), U32)],
        compiler_params=_cparams(("arbitrary",)),
        name="peer_u",
    )(rows, par, x2, tab)


SC_CORES = 2
SC_SUBCORES = 16
SC_LANES = 16
SC_WORKERS = SC_CORES * SC_SUBCORES
PEER_U_SC_SHARE_NUM, PEER_U_SC_SHARE_DEN = 43, 128
PEER_V_SC_SHARE_NUM, PEER_V_SC_SHARE_DEN = 29, 128


def _peer_u_sc(tab, expert, xn, t0, t_sc):
    tw = t_sc // SC_WORKERS
    heads = PEER_PAIRS // PEER_TOPK
    chunks = D_MODEL // SC_LANES
    mesh = plsc.VectorSubcoreMesh(core_axis_name="c", subcore_axis_name="s")

    def body(tab_hbm, idx_hbm, x_hbm, d_hbm, idx_v, x_v, rows_a, rows_b, d_v, sem_a, sem_b):
        wid = lax.axis_index("s") * SC_CORES + lax.axis_index("c")
        lane = lax.iota(I32, SC_LANES)
        bufs = ((rows_a, sem_a), (rows_b, sem_b))

        def gather(h):
            buf, sem = bufs[h % 2]
            return pltpu.make_async_copy(tab_hbm.at[idx_v.at[pl.ds(h * PEER_TOPK, PEER_TOPK)]], buf, sem)

        def token(i, carry):
            t = t0 + wid * tw + i
            pltpu.sync_copy(idx_hbm.at[t], idx_v)
            pltpu.sync_copy(x_hbm.at[t], x_v)
            gather(0).start()
            for h in range(heads):
                if h + 1 < heads:
                    gather(h + 1).start()
                gather(h).wait()
                rows_v = bufs[h % 2][0]

                def chunk(j, accs):
                    xj = x_v[pl.ds(j * SC_LANES, SC_LANES)]
                    return tuple(a + rows_v[r, pl.ds(j * SC_LANES, SC_LANES)] * xj for r, a in enumerate(accs))

                accs = lax.fori_loop(0, chunks, chunk,
                                     tuple(jnp.zeros((SC_LANES,), F32) for _ in range(PEER_TOPK)))
                out = jnp.zeros((SC_LANES,), F32)
                for r in range(PEER_TOPK):
                    out = jnp.where(lane == r, jnp.sum(accs[r]), out)
                d_v[pl.ds(h * PEER_TOPK, PEER_TOPK)] = out
            pltpu.sync_copy(d_v, d_hbm.at[t - t0])
            return carry

        lax.fori_loop(0, tw, token, 0)

    return pl.kernel(
        body, mesh=mesh,
        out_type=jax.ShapeDtypeStruct((t_sc, PEER_PAIRS), F32),
        compiler_params=pltpu.CompilerParams(needs_layout_passes=False),
        scratch_types=[pltpu.VMEM((PEER_PAIRS,), I32), pltpu.VMEM((D_MODEL,), F32),
                       pltpu.VMEM((PEER_TOPK, D_MODEL), F32), pltpu.VMEM((PEER_TOPK, D_MODEL), F32),
                       pltpu.VMEM((PEER_PAIRS,), F32), pltpu.SemaphoreType.DMA, pltpu.SemaphoreType.DMA],
        name="peer_u_sc",
    )(tab, expert, xn)


def _peer_coef_kernel(d_ref, gate_ref, c_ref):
    c_ref[...] = gate_ref[...] * _gelu_tanh(d_ref[...])


def _peer_coef(d_tp, gate_tp, tm=2048):
    t = d_tp.shape[0]
    tm = min(tm, t)
    return pl.pallas_call(
        _peer_coef_kernel,
        grid=(t // tm,),
        in_specs=[_tok_spec(tm), _tok_spec(tm)],
        out_specs=_tok_spec(tm),
        out_shape=jax.ShapeDtypeStruct((t, PEER_PAIRS), F32),
        compiler_params=_cparams(("parallel",)),
        name="peer_coef",
    )(d_tp, gate_tp)


PEER_V_TOKENS_PER_STEP = 8
PEER_V_KDIM = (PEER_PAIRS // 2) * 16


def _peer_v_kernel(row_ref, par_ref, coef_ref, h_ref, tab_ref, out_ref, cz_s, *, tb):
    half = PEER_PAIRS // 2
    kdim = PEER_V_KDIM
    nt = PEER_V_TOKENS_PER_STEP
    pk = lax.broadcasted_iota(I32, (PEER_PAIRS, kdim), 0)
    qk = lax.shift_right_logical(lax.broadcasted_iota(I32, (PEER_PAIRS, kdim), 1), 4)
    hk = (lax.broadcasted_iota(I32, (tb, kdim), 1) & 1).astype(F32)
    coef = coef_ref[...].astype(BF16)
    par = par_ref[...].astype(BF16)
    for hf in range(2):
        expand = jnp.where(pk == qk + hf * half, 1.0, 0.0).astype(BF16)
        cz_s[hf] = jnp.where(hk == _dot(par, expand), _dot(coef, expand), 0.0)
    ks = lax.broadcasted_iota(I32, (8, kdim), 0)
    k8 = lax.broadcasted_iota(I32, (8, kdim), 1)
    own_sublane = lax.shift_right_logical(k8 & 15, 1) == ks

    def tok_body(tt, carry):
        rows8 = pl.ds(pl.multiple_of(tt * nt, nt), nt)
        hb = h_ref[rows8, :]
        tiles = []
        for j in range(nt):
            t = tt * nt + j
            pieces = []
            for q in range(half):
                wa = pltpu.bitcast(_load_words(tab_ref, row_ref, t, q), BF16)
                wb = pltpu.bitcast(_load_words(tab_ref, row_ref, t, half + q), BF16)
                pieces.append(jnp.concatenate([wa, wb], axis=1))
            g = jnp.concatenate(pieces, axis=0)
            ck = jnp.concatenate(
                [jnp.where(own_sublane, jnp.broadcast_to(cz_s[hf, pl.ds(t, 1), :], (8, kdim)), 0.0)
                 for hf in range(2)], axis=0).astype(BF16)
            o = _dot(ck, g)
            tiles.append(_token_tile(hb, j) + (o[0:8, 0:128] + o[8:16, 128:256]))
        out_ref[rows8, :] = _tiles_to_rows(tiles)
        return carry

    lax.fori_loop(0, tb // nt, tok_body, 0)


def _peer_v_sc(tab, expert, coef, h2, t0, t_sc):
    tw = t_sc // SC_WORKERS
    heads = PEER_PAIRS // PEER_TOPK
    chunks = D_MODEL // SC_LANES
    mesh = plsc.VectorSubcoreMesh(core_axis_name="c", subcore_axis_name="s")

    def body(tab_hbm, idx_hbm, coef_hbm, h_hbm, out_hbm, idx_v, c_v, acc_v, rows_a, rows_b, sem_a, sem_b):
        wid = lax.axis_index("s") * SC_CORES + lax.axis_index("c")
        lane = lax.iota(I32, SC_LANES)
        bufs = ((rows_a, sem_a), (rows_b, sem_b))

        def gather(h):
            buf, sem = bufs[h % 2]
            return pltpu.make_async_copy(tab_hbm.at[idx_v.at[pl.ds(h * PEER_TOPK, PEER_TOPK)]], buf, sem)

        def token(i, carry):
            t = t0 + wid * tw + i
            pltpu.sync_copy(idx_hbm.at[t], idx_v)
            gather(0).start()
            pltpu.sync_copy(coef_hbm.at[t], c_v)
            pltpu.sync_copy(h_hbm.at[t], acc_v)
            for h in range(heads):
                if h + 1 < heads:
                    gather(h + 1).start()
                gather(h).wait()
                rows_v = bufs[h % 2][0]
                cvec = c_v[pl.ds(h * PEER_TOPK, PEER_TOPK)]
                cs = [jnp.sum(jnp.where(lane == r, cvec, 0.0)) for r in range(PEER_TOPK)]

                def chunk(j, carry2):
                    cols = pl.ds(j * SC_LANES, SC_LANES)
                    a = acc_v[cols]
                    for r in range(PEER_TOPK):
                        a = a + cs[r] * rows_v[r, cols]
                    acc_v[cols] = a
                    return carry2

                lax.fori_loop(0, chunks, chunk, 0)
            pltpu.sync_copy(acc_v, out_hbm.at[t - t0])
            return carry

        lax.fori_loop(0, tw, token, 0)

    return pl.kernel(
        body, mesh=mesh,
        out_type=jax.ShapeDtypeStruct((t_sc, D_MODEL), F32),
        compiler_params=pltpu.CompilerParams(needs_layout_passes=False),
        scratch_types=[pltpu.VMEM((PEER_PAIRS,), I32), pltpu.VMEM((PEER_PAIRS,), F32), pltpu.VMEM((D_MODEL,), F32),
                       pltpu.VMEM((PEER_TOPK, D_MODEL), F32), pltpu.VMEM((PEER_TOPK, D_MODEL), F32),
                       pltpu.SemaphoreType.DMA, pltpu.SemaphoreType.DMA],
        name="peer_v_sc",
    )(tab, expert, coef, h2)


def _peer_v(rows, par, coef, h2, tab, t, tb=128):
    return pl.pallas_call(
        functools.partial(_peer_v_kernel, tb=tb),
        grid=(t // tb,),
        in_specs=[_tok_spec(tb, pltpu.SMEM), _tok_spec(tb), _tok_spec(tb),
                  pl.BlockSpec((tb, D_MODEL), lambda i: (i, 0)), _table_spec()],
        out_specs=pl.BlockSpec((tb, D_MODEL), lambda i: (i, 0)),
        out_shape=jax.ShapeDtypeStruct((t, D_MODEL), F32),
        scratch_shapes=[pltpu.VMEM((2, tb, PEER_V_KDIM), F32)],
        compiler_params=_cparams(("arbitrary",)),
        name="peer_v",
    )(rows, par, coef, h2, tab)


def _final_norm_kernel(x_ref, g_ref, o_ref):
    x = x_ref[...]
    ms = jnp.mean(x * x, axis=-1, keepdims=True)
    o_ref[...] = x * lax.rsqrt(ms + NORM_EPS) * g_ref[...]


def _final_norm(x2, gain, tm=1024):
    t = x2.shape[0]
    tm = min(tm, t)
    row = lambda i: (i, 0)
    return pl.pallas_call(
        _final_norm_kernel,
        grid=(t // tm,),
        in_specs=[pl.BlockSpec((tm, D_MODEL), row), pl.BlockSpec((1, D_MODEL), lambda i: (0, 0))],
        out_specs=pl.BlockSpec((tm, D_MODEL), row),
        out_shape=jax.ShapeDtypeStruct((t, D_MODEL), F32),
        compiler_params=_cparams(("parallel",)),
        name="final_norm",
    )(x2, gain)


def _pack_table(tab):
    bits = lax.bitcast_convert_type(tab.astype(BF16), jnp.uint16).astype(U32).reshape(TAB_ROWS, 2, 8, 128)
    return (bits[:, 0] | (bits[:, 1] << 16)).reshape(TAB_ROWS * 8, 128)


def _peer(h2, layer, p):
    t = h2.shape[0]
    keys = p["peer_keys"][layer].reshape(2 * PEER_HEADS, N_KEYS, PEER_DHALF).astype(BF16)
    xn, expert, rows, par, gate = _peer_route(h2, p["norm2_g"][layer][None, :],
                                              p["peer_wq"][layer].T.astype(BF16), keys)
    expert, rows, par, gate = expert.T, rows.T, par.T, gate.T
    t_sc = (t * PEER_U_SC_SHARE_NUM // PEER_U_SC_SHARE_DEN) // (8 * SC_WORKERS) * (8 * SC_WORKERS)
    t_tc = t - t_sc
    d_sc = _peer_u_sc(p["peer_u"][layer], expert, xn, t_tc, t_sc)
    d = _peer_u(rows, par, xn, _pack_table(p["peer_u"][layer]), t_tc)
    d_tp = d.reshape(t_tc // 8, 8, 8, PEER_TOPK).transpose(0, 2, 3, 1).reshape(t_tc, PEER_PAIRS)
    coef = _peer_coef(jnp.concatenate([d_tp, d_sc], axis=0), gate)
    t_sc = (t * PEER_V_SC_SHARE_NUM // PEER_V_SC_SHARE_DEN) // (8 * SC_WORKERS) * (8 * SC_WORKERS)
    t_tc = t - t_sc
    out_sc = _peer_v_sc(p["peer_v"][layer], expert, coef, h2, t_tc, t_sc)
    out_tc = _peer_v(rows, par, coef, h2, _pack_table(p["peer_v"][layer]), t_tc)
    return jnp.concatenate([out_tc, out_sc], axis=0)


def _rep(x, n):
    return jnp.repeat(x, n, axis=-1)


def _inproj_weight(w_in_l):
    o = 0
    qkv = w_in_l[:, o:o + GDN_QKV_W]; o += GDN_QKV_W
    z = w_in_l[:, o:o + GDN_W]; o += GDN_W
    a = w_in_l[:, o:o + GDN_HEADS]; o += GDN_HEADS
    b = w_in_l[:, o:o + GDN_HEADS]; o += GDN_HEADS
    rest = w_in_l[:, o:]
    return jnp.concatenate([qkv, z, _rep(a, GDN_DK), _rep(b, GDN_DK), rest], axis=1).astype(BF16)


def _block_diag_gc(b_gnc):
    g = b_gnc.shape[0]
    eye = jnp.eye(g, dtype=b_gnc.dtype)
    t = jnp.swapaxes(b_gnc, 1, 2)
    return (t[:, :, None, :] * eye[:, None, :, None]).reshape(g * t.shape[1], g * t.shape[2])


def _mixers(h, layer, p):
    bn, seq, _ = h.shape
    t = bn * seq
    lam_init = 0.8 - 0.6 * math.exp(-0.3 * layer)
    gdn_in, us, qc, kc, vc = _inproj(h.reshape(t, D_MODEL), p["norm1_g"][layer][None, :],
                                     _inproj_weight(p["w_in"][layer]))
    a_log = _rep(p["gdn_a_log"][layer].astype(F32), GDN_DK)[None, :]
    dtb = _rep(p["gdn_dt_bias"][layer].astype(F32), GDN_DK)[None, :]
    o_a = _gdn(gdn_in.reshape(bn, seq, INPROJ_GDN_W), p["gdn_conv_w"][layer], a_log, dtb,
               p["gdn_norm_g"][layer][None, :])
    flat = lambda x: x.reshape(1, S5_NS)
    bre_bd = _block_diag_gc(p["s5_b_re"][layer])
    bim_bd = _block_diag_gc(p["s5_b_im"][layer])
    cre_bd = _block_diag_gc(p["s5_c_re"][layer])
    cim_bd = _block_diag_gc(p["s5_c_im"][layer])
    ys = _s5(us.reshape(bn, seq // S5_SUB, S5_SUB * S5_WIDTH),
             flat(p["s5_lambda_re"][layer]), flat(p["s5_lambda_im"][layer]),
             flat(_rep(p["s5_log_step"][layer][:, None], S5_STATE)),
             bre_bd, bim_bd, cre_bd, cim_bd, p["s5_d"][layer][None, :])
    row = lambda x: x[None, :]
    o_c = _attn(qc.reshape(bn, seq, DIFF_W), kc.reshape(bn, seq, DIFF_W), vc.reshape(bn, seq, DIFF_W),
                row(p["diff_lq1"][layer]), row(p["diff_lk1"][layer]), row(p["diff_lq2"][layer]),
                row(p["diff_lk2"][layer]), row(p["diff_norm_g"][layer]), lam_init)
    h2 = _outproj(h.reshape(t, D_MODEL), o_a.reshape(t, GDN_W), ys.reshape(t, S5_WIDTH),
                  o_c.reshape(t, DIFF_W), p["s5_glu_w"][layer].astype(BF16), p["s5_glu_b"][layer][None, :],
                  p["w_out"][layer].astype(BF16))
    return h2


def kernel(x, norm1_g, w_in, gdn_conv_w, gdn_a_log, gdn_dt_bias, gdn_norm_g, s5_lambda_re, s5_lambda_im, s5_b_re, s5_b_im, s5_c_re, s5_c_im, s5_d, s5_log_step, s5_glu_w, s5_glu_b, diff_lq1, diff_lk1, diff_lq2, diff_lk2, diff_norm_g, w_out, norm2_g, peer_wq, peer_keys, peer_u, peer_v, final_g):
    p = dict(norm1_g=norm1_g, w_in=w_in, gdn_conv_w=gdn_conv_w, gdn_a_log=gdn_a_log, gdn_dt_bias=gdn_dt_bias,
             gdn_norm_g=gdn_norm_g, s5_lambda_re=s5_lambda_re, s5_lambda_im=s5_lambda_im, s5_b_re=s5_b_re,
             s5_b_im=s5_b_im, s5_c_re=s5_c_re, s5_c_im=s5_c_im, s5_d=s5_d, s5_log_step=s5_log_step,
             s5_glu_w=s5_glu_w, s5_glu_b=s5_glu_b, diff_lq1=diff_lq1, diff_lk1=diff_lk1, diff_lq2=diff_lq2,
             diff_lk2=diff_lk2, diff_norm_g=diff_norm_g, w_out=w_out, norm2_g=norm2_g, peer_wq=peer_wq,
             peer_keys=peer_keys, peer_u=peer_u, peer_v=peer_v, final_g=final_g)
    h = x
    for layer in range(DEPTH):
        h2 = _mixers(h, layer, p)
        h = _peer(h2, layer, p).reshape(x.shape)
    return _final_norm(h.reshape(-1, D_MODEL), final_g[None, :]).reshape(x.shape)
```

```python
import functools
import math

import jax
import jax.numpy as jnp
from jax import lax
from jax.experimental import pallas as pl
from jax.experimental.pallas import tpu as pltpu
from jax.experimental.pallas import tpu_sc as plsc

F32 = jnp.float32
BF16 = jnp.bfloat16
I32 = jnp.int32
U32 = jnp.uint32

D_MODEL = 1024
DEPTH = 2
GDN_HEADS = 4
GDN_DK = 64
GDN_CHUNK = 64
GDN_W = GDN_HEADS * GDN_DK
GDN_QKV_W = 3 * GDN_W
GDN_CONV = 4
S5_WIDTH = 256
S5_GROUPS = 16
S5_GROUP_CH = 16
S5_STATE = 64
S5_NS = S5_GROUPS * S5_STATE
S5_SUB = 16
DIFF_HEADS = 4
DIFF_DQK = 64
DIFF_DV = 128
DIFF_W = 512
PEER_HEADS = 8
PEER_DHALF = 64
N_KEYS = 128
N_EXPERTS = N_KEYS * N_KEYS
PEER_TOPK = 16
PEER_PAIRS = PEER_HEADS * PEER_TOPK
NORM_EPS = 1e-6
NEG_INF = float("-inf")

VMEM_LIMIT_BYTES = 56 * 1024 * 1024


def _cparams(sem, vmem=VMEM_LIMIT_BYTES):
    return pltpu.CompilerParams(dimension_semantics=sem, vmem_limit_bytes=vmem)


def _dot(a, b):
    return jnp.dot(a, b, preferred_element_type=F32)


def _dot_nt(a, b):
    return lax.dot_general(a, b, (((1,), (1,)), ((), ())), preferred_element_type=F32)


def _dot_tn(a, b):
    return lax.dot_general(a, b, (((0,), (0,)), ((), ())), preferred_element_type=F32)


def _split(x):
    hi = x.astype(BF16)
    lo = (x - hi.astype(F32)).astype(BF16)
    return hi, lo


def _dot_sel_r(x, sel):
    hi, lo = _split(x)
    return _dot(hi, sel) + _dot(lo, sel)


def _dot_sel_l(sel, x):
    hi, lo = _split(x)
    return _dot(sel, hi) + _dot(sel, lo)


def _mm3(a, b):
    ah, al = _split(a)
    bh, bl = _split(b)
    return _dot(ah, bh) + (_dot(ah, bl) + _dot(al, bh))


def _sigmoid(x):
    return 1.0 / (1.0 + jnp.exp(-x))


def _softplus(x):
    return jnp.maximum(x, 0.0) + jnp.log1p(jnp.exp(-jnp.abs(x)))


def _gelu_tanh(x):
    c = math.sqrt(2.0 / math.pi)
    return 0.5 * x * (1.0 + jnp.tanh(c * (x + 0.044715 * (x * x * x))))


INPROJ_GDN_W = GDN_QKV_W + 3 * GDN_W


def _inproj_kernel(x_ref, g_ref, w_ref, gdn_ref, us_ref, q_ref, k_ref, v_ref):
    x = x_ref[...]
    ms = jnp.mean(x * x, axis=-1, keepdims=True)
    xn = (x * lax.rsqrt(ms + NORM_EPS) * g_ref[...]).astype(BF16)
    o = INPROJ_GDN_W
    gdn_ref[...] = _dot(xn, w_ref[:, 0:o])
    us_ref[...] = _dot(xn, w_ref[:, o:o + S5_WIDTH])
    o += S5_WIDTH
    q_ref[...] = (_dot(xn, w_ref[:, o:o + DIFF_W]) * (DIFF_DQK ** -0.5)).astype(BF16)
    k_ref[...] = _dot(xn, w_ref[:, o + DIFF_W:o + 2 * DIFF_W]).astype(BF16)
    v_ref[...] = _dot(xn, w_ref[:, o + 2 * DIFF_W:o + 3 * DIFF_W]).astype(BF16)


def _inproj(x2, gain, w, tm=512):
    t = x2.shape[0]
    nw = w.shape[1]
    row = lambda i: (i, 0)
    fixed = lambda i: (0, 0)
    return pl.pallas_call(
        _inproj_kernel,
        grid=(t // tm,),
        in_specs=[pl.BlockSpec((tm, D_MODEL), row), pl.BlockSpec((1, D_MODEL), fixed),
                  pl.BlockSpec((D_MODEL, nw), fixed)],
        out_specs=[pl.BlockSpec((tm, INPROJ_GDN_W), row), pl.BlockSpec((tm, S5_WIDTH), row),
                   pl.BlockSpec((tm, DIFF_W), row), pl.BlockSpec((tm, DIFF_W), row),
                   pl.BlockSpec((tm, DIFF_W), row)],
        out_shape=[jax.ShapeDtypeStruct((t, INPROJ_GDN_W), F32), jax.ShapeDtypeStruct((t, S5_WIDTH), F32),
                   jax.ShapeDtypeStruct((t, DIFF_W), BF16), jax.ShapeDtypeStruct((t, DIFF_W), BF16),
                   jax.ShapeDtypeStruct((t, DIFF_W), BF16)],
        compiler_params=_cparams(("parallel",)),
        name="inproj",
    )(x2, gain, w)


def _gdn_kernel(blk_ref, convw_ref, alog_ref, dtb_ref, ng_ref, out_ref,
                s_ref, tail_ref, xp_ref, q_s, k_s, v_s, b_s, g_s, *, ct):
    c64 = GDN_CHUNK

    @pl.when(pl.program_id(1) == 0)
    def _():
        s_ref[...] = jnp.zeros_like(s_ref)
        tail_ref[...] = jnp.zeros_like(tail_ref)

    qkv = blk_ref[0, :, 0:GDN_QKV_W]
    xp_ref[0:8, :] = tail_ref[...]
    xp_ref[8:8 + ct, :] = qkv
    tail_ref[...] = qkv[ct - 8:ct, :]
    cw = convw_ref[...]
    y = cw[0:1, :] * xp_ref[5:5 + ct, :]
    for j in range(1, GDN_CONV):
        y = y + cw[j:j + 1, :] * xp_ref[5 + j:5 + j + ct, :]
    y = y * _sigmoid(y)

    ri = lax.broadcasted_iota(I32, (GDN_W, GDN_W), 0)
    ci = lax.broadcasted_iota(I32, (GDN_W, GDN_W), 1)
    head_ones = jnp.where((ri // c64) == (ci // c64), 1.0, 0.0).astype(BF16)

    q = y[:, 0:GDN_W]
    k = y[:, GDN_W:2 * GDN_W]
    q_s[...] = q * lax.rsqrt(_dot_sel_r(q * q, head_ones) + 1e-6) * (GDN_DK ** -0.5)
    k_s[...] = k * lax.rsqrt(_dot_sel_r(k * k, head_ones) + 1e-6)
    v_s[...] = y[:, 2 * GDN_W:3 * GDN_W]
    a_rep = blk_ref[0, :, GDN_QKV_W + GDN_W:GDN_QKV_W + 2 * GDN_W]
    b_rep = blk_ref[0, :, GDN_QKV_W + 2 * GDN_W:GDN_QKV_W + 3 * GDN_W]
    b_s[...] = _sigmoid(b_rep)
    g_raw = -jnp.exp(alog_ref[...]) * _softplus(a_rep + dtb_ref[...])
    rt = lax.broadcasted_iota(I32, (ct, ct), 0)
    ctk = lax.broadcasted_iota(I32, (ct, ct), 1)
    tri_bd = jnp.where(((rt // c64) == (ctk // c64)) & (ctk <= rt), 1.0, 0.0).astype(BF16)
    g_s[...] = _dot_sel_l(tri_bd, g_raw)

    r64 = lax.broadcasted_iota(I32, (c64, c64), 0)
    col64 = lax.broadcasted_iota(I32, (c64, c64), 1)
    incl = col64 <= r64
    strict = col64 < r64
    eye_b = col64 == r64
    eye_f = jnp.where(eye_b, 1.0, 0.0)
    ones64 = jnp.ones((c64, c64), BF16)
    ng = ng_ref[...]

    n_chunks = ct // c64
    heads = range(GDN_HEADS)
    chains = [(slice(c * c64, (c + 1) * c64), slice(h * c64, (h + 1) * c64))
              for c in range(n_chunks) for h in heads]
    each = lambda f, *ls: [f(*xs) for xs in zip(*ls)]
    gi = [g_s[r, l] for r, l in chains]
    gj = each(lambda g: _dot_sel_l(ones64, jnp.where(eye_b, g, 0.0)), gi)
    dec = each(lambda a, b: jnp.exp(jnp.where(incl, a - b, NEG_INF)), gi, gj)
    kh = [k_s[r, l] for r, l in chains]
    bi = [b_s[r, l] for r, l in chains]
    kb = each(lambda x: x.astype(BF16), kh)
    kk = each(_dot_nt, kb, kb)
    lm = each(lambda b, x, d: jnp.where(strict, b * x * d, 0.0), bi, kk, dec)
    tinv = each(lambda x: eye_f - x, lm)
    pw = lm
    for _ in range(5):
        pw = each(_mm3, pw, pw)
        tinv = each(lambda t, m: t + _mm3(t, m), tinv, pw)
    tb = each(lambda x: x.astype(BF16), tinv)
    eg = each(jnp.exp, gi)
    u = each(lambda t, rl, b: _dot(t, (v_s[rl[0], rl[1]] * b).astype(BF16)), tb, chains, bi)
    w = each(lambda t, k, b, e: _dot(t, (k * (b * e)).astype(BF16)).astype(BF16), tb, kh, bi, eg)
    qh = [q_s[r, l] for r, l in chains]
    qk = each(lambda q, k, d: jnp.where(incl, _dot_nt(q.astype(BF16), k) * d, 0.0).astype(BF16), qh, kb, dec)
    qg = each(lambda q, e: (q * e).astype(BF16), qh, eg)
    g_last = each(lambda g: g[c64 - 1:c64, :], gi)
    kg = each(lambda k, gl, g: (k * jnp.exp(gl - g)).astype(BF16), kh, g_last, gi)
    decay_last = each(jnp.exp, g_last)

    states = [s_ref[h] for h in heads]
    for c in range(n_chunks):
        ids = [c * GDN_HEADS + h for h in heads]
        sb = [s.astype(BF16) for s in states]
        vnb = [(u[n] - _dot(w[n], sb[h])).astype(BF16) for h, n in zip(heads, ids)]
        o = [_dot(qg[n], sb[h]) + _dot(qk[n], vnb[h]) for h, n in zip(heads, ids)]
        states = [states[h] * decay_last[n] + _dot_tn(kg[n], vnb[h]) for h, n in zip(heads, ids)]
        ms = [_dot_sel_r(x * x, ones64) * (1.0 / c64) for x in o]
        outs = [x * lax.rsqrt(m + NORM_EPS) * ng for x, m in zip(o, ms)]
        rows = slice(c * c64, (c + 1) * c64)
        z = blk_ref[0, rows, GDN_QKV_W:GDN_QKV_W + GDN_W]
        out_ref[0, rows, :] = (jnp.concatenate(outs, axis=1) * (z * _sigmoid(z))).astype(out_ref.dtype)
    for h in heads:
        s_ref[h] = states[h]


def _gdn(gdn_in, conv_w, a_log_rep, dtb_rep, ng, ct=256):
    bn, seq, _ = gdn_in.shape
    fixed = lambda b, l: (0, 0)
    return pl.pallas_call(
        functools.partial(_gdn_kernel, ct=ct),
        grid=(bn, seq // ct),
        in_specs=[pl.BlockSpec((1, ct, INPROJ_GDN_W), lambda b, l: (b, l, 0)),
                  pl.BlockSpec((GDN_CONV, GDN_QKV_W), fixed), pl.BlockSpec((1, GDN_W), fixed),
                  pl.BlockSpec((1, GDN_W), fixed), pl.BlockSpec((1, GDN_DK), fixed)],
        out_specs=pl.BlockSpec((1, ct, GDN_W), lambda b, l: (b, l, 0)),
        out_shape=jax.ShapeDtypeStruct((bn, seq, GDN_W), BF16),
        scratch_shapes=[pltpu.VMEM((GDN_HEADS, GDN_DK, GDN_DK), F32), pltpu.VMEM((8, GDN_QKV_W), F32),
                        pltpu.VMEM((ct + 8, GDN_QKV_W), F32)] + [pltpu.VMEM((ct, GDN_W), F32)] * 5,
        compiler_params=_cparams(("arbitrary", "arbitrary")),
        name="gdn",
    )(gdn_in, conv_w, a_log_rep, dtb_rep, ng)


def _s5_kernel(u_ref, lre_ref, lim_ref, lstep_ref, bre_ref, bim_ref, cre_ref, cim_ref, d_ref, y_ref,
               bmat, cmat, avec, carry, zr_s, zi_s, xr_s, xi_s, *, rb):
    ns = S5_NS
    cw = S5_WIDTH

    @pl.when(pl.program_id(1) == 0)
    def _():
        step = jnp.exp(lstep_ref[...])
        lr = lre_ref[...]
        li = lim_ref[...]
        mag = jnp.exp(lr * step)
        ar = mag * jnp.cos(li * step)
        ai = mag * jnp.sin(li * step)
        den = lr * lr + li * li
        mr = ((ar - 1.0) * lr + ai * li) / den
        mi = (ai * lr - (ar - 1.0) * li) / den
        bre = bre_ref[...]
        bim = bim_ref[...]
        bmat[:, 0:ns] = (mr * bre - mi * bim).astype(BF16)
        bmat[:, ns:2 * ns] = (mr * bim + mi * bre).astype(BF16)
        cmat[0:ns, :] = cre_ref[...].astype(BF16)
        cmat[ns:2 * ns, :] = (-cim_ref[...]).astype(BF16)
        avec[0:1, :] = ar
        avec[1:2, :] = ai
        pr, pi = ar, ai
        for _ in range(4):
            pr, pi = pr * pr - pi * pi, 2.0 * pr * pi
        avec[2:3, :] = pr
        avec[3:4, :] = pi
        carry[...] = jnp.zeros_like(carry)

    ar = avec[0:1, :]
    ai = avec[1:2, :]
    a16r = avec[2:3, :]
    a16i = avec[3:4, :]

    def inject(s):
        ub = u_ref[0, :, s * cw:(s + 1) * cw].astype(BF16)
        return _dot(ub, bmat[...])

    xr = jnp.zeros((rb, ns), F32)
    xi = jnp.zeros((rb, ns), F32)
    for s in range(S5_SUB):
        p = inject(s)
        xr, xi = ar * xr - ai * xi + p[:, 0:ns], ar * xi + ai * xr + p[:, ns:2 * ns]
    zr_s[...] = xr
    zi_s[...] = xi

    def row_step(kk, c):
        cr, ci_ = c
        xr_s[pl.ds(kk, 1), :] = cr
        xi_s[pl.ds(kk, 1), :] = ci_
        zr = zr_s[pl.ds(kk, 1), :]
        zi = zi_s[pl.ds(kk, 1), :]
        return (a16r * cr - a16i * ci_ + zr, a16r * ci_ + a16i * cr + zi)

    cr, ci_ = lax.fori_loop(0, rb, row_step, (carry[0:1, :], carry[1:2, :]))
    carry[0:1, :] = cr
    carry[1:2, :] = ci_

    xr = xr_s[...]
    xi = xi_s[...]
    dsk = d_ref[...]
    for s in range(S5_SUB):
        p = inject(s)
        xr, xi = ar * xr - ai * xi + p[:, 0:ns], ar * xi + ai * xr + p[:, ns:2 * ns]
        yv = _dot(xr.astype(BF16), cmat[0:ns, :]) + _dot(xi.astype(BF16), cmat[ns:2 * ns, :])
        yv = yv + dsk * u_ref[0, :, s * cw:(s + 1) * cw]
        y_ref[0, :, s * cw:(s + 1) * cw] = _gelu_tanh(yv).astype(y_ref.dtype)


def _s5(u_rows, lre, lim, lstep, bre_bd, bim_bd, cre_bd, cim_bd, dskip, rb=128):
    bn, nrows, rw = u_rows.shape
    fixed = lambda b, r: (0, 0)
    ns = S5_NS
    return pl.pallas_call(
        functools.partial(_s5_kernel, rb=rb),
        grid=(bn, nrows // rb),
        in_specs=[pl.BlockSpec((1, rb, rw), lambda b, r: (b, r, 0)),
                  pl.BlockSpec((1, ns), fixed), pl.BlockSpec((1, ns), fixed), pl.BlockSpec((1, ns), fixed),
                  pl.BlockSpec((S5_WIDTH, ns), fixed), pl.BlockSpec((S5_WIDTH, ns), fixed),
                  pl.BlockSpec((ns, S5_WIDTH), fixed), pl.BlockSpec((ns, S5_WIDTH), fixed),
                  pl.BlockSpec((1, S5_WIDTH), fixed)],
        out_specs=pl.BlockSpec((1, rb, rw), lambda b, r: (b, r, 0)),
        out_shape=jax.ShapeDtypeStruct((bn, nrows, rw), BF16),
        scratch_shapes=[pltpu.VMEM((S5_WIDTH, 2 * ns), BF16), pltpu.VMEM((2 * ns, S5_WIDTH), BF16),
                        pltpu.VMEM((8, ns), F32), pltpu.VMEM((8, ns), F32)]
                       + [pltpu.VMEM((rb, ns), F32)] * 4,
        compiler_params=_cparams(("arbitrary", "arbitrary")),
        name="s5",
    )(u_rows, lre, lim, lstep, bre_bd, bim_bd, cre_bd, cim_bd, dskip)


def _attn_kernel(q_ref, k_ref, v_ref, lq1_ref, lk1_ref, lq2_ref, lk2_ref, ng_ref, o_ref,
                 qs_s, m_s, l_s, acc_s, *, blk, lam_init):
    i = pl.program_id(2)
    q = q_ref[0]
    lane = lax.broadcasted_iota(I32, q.shape, 1)
    zero = jnp.zeros_like(q)
    qs_s[0:blk, :] = jnp.where(lane < DIFF_DQK, q, zero)
    qs_s[blk:2 * blk, :] = jnp.where(lane >= DIFF_DQK, q, zero)
    m_s[...] = jnp.full_like(m_s, NEG_INF)
    l_s[...] = jnp.zeros_like(l_s)
    acc_s[...] = jnp.zeros_like(acc_s)
    ones = jnp.ones((blk, DIFF_DV), BF16)

    def block_rows(j):
        return pl.ds(pl.multiple_of(j * blk, blk), blk)

    def scores(j):
        return _dot_nt(qs_s[...], k_ref[0, block_rows(j), :])

    def update(j, s):
        m_old = m_s[...]
        m_new = jnp.maximum(m_old, jnp.max(s, axis=-1, keepdims=True))
        p = jnp.exp(s - jnp.concatenate([m_new] * (blk // DIFF_DV), axis=1)).astype(BF16)
        alpha = jnp.exp(m_old - m_new)
        pv = _dot(p, jnp.concatenate([v_ref[0, block_rows(j), :], ones], axis=1))
        acc_s[...] = alpha * acc_s[...] + pv[:, 0:DIFF_DV]
        l_s[...] = alpha * l_s[...] + pv[:, DIFF_DV:2 * DIFF_DV]
        m_s[...] = m_new

    def body(j, s):
        s_next = scores(j + 1)
        update(j, s)
        return s_next

    s = lax.fori_loop(0, i, body, scores(0))
    row = lax.broadcasted_iota(I32, s.shape, 0) & (blk - 1)
    col = lax.broadcasted_iota(I32, s.shape, 1)
    update(i, jnp.where(col <= row, s, NEG_INF))
    lam = (jnp.exp(jnp.sum(lq1_ref[...] * lk1_ref[...], axis=-1, keepdims=True))
           - jnp.exp(jnp.sum(lq2_ref[...] * lk2_ref[...], axis=-1, keepdims=True)) + lam_init)
    o = acc_s[0:blk, :] / l_s[0:blk, :] - lam * (acc_s[blk:2 * blk, :] / l_s[blk:2 * blk, :])
    ms = jnp.mean(o * o, axis=-1, keepdims=True)
    o = o * lax.rsqrt(ms + NORM_EPS) * ng_ref[...] * (1.0 - lam_init)
    o_ref[0] = o.astype(o_ref.dtype)


def _attn(q, k, v, lq1, lk1, lq2, lk2, ng, lam_init, blk=512):
    bn, seq, _ = q.shape
    blk = min(blk, seq)
    fixed = lambda b, h, i: (0, 0)
    kv_spec = pl.BlockSpec((1, seq, DIFF_DV), lambda b, h, i: (b, 0, h))
    return pl.pallas_call(
        functools.partial(_attn_kernel, blk=blk, lam_init=lam_init),
        grid=(bn, DIFF_HEADS, seq // blk),
        in_specs=[pl.BlockSpec((1, blk, DIFF_DV), lambda b, h, i: (b, i, h)), kv_spec, kv_spec]
                 + [pl.BlockSpec((1, DIFF_DQK), fixed)] * 4 + [pl.BlockSpec((1, DIFF_DV), fixed)],
        out_specs=pl.BlockSpec((1, blk, DIFF_DV), lambda b, h, i: (b, i, h)),
        out_shape=jax.ShapeDtypeStruct((bn, seq, DIFF_W), BF16),
        scratch_shapes=[pltpu.VMEM((2 * blk, DIFF_DV), BF16), pltpu.VMEM((2 * blk, DIFF_DV), F32),
                        pltpu.VMEM((2 * blk, DIFF_DV), F32), pltpu.VMEM((2 * blk, DIFF_DV), F32)],
        compiler_params=_cparams(("parallel", "parallel", "arbitrary")),
        name="diff_attn",
    )(q, k, v, lq1, lk1, lq2, lk2, ng)


def _outproj_kernel(h_ref, oa_ref, ys_ref, oc_ref, gw_ref, gb_ref, wo_ref, out_ref):
    zg = _dot(ys_ref[...], gw_ref[...]) + gb_ref[...]
    ob = (zg[:, 0:S5_WIDTH] * _sigmoid(zg[:, S5_WIDTH:2 * S5_WIDTH])).astype(BF16)
    acc = _dot(oa_ref[...], wo_ref[0:GDN_W, :])
    acc = acc + _dot(ob, wo_ref[GDN_W:GDN_W + S5_WIDTH, :])
    acc = acc + _dot(oc_ref[...], wo_ref[GDN_W + S5_WIDTH:, :])
    out_ref[...] = h_ref[...] + acc


def _outproj(h2, oa, ys, oc, glu_w, glu_b, w_out, tm=512):
    t = h2.shape[0]
    row = lambda i: (i, 0)
    fixed = lambda i: (0, 0)
    return pl.pallas_call(
        _outproj_kernel,
        grid=(t // tm,),
        in_specs=[pl.BlockSpec((tm, D_MODEL), row), pl.BlockSpec((tm, GDN_W), row),
                  pl.BlockSpec((tm, S5_WIDTH), row), pl.BlockSpec((tm, DIFF_W), row),
                  pl.BlockSpec((S5_WIDTH, 2 * S5_WIDTH), fixed), pl.BlockSpec((1, 2 * S5_WIDTH), fixed),
                  pl.BlockSpec((D_MODEL, D_MODEL), fixed)],
        out_specs=pl.BlockSpec((tm, D_MODEL), row),
        out_shape=jax.ShapeDtypeStruct((t, D_MODEL), F32),
        compiler_params=_cparams(("parallel",)),
        name="outproj",
    )(h2, oa, ys, oc, glu_w, glu_b, w_out)


_BIG_ID = 1.0e9


def _top16(x, ids, payload):
    n = x.shape[1]
    r16 = lax.broadcasted_iota(I32, (PEER_TOPK, n), 0)
    vals = jnp.zeros((PEER_TOPK, n), F32)
    pays = jnp.zeros((PEER_TOPK, n), F32)
    for kk in range(PEER_TOPK):
        m = jnp.max(x, axis=0, keepdims=True)
        first = jnp.min(jnp.where(x == m, ids, _BIG_ID), axis=0, keepdims=True)
        hit = ids == first
        pay = first if payload is None else jnp.max(jnp.where(hit, payload, -1.0), axis=0, keepdims=True)
        x = jnp.where(hit, NEG_INF, x)
        vals = jnp.where(r16 == kk, m, vals)
        pays = jnp.where(r16 == kk, pay, pays)
    return vals, pays


def _peer_route_kernel(h_ref, g_ref, wqt_ref, keys_ref, xn_ref, exp_ref, row_ref, par_ref, gate_ref,
                       qt_s, sv_s, si_s, *, tm):
    x = h_ref[...]
    ms = jnp.mean(x * x, axis=-1, keepdims=True)
    xn = x * lax.rsqrt(ms + NORM_EPS) * g_ref[...]
    xn_ref[...] = xn
    qt_s[...] = _dot_nt(wqt_ref[...], xn.astype(BF16)).astype(BF16)

    key_id = lax.broadcasted_iota(I32, (N_KEYS, tm), 0).astype(F32)

    def half_body(hp, carry):
        r0 = pl.multiple_of(hp * PEER_DHALF, PEER_DHALF)
        s = _dot(keys_ref[hp], qt_s[pl.ds(r0, PEER_DHALF), :])
        vals, ids = _top16(s, key_id, None)
        sv_s[hp] = vals
        si_s[hp] = ids
        return carry

    lax.fori_loop(0, 2 * PEER_HEADS, half_body, 0)

    i8 = lax.broadcasted_iota(I32, (8, tm), 0).astype(F32)

    def head_body(hd, carry):
        a0 = sv_s[2 * hd]
        a1 = sv_s[2 * hd + 1]
        e0 = si_s[2 * hd] * float(N_KEYS)
        e1 = si_s[2 * hd + 1]
        cs, es, fs = [], [], []
        for i in range(8):
            cs.append(a0[i:i + 1, :] + a1[0:8, :])
            es.append(e0[i:i + 1, :] + e1[0:8, :])
            fs.append(i8 + float(i * PEER_TOPK))
        cs.append(a0[0:1, :] + a1[8:16, :])
        es.append(e0[0:1, :] + e1[8:16, :])
        fs.append(i8 + 8.0)
        cs.append(a0[8:16, :] + a1[0:1, :])
        es.append(e0[8:16, :] + e1[0:1, :])
        fs.append((i8 + 8.0) * float(PEER_TOPK))
        top_s, experts = _top16(jnp.concatenate(cs, axis=0), jnp.concatenate(fs, axis=0),
                                jnp.concatenate(es, axis=0))
        ex = jnp.exp(top_s - jnp.max(top_s, axis=0, keepdims=True))
        rows = pl.ds(pl.multiple_of(hd * PEER_TOPK, PEER_TOPK), PEER_TOPK)
        gate_ref[rows, :] = ex / jnp.sum(ex, axis=0, keepdims=True)
        expert = experts.astype(I32)
        exp_ref[rows, :] = expert
        row_ref[rows, :] = (expert & (TAB_ROWS - 1)) * 8
        par_ref[rows, :] = lax.shift_right_logical(expert, TAB_ROWS.bit_length() - 1).astype(F32)
        return carry

    lax.fori_loop(0, PEER_HEADS, head_body, 0)


def _peer_route(h2, gain, wqt, keys, tm=512):
    t = h2.shape[0]
    tm = min(tm, t)
    row = lambda i: (i, 0)
    col = lambda i: (0, i)
    return pl.pallas_call(
        functools.partial(_peer_route_kernel, tm=tm),
        grid=(t // tm,),
        in_specs=[pl.BlockSpec((tm, D_MODEL), row), pl.BlockSpec((1, D_MODEL), lambda i: (0, 0)),
                  pl.BlockSpec((D_MODEL, D_MODEL), lambda i: (0, 0)),
                  pl.BlockSpec((2 * PEER_HEADS, N_KEYS, PEER_DHALF), lambda i: (0, 0, 0))],
        out_specs=[pl.BlockSpec((tm, D_MODEL), row)] + [pl.BlockSpec((PEER_PAIRS, tm), col)] * 4,
        out_shape=[jax.ShapeDtypeStruct((t, D_MODEL), F32), jax.ShapeDtypeStruct((PEER_PAIRS, t), I32),
                   jax.ShapeDtypeStruct((PEER_PAIRS, t), I32), jax.ShapeDtypeStruct((PEER_PAIRS, t), F32),
                   jax.ShapeDtypeStruct((PEER_PAIRS, t), F32)],
        scratch_shapes=[pltpu.VMEM((D_MODEL, tm), BF16), pltpu.VMEM((2 * PEER_HEADS, PEER_TOPK, tm), F32),
                        pltpu.VMEM((2 * PEER_HEADS, PEER_TOPK, tm), F32)],
        compiler_params=_cparams(("parallel",)),
        name="peer_route",
    )(h2, gain, wqt, keys)


TAB_ROWS = N_EXPERTS // 2
HIGH_HALF = 0xFFFF0000


def _splat_rows(row):
    r = lax.broadcasted_iota(I32, (128, 128), 0)
    c = lax.broadcasted_iota(I32, (128, 128), 1)
    diag = jnp.where(r == c, jnp.broadcast_to(row, (128, 128)), 0.0).astype(BF16)
    return _dot(diag, jnp.ones((128, 128), BF16))


def _stage_shifts(par_ref, t, srep_s):
    shift = 16.0 - 16.0 * par_ref[pl.ds(t, 1), :]
    srep_s[...] = pltpu.bitcast(_splat_rows(shift).astype(I32), U32)


def _token_tile(block, tl):
    return jnp.concatenate([block[tl:tl + 1, s * 128:(s + 1) * 128] for s in range(8)], axis=0)


def _tiles_to_rows(tiles):
    return jnp.concatenate([jnp.concatenate([tile[s:s + 1, :] for tile in tiles], axis=0) for s in range(8)], axis=1)


def _load_words(tab_ref, row_ref, t, p):
    return tab_ref[pl.ds(pl.multiple_of(row_ref[t, p], 8), 8), :]


def _load_expert(tab_ref, row_ref, srep_s, t, p):
    w = lax.shift_left(_load_words(tab_ref, row_ref, t, p), jnp.broadcast_to(srep_s[p:p + 1, :], (8, 128)))
    return pltpu.bitcast(w & jnp.uint32(HIGH_HALF), F32)


_TREE8_SUBLANE = (3, 7, 1, 5, 2, 6, 0, 4)


def _tree8(v, upper, bit2, bit1):
    c = []
    for a, b in ((v[0], v[1]), (v[2], v[3]), (v[4], v[5]), (v[6], v[7])):
        c.append(jnp.where(upper, b + pltpu.roll(b, 4, 0), a + pltpu.roll(a, 4, 0)))
    e = []
    for c1, c2 in ((c[0], c[1]), (c[2], c[3])):
        e.append(jnp.where(bit2, c1 + pltpu.roll(c1, 2, 0), c2 + pltpu.roll(c2, 6, 0)))
    return jnp.where(bit1, e[0] + pltpu.roll(e[0], 1, 0), e[1] + pltpu.roll(e[1], 7, 0))


PEER_U_TOKENS_PER_STEP = 2


def _peer_u_kernel(row_ref, par_ref, x_ref, tab_ref, d_ref, r_s, srep_a, srep_b, *, tb):
    sub = lax.broadcasted_iota(I32, (8, 128), 0)
    lane = lax.broadcasted_iota(I32, (8, 128), 1)
    upper = sub >= 4
    bit2 = (sub & 2) != 0
    bit1 = (sub & 1) != 0
    groups = PEER_PAIRS // 8
    nt = PEER_U_TOKENS_PER_STEP
    steps = tb // nt

    def stage(tt, srep_s):
        for j in range(nt):
            _stage_shifts(par_ref, tt * nt + j, srep_s.at[j])

    def run(tt, srep_s, xb, first):
        for j in range(nt):
            t = tt * nt + j
            xv = _token_tile(xb, first + j)
            prod = lambda p: _load_expert(tab_ref, row_ref, srep_s.at[j], t, p) * xv
            for g in range(groups):
                r_s[t * groups + g] = _tree8([prod(g * 8 + _TREE8_SUBLANE[m]) for m in range(8)],
                                             upper, bit2, bit1)

    stage(0, srep_a)
    steps_per_body = 8 // nt

    def eight_tokens(k, carry):
        xb = x_ref[pl.ds(pl.multiple_of(k * 8, 8), 8), :]
        for i in range(steps_per_body):
            tt = k * steps_per_body + i
            cur, nxt = (srep_a, srep_b) if i % 2 == 0 else (srep_b, srep_a)
            stage(jnp.minimum(tt + 1, steps - 1), nxt)
            run(tt, cur, xb, i * nt)
        return carry

    lax.fori_loop(0, tb // 8, eight_tokens, 0)

    def red_body(t8, carry):
        dacc = jnp.zeros((8, 128), F32)
        for tl in range(8):
            for g in range(groups):
                d = jnp.sum(r_s[(t8 * 8 + tl) * groups + g], axis=1, keepdims=True)
                dacc = jnp.where(lane == tl * PEER_TOPK + g, d, dacc)
        d_ref[t8] = dacc
        return carry

    lax.fori_loop(0, tb // 8, red_body, 0)


def _tok_spec(tb, space=None):
    return pl.BlockSpec((tb, PEER_PAIRS), lambda i: (i, 0), memory_space=space)


def _table_spec():
    return pl.BlockSpec((TAB_ROWS * 8, 128), lambda i: (0, 0), pipeline_mode=pl.Buffered(1))


def _peer_u(rows, par, x2, tab, t, tb=128):
    return pl.pallas_call(
        functools.partial(_peer_u_kernel, tb=tb),
        grid=(t // tb,),
        in_specs=[_tok_spec(tb, pltpu.SMEM), _tok_spec(tb), pl.BlockSpec((tb, D_MODEL), lambda i: (i, 0)),
                  _table_spec()],
        out_specs=pl.BlockSpec((tb // 8, 8, 128), lambda i: (i, 0, 0)),
        out_shape=jax.ShapeDtypeStruct((t // 8, 8, 128), F32),
        scratch_shapes=[pltpu.VMEM((tb * (PEER_PAIRS // 8), 8, 128), F32),
                        pltpu.VMEM((PEER_U_TOKENS_PER_STEP, 128, 128), U32),
                        pltpu.VMEM((PEER_U_TOKENS_PER_STEP, 128, 128), U32)],
        compiler_params=_cparams(("arbitrary",)),
        name="peer_u",
    )(rows, par, x2, tab)


SC_CORES = 2
SC_SUBCORES = 16
SC_LANES = 16
SC_WORKERS = SC_CORES * SC_SUBCORES
PEER_U_SC_SHARE_NUM, PEER_U_SC_SHARE_DEN = 45, 128
PEER_V_SC_SHARE_NUM, PEER_V_SC_SHARE_DEN = 29, 128


def _peer_u_sc(tab, expert, xn, t0, t_sc):
    tw = t_sc // SC_WORKERS
    heads = PEER_PAIRS // PEER_TOPK
    chunks = D_MODEL // SC_LANES
    mesh = plsc.VectorSubcoreMesh(core_axis_name="c", subcore_axis_name="s")

    def body(tab_hbm, idx_hbm, x_hbm, d_hbm, idx_v, x_v, rows_a, rows_b, d_v, sem_a, sem_b):
        wid = lax.axis_index("s") * SC_CORES + lax.axis_index("c")
        lane = lax.iota(I32, SC_LANES)
        bufs = ((rows_a, sem_a), (rows_b, sem_b))

        def gather(h):
            buf, sem = bufs[h % 2]
            return pltpu.make_async_copy(tab_hbm.at[idx_v.at[pl.ds(h * PEER_TOPK, PEER_TOPK)]], buf, sem)

        def token(i, carry):
            t = t0 + wid * tw + i
            pltpu.sync_copy(idx_hbm.at[t], idx_v)
            pltpu.sync_copy(x_hbm.at[t], x_v)
            gather(0).start()
            for h in range(heads):
                if h + 1 < heads:
                    gather(h + 1).start()
                gather(h).wait()
                rows_v = bufs[h % 2][0]

                def chunk(j, accs):
                    xj = x_v[pl.ds(j * SC_LANES, SC_LANES)]
                    return tuple(a + rows_v[r, pl.ds(j * SC_LANES, SC_LANES)] * xj for r, a in enumerate(accs))

                accs = lax.fori_loop(0, chunks, chunk,
                                     tuple(jnp.zeros((SC_LANES,), F32) for _ in range(PEER_TOPK)))
                out = jnp.zeros((SC_LANES,), F32)
                for r in range(PEER_TOPK):
                    out = jnp.where(lane == r, jnp.sum(accs[r]), out)
                d_v[pl.ds(h * PEER_TOPK, PEER_TOPK)] = out
            pltpu.sync_copy(d_v, d_hbm.at[t - t0])
            return carry

        lax.fori_loop(0, tw, token, 0)

    return pl.kernel(
        body, mesh=mesh,
        out_type=jax.ShapeDtypeStruct((t_sc, PEER_PAIRS), F32),
        compiler_params=pltpu.CompilerParams(needs_layout_passes=False),
        scratch_types=[pltpu.VMEM((PEER_PAIRS,), I32), pltpu.VMEM((D_MODEL,), F32),
                       pltpu.VMEM((PEER_TOPK, D_MODEL), F32), pltpu.VMEM((PEER_TOPK, D_MODEL), F32),
                       pltpu.VMEM((PEER_PAIRS,), F32), pltpu.SemaphoreType.DMA, pltpu.SemaphoreType.DMA],
        name="peer_u_sc",
    )(tab, expert, xn)


def _peer_coef_kernel(d_ref, gate_ref, c_ref):
    c_ref[...] = gate_ref[...] * _gelu_tanh(d_ref[...])


def _peer_coef(d_tp, gate_tp, tm=2048):
    t = d_tp.shape[0]
    tm = min(tm, t)
    return pl.pallas_call(
        _peer_coef_kernel,
        grid=(t // tm,),
        in_specs=[_tok_spec(tm), _tok_spec(tm)],
        out_specs=_tok_spec(tm),
        out_shape=jax.ShapeDtypeStruct((t, PEER_PAIRS), F32),
        compiler_params=_cparams(("parallel",)),
        name="peer_coef",
    )(d_tp, gate_tp)


PEER_V_TOKENS_PER_STEP = 8
PEER_V_KDIM = (PEER_PAIRS // 2) * 16


def _peer_v_kernel(row_ref, par_ref, coef_ref, h_ref, tab_ref, out_ref, cz_s, *, tb):
    half = PEER_PAIRS // 2
    kdim = PEER_V_KDIM
    nt = PEER_V_TOKENS_PER_STEP
    pk = lax.broadcasted_iota(I32, (PEER_PAIRS, kdim), 0)
    qk = lax.shift_right_logical(lax.broadcasted_iota(I32, (PEER_PAIRS, kdim), 1), 4)
    hk = (lax.broadcasted_iota(I32, (tb, kdim), 1) & 1).astype(F32)
    coef = coef_ref[...].astype(BF16)
    par = par_ref[...].astype(BF16)
    for hf in range(2):
        expand = jnp.where(pk == qk + hf * half, 1.0, 0.0).astype(BF16)
        cz_s[hf] = jnp.where(hk == _dot(par, expand), _dot(coef, expand), 0.0)
    ks = lax.broadcasted_iota(I32, (8, kdim), 0)
    k8 = lax.broadcasted_iota(I32, (8, kdim), 1)
    own_sublane = lax.shift_right_logical(k8 & 15, 1) == ks

    def tok_body(tt, carry):
        rows8 = pl.ds(pl.multiple_of(tt * nt, nt), nt)
        hb = h_ref[rows8, :]
        tiles = []
        for j in range(nt):
            t = tt * nt + j
            pieces = []
            for q in range(half):
                wa = pltpu.bitcast(_load_words(tab_ref, row_ref, t, q), BF16)
                wb = pltpu.bitcast(_load_words(tab_ref, row_ref, t, half + q), BF16)
                pieces.append(jnp.concatenate([wa, wb], axis=1))
            g = jnp.concatenate(pieces, axis=0)
            ck = jnp.concatenate(
                [jnp.where(own_sublane, jnp.broadcast_to(cz_s[hf, pl.ds(t, 1), :], (8, kdim)), 0.0)
                 for hf in range(2)], axis=0).astype(BF16)
            o = _dot(ck, g)
            tiles.append(_token_tile(hb, j) + (o[0:8, 0:128] + o[8:16, 128:256]))
        out_ref[rows8, :] = _tiles_to_rows(tiles)
        return carry

    lax.fori_loop(0, tb // nt, tok_body, 0)


def _peer_v_sc(tab, expert, coef, h2, t0, t_sc):
    tw = t_sc // SC_WORKERS
    heads = PEER_PAIRS // PEER_TOPK
    chunks = D_MODEL // SC_LANES
    mesh = plsc.VectorSubcoreMesh(core_axis_name="c", subcore_axis_name="s")

    def body(tab_hbm, idx_hbm, coef_hbm, h_hbm, out_hbm, idx_v, c_v, acc_v, rows_a, rows_b, sem_a, sem_b):
        wid = lax.axis_index("s") * SC_CORES + lax.axis_index("c")
        lane = lax.iota(I32, SC_LANES)
        bufs = ((rows_a, sem_a), (rows_b, sem_b))

        def gather(h):
            buf, sem = bufs[h % 2]
            return pltpu.make_async_copy(tab_hbm.at[idx_v.at[pl.ds(h * PEER_TOPK, PEER_TOPK)]], buf, sem)

        def token(i, carry):
            t = t0 + wid * tw + i
            pltpu.sync_copy(idx_hbm.at[t], idx_v)
            gather(0).start()
            pltpu.sync_copy(coef_hbm.at[t], c_v)
            pltpu.sync_copy(h_hbm.at[t], acc_v)
            for h in range(heads):
                if h + 1 < heads:
                    gather(h + 1).start()
                gather(h).wait()
                rows_v = bufs[h % 2][0]
                cvec = c_v[pl.ds(h * PEER_TOPK, PEER_TOPK)]
                cs = [jnp.sum(jnp.where(lane == r, cvec, 0.0)) for r in range(PEER_TOPK)]

                def chunk(j, carry2):
                    cols = pl.ds(j * SC_LANES, SC_LANES)
                    a = acc_v[cols]
                    for r in range(PEER_TOPK):
                        a = a + cs[r] * rows_v[r, cols]
                    acc_v[cols] = a
                    return carry2

                lax.fori_loop(0, chunks, chunk, 0)
            pltpu.sync_copy(acc_v, out_hbm.at[t - t0])
            return carry

        lax.fori_loop(0, tw, token, 0)

    return pl.kernel(
        body, mesh=mesh,
        out_type=jax.ShapeDtypeStruct((t_sc, D_MODEL), F32),
        compiler_params=pltpu.CompilerParams(needs_layout_passes=False),
        scratch_types=[pltpu.VMEM((PEER_PAIRS,), I32), pltpu.VMEM((PEER_PAIRS,), F32), pltpu.VMEM((D_MODEL,), F32),
                       pltpu.VMEM((PEER_TOPK, D_MODEL), F32), pltpu.VMEM((PEER_TOPK, D_MODEL), F32),
                       pltpu.SemaphoreType.DMA, pltpu.SemaphoreType.DMA],
        name="peer_v_sc",
    )(tab, expert, coef, h2)


def _peer_v(rows, par, coef, h2, tab, t, tb=128):
    return pl.pallas_call(
        functools.partial(_peer_v_kernel, tb=tb),
        grid=(t // tb,),
        in_specs=[_tok_spec(tb, pltpu.SMEM), _tok_spec(tb), _tok_spec(tb),
                  pl.BlockSpec((tb, D_MODEL), lambda i: (i, 0)), _table_spec()],
        out_specs=pl.BlockSpec((tb, D_MODEL), lambda i: (i, 0)),
        out_shape=jax.ShapeDtypeStruct((t, D_MODEL), F32),
        scratch_shapes=[pltpu.VMEM((2, tb, PEER_V_KDIM), F32)],
        compiler_params=_cparams(("arbitrary",)),
        name="peer_v",
    )(rows, par, coef, h2, tab)


def _final_norm_kernel(x_ref, g_ref, o_ref):
    x = x_ref[...]
    ms = jnp.mean(x * x, axis=-1, keepdims=True)
    o_ref[...] = x * lax.rsqrt(ms + NORM_EPS) * g_ref[...]


def _final_norm(x2, gain, tm=1024):
    t = x2.shape[0]
    tm = min(tm, t)
    row = lambda i: (i, 0)
    return pl.pallas_call(
        _final_norm_kernel,
        grid=(t // tm,),
        in_specs=[pl.BlockSpec((tm, D_MODEL), row), pl.BlockSpec((1, D_MODEL), lambda i: (0, 0))],
        out_specs=pl.BlockSpec((tm, D_MODEL), row),
        out_shape=jax.ShapeDtypeStruct((t, D_MODEL), F32),
        compiler_params=_cparams(("parallel",)),
        name="final_norm",
    )(x2, gain)


def _pack_table(tab):
    bits = lax.bitcast_convert_type(tab.astype(BF16), jnp.uint16).astype(U32)
    return (bits[:TAB_ROWS] | (bits[TAB_ROWS:] << 16)).reshape(TAB_ROWS * 8, 128)


def _peer(h2, layer, p):
    t = h2.shape[0]
    keys = p["peer_keys"][layer].reshape(2 * PEER_HEADS, N_KEYS, PEER_DHALF).astype(BF16)
    xn, expert, rows, par, gate = _peer_route(h2, p["norm2_g"][layer][None, :],
                                              p["peer_wq"][layer].T.astype(BF16), keys)
    expert, rows, par, gate = expert.T, rows.T, par.T, gate.T
    t_sc = (t * PEER_U_SC_SHARE_NUM // PEER_U_SC_SHARE_DEN) // (8 * SC_WORKERS) * (8 * SC_WORKERS)
    t_tc = t - t_sc
    expert_all = expert + layer * N_EXPERTS
    stacked = lambda tab: tab.reshape(DEPTH * N_EXPERTS, D_MODEL)
    d_sc = _peer_u_sc(stacked(p["peer_u"]), expert_all, xn, t_tc, t_sc)
    d = _peer_u(rows, par, xn, _pack_table(p["peer_u"][layer]), t_tc)
    d_tp = d.reshape(t_tc // 8, 8, 8, PEER_TOPK).transpose(0, 2, 3, 1).reshape(t_tc, PEER_PAIRS)
    coef = _peer_coef(jnp.concatenate([d_tp, d_sc], axis=0), gate)
    t_sc = (t * PEER_V_SC_SHARE_NUM // PEER_V_SC_SHARE_DEN) // (8 * SC_WORKERS) * (8 * SC_WORKERS)
    t_tc = t - t_sc
    out_sc = _peer_v_sc(stacked(p["peer_v"]), expert_all, coef, h2, t_tc, t_sc)
    out_tc = _peer_v(rows, par, coef, h2, _pack_table(p["peer_v"][layer]), t_tc)
    return jnp.concatenate([out_tc, out_sc], axis=0)


def _rep(x, n):
    return jnp.repeat(x, n, axis=-1)


def _inproj_weight(w_in_l):
    o = 0
    qkv = w_in_l[:, o:o + GDN_QKV_W]; o += GDN_QKV_W
    z = w_in_l[:, o:o + GDN_W]; o += GDN_W
    a = w_in_l[:, o:o + GDN_HEADS]; o += GDN_HEADS
    b = w_in_l[:, o:o + GDN_HEADS]; o += GDN_HEADS
    rest = w_in_l[:, o:]
    return jnp.concatenate([qkv, z, _rep(a, GDN_DK), _rep(b, GDN_DK), rest], axis=1).astype(BF16)


def _block_diag_gc(b_gnc):
    g = b_gnc.shape[0]
    eye = jnp.eye(g, dtype=b_gnc.dtype)
    t = jnp.swapaxes(b_gnc, 1, 2)
    return (t[:, :, None, :] * eye[:, None, :, None]).reshape(g * t.shape[1], g * t.shape[2])


def _mixers(h, layer, p):
    bn, seq, _ = h.shape
    t = bn * seq
    lam_init = 0.8 - 0.6 * math.exp(-0.3 * layer)
    gdn_in, us, qc, kc, vc = _inproj(h.reshape(t, D_MODEL), p["norm1_g"][layer][None, :],
                                     _inproj_weight(p["w_in"][layer]))
    a_log = _rep(p["gdn_a_log"][layer].astype(F32), GDN_DK)[None, :]
    dtb = _rep(p["gdn_dt_bias"][layer].astype(F32), GDN_DK)[None, :]
    o_a = _gdn(gdn_in.reshape(bn, seq, INPROJ_GDN_W), p["gdn_conv_w"][layer], a_log, dtb,
               p["gdn_norm_g"][layer][None, :])
    flat = lambda x: x.reshape(1, S5_NS)
    bre_bd = _block_diag_gc(p["s5_b_re"][layer])
    bim_bd = _block_diag_gc(p["s5_b_im"][layer])
    cre_bd = _block_diag_gc(p["s5_c_re"][layer])
    cim_bd = _block_diag_gc(p["s5_c_im"][layer])
    ys = _s5(us.reshape(bn, seq // S5_SUB, S5_SUB * S5_WIDTH),
             flat(p["s5_lambda_re"][layer]), flat(p["s5_lambda_im"][layer]),
             flat(_rep(p["s5_log_step"][layer][:, None], S5_STATE)),
             bre_bd, bim_bd, cre_bd, cim_bd, p["s5_d"][layer][None, :])
    row = lambda x: x[None, :]
    o_c = _attn(qc.reshape(bn, seq, DIFF_W), kc.reshape(bn, seq, DIFF_W), vc.reshape(bn, seq, DIFF_W),
                row(p["diff_lq1"][layer]), row(p["diff_lk1"][layer]), row(p["diff_lq2"][layer]),
                row(p["diff_lk2"][layer]), row(p["diff_norm_g"][layer]), lam_init)
    h2 = _outproj(h.reshape(t, D_MODEL), o_a.reshape(t, GDN_W), ys.reshape(t, S5_WIDTH),
                  o_c.reshape(t, DIFF_W), p["s5_glu_w"][layer].astype(BF16), p["s5_glu_b"][layer][None, :],
                  p["w_out"][layer].astype(BF16))
    return h2


def kernel(x, norm1_g, w_in, gdn_conv_w, gdn_a_log, gdn_dt_bias, gdn_norm_g, s5_lambda_re, s5_lambda_im, s5_b_re, s5_b_im, s5_c_re, s5_c_im, s5_d, s5_log_step, s5_glu_w, s5_glu_b, diff_lq1, diff_lk1, diff_lq2, diff_lk2, diff_norm_g, w_out, norm2_g, peer_wq, peer_keys, peer_u, peer_v, final_g):
    p = dict(norm1_g=norm1_g, w_in=w_in, gdn_conv_w=gdn_conv_w, gdn_a_log=gdn_a_log, gdn_dt_bias=gdn_dt_bias,
             gdn_norm_g=gdn_norm_g, s5_lambda_re=s5_lambda_re, s5_lambda_im=s5_lambda_im, s5_b_re=s5_b_re,
             s5_b_im=s5_b_im, s5_c_re=s5_c_re, s5_c_im=s5_c_im, s5_d=s5_d, s5_log_step=s5_log_step,
             s5_glu_w=s5_glu_w, s5_glu_b=s5_glu_b, diff_lq1=diff_lq1, diff_lk1=diff_lk1, diff_lq2=diff_lq2,
             diff_lk2=diff_lk2, diff_norm_g=diff_norm_g, w_out=w_out, norm2_g=norm2_g, peer_wq=peer_wq,
             peer_keys=peer_keys, peer_u=peer_u, peer_v=peer_v, final_g=final_g)
    h = x
    for layer in range(DEPTH):
        h2 = _mixers(h, layer, p)
        h = _peer(h2, layer, p).reshape(x.shape)
    return _final_norm(h.reshape(-1, D_MODEL), final_g[None, :]).reshape(x.shape)
```

```python
import functools
import math

import jax
import jax.numpy as jnp
from jax import lax
from jax.experimental import pallas as pl
from jax.experimental.pallas import tpu as pltpu
from jax.experimental.pallas import tpu_sc as plsc

F32 = jnp.float32
BF16 = jnp.bfloat16
I32 = jnp.int32
U32 = jnp.uint32

D_MODEL = 1024
DEPTH = 2
GDN_HEADS = 4
GDN_DK = 64
GDN_CHUNK = 64
GDN_W = GDN_HEADS * GDN_DK
GDN_QKV_W = 3 * GDN_W
GDN_CONV = 4
S5_WIDTH = 256
S5_GROUPS = 16
S5_GROUP_CH = 16
S5_STATE = 64
S5_NS = S5_GROUPS * S5_STATE
S5_SUB = 16
DIFF_HEADS = 4
DIFF_DQK = 64
DIFF_DV = 128
DIFF_W = 512
PEER_HEADS = 8
PEER_DHALF = 64
N_KEYS = 128
N_EXPERTS = N_KEYS * N_KEYS
PEER_TOPK = 16
PEER_PAIRS = PEER_HEADS * PEER_TOPK
NORM_EPS = 1e-6
NEG_INF = float("-inf")

VMEM_LIMIT_BYTES = 56 * 1024 * 1024


def _cparams(sem, vmem=VMEM_LIMIT_BYTES):
    return pltpu.CompilerParams(dimension_semantics=sem, vmem_limit_bytes=vmem)


def _dot(a, b):
    return jnp.dot(a, b, preferred_element_type=F32)


def _dot_nt(a, b):
    return lax.dot_general(a, b, (((1,), (1,)), ((), ())), preferred_element_type=F32)


def _dot_tn(a, b):
    return lax.dot_general(a, b, (((0,), (0,)), ((), ())), preferred_element_type=F32)


def _split(x):
    hi = x.astype(BF16)
    lo = (x - hi.astype(F32)).astype(BF16)
    return hi, lo


def _dot_sel_r(x, sel):
    hi, lo = _split(x)
    return _dot(hi, sel) + _dot(lo, sel)


def _dot_sel_l(sel, x):
    hi, lo = _split(x)
    return _dot(sel, hi) + _dot(sel, lo)


def _mm3(a, b):
    ah, al = _split(a)
    bh, bl = _split(b)
    return _dot(ah, bh) + (_dot(ah, bl) + _dot(al, bh))


def _sigmoid(x):
    return 1.0 / (1.0 + jnp.exp(-x))


def _softplus(x):
    return jnp.maximum(x, 0.0) + jnp.log1p(jnp.exp(-jnp.abs(x)))


def _gelu_tanh(x):
    c = math.sqrt(2.0 / math.pi)
    return 0.5 * x * (1.0 + jnp.tanh(c * (x + 0.044715 * (x * x * x))))


INPROJ_GDN_W = GDN_QKV_W + 3 * GDN_W


def _inproj_kernel(x_ref, g_ref, w_ref, gdn_ref, us_ref, q_ref, k_ref, v_ref):
    x = x_ref[...]
    ms = jnp.mean(x * x, axis=-1, keepdims=True)
    xn = (x * lax.rsqrt(ms + NORM_EPS) * g_ref[...]).astype(BF16)
    o = INPROJ_GDN_W
    gdn_ref[...] = _dot(xn, w_ref[:, 0:o])
    us_ref[...] = _dot(xn, w_ref[:, o:o + S5_WIDTH])
    o += S5_WIDTH
    q_ref[...] = (_dot(xn, w_ref[:, o:o + DIFF_W]) * (DIFF_DQK ** -0.5)).astype(BF16)
    k_ref[...] = _dot(xn, w_ref[:, o + DIFF_W:o + 2 * DIFF_W]).astype(BF16)
    v_ref[...] = _dot(xn, w_ref[:, o + 2 * DIFF_W:o + 3 * DIFF_W]).astype(BF16)


def _inproj(x2, gain, w, tm=512):
    t = x2.shape[0]
    nw = w.shape[1]
    row = lambda i: (i, 0)
    fixed = lambda i: (0, 0)
    return pl.pallas_call(
        _inproj_kernel,
        grid=(t // tm,),
        in_specs=[pl.BlockSpec((tm, D_MODEL), row), pl.BlockSpec((1, D_MODEL), fixed),
                  pl.BlockSpec((D_MODEL, nw), fixed)],
        out_specs=[pl.BlockSpec((tm, INPROJ_GDN_W), row), pl.BlockSpec((tm, S5_WIDTH), row),
                   pl.BlockSpec((tm, DIFF_W), row), pl.BlockSpec((tm, DIFF_W), row),
                   pl.BlockSpec((tm, DIFF_W), row)],
        out_shape=[jax.ShapeDtypeStruct((t, INPROJ_GDN_W), F32), jax.ShapeDtypeStruct((t, S5_WIDTH), F32),
                   jax.ShapeDtypeStruct((t, DIFF_W), BF16), jax.ShapeDtypeStruct((t, DIFF_W), BF16),
                   jax.ShapeDtypeStruct((t, DIFF_W), BF16)],
        compiler_params=_cparams(("parallel",)),
        name="inproj",
    )(x2, gain, w)


def _gdn_kernel(blk_ref, convw_ref, alog_ref, dtb_ref, ng_ref, out_ref,
                s_ref, tail_ref, xp_ref, q_s, k_s, v_s, b_s, g_s, *, ct):
    c64 = GDN_CHUNK

    @pl.when(pl.program_id(1) == 0)
    def _():
        s_ref[...] = jnp.zeros_like(s_ref)
        tail_ref[...] = jnp.zeros_like(tail_ref)

    qkv = blk_ref[0, :, 0:GDN_QKV_W]
    xp_ref[0:8, :] = tail_ref[...]
    xp_ref[8:8 + ct, :] = qkv
    tail_ref[...] = qkv[ct - 8:ct, :]
    cw = convw_ref[...]
    y = cw[0:1, :] * xp_ref[5:5 + ct, :]
    for j in range(1, GDN_CONV):
        y = y + cw[j:j + 1, :] * xp_ref[5 + j:5 + j + ct, :]
    y = y * _sigmoid(y)

    ri = lax.broadcasted_iota(I32, (GDN_W, GDN_W), 0)
    ci = lax.broadcasted_iota(I32, (GDN_W, GDN_W), 1)
    head_ones = jnp.where((ri // c64) == (ci // c64), 1.0, 0.0).astype(BF16)

    q = y[:, 0:GDN_W]
    k = y[:, GDN_W:2 * GDN_W]
    q_s[...] = q * lax.rsqrt(_dot_sel_r(q * q, head_ones) + 1e-6) * (GDN_DK ** -0.5)
    k_s[...] = k * lax.rsqrt(_dot_sel_r(k * k, head_ones) + 1e-6)
    v_s[...] = y[:, 2 * GDN_W:3 * GDN_W]
    a_rep = blk_ref[0, :, GDN_QKV_W + GDN_W:GDN_QKV_W + 2 * GDN_W]
    b_rep = blk_ref[0, :, GDN_QKV_W + 2 * GDN_W:GDN_QKV_W + 3 * GDN_W]
    b_s[...] = _sigmoid(b_rep)
    g_raw = -jnp.exp(alog_ref[...]) * _softplus(a_rep + dtb_ref[...])
    rt = lax.broadcasted_iota(I32, (ct, ct), 0)
    ctk = lax.broadcasted_iota(I32, (ct, ct), 1)
    tri_bd = jnp.where(((rt // c64) == (ctk // c64)) & (ctk <= rt), 1.0, 0.0).astype(BF16)
    g_s[...] = _dot_sel_l(tri_bd, g_raw)

    r64 = lax.broadcasted_iota(I32, (c64, c64), 0)
    col64 = lax.broadcasted_iota(I32, (c64, c64), 1)
    incl = col64 <= r64
    strict = col64 < r64
    eye_b = col64 == r64
    eye_f = jnp.where(eye_b, 1.0, 0.0)
    ones64 = jnp.ones((c64, c64), BF16)
    ng = ng_ref[...]

    n_chunks = ct // c64
    heads = range(GDN_HEADS)
    chains = [(slice(c * c64, (c + 1) * c64), slice(h * c64, (h + 1) * c64))
              for c in range(n_chunks) for h in heads]
    each = lambda f, *ls: [f(*xs) for xs in zip(*ls)]
    gi = [g_s[r, l] for r, l in chains]
    gj = each(lambda g: _dot_sel_l(ones64, jnp.where(eye_b, g, 0.0)), gi)
    dec = each(lambda a, b: jnp.exp(jnp.where(incl, a - b, NEG_INF)), gi, gj)
    kh = [k_s[r, l] for r, l in chains]
    bi = [b_s[r, l] for r, l in chains]
    kb = each(lambda x: x.astype(BF16), kh)
    kk = each(_dot_nt, kb, kb)
    lm = each(lambda b, x, d: jnp.where(strict, b * x * d, 0.0), bi, kk, dec)
    tinv = each(lambda x: eye_f - x, lm)
    pw = lm
    for _ in range(5):
        pw = each(_mm3, pw, pw)
        tinv = each(lambda t, m: t + _mm3(t, m), tinv, pw)
    tb = each(lambda x: x.astype(BF16), tinv)
    eg = each(jnp.exp, gi)
    u = each(lambda t, rl, b: _dot(t, (v_s[rl[0], rl[1]] * b).astype(BF16)), tb, chains, bi)
    w = each(lambda t, k, b, e: _dot(t, (k * (b * e)).astype(BF16)).astype(BF16), tb, kh, bi, eg)
    qh = [q_s[r, l] for r, l in chains]
    qk = each(lambda q, k, d: jnp.where(incl, _dot_nt(q.astype(BF16), k) * d, 0.0).astype(BF16), qh, kb, dec)
    qg = each(lambda q, e: (q * e).astype(BF16), qh, eg)
    g_last = each(lambda g: g[c64 - 1:c64, :], gi)
    kg = each(lambda k, gl, g: (k * jnp.exp(gl - g)).astype(BF16), kh, g_last, gi)
    decay_last = each(jnp.exp, g_last)

    states = [s_ref[h] for h in heads]
    for c in range(n_chunks):
        ids = [c * GDN_HEADS + h for h in heads]
        sb = [s.astype(BF16) for s in states]
        vnb = [(u[n] - _dot(w[n], sb[h])).astype(BF16) for h, n in zip(heads, ids)]
        o = [_dot(qg[n], sb[h]) + _dot(qk[n], vnb[h]) for h, n in zip(heads, ids)]
        states = [states[h] * decay_last[n] + _dot_tn(kg[n], vnb[h]) for h, n in zip(heads, ids)]
        ms = [_dot_sel_r(x * x, ones64) * (1.0 / c64) for x in o]
        outs = [x * lax.rsqrt(m + NORM_EPS) * ng for x, m in zip(o, ms)]
        rows = slice(c * c64, (c + 1) * c64)
        z = blk_ref[0, rows, GDN_QKV_W:GDN_QKV_W + GDN_W]
        out_ref[0, rows, :] = (jnp.concatenate(outs, axis=1) * (z * _sigmoid(z))).astype(out_ref.dtype)
    for h in heads:
        s_ref[h] = states[h]


def _gdn(gdn_in, conv_w, a_log_rep, dtb_rep, ng, ct=256):
    bn, seq, _ = gdn_in.shape
    fixed = lambda b, l: (0, 0)
    return pl.pallas_call(
        functools.partial(_gdn_kernel, ct=ct),
        grid=(bn, seq // ct),
        in_specs=[pl.BlockSpec((1, ct, INPROJ_GDN_W), lambda b, l: (b, l, 0)),
                  pl.BlockSpec((GDN_CONV, GDN_QKV_W), fixed), pl.BlockSpec((1, GDN_W), fixed),
                  pl.BlockSpec((1, GDN_W), fixed), pl.BlockSpec((1, GDN_DK), fixed)],
        out_specs=pl.BlockSpec((1, ct, GDN_W), lambda b, l: (b, l, 0)),
        out_shape=jax.ShapeDtypeStruct((bn, seq, GDN_W), BF16),
        scratch_shapes=[pltpu.VMEM((GDN_HEADS, GDN_DK, GDN_DK), F32), pltpu.VMEM((8, GDN_QKV_W), F32),
                        pltpu.VMEM((ct + 8, GDN_QKV_W), F32)] + [pltpu.VMEM((ct, GDN_W), F32)] * 5,
        compiler_params=_cparams(("arbitrary", "arbitrary")),
        name="gdn",
    )(gdn_in, conv_w, a_log_rep, dtb_rep, ng)


def _s5_kernel(u_ref, lre_ref, lim_ref, lstep_ref, bre_ref, bim_ref, cre_ref, cim_ref, d_ref, y_ref,
               bmat, cmat, avec, carry, zr_s, zi_s, xr_s, xi_s, *, rb):
    ns = S5_NS
    cw = S5_WIDTH

    @pl.when(pl.program_id(1) == 0)
    def _():
        step = jnp.exp(lstep_ref[...])
        lr = lre_ref[...]
        li = lim_ref[...]
        mag = jnp.exp(lr * step)
        ar = mag * jnp.cos(li * step)
        ai = mag * jnp.sin(li * step)
        den = lr * lr + li * li
        mr = ((ar - 1.0) * lr + ai * li) / den
        mi = (ai * lr - (ar - 1.0) * li) / den
        bre = bre_ref[...]
        bim = bim_ref[...]
        bmat[:, 0:ns] = (mr * bre - mi * bim).astype(BF16)
        bmat[:, ns:2 * ns] = (mr * bim + mi * bre).astype(BF16)
        cmat[0:ns, :] = cre_ref[...].astype(BF16)
        cmat[ns:2 * ns, :] = (-cim_ref[...]).astype(BF16)
        avec[0:1, :] = ar
        avec[1:2, :] = ai
        pr, pi = ar, ai
        for _ in range(4):
            pr, pi = pr * pr - pi * pi, 2.0 * pr * pi
        avec[2:3, :] = pr
        avec[3:4, :] = pi
        carry[...] = jnp.zeros_like(carry)

    ar = avec[0:1, :]
    ai = avec[1:2, :]
    a16r = avec[2:3, :]
    a16i = avec[3:4, :]

    def inject(s):
        ub = u_ref[0, :, s * cw:(s + 1) * cw].astype(BF16)
        return _dot(ub, bmat[...])

    xr = jnp.zeros((rb, ns), F32)
    xi = jnp.zeros((rb, ns), F32)
    for s in range(S5_SUB):
        p = inject(s)
        xr, xi = ar * xr - ai * xi + p[:, 0:ns], ar * xi + ai * xr + p[:, ns:2 * ns]
    zr_s[...] = xr
    zi_s[...] = xi

    def row_step(kk, c):
        cr, ci_ = c
        xr_s[pl.ds(kk, 1), :] = cr
        xi_s[pl.ds(kk, 1), :] = ci_
        zr = zr_s[pl.ds(kk, 1), :]
        zi = zi_s[pl.ds(kk, 1), :]
        return (a16r * cr - a16i * ci_ + zr, a16r * ci_ + a16i * cr + zi)

    cr, ci_ = lax.fori_loop(0, rb, row_step, (carry[0:1, :], carry[1:2, :]))
    carry[0:1, :] = cr
    carry[1:2, :] = ci_

    xr = xr_s[...]
    xi = xi_s[...]
    dsk = d_ref[...]
    for s in range(S5_SUB):
        p = inject(s)
        xr, xi = ar * xr - ai * xi + p[:, 0:ns], ar * xi + ai * xr + p[:, ns:2 * ns]
        yv = _dot(xr.astype(BF16), cmat[0:ns, :]) + _dot(xi.astype(BF16), cmat[ns:2 * ns, :])
        yv = yv + dsk * u_ref[0, :, s * cw:(s + 1) * cw]
        y_ref[0, :, s * cw:(s + 1) * cw] = _gelu_tanh(yv).astype(y_ref.dtype)


def _s5(u_rows, lre, lim, lstep, bre_bd, bim_bd, cre_bd, cim_bd, dskip, rb=128):
    bn, nrows, rw = u_rows.shape
    fixed = lambda b, r: (0, 0)
    ns = S5_NS
    return pl.pallas_call(
        functools.partial(_s5_kernel, rb=rb),
        grid=(bn, nrows // rb),
        in_specs=[pl.BlockSpec((1, rb, rw), lambda b, r: (b, r, 0)),
                  pl.BlockSpec((1, ns), fixed), pl.BlockSpec((1, ns), fixed), pl.BlockSpec((1, ns), fixed),
                  pl.BlockSpec((S5_WIDTH, ns), fixed), pl.BlockSpec((S5_WIDTH, ns), fixed),
                  pl.BlockSpec((ns, S5_WIDTH), fixed), pl.BlockSpec((ns, S5_WIDTH), fixed),
                  pl.BlockSpec((1, S5_WIDTH), fixed)],
        out_specs=pl.BlockSpec((1, rb, rw), lambda b, r: (b, r, 0)),
        out_shape=jax.ShapeDtypeStruct((bn, nrows, rw), BF16),
        scratch_shapes=[pltpu.VMEM((S5_WIDTH, 2 * ns), BF16), pltpu.VMEM((2 * ns, S5_WIDTH), BF16),
                        pltpu.VMEM((8, ns), F32), pltpu.VMEM((8, ns), F32)]
                       + [pltpu.VMEM((rb, ns), F32)] * 4,
        compiler_params=_cparams(("arbitrary", "arbitrary")),
        name="s5",
    )(u_rows, lre, lim, lstep, bre_bd, bim_bd, cre_bd, cim_bd, dskip)


def _attn_kernel(q_ref, k_ref, v_ref, lq1_ref, lk1_ref, lq2_ref, lk2_ref, ng_ref, o_ref,
                 qs_s, m_s, l_s, acc_s, *, blk, lam_init):
    i = pl.program_id(2)
    q = q_ref[0]
    lane = lax.broadcasted_iota(I32, q.shape, 1)
    zero = jnp.zeros_like(q)
    qs_s[0:blk, :] = jnp.where(lane < DIFF_DQK, q, zero)
    qs_s[blk:2 * blk, :] = jnp.where(lane >= DIFF_DQK, q, zero)
    m_s[...] = jnp.full_like(m_s, NEG_INF)
    l_s[...] = jnp.zeros_like(l_s)
    acc_s[...] = jnp.zeros_like(acc_s)
    ones = jnp.ones((blk, DIFF_DV), BF16)

    def block_rows(j):
        return pl.ds(pl.multiple_of(j * blk, blk), blk)

    def scores(j):
        return _dot_nt(qs_s[...], k_ref[0, block_rows(j), :])

    def update(j, s):
        m_old = m_s[...]
        m_new = jnp.maximum(m_old, jnp.max(s, axis=-1, keepdims=True))
        p = jnp.exp(s - jnp.concatenate([m_new] * (blk // DIFF_DV), axis=1)).astype(BF16)
        alpha = jnp.exp(m_old - m_new)
        pv = _dot(p, jnp.concatenate([v_ref[0, block_rows(j), :], ones], axis=1))
        acc_s[...] = alpha * acc_s[...] + pv[:, 0:DIFF_DV]
        l_s[...] = alpha * l_s[...] + pv[:, DIFF_DV:2 * DIFF_DV]
        m_s[...] = m_new

    def body(j, s):
        s_next = scores(j + 1)
        update(j, s)
        return s_next

    s = lax.fori_loop(0, i, body, scores(0))
    row = lax.broadcasted_iota(I32, s.shape, 0) & (blk - 1)
    col = lax.broadcasted_iota(I32, s.shape, 1)
    update(i, jnp.where(col <= row, s, NEG_INF))
    lam = (jnp.exp(jnp.sum(lq1_ref[...] * lk1_ref[...], axis=-1, keepdims=True))
           - jnp.exp(jnp.sum(lq2_ref[...] * lk2_ref[...], axis=-1, keepdims=True)) + lam_init)
    o = acc_s[0:blk, :] / l_s[0:blk, :] - lam * (acc_s[blk:2 * blk, :] / l_s[blk:2 * blk, :])
    ms = jnp.mean(o * o, axis=-1, keepdims=True)
    o = o * lax.rsqrt(ms + NORM_EPS) * ng_ref[...] * (1.0 - lam_init)
    o_ref[0] = o.astype(o_ref.dtype)


def _attn(q, k, v, lq1, lk1, lq2, lk2, ng, lam_init, blk=512):
    bn, seq, _ = q.shape
    blk = min(blk, seq)
    fixed = lambda b, h, i: (0, 0)
    kv_spec = pl.BlockSpec((1, seq, DIFF_DV), lambda b, h, i: (b, 0, h))
    return pl.pallas_call(
        functools.partial(_attn_kernel, blk=blk, lam_init=lam_init),
        grid=(bn, DIFF_HEADS, seq // blk),
        in_specs=[pl.BlockSpec((1, blk, DIFF_DV), lambda b, h, i: (b, i, h)), kv_spec, kv_spec]
                 + [pl.BlockSpec((1, DIFF_DQK), fixed)] * 4 + [pl.BlockSpec((1, DIFF_DV), fixed)],
        out_specs=pl.BlockSpec((1, blk, DIFF_DV), lambda b, h, i: (b, i, h)),
        out_shape=jax.ShapeDtypeStruct((bn, seq, DIFF_W), BF16),
        scratch_shapes=[pltpu.VMEM((2 * blk, DIFF_DV), BF16), pltpu.VMEM((2 * blk, DIFF_DV), F32),
                        pltpu.VMEM((2 * blk, DIFF_DV), F32), pltpu.VMEM((2 * blk, DIFF_DV), F32)],
        compiler_params=_cparams(("parallel", "parallel", "arbitrary")),
        name="diff_attn",
    )(q, k, v, lq1, lk1, lq2, lk2, ng)


def _outproj_kernel(h_ref, oa_ref, ys_ref, oc_ref, gw_ref, gb_ref, wo_ref, out_ref):
    zg = _dot(ys_ref[...], gw_ref[...]) + gb_ref[...]
    ob = (zg[:, 0:S5_WIDTH] * _sigmoid(zg[:, S5_WIDTH:2 * S5_WIDTH])).astype(BF16)
    acc = _dot(oa_ref[...], wo_ref[0:GDN_W, :])
    acc = acc + _dot(ob, wo_ref[GDN_W:GDN_W + S5_WIDTH, :])
    acc = acc + _dot(oc_ref[...], wo_ref[GDN_W + S5_WIDTH:, :])
    out_ref[...] = h_ref[...] + acc


def _outproj(h2, oa, ys, oc, glu_w, glu_b, w_out, tm=512):
    t = h2.shape[0]
    row = lambda i: (i, 0)
    fixed = lambda i: (0, 0)
    return pl.pallas_call(
        _outproj_kernel,
        grid=(t // tm,),
        in_specs=[pl.BlockSpec((tm, D_MODEL), row), pl.BlockSpec((tm, GDN_W), row),
                  pl.BlockSpec((tm, S5_WIDTH), row), pl.BlockSpec((tm, DIFF_W), row),
                  pl.BlockSpec((S5_WIDTH, 2 * S5_WIDTH), fixed), pl.BlockSpec((1, 2 * S5_WIDTH), fixed),
                  pl.BlockSpec((D_MODEL, D_MODEL), fixed)],
        out_specs=pl.BlockSpec((tm, D_MODEL), row),
        out_shape=jax.ShapeDtypeStruct((t, D_MODEL), F32),
        compiler_params=_cparams(("parallel",)),
        name="outproj",
    )(h2, oa, ys, oc, glu_w, glu_b, w_out)


_BIG_ID = 1.0e9


def _top16(x, ids, payload):
    n = x.shape[1]
    r16 = lax.broadcasted_iota(I32, (PEER_TOPK, n), 0)
    vals = jnp.zeros((PEER_TOPK, n), F32)
    pays = jnp.zeros((PEER_TOPK, n), F32)
    for kk in range(PEER_TOPK):
        m = jnp.max(x, axis=0, keepdims=True)
        first = jnp.min(jnp.where(x == m, ids, _BIG_ID), axis=0, keepdims=True)
        hit = ids == first
        pay = first if payload is None else jnp.max(jnp.where(hit, payload, -1.0), axis=0, keepdims=True)
        x = jnp.where(hit, NEG_INF, x)
        vals = jnp.where(r16 == kk, m, vals)
        pays = jnp.where(r16 == kk, pay, pays)
    return vals, pays


def _peer_route_kernel(h_ref, g_ref, wqt_ref, keys_ref, xn_ref, exp_ref, row_ref, par_ref, gate_ref,
                       qt_s, sv_s, si_s, *, tm):
    x = h_ref[...]
    ms = jnp.mean(x * x, axis=-1, keepdims=True)
    xn = x * lax.rsqrt(ms + NORM_EPS) * g_ref[...]
    xn_ref[...] = xn
    qt_s[...] = _dot_nt(wqt_ref[...], xn.astype(BF16)).astype(BF16)

    key_id = lax.broadcasted_iota(I32, (N_KEYS, tm), 0).astype(F32)

    def half_body(hp, carry):
        r0 = pl.multiple_of(hp * PEER_DHALF, PEER_DHALF)
        s = _dot(keys_ref[hp], qt_s[pl.ds(r0, PEER_DHALF), :])
        vals, ids = _top16(s, key_id, None)
        sv_s[hp] = vals
        si_s[hp] = ids
        return carry

    lax.fori_loop(0, 2 * PEER_HEADS, half_body, 0)

    i8 = lax.broadcasted_iota(I32, (8, tm), 0).astype(F32)

    def head_body(hd, carry):
        a0 = sv_s[2 * hd]
        a1 = sv_s[2 * hd + 1]
        e0 = si_s[2 * hd] * float(N_KEYS)
        e1 = si_s[2 * hd + 1]
        cs, es, fs = [], [], []
        for i in range(8):
            cs.append(a0[i:i + 1, :] + a1[0:8, :])
            es.append(e0[i:i + 1, :] + e1[0:8, :])
            fs.append(i8 + float(i * PEER_TOPK))
        cs.append(a0[0:1, :] + a1[8:16, :])
        es.append(e0[0:1, :] + e1[8:16, :])
        fs.append(i8 + 8.0)
        cs.append(a0[8:16, :] + a1[0:1, :])
        es.append(e0[8:16, :] + e1[0:1, :])
        fs.append((i8 + 8.0) * float(PEER_TOPK))
        top_s, experts = _top16(jnp.concatenate(cs, axis=0), jnp.concatenate(fs, axis=0),
                                jnp.concatenate(es, axis=0))
        ex = jnp.exp(top_s - jnp.max(top_s, axis=0, keepdims=True))
        rows = pl.ds(pl.multiple_of(hd * PEER_TOPK, PEER_TOPK), PEER_TOPK)
        gate_ref[rows, :] = ex / jnp.sum(ex, axis=0, keepdims=True)
        expert = experts.astype(I32)
        exp_ref[rows, :] = expert
        row_ref[rows, :] = (expert & (TAB_ROWS - 1)) * 8
        par_ref[rows, :] = lax.shift_right_logical(expert, TAB_ROWS.bit_length() - 1).astype(F32)
        return carry

    lax.fori_loop(0, PEER_HEADS, head_body, 0)


def _peer_route(h2, gain, wqt, keys, tm=512):
    t = h2.shape[0]
    tm = min(tm, t)
    row = lambda i: (i, 0)
    col = lambda i: (0, i)
    return pl.pallas_call(
        functools.partial(_peer_route_kernel, tm=tm),
        grid=(t // tm,),
        in_specs=[pl.BlockSpec((tm, D_MODEL), row), pl.BlockSpec((1, D_MODEL), lambda i: (0, 0)),
                  pl.BlockSpec((D_MODEL, D_MODEL), lambda i: (0, 0)),
                  pl.BlockSpec((2 * PEER_HEADS, N_KEYS, PEER_DHALF), lambda i: (0, 0, 0))],
        out_specs=[pl.BlockSpec((tm, D_MODEL), row)] + [pl.BlockSpec((PEER_PAIRS, tm), col)] * 4,
        out_shape=[jax.ShapeDtypeStruct((t, D_MODEL), F32), jax.ShapeDtypeStruct((PEER_PAIRS, t), I32),
                   jax.ShapeDtypeStruct((PEER_PAIRS, t), I32), jax.ShapeDtypeStruct((PEER_PAIRS, t), F32),
                   jax.ShapeDtypeStruct((PEER_PAIRS, t), F32)],
        scratch_shapes=[pltpu.VMEM((D_MODEL, tm), BF16), pltpu.VMEM((2 * PEER_HEADS, PEER_TOPK, tm), F32),
                        pltpu.VMEM((2 * PEER_HEADS, PEER_TOPK, tm), F32)],
        compiler_params=_cparams(("parallel",)),
        name="peer_route",
    )(h2, gain, wqt, keys)


TAB_ROWS = N_EXPERTS // 2
HIGH_HALF = 0xFFFF0000


def _splat_rows(row):
    r = lax.broadcasted_iota(I32, (128, 128), 0)
    c = lax.broadcasted_iota(I32, (128, 128), 1)
    diag = jnp.where(r == c, jnp.broadcast_to(row, (128, 128)), 0.0).astype(BF16)
    return _dot(diag, jnp.ones((128, 128), BF16))


def _stage_shifts(par_ref, t, srep_s):
    shift = 16.0 - 16.0 * par_ref[pl.ds(t, 1), :]
    srep_s[...] = pltpu.bitcast(_splat_rows(shift).astype(I32), U32)


def _token_tile(block, tl):
    return jnp.concatenate([block[tl:tl + 1, s * 128:(s + 1) * 128] for s in range(8)], axis=0)


def _tiles_to_rows(tiles):
    return jnp.concatenate([jnp.concatenate([tile[s:s + 1, :] for tile in tiles], axis=0) for s in range(8)], axis=1)


def _load_words(tab_ref, row_ref, t, p):
    return tab_ref[pl.ds(pl.multiple_of(row_ref[t, p], 8), 8), :]


def _load_expert(tab_ref, row_ref, srep_s, t, p):
    w = lax.shift_left(_load_words(tab_ref, row_ref, t, p), jnp.broadcast_to(srep_s[p:p + 1, :], (8, 128)))
    return pltpu.bitcast(w & jnp.uint32(HIGH_HALF), F32)


_TREE8_SUBLANE = (3, 7, 1, 5, 2, 6, 0, 4)


def _tree8(v, upper, bit2, bit1):
    c = []
    for a, b in ((v[0], v[1]), (v[2], v[3]), (v[4], v[5]), (v[6], v[7])):
        c.append(jnp.where(upper, b + pltpu.roll(b, 4, 0), a + pltpu.roll(a, 4, 0)))
    e = []
    for c1, c2 in ((c[0], c[1]), (c[2], c[3])):
        e.append(jnp.where(bit2, c1 + pltpu.roll(c1, 2, 0), c2 + pltpu.roll(c2, 6, 0)))
    return jnp.where(bit1, e[0] + pltpu.roll(e[0], 1, 0), e[1] + pltpu.roll(e[1], 7, 0))


PEER_U_TOKENS_PER_STEP = 2


def _peer_u_kernel(row_ref, par_ref, x_ref, tab_ref, d_ref, r_s, srep_a, srep_b, *, tb):
    sub = lax.broadcasted_iota(I32, (8, 128), 0)
    lane = lax.broadcasted_iota(I32, (8, 128), 1)
    upper = sub >= 4
    bit2 = (sub & 2) != 0
    bit1 = (sub & 1) != 0
    groups = PEER_PAIRS // 8
    nt = PEER_U_TOKENS_PER_STEP
    steps = tb // nt

    def stage(tt, srep_s):
        for j in range(nt):
            _stage_shifts(par_ref, tt * nt + j, srep_s.at[j])

    def run(tt, srep_s, xb, first):
        for j in range(nt):
            t = tt * nt + j
            xv = _token_tile(xb, first + j)
            prod = lambda p: _load_expert(tab_ref, row_ref, srep_s.at[j], t, p) * xv
            for g in range(groups):
                r_s[t * groups + g] = _tree8([prod(g * 8 + _TREE8_SUBLANE[m]) for m in range(8)],
                                             upper, bit2, bit1)

    stage(0, srep_a)
    steps_per_body = 8 // nt

    def eight_tokens(k):
        xb = x_ref[pl.ds(pl.multiple_of(k * 8, 8), 8), :]
        for i in range(steps_per_body):
            tt = k * steps_per_body + i
            cur, nxt = (srep_a, srep_b) if i % 2 == 0 else (srep_b, srep_a)
            stage(jnp.minimum(tt + 1, steps - 1), nxt)
            run(tt, cur, xb, i * nt)

    def lane_sums(t8):
        dacc = jnp.zeros((8, 128), F32)
        for tl in range(8):
            for g in range(groups):
                d = jnp.sum(r_s[(t8 * 8 + tl) * groups + g], axis=1, keepdims=True)
                dacc = jnp.where(lane == tl * PEER_TOPK + g, d, dacc)
        d_ref[t8] = dacc

    eight_tokens(0)

    def body(k, carry):
        lane_sums(k - 1)
        eight_tokens(k)
        return carry

    lax.fori_loop(1, tb // 8, body, 0)
    lane_sums(tb // 8 - 1)


def _tok_spec(tb, space=None):
    return pl.BlockSpec((tb, PEER_PAIRS), lambda i: (i, 0), memory_space=space)


def _table_spec():
    return pl.BlockSpec((TAB_ROWS * 8, 128), lambda i: (0, 0), pipeline_mode=pl.Buffered(1))


def _peer_u(rows, par, x2, tab, t, tb=128):
    return pl.pallas_call(
        functools.partial(_peer_u_kernel, tb=tb),
        grid=(t // tb,),
        in_specs=[_tok_spec(tb, pltpu.SMEM), _tok_spec(tb), pl.BlockSpec((tb, D_MODEL), lambda i: (i, 0)),
                  _table_spec()],
        out_specs=pl.BlockSpec((tb // 8, 8, 128), lambda i: (i, 0, 0)),
        out_shape=jax.ShapeDtypeStruct((t // 8, 8, 128), F32),
        scratch_shapes=[pltpu.VMEM((tb * (PEER_PAIRS // 8), 8, 128), F32),
                        pltpu.VMEM((PEER_U_TOKENS_PER_STEP, 128, 128), U32),
                        pltpu.VMEM((PEER_U_TOKENS_PER_STEP, 128, 128), U32)],
        compiler_params=_cparams(("arbitrary",)),
        name="peer_u",
    )(rows, par, x2, tab)


SC_CORES = 2
SC_SUBCORES = 16
SC_LANES = 16
SC_WORKERS = SC_CORES * SC_SUBCORES
PEER_U_SC_SHARE_NUM, PEER_U_SC_SHARE_DEN = 41, 128
PEER_V_SC_SHARE_NUM, PEER_V_SC_SHARE_DEN = 28, 128


def _peer_u_sc(tab, expert, xn, t0, t_sc):
    tw = t_sc // SC_WORKERS
    heads = PEER_PAIRS // PEER_TOPK
    chunks = D_MODEL // SC_LANES
    mesh = plsc.VectorSubcoreMesh(core_axis_name="c", subcore_axis_name="s")

    def body(tab_hbm, idx_hbm, x_hbm, d_hbm, idx_v, x_v, rows_a, rows_b, d_v, sem_a, sem_b):
        wid = lax.axis_index("s") * SC_CORES + lax.axis_index("c")
        lane = lax.iota(I32, SC_LANES)
        bufs = ((rows_a, sem_a), (rows_b, sem_b))

        def gather(h):
            buf, sem = bufs[h % 2]
            return pltpu.make_async_copy(tab_hbm.at[idx_v.at[pl.ds(h * PEER_TOPK, PEER_TOPK)]], buf, sem)

        def token(i, carry):
            t = t0 + wid * tw + i
            pltpu.sync_copy(idx_hbm.at[t], idx_v)
            pltpu.sync_copy(x_hbm.at[t], x_v)
            gather(0).start()
            for h in range(heads):
                if h + 1 < heads:
                    gather(h + 1).start()
                gather(h).wait()
                rows_v = bufs[h % 2][0]

                def chunk(j, accs):
                    xj = x_v[pl.ds(j * SC_LANES, SC_LANES)]
                    return tuple(a + rows_v[r, pl.ds(j * SC_LANES, SC_LANES)] * xj for r, a in enumerate(accs))

                accs = lax.fori_loop(0, chunks, chunk,
                                     tuple(jnp.zeros((SC_LANES,), F32) for _ in range(PEER_TOPK)))
                out = jnp.zeros((SC_LANES,), F32)
                for r in range(PEER_TOPK):
                    out = jnp.where(lane == r, jnp.sum(accs[r]), out)
                d_v[pl.ds(h * PEER_TOPK, PEER_TOPK)] = out
            pltpu.sync_copy(d_v, d_hbm.at[t - t0])
            return carry

        lax.fori_loop(0, tw, token, 0)

    return pl.kernel(
        body, mesh=mesh,
        out_type=jax.ShapeDtypeStruct((t_sc, PEER_PAIRS), F32),
        compiler_params=pltpu.CompilerParams(needs_layout_passes=False),
        scratch_types=[pltpu.VMEM((PEER_PAIRS,), I32), pltpu.VMEM((D_MODEL,), F32),
                       pltpu.VMEM((PEER_TOPK, D_MODEL), F32), pltpu.VMEM((PEER_TOPK, D_MODEL), F32),
                       pltpu.VMEM((PEER_PAIRS,), F32), pltpu.SemaphoreType.DMA, pltpu.SemaphoreType.DMA],
        name="peer_u_sc",
    )(tab, expert, xn)


def _peer_coef_kernel(d_ref, gate_ref, c_ref):
    c_ref[...] = gate_ref[...] * _gelu_tanh(d_ref[...])


def _peer_coef(d_tp, gate_tp, tm=2048):
    t = d_tp.shape[0]
    tm = min(tm, t)
    return pl.pallas_call(
        _peer_coef_kernel,
        grid=(t // tm,),
        in_specs=[_tok_spec(tm), _tok_spec(tm)],
        out_specs=_tok_spec(tm),
        out_shape=jax.ShapeDtypeStruct((t, PEER_PAIRS), F32),
        compiler_params=_cparams(("parallel",)),
        name="peer_coef",
    )(d_tp, gate_tp)


PEER_V_TOKENS_PER_STEP = 8
PEER_V_KDIM = (PEER_PAIRS // 2) * 16


def _peer_v_kernel(row_ref, par_ref, coef_ref, h_ref, tab_ref, out_ref, cz_s, *, tb):
    half = PEER_PAIRS // 2
    kdim = PEER_V_KDIM
    nt = PEER_V_TOKENS_PER_STEP
    pk = lax.broadcasted_iota(I32, (PEER_PAIRS, kdim), 0)
    qk = lax.shift_right_logical(lax.broadcasted_iota(I32, (PEER_PAIRS, kdim), 1), 4)
    hk = (lax.broadcasted_iota(I32, (tb, kdim), 1) & 1).astype(F32)
    coef = coef_ref[...].astype(BF16)
    par = par_ref[...].astype(BF16)
    for hf in range(2):
        expand = jnp.where(pk == qk + hf * half, 1.0, 0.0).astype(BF16)
        cz_s[hf] = jnp.where(hk == _dot(par, expand), _dot(coef, expand), 0.0)
    ks = lax.broadcasted_iota(I32, (8, kdim), 0)
    k8 = lax.broadcasted_iota(I32, (8, kdim), 1)
    own_sublane = lax.shift_right_logical(k8 & 15, 1) == ks

    def tok_body(tt, carry):
        rows8 = pl.ds(pl.multiple_of(tt * nt, nt), nt)
        hb = h_ref[rows8, :]
        tiles = []
        for j in range(nt):
            t = tt * nt + j
            pieces = []
            for q in range(half):
                wa = pltpu.bitcast(_load_words(tab_ref, row_ref, t, q), BF16)
                wb = pltpu.bitcast(_load_words(tab_ref, row_ref, t, half + q), BF16)
                pieces.append(jnp.concatenate([wa, wb], axis=1))
            g = jnp.concatenate(pieces, axis=0)
            ck = jnp.concatenate(
                [jnp.where(own_sublane, jnp.broadcast_to(cz_s[hf, pl.ds(t, 1), :], (8, kdim)), 0.0)
                 for hf in range(2)], axis=0).astype(BF16)
            o = _dot(ck, g)
            tiles.append(_token_tile(hb, j) + (o[0:8, 0:128] + o[8:16, 128:256]))
        out_ref[rows8, :] = _tiles_to_rows(tiles)
        return carry

    lax.fori_loop(0, tb // nt, tok_body, 0)


def _peer_v_sc(tab, expert, coef, h2, t0, t_sc):
    tw = t_sc // SC_WORKERS
    heads = PEER_PAIRS // PEER_TOPK
    chunks = D_MODEL // SC_LANES
    mesh = plsc.VectorSubcoreMesh(core_axis_name="c", subcore_axis_name="s")

    def body(tab_hbm, idx_hbm, coef_hbm, h_hbm, out_hbm, idx_v, c_v, acc_v, rows_a, rows_b, sem_a, sem_b):
        wid = lax.axis_index("s") * SC_CORES + lax.axis_index("c")
        lane = lax.iota(I32, SC_LANES)
        bufs = ((rows_a, sem_a), (rows_b, sem_b))

        def gather(h):
            buf, sem = bufs[h % 2]
            return pltpu.make_async_copy(tab_hbm.at[idx_v.at[pl.ds(h * PEER_TOPK, PEER_TOPK)]], buf, sem)

        def token(i, carry):
            t = t0 + wid * tw + i
            pltpu.sync_copy(idx_hbm.at[t], idx_v)
            gather(0).start()
            pltpu.sync_copy(coef_hbm.at[t], c_v)
            pltpu.sync_copy(h_hbm.at[t], acc_v)
            for h in range(heads):
                if h + 1 < heads:
                    gather(h + 1).start()
                gather(h).wait()
                rows_v = bufs[h % 2][0]
                cvec = c_v[pl.ds(h * PEER_TOPK, PEER_TOPK)]
                cs = [jnp.sum(jnp.where(lane == r, cvec, 0.0)) for r in range(PEER_TOPK)]

                def chunk(j, carry2):
                    cols = pl.ds(j * SC_LANES, SC_LANES)
                    a = acc_v[cols]
                    for r in range(PEER_TOPK):
                        a = a + cs[r] * rows_v[r, cols]
                    acc_v[cols] = a
                    return carry2

                lax.fori_loop(0, chunks, chunk, 0)
            pltpu.sync_copy(acc_v, out_hbm.at[t - t0])
            return carry

        lax.fori_loop(0, tw, token, 0)

    return pl.kernel(
        body, mesh=mesh,
        out_type=jax.ShapeDtypeStruct((t_sc, D_MODEL), F32),
        compiler_params=pltpu.CompilerParams(needs_layout_passes=False),
        scratch_types=[pltpu.VMEM((PEER_PAIRS,), I32), pltpu.VMEM((PEER_PAIRS,), F32), pltpu.VMEM((D_MODEL,), F32),
                       pltpu.VMEM((PEER_TOPK, D_MODEL), F32), pltpu.VMEM((PEER_TOPK, D_MODEL), F32),
                       pltpu.SemaphoreType.DMA, pltpu.SemaphoreType.DMA],
        name="peer_v_sc",
    )(tab, expert, coef, h2)


def _peer_v(rows, par, coef, h2, tab, t, tb=128):
    return pl.pallas_call(
        functools.partial(_peer_v_kernel, tb=tb),
        grid=(t // tb,),
        in_specs=[_tok_spec(tb, pltpu.SMEM), _tok_spec(tb), _tok_spec(tb),
                  pl.BlockSpec((tb, D_MODEL), lambda i: (i, 0)), _table_spec()],
        out_specs=pl.BlockSpec((tb, D_MODEL), lambda i: (i, 0)),
        out_shape=jax.ShapeDtypeStruct((t, D_MODEL), F32),
        scratch_shapes=[pltpu.VMEM((2, tb, PEER_V_KDIM), F32)],
        compiler_params=_cparams(("arbitrary",)),
        name="peer_v",
    )(rows, par, coef, h2, tab)


def _final_norm_kernel(x_ref, g_ref, o_ref):
    x = x_ref[...]
    ms = jnp.mean(x * x, axis=-1, keepdims=True)
    o_ref[...] = x * lax.rsqrt(ms + NORM_EPS) * g_ref[...]


def _final_norm(x2, gain, tm=1024):
    t = x2.shape[0]
    tm = min(tm, t)
    row = lambda i: (i, 0)
    return pl.pallas_call(
        _final_norm_kernel,
        grid=(t // tm,),
        in_specs=[pl.BlockSpec((tm, D_MODEL), row), pl.BlockSpec((1, D_MODEL), lambda i: (0, 0))],
        out_specs=pl.BlockSpec((tm, D_MODEL), row),
        out_shape=jax.ShapeDtypeStruct((t, D_MODEL), F32),
        compiler_params=_cparams(("parallel",)),
        name="final_norm",
    )(x2, gain)


def _pack_table(tab):
    bits = lax.bitcast_convert_type(tab.astype(BF16), jnp.uint16).astype(U32)
    return (bits[:TAB_ROWS] | (bits[TAB_ROWS:] << 16)).reshape(TAB_ROWS * 8, 128)


def _peer(h2, layer, p):
    t = h2.shape[0]
    keys = p["peer_keys"][layer].reshape(2 * PEER_HEADS, N_KEYS, PEER_DHALF).astype(BF16)
    xn, expert, rows, par, gate = _peer_route(h2, p["norm2_g"][layer][None, :],
                                              p["peer_wq"][layer].T.astype(BF16), keys)
    expert, rows, par, gate = expert.T, rows.T, par.T, gate.T
    t_sc = (t * PEER_U_SC_SHARE_NUM // PEER_U_SC_SHARE_DEN) // (8 * SC_WORKERS) * (8 * SC_WORKERS)
    t_tc = t - t_sc
    expert_all = expert + layer * N_EXPERTS
    stacked = lambda tab: tab.reshape(DEPTH * N_EXPERTS, D_MODEL)
    d_sc = _peer_u_sc(stacked(p["peer_u"]), expert_all, xn, t_tc, t_sc)
    d = _peer_u(rows, par, xn, _pack_table(p["peer_u"][layer]), t_tc)
    d_tp = d.reshape(t_tc // 8, 8, 8, PEER_TOPK).transpose(0, 2, 3, 1).reshape(t_tc, PEER_PAIRS)
    coef = _peer_coef(jnp.concatenate([d_tp, d_sc], axis=0), gate)
    t_sc = (t * PEER_V_SC_SHARE_NUM // PEER_V_SC_SHARE_DEN) // (8 * SC_WORKERS) * (8 * SC_WORKERS)
    t_tc = t - t_sc
    out_sc = _peer_v_sc(stacked(p["peer_v"]), expert_all, coef, h2, t_tc, t_sc)
    out_tc = _peer_v(rows, par, coef, h2, _pack_table(p["peer_v"][layer]), t_tc)
    return jnp.concatenate([out_tc, out_sc], axis=0)


def _rep(x, n):
    return jnp.repeat(x, n, axis=-1)


def _inproj_weight(w_in_l):
    o = 0
    qkv = w_in_l[:, o:o + GDN_QKV_W]; o += GDN_QKV_W
    z = w_in_l[:, o:o + GDN_W]; o += GDN_W
    a = w_in_l[:, o:o + GDN_HEADS]; o += GDN_HEADS
    b = w_in_l[:, o:o + GDN_HEADS]; o += GDN_HEADS
    rest = w_in_l[:, o:]
    return jnp.concatenate([qkv, z, _rep(a, GDN_DK), _rep(b, GDN_DK), rest], axis=1).astype(BF16)


def _block_diag_gc(b_gnc):
    g = b_gnc.shape[0]
    eye = jnp.eye(g, dtype=b_gnc.dtype)
    t = jnp.swapaxes(b_gnc, 1, 2)
    return (t[:, :, None, :] * eye[:, None, :, None]).reshape(g * t.shape[1], g * t.shape[2])


def _mixers(h, layer, p):
    bn, seq, _ = h.shape
    t = bn * seq
    lam_init = 0.8 - 0.6 * math.exp(-0.3 * layer)
    gdn_in, us, qc, kc, vc = _inproj(h.reshape(t, D_MODEL), p["norm1_g"][layer][None, :],
                                     _inproj_weight(p["w_in"][layer]))
    a_log = _rep(p["gdn_a_log"][layer].astype(F32), GDN_DK)[None, :]
    dtb = _rep(p["gdn_dt_bias"][layer].astype(F32), GDN_DK)[None, :]
    o_a = _gdn(gdn_in.reshape(bn, seq, INPROJ_GDN_W), p["gdn_conv_w"][layer], a_log, dtb,
               p["gdn_norm_g"][layer][None, :])
    flat = lambda x: x.reshape(1, S5_NS)
    bre_bd = _block_diag_gc(p["s5_b_re"][layer])
    bim_bd = _block_diag_gc(p["s5_b_im"][layer])
    cre_bd = _block_diag_gc(p["s5_c_re"][layer])
    cim_bd = _block_diag_gc(p["s5_c_im"][layer])
    ys = _s5(us.reshape(bn, seq // S5_SUB, S5_SUB * S5_WIDTH),
             flat(p["s5_lambda_re"][layer]), flat(p["s5_lambda_im"][layer]),
             flat(_rep(p["s5_log_step"][layer][:, None], S5_STATE)),
             bre_bd, bim_bd, cre_bd, cim_bd, p["s5_d"][layer][None, :])
    row = lambda x: x[None, :]
    o_c = _attn(qc.reshape(bn, seq, DIFF_W), kc.reshape(bn, seq, DIFF_W), vc.reshape(bn, seq, DIFF_W),
                row(p["diff_lq1"][layer]), row(p["diff_lk1"][layer]), row(p["diff_lq2"][layer]),
                row(p["diff_lk2"][layer]), row(p["diff_norm_g"][layer]), lam_init)
    h2 = _outproj(h.reshape(t, D_MODEL), o_a.reshape(t, GDN_W), ys.reshape(t, S5_WIDTH),
                  o_c.reshape(t, DIFF_W), p["s5_glu_w"][layer].astype(BF16), p["s5_glu_b"][layer][None, :],
                  p["w_out"][layer].astype(BF16))
    return h2


def kernel(x, norm1_g, w_in, gdn_conv_w, gdn_a_log, gdn_dt_bias, gdn_norm_g, s5_lambda_re, s5_lambda_im, s5_b_re, s5_b_im, s5_c_re, s5_c_im, s5_d, s5_log_step, s5_glu_w, s5_glu_b, diff_lq1, diff_lk1, diff_lq2, diff_lk2, diff_norm_g, w_out, norm2_g, peer_wq, peer_keys, peer_u, peer_v, final_g):
    p = dict(norm1_g=norm1_g, w_in=w_in, gdn_conv_w=gdn_conv_w, gdn_a_log=gdn_a_log, gdn_dt_bias=gdn_dt_bias,
             gdn_norm_g=gdn_norm_g, s5_lambda_re=s5_lambda_re, s5_lambda_im=s5_lambda_im, s5_b_re=s5_b_re,
             s5_b_im=s5_b_im, s5_c_re=s5_c_re, s5_c_im=s5_c_im, s5_d=s5_d, s5_log_step=s5_log_step,
             s5_glu_w=s5_glu_w, s5_glu_b=s5_glu_b, diff_lq1=diff_lq1, diff_lk1=diff_lk1, diff_lq2=diff_lq2,
             diff_lk2=diff_lk2, diff_norm_g=diff_norm_g, w_out=w_out, norm2_g=norm2_g, peer_wq=peer_wq,
             peer_keys=peer_keys, peer_u=peer_u, peer_v=peer_v, final_g=final_g)
    h = x
    for layer in range(DEPTH):
        h2 = _mixers(h, layer, p)
        h = _peer(h2, layer, p).reshape(x.shape)
    return _final_norm(h.reshape(-1, D_MODEL), final_g[None, :]).reshape(x.shape)
```

```python
import functools
import math

import jax
import jax.numpy as jnp
from jax import lax
from jax.experimental import pallas as pl
from jax.experimental.pallas import tpu as pltpu
from jax.experimental.pallas import tpu_sc as plsc

F32 = jnp.float32
BF16 = jnp.bfloat16
I32 = jnp.int32
U32 = jnp.uint32

D_MODEL = 1024
DEPTH = 2
GDN_HEADS = 4
GDN_DK = 64
GDN_CHUNK = 64
GDN_W = GDN_HEADS * GDN_DK
GDN_QKV_W = 3 * GDN_W
GDN_CONV = 4
S5_WIDTH = 256
S5_GROUPS = 16
S5_GROUP_CH = 16
S5_STATE = 64
S5_NS = S5_GROUPS * S5_STATE
S5_SUB = 16
DIFF_HEADS = 4
DIFF_DQK = 64
DIFF_DV = 128
DIFF_W = 512
PEER_HEADS = 8
PEER_DHALF = 64
N_KEYS = 128
N_EXPERTS = N_KEYS * N_KEYS
PEER_TOPK = 16
PEER_PAIRS = PEER_HEADS * PEER_TOPK
NORM_EPS = 1e-6
NEG_INF = float("-inf")

VMEM_LIMIT_BYTES = 56 * 1024 * 1024


def _cparams(sem, vmem=VMEM_LIMIT_BYTES):
    return pltpu.CompilerParams(dimension_semantics=sem, vmem_limit_bytes=vmem)


def _dot(a, b):
    return jnp.dot(a, b, preferred_element_type=F32)


def _dot_nt(a, b):
    return lax.dot_general(a, b, (((1,), (1,)), ((), ())), preferred_element_type=F32)


def _dot_tn(a, b):
    return lax.dot_general(a, b, (((0,), (0,)), ((), ())), preferred_element_type=F32)


def _split(x):
    hi = x.astype(BF16)
    lo = (x - hi.astype(F32)).astype(BF16)
    return hi, lo


def _dot_sel_r(x, sel):
    hi, lo = _split(x)
    return _dot(hi, sel) + _dot(lo, sel)


def _dot_sel_l(sel, x):
    hi, lo = _split(x)
    return _dot(sel, hi) + _dot(sel, lo)


def _mm3(a, b):
    ah, al = _split(a)
    bh, bl = _split(b)
    return _dot(ah, bh) + (_dot(ah, bl) + _dot(al, bh))


def _sigmoid(x):
    return 1.0 / (1.0 + jnp.exp(-x))


def _softplus(x):
    return jnp.maximum(x, 0.0) + jnp.log1p(jnp.exp(-jnp.abs(x)))


def _gelu_tanh(x):
    c = math.sqrt(2.0 / math.pi)
    return 0.5 * x * (1.0 + jnp.tanh(c * (x + 0.044715 * (x * x * x))))


INPROJ_GDN_W = GDN_QKV_W + 3 * GDN_W


def _inproj_kernel(x_ref, g_ref, w_ref, gdn_ref, us_ref, q_ref, k_ref, v_ref):
    x = x_ref[...]
    ms = jnp.mean(x * x, axis=-1, keepdims=True)
    xn = (x * lax.rsqrt(ms + NORM_EPS) * g_ref[...]).astype(BF16)
    o = INPROJ_GDN_W
    gdn_ref[...] = _dot(xn, w_ref[:, 0:o])
    us_ref[...] = _dot(xn, w_ref[:, o:o + S5_WIDTH])
    o += S5_WIDTH
    q_ref[...] = (_dot(xn, w_ref[:, o:o + DIFF_W]) * (DIFF_DQK ** -0.5)).astype(BF16)
    k_ref[...] = _dot(xn, w_ref[:, o + DIFF_W:o + 2 * DIFF_W]).astype(BF16)
    v_ref[...] = _dot(xn, w_ref[:, o + 2 * DIFF_W:o + 3 * DIFF_W]).astype(BF16)


def _inproj(x2, gain, w, tm=512):
    t = x2.shape[0]
    nw = w.shape[1]
    row = lambda i: (i, 0)
    fixed = lambda i: (0, 0)
    return pl.pallas_call(
        _inproj_kernel,
        grid=(t // tm,),
        in_specs=[pl.BlockSpec((tm, D_MODEL), row), pl.BlockSpec((1, D_MODEL), fixed),
                  pl.BlockSpec((D_MODEL, nw), fixed)],
        out_specs=[pl.BlockSpec((tm, INPROJ_GDN_W), row), pl.BlockSpec((tm, S5_WIDTH), row),
                   pl.BlockSpec((tm, DIFF_W), row), pl.BlockSpec((tm, DIFF_W), row),
                   pl.BlockSpec((tm, DIFF_W), row)],
        out_shape=[jax.ShapeDtypeStruct((t, INPROJ_GDN_W), F32), jax.ShapeDtypeStruct((t, S5_WIDTH), F32),
                   jax.ShapeDtypeStruct((t, DIFF_W), BF16), jax.ShapeDtypeStruct((t, DIFF_W), BF16),
                   jax.ShapeDtypeStruct((t, DIFF_W), BF16)],
        compiler_params=_cparams(("parallel",)),
        name="inproj",
    )(x2, gain, w)


def _gdn_kernel(blk_ref, convw_ref, alog_ref, dtb_ref, ng_ref, out_ref,
                s_ref, tail_ref, xp_ref, q_s, k_s, v_s, b_s, g_s, *, ct):
    c64 = GDN_CHUNK

    @pl.when(pl.program_id(1) == 0)
    def _():
        s_ref[...] = jnp.zeros_like(s_ref)
        tail_ref[...] = jnp.zeros_like(tail_ref)

    qkv = blk_ref[0, :, 0:GDN_QKV_W]
    xp_ref[0:8, :] = tail_ref[...]
    xp_ref[8:8 + ct, :] = qkv
    tail_ref[...] = qkv[ct - 8:ct, :]
    cw = convw_ref[...]
    y = cw[0:1, :] * xp_ref[5:5 + ct, :]
    for j in range(1, GDN_CONV):
        y = y + cw[j:j + 1, :] * xp_ref[5 + j:5 + j + ct, :]
    y = y * _sigmoid(y)

    ri = lax.broadcasted_iota(I32, (GDN_W, GDN_W), 0)
    ci = lax.broadcasted_iota(I32, (GDN_W, GDN_W), 1)
    head_ones = jnp.where((ri // c64) == (ci // c64), 1.0, 0.0).astype(BF16)

    q = y[:, 0:GDN_W]
    k = y[:, GDN_W:2 * GDN_W]
    q_s[...] = q * lax.rsqrt(_dot_sel_r(q * q, head_ones) + 1e-6) * (GDN_DK ** -0.5)
    k_s[...] = k * lax.rsqrt(_dot_sel_r(k * k, head_ones) + 1e-6)
    v_s[...] = y[:, 2 * GDN_W:3 * GDN_W]
    a_rep = blk_ref[0, :, GDN_QKV_W + GDN_W:GDN_QKV_W + 2 * GDN_W]
    b_rep = blk_ref[0, :, GDN_QKV_W + 2 * GDN_W:GDN_QKV_W + 3 * GDN_W]
    b_s[...] = _sigmoid(b_rep)
    g_raw = -jnp.exp(alog_ref[...]) * _softplus(a_rep + dtb_ref[...])
    rt = lax.broadcasted_iota(I32, (ct, ct), 0)
    ctk = lax.broadcasted_iota(I32, (ct, ct), 1)
    tri_bd = jnp.where(((rt // c64) == (ctk // c64)) & (ctk <= rt), 1.0, 0.0).astype(BF16)
    g_s[...] = _dot_sel_l(tri_bd, g_raw)

    r64 = lax.broadcasted_iota(I32, (c64, c64), 0)
    col64 = lax.broadcasted_iota(I32, (c64, c64), 1)
    incl = col64 <= r64
    strict = col64 < r64
    eye_b = col64 == r64
    eye_f = jnp.where(eye_b, 1.0, 0.0)
    ones64 = jnp.ones((c64, c64), BF16)
    ng = ng_ref[...]

    n_chunks = ct // c64
    heads = range(GDN_HEADS)
    chains = [(slice(c * c64, (c + 1) * c64), slice(h * c64, (h + 1) * c64))
              for c in range(n_chunks) for h in heads]
    each = lambda f, *ls: [f(*xs) for xs in zip(*ls)]
    gi = [g_s[r, l] for r, l in chains]
    gj = each(lambda g: _dot_sel_l(ones64, jnp.where(eye_b, g, 0.0)), gi)
    dec = each(lambda a, b: jnp.exp(jnp.where(incl, a - b, NEG_INF)), gi, gj)
    kh = [k_s[r, l] for r, l in chains]
    bi = [b_s[r, l] for r, l in chains]
    kb = each(lambda x: x.astype(BF16), kh)
    kk = each(_dot_nt, kb, kb)
    lm = each(lambda b, x, d: jnp.where(strict, b * x * d, 0.0), bi, kk, dec)
    tinv = each(lambda x: eye_f - x, lm)
    pw = lm
    for _ in range(5):
        pw = each(_mm3, pw, pw)
        tinv = each(lambda t, m: t + _mm3(t, m), tinv, pw)
    tb = each(lambda x: x.astype(BF16), tinv)
    eg = each(jnp.exp, gi)
    u = each(lambda t, rl, b: _dot(t, (v_s[rl[0], rl[1]] * b).astype(BF16)), tb, chains, bi)
    w = each(lambda t, k, b, e: _dot(t, (k * (b * e)).astype(BF16)).astype(BF16), tb, kh, bi, eg)
    qh = [q_s[r, l] for r, l in chains]
    qk = each(lambda q, k, d: jnp.where(incl, _dot_nt(q.astype(BF16), k) * d, 0.0).astype(BF16), qh, kb, dec)
    qg = each(lambda q, e: (q * e).astype(BF16), qh, eg)
    g_last = each(lambda g: g[c64 - 1:c64, :], gi)
    kg = each(lambda k, gl, g: (k * jnp.exp(gl - g)).astype(BF16), kh, g_last, gi)
    decay_last = each(jnp.exp, g_last)

    states = [s_ref[h] for h in heads]
    for c in range(n_chunks):
        ids = [c * GDN_HEADS + h for h in heads]
        sb = [s.astype(BF16) for s in states]
        vnb = [(u[n] - _dot(w[n], sb[h])).astype(BF16) for h, n in zip(heads, ids)]
        o = [_dot(qg[n], sb[h]) + _dot(qk[n], vnb[h]) for h, n in zip(heads, ids)]
        states = [states[h] * decay_last[n] + _dot_tn(kg[n], vnb[h]) for h, n in zip(heads, ids)]
        ms = [_dot_sel_r(x * x, ones64) * (1.0 / c64) for x in o]
        outs = [x * lax.rsqrt(m + NORM_EPS) * ng for x, m in zip(o, ms)]
        rows = slice(c * c64, (c + 1) * c64)
        z = blk_ref[0, rows, GDN_QKV_W:GDN_QKV_W + GDN_W]
        out_ref[0, rows, :] = (jnp.concatenate(outs, axis=1) * (z * _sigmoid(z))).astype(out_ref.dtype)
    for h in heads:
        s_ref[h] = states[h]


def _gdn(gdn_in, conv_w, a_log_rep, dtb_rep, ng, ct=256):
    bn, seq, _ = gdn_in.shape
    fixed = lambda b, l: (0, 0)
    return pl.pallas_call(
        functools.partial(_gdn_kernel, ct=ct),
        grid=(bn, seq // ct),
        in_specs=[pl.BlockSpec((1, ct, INPROJ_GDN_W), lambda b, l: (b, l, 0)),
                  pl.BlockSpec((GDN_CONV, GDN_QKV_W), fixed), pl.BlockSpec((1, GDN_W), fixed),
                  pl.BlockSpec((1, GDN_W), fixed), pl.BlockSpec((1, GDN_DK), fixed)],
        out_specs=pl.BlockSpec((1, ct, GDN_W), lambda b, l: (b, l, 0)),
        out_shape=jax.ShapeDtypeStruct((bn, seq, GDN_W), BF16),
        scratch_shapes=[pltpu.VMEM((GDN_HEADS, GDN_DK, GDN_DK), F32), pltpu.VMEM((8, GDN_QKV_W), F32),
                        pltpu.VMEM((ct + 8, GDN_QKV_W), F32)] + [pltpu.VMEM((ct, GDN_W), F32)] * 5,
        compiler_params=_cparams(("arbitrary", "arbitrary")),
        name="gdn",
    )(gdn_in, conv_w, a_log_rep, dtb_rep, ng)


def _s5_kernel(u_ref, lre_ref, lim_ref, lstep_ref, bre_ref, bim_ref, cre_ref, cim_ref, d_ref, y_ref,
               bmat, cmat, avec, carry, zr_s, zi_s, xr_s, xi_s, *, rb):
    ns = S5_NS
    cw = S5_WIDTH

    @pl.when(pl.program_id(1) == 0)
    def _():
        step = jnp.exp(lstep_ref[...])
        lr = lre_ref[...]
        li = lim_ref[...]
        mag = jnp.exp(lr * step)
        ar = mag * jnp.cos(li * step)
        ai = mag * jnp.sin(li * step)
        den = lr * lr + li * li
        mr = ((ar - 1.0) * lr + ai * li) / den
        mi = (ai * lr - (ar - 1.0) * li) / den
        bre = bre_ref[...]
        bim = bim_ref[...]
        bmat[:, 0:ns] = (mr * bre - mi * bim).astype(BF16)
        bmat[:, ns:2 * ns] = (mr * bim + mi * bre).astype(BF16)
        cmat[0:ns, :] = cre_ref[...].astype(BF16)
        cmat[ns:2 * ns, :] = (-cim_ref[...]).astype(BF16)
        avec[0:1, :] = ar
        avec[1:2, :] = ai
        pr, pi = ar, ai
        for _ in range(4):
            pr, pi = pr * pr - pi * pi, 2.0 * pr * pi
        avec[2:3, :] = pr
        avec[3:4, :] = pi
        carry[...] = jnp.zeros_like(carry)

    ar = avec[0:1, :]
    ai = avec[1:2, :]
    a16r = avec[2:3, :]
    a16i = avec[3:4, :]

    def inject(s):
        ub = u_ref[0, :, s * cw:(s + 1) * cw].astype(BF16)
        return _dot(ub, bmat[...])

    xr = jnp.zeros((rb, ns), F32)
    xi = jnp.zeros((rb, ns), F32)
    for s in range(S5_SUB):
        p = inject(s)
        xr, xi = ar * xr - ai * xi + p[:, 0:ns], ar * xi + ai * xr + p[:, ns:2 * ns]
    zr_s[...] = xr
    zi_s[...] = xi

    def row_step(kk, c):
        cr, ci_ = c
        xr_s[pl.ds(kk, 1), :] = cr
        xi_s[pl.ds(kk, 1), :] = ci_
        zr = zr_s[pl.ds(kk, 1), :]
        zi = zi_s[pl.ds(kk, 1), :]
        return (a16r * cr - a16i * ci_ + zr, a16r * ci_ + a16i * cr + zi)

    cr, ci_ = lax.fori_loop(0, rb, row_step, (carry[0:1, :], carry[1:2, :]))
    carry[0:1, :] = cr
    carry[1:2, :] = ci_

    xr = xr_s[...]
    xi = xi_s[...]
    dsk = d_ref[...]
    for s in range(S5_SUB):
        p = inject(s)
        xr, xi = ar * xr - ai * xi + p[:, 0:ns], ar * xi + ai * xr + p[:, ns:2 * ns]
        yv = _dot(xr.astype(BF16), cmat[0:ns, :]) + _dot(xi.astype(BF16), cmat[ns:2 * ns, :])
        yv = yv + dsk * u_ref[0, :, s * cw:(s + 1) * cw]
        y_ref[0, :, s * cw:(s + 1) * cw] = _gelu_tanh(yv).astype(y_ref.dtype)


def _s5(u_rows, lre, lim, lstep, bre_bd, bim_bd, cre_bd, cim_bd, dskip, rb=128):
    bn, nrows, rw = u_rows.shape
    fixed = lambda b, r: (0, 0)
    ns = S5_NS
    return pl.pallas_call(
        functools.partial(_s5_kernel, rb=rb),
        grid=(bn, nrows // rb),
        in_specs=[pl.BlockSpec((1, rb, rw), lambda b, r: (b, r, 0)),
                  pl.BlockSpec((1, ns), fixed), pl.BlockSpec((1, ns), fixed), pl.BlockSpec((1, ns), fixed),
                  pl.BlockSpec((S5_WIDTH, ns), fixed), pl.BlockSpec((S5_WIDTH, ns), fixed),
                  pl.BlockSpec((ns, S5_WIDTH), fixed), pl.BlockSpec((ns, S5_WIDTH), fixed),
                  pl.BlockSpec((1, S5_WIDTH), fixed)],
        out_specs=pl.BlockSpec((1, rb, rw), lambda b, r: (b, r, 0)),
        out_shape=jax.ShapeDtypeStruct((bn, nrows, rw), BF16),
        scratch_shapes=[pltpu.VMEM((S5_WIDTH, 2 * ns), BF16), pltpu.VMEM((2 * ns, S5_WIDTH), BF16),
                        pltpu.VMEM((8, ns), F32), pltpu.VMEM((8, ns), F32)]
                       + [pltpu.VMEM((rb, ns), F32)] * 4,
        compiler_params=_cparams(("arbitrary", "arbitrary")),
        name="s5",
    )(u_rows, lre, lim, lstep, bre_bd, bim_bd, cre_bd, cim_bd, dskip)


def _attn_kernel(q_ref, k_ref, v_ref, lq1_ref, lk1_ref, lq2_ref, lk2_ref, ng_ref, o_ref,
                 qs_s, m_s, l_s, acc_s, *, blk, lam_init):
    i = pl.program_id(2)
    q = q_ref[0]
    lane = lax.broadcasted_iota(I32, q.shape, 1)
    zero = jnp.zeros_like(q)
    qs_s[0:blk, :] = jnp.where(lane < DIFF_DQK, q, zero)
    qs_s[blk:2 * blk, :] = jnp.where(lane >= DIFF_DQK, q, zero)
    m_s[...] = jnp.full_like(m_s, NEG_INF)
    l_s[...] = jnp.zeros_like(l_s)
    acc_s[...] = jnp.zeros_like(acc_s)
    ones = jnp.ones((blk, DIFF_DV), BF16)

    def block_rows(j):
        return pl.ds(pl.multiple_of(j * blk, blk), blk)

    def scores(j):
        return _dot_nt(qs_s[...], k_ref[0, block_rows(j), :])

    def update(j, s):
        m_old = m_s[...]
        m_new = jnp.maximum(m_old, jnp.max(s, axis=-1, keepdims=True))
        p = jnp.exp(s - jnp.concatenate([m_new] * (blk // DIFF_DV), axis=1)).astype(BF16)
        alpha = jnp.exp(m_old - m_new)
        pv = _dot(p, jnp.concatenate([v_ref[0, block_rows(j), :], ones], axis=1))
        acc_s[...] = alpha * acc_s[...] + pv[:, 0:DIFF_DV]
        l_s[...] = alpha * l_s[...] + pv[:, DIFF_DV:2 * DIFF_DV]
        m_s[...] = m_new

    def body(j, s):
        s_next = scores(j + 1)
        update(j, s)
        return s_next

    s = lax.fori_loop(0, i, body, scores(0))
    row = lax.broadcasted_iota(I32, s.shape, 0) & (blk - 1)
    col = lax.broadcasted_iota(I32, s.shape, 1)
    update(i, jnp.where(col <= row, s, NEG_INF))
    lam = (jnp.exp(jnp.sum(lq1_ref[...] * lk1_ref[...], axis=-1, keepdims=True))
           - jnp.exp(jnp.sum(lq2_ref[...] * lk2_ref[...], axis=-1, keepdims=True)) + lam_init)
    o = acc_s[0:blk, :] / l_s[0:blk, :] - lam * (acc_s[blk:2 * blk, :] / l_s[blk:2 * blk, :])
    ms = jnp.mean(o * o, axis=-1, keepdims=True)
    o = o * lax.rsqrt(ms + NORM_EPS) * ng_ref[...] * (1.0 - lam_init)
    o_ref[0] = o.astype(o_ref.dtype)


def _attn(q, k, v, lq1, lk1, lq2, lk2, ng, lam_init, blk=512):
    bn, seq, _ = q.shape
    blk = min(blk, seq)
    fixed = lambda b, h, i: (0, 0)
    kv_spec = pl.BlockSpec((1, seq, DIFF_DV), lambda b, h, i: (b, 0, h))
    return pl.pallas_call(
        functools.partial(_attn_kernel, blk=blk, lam_init=lam_init),
        grid=(bn, DIFF_HEADS, seq // blk),
        in_specs=[pl.BlockSpec((1, blk, DIFF_DV), lambda b, h, i: (b, i, h)), kv_spec, kv_spec]
                 + [pl.BlockSpec((1, DIFF_DQK), fixed)] * 4 + [pl.BlockSpec((1, DIFF_DV), fixed)],
        out_specs=pl.BlockSpec((1, blk, DIFF_DV), lambda b, h, i: (b, i, h)),
        out_shape=jax.ShapeDtypeStruct((bn, seq, DIFF_W), BF16),
        scratch_shapes=[pltpu.VMEM((2 * blk, DIFF_DV), BF16), pltpu.VMEM((2 * blk, DIFF_DV), F32),
                        pltpu.VMEM((2 * blk, DIFF_DV), F32), pltpu.VMEM((2 * blk, DIFF_DV), F32)],
        compiler_params=_cparams(("parallel", "parallel", "arbitrary")),
        name="diff_attn",
    )(q, k, v, lq1, lk1, lq2, lk2, ng)


def _outproj_kernel(h_ref, oa_ref, ys_ref, oc_ref, gw_ref, gb_ref, wo_ref, out_ref):
    zg = _dot(ys_ref[...], gw_ref[...]) + gb_ref[...]
    ob = (zg[:, 0:S5_WIDTH] * _sigmoid(zg[:, S5_WIDTH:2 * S5_WIDTH])).astype(BF16)
    acc = _dot(oa_ref[...], wo_ref[0:GDN_W, :])
    acc = acc + _dot(ob, wo_ref[GDN_W:GDN_W + S5_WIDTH, :])
    acc = acc + _dot(oc_ref[...], wo_ref[GDN_W + S5_WIDTH:, :])
    out_ref[...] = h_ref[...] + acc


def _outproj(h2, oa, ys, oc, glu_w, glu_b, w_out, tm=512):
    t = h2.shape[0]
    row = lambda i: (i, 0)
    fixed = lambda i: (0, 0)
    return pl.pallas_call(
        _outproj_kernel,
        grid=(t // tm,),
        in_specs=[pl.BlockSpec((tm, D_MODEL), row), pl.BlockSpec((tm, GDN_W), row),
                  pl.BlockSpec((tm, S5_WIDTH), row), pl.BlockSpec((tm, DIFF_W), row),
                  pl.BlockSpec((S5_WIDTH, 2 * S5_WIDTH), fixed), pl.BlockSpec((1, 2 * S5_WIDTH), fixed),
                  pl.BlockSpec((D_MODEL, D_MODEL), fixed)],
        out_specs=pl.BlockSpec((tm, D_MODEL), row),
        out_shape=jax.ShapeDtypeStruct((t, D_MODEL), F32),
        compiler_params=_cparams(("parallel",)),
        name="outproj",
    )(h2, oa, ys, oc, glu_w, glu_b, w_out)


_BIG_ID = 1.0e9


def _top16(x, ids, payload):
    n = x.shape[1]
    r16 = lax.broadcasted_iota(I32, (PEER_TOPK, n), 0)
    vals = jnp.zeros((PEER_TOPK, n), F32)
    pays = jnp.zeros((PEER_TOPK, n), F32)
    for kk in range(PEER_TOPK):
        m = jnp.max(x, axis=0, keepdims=True)
        first = jnp.min(jnp.where(x == m, ids, _BIG_ID), axis=0, keepdims=True)
        hit = ids == first
        pay = first if payload is None else jnp.max(jnp.where(hit, payload, -1.0), axis=0, keepdims=True)
        x = jnp.where(hit, NEG_INF, x)
        vals = jnp.where(r16 == kk, m, vals)
        pays = jnp.where(r16 == kk, pay, pays)
    return vals, pays


def _peer_route_kernel(h_ref, g_ref, wqt_ref, keys_ref, xn_ref, exp_ref, row_ref, par_ref, gate_ref,
                       qt_s, sv_s, si_s, *, tm):
    x = h_ref[...]
    ms = jnp.mean(x * x, axis=-1, keepdims=True)
    xn = x * lax.rsqrt(ms + NORM_EPS) * g_ref[...]
    xn_ref[...] = xn
    qt_s[...] = _dot_nt(wqt_ref[...], xn.astype(BF16)).astype(BF16)

    key_id = lax.broadcasted_iota(I32, (N_KEYS, tm), 0).astype(F32)

    def half_body(hp, carry):
        r0 = pl.multiple_of(hp * PEER_DHALF, PEER_DHALF)
        s = _dot(keys_ref[hp], qt_s[pl.ds(r0, PEER_DHALF), :])
        vals, ids = _top16(s, key_id, None)
        sv_s[hp] = vals
        si_s[hp] = ids
        return carry

    lax.fori_loop(0, 2 * PEER_HEADS, half_body, 0)

    i8 = lax.broadcasted_iota(I32, (8, tm), 0).astype(F32)

    def head_body(hd, carry):
        a0 = sv_s[2 * hd]
        a1 = sv_s[2 * hd + 1]
        e0 = si_s[2 * hd] * float(N_KEYS)
        e1 = si_s[2 * hd + 1]
        cs, es, fs = [], [], []
        for i in range(8):
            cs.append(a0[i:i + 1, :] + a1[0:8, :])
            es.append(e0[i:i + 1, :] + e1[0:8, :])
            fs.append(i8 + float(i * PEER_TOPK))
        cs.append(a0[0:1, :] + a1[8:16, :])
        es.append(e0[0:1, :] + e1[8:16, :])
        fs.append(i8 + 8.0)
        cs.append(a0[8:16, :] + a1[0:1, :])
        es.append(e0[8:16, :] + e1[0:1, :])
        fs.append((i8 + 8.0) * float(PEER_TOPK))
        top_s, experts = _top16(jnp.concatenate(cs, axis=0), jnp.concatenate(fs, axis=0),
                                jnp.concatenate(es, axis=0))
        ex = jnp.exp(top_s - jnp.max(top_s, axis=0, keepdims=True))
        rows = pl.ds(pl.multiple_of(hd * PEER_TOPK, PEER_TOPK), PEER_TOPK)
        gate_ref[rows, :] = ex / jnp.sum(ex, axis=0, keepdims=True)
        expert = experts.astype(I32)
        exp_ref[rows, :] = expert
        row_ref[rows, :] = (expert & (TAB_ROWS - 1)) * 8
        par_ref[rows, :] = lax.shift_right_logical(expert, TAB_ROWS.bit_length() - 1).astype(F32)
        return carry

    lax.fori_loop(0, PEER_HEADS, head_body, 0)


def _peer_route(h2, gain, wqt, keys, tm=512):
    t = h2.shape[0]
    tm = min(tm, t)
    row = lambda i: (i, 0)
    col = lambda i: (0, i)
    return pl.pallas_call(
        functools.partial(_peer_route_kernel, tm=tm),
        grid=(t // tm,),
        in_specs=[pl.BlockSpec((tm, D_MODEL), row), pl.BlockSpec((1, D_MODEL), lambda i: (0, 0)),
                  pl.BlockSpec((D_MODEL, D_MODEL), lambda i: (0, 0)),
                  pl.BlockSpec((2 * PEER_HEADS, N_KEYS, PEER_DHALF), lambda i: (0, 0, 0))],
        out_specs=[pl.BlockSpec((tm, D_MODEL), row)] + [pl.BlockSpec((PEER_PAIRS, tm), col)] * 4,
        out_shape=[jax.ShapeDtypeStruct((t, D_MODEL), F32), jax.ShapeDtypeStruct((PEER_PAIRS, t), I32),
                   jax.ShapeDtypeStruct((PEER_PAIRS, t), I32), jax.ShapeDtypeStruct((PEER_PAIRS, t), F32),
                   jax.ShapeDtypeStruct((PEER_PAIRS, t), F32)],
        scratch_shapes=[pltpu.VMEM((D_MODEL, tm), BF16), pltpu.VMEM((2 * PEER_HEADS, PEER_TOPK, tm), F32),
                        pltpu.VMEM((2 * PEER_HEADS, PEER_TOPK, tm), F32)],
        compiler_params=_cparams(("parallel",)),
        name="peer_route",
    )(h2, gain, wqt, keys)


TAB_ROWS = N_EXPERTS // 2
HIGH_HALF = 0xFFFF0000


def _splat_rows(row):
    r = lax.broadcasted_iota(I32, (128, 128), 0)
    c = lax.broadcasted_iota(I32, (128, 128), 1)
    diag = jnp.where(r == c, jnp.broadcast_to(row, (128, 128)), 0.0).astype(BF16)
    return _dot(diag, jnp.ones((128, 128), BF16))


def _stage_shifts(par_ref, t, srep_s):
    shift = 16.0 - 16.0 * par_ref[pl.ds(t, 1), :]
    srep_s[...] = pltpu.bitcast(_splat_rows(shift).astype(I32), U32)


def _token_tile(block, tl):
    return jnp.concatenate([block[tl:tl + 1, s * 128:(s + 1) * 128] for s in range(8)], axis=0)


def _tiles_to_rows(tiles):
    return jnp.concatenate([jnp.concatenate([tile[s:s + 1, :] for tile in tiles], axis=0) for s in range(8)], axis=1)


def _load_words(tab_ref, row_ref, t, p):
    return tab_ref[pl.ds(pl.multiple_of(row_ref[t, p], 8), 8), :]


def _load_expert(tab_ref, row_ref, srep_s, t, p):
    w = lax.shift_left(_load_words(tab_ref, row_ref, t, p), jnp.broadcast_to(srep_s[p:p + 1, :], (8, 128)))
    return pltpu.bitcast(w & jnp.uint32(HIGH_HALF), F32)


_TREE8_SUBLANE = (3, 7, 1, 5, 2, 6, 0, 4)


def _tree8(v, upper, bit2, bit1):
    c = []
    for a, b in ((v[0], v[1]), (v[2], v[3]), (v[4], v[5]), (v[6], v[7])):
        c.append(jnp.where(upper, b + pltpu.roll(b, 4, 0), a + pltpu.roll(a, 4, 0)))
    e = []
    for c1, c2 in ((c[0], c[1]), (c[2], c[3])):
        e.append(jnp.where(bit2, c1 + pltpu.roll(c1, 2, 0), c2 + pltpu.roll(c2, 6, 0)))
    return jnp.where(bit1, e[0] + pltpu.roll(e[0], 1, 0), e[1] + pltpu.roll(e[1], 7, 0))


PEER_U_TOKENS_PER_STEP = 2


def _peer_u_kernel(row_ref, par_ref, x_ref, tab_ref, d_ref, r_s, srep_a, srep_b, *, tb):
    sub = lax.broadcasted_iota(I32, (8, 128), 0)
    lane = lax.broadcasted_iota(I32, (8, 128), 1)
    upper = sub >= 4
    bit2 = (sub & 2) != 0
    bit1 = (sub & 1) != 0
    groups = PEER_PAIRS // 8
    nt = PEER_U_TOKENS_PER_STEP
    steps = tb // nt

    def stage(tt, srep_s):
        for j in range(nt):
            _stage_shifts(par_ref, tt * nt + j, srep_s.at[j])

    def run(tt, srep_s, xb, first):
        for j in range(nt):
            t = tt * nt + j
            xv = _token_tile(xb, first + j)
            prod = lambda p: _load_expert(tab_ref, row_ref, srep_s.at[j], t, p) * xv
            for g in range(groups):
                r_s[t * groups + g] = _tree8([prod(g * 8 + _TREE8_SUBLANE[m]) for m in range(8)],
                                             upper, bit2, bit1)

    stage(0, srep_a)
    steps_per_body = 8 // nt

    def eight_tokens(k):
        xb = x_ref[pl.ds(pl.multiple_of(k * 8, 8), 8), :]
        for i in range(steps_per_body):
            tt = k * steps_per_body + i
            cur, nxt = (srep_a, srep_b) if i % 2 == 0 else (srep_b, srep_a)
            stage(jnp.minimum(tt + 1, steps - 1), nxt)
            run(tt, cur, xb, i * nt)

    def lane_sums(t8):
        dacc = jnp.zeros((8, 128), F32)
        for tl in range(8):
            for g in range(groups):
                d = jnp.sum(r_s[(t8 * 8 + tl) * groups + g], axis=1, keepdims=True)
                dacc = jnp.where(lane == tl * PEER_TOPK + g, d, dacc)
        d_ref[t8] = dacc

    eight_tokens(0)

    def body(k, carry):
        lane_sums(k - 1)
        eight_tokens(k)
        return carry

    lax.fori_loop(1, tb // 8, body, 0)
    lane_sums(tb // 8 - 1)


def _tok_spec(tb, space=None):
    return pl.BlockSpec((tb, PEER_PAIRS), lambda i: (i, 0), memory_space=space)


def _table_spec():
    return pl.BlockSpec((TAB_ROWS * 8, 128), lambda i: (0, 0), pipeline_mode=pl.Buffered(1))


def _peer_u(rows, par, x2, tab, t, tb=128):
    return pl.pallas_call(
        functools.partial(_peer_u_kernel, tb=tb),
        grid=(t // tb,),
        in_specs=[_tok_spec(tb, pltpu.SMEM), _tok_spec(tb), pl.BlockSpec((tb, D_MODEL), lambda i: (i, 0)),
                  _table_spec()],
        out_specs=pl.BlockSpec((tb // 8, 8, 128), lambda i: (i, 0, 0)),
        out_shape=jax.ShapeDtypeStruct((t // 8, 8, 128), F32),
        scratch_shapes=[pltpu.VMEM((tb * (PEER_PAIRS // 8), 8, 128), F32),
                        pltpu.VMEM((PEER_U_TOKENS_PER_STEP, 128, 128), U32),
                        pltpu.VMEM((PEER_U_TOKENS_PER_STEP, 128, 128), U32)],
        compiler_params=_cparams(("arbitrary",)),
        name="peer_u",
    )(rows, par, x2, tab)


SC_CORES = 2
SC_SUBCORES = 16
SC_LANES = 16
SC_WORKERS = SC_CORES * SC_SUBCORES
PEER_U_SC_SHARE_NUM, PEER_U_SC_SHARE_DEN = 41, 128
PEER_V_SC_SHARE_NUM, PEER_V_SC_SHARE_DEN = 28, 128


def _peer_u_sc(tab, expert, xn, t0, t_sc):
    tw = t_sc // SC_WORKERS
    heads = PEER_PAIRS // PEER_TOPK
    chunks = D_MODEL // SC_LANES
    mesh = plsc.VectorSubcoreMesh(core_axis_name="c", subcore_axis_name="s")

    def body(tab_hbm, idx_hbm, x_hbm, d_hbm, idx_v, x_v, rows_a, rows_b, d_v, sem_a, sem_b):
        wid = lax.axis_index("s") * SC_CORES + lax.axis_index("c")
        lane = lax.iota(I32, SC_LANES)
        bufs = ((rows_a, sem_a), (rows_b, sem_b))

        def gather(h):
            buf, sem = bufs[h % 2]
            return pltpu.make_async_copy(tab_hbm.at[idx_v.at[pl.ds(h * PEER_TOPK, PEER_TOPK)]], buf, sem)

        def token(i, carry):
            t = t0 + wid * tw + i
            pltpu.sync_copy(idx_hbm.at[t], idx_v)
            pltpu.sync_copy(x_hbm.at[t], x_v)
            gather(0).start()
            for h in range(heads):
                if h + 1 < heads:
                    gather(h + 1).start()
                gather(h).wait()
                rows_v = bufs[h % 2][0]

                def chunk(j, accs):
                    xj = x_v[pl.ds(j * SC_LANES, SC_LANES)]
                    return tuple(a + rows_v[r, pl.ds(j * SC_LANES, SC_LANES)] * xj for r, a in enumerate(accs))

                accs = lax.fori_loop(0, chunks, chunk,
                                     tuple(jnp.zeros((SC_LANES,), F32) for _ in range(PEER_TOPK)))
                out = jnp.zeros((SC_LANES,), F32)
                for r in range(PEER_TOPK):
                    out = jnp.where(lane == r, jnp.sum(accs[r]), out)
                d_v[pl.ds(h * PEER_TOPK, PEER_TOPK)] = out
            pltpu.sync_copy(d_v, d_hbm.at[t - t0])
            return carry

        lax.fori_loop(0, tw, token, 0)

    return pl.kernel(
        body, mesh=mesh,
        out_type=jax.ShapeDtypeStruct((t_sc, PEER_PAIRS), F32),
        compiler_params=pltpu.CompilerParams(needs_layout_passes=False),
        scratch_types=[pltpu.VMEM((PEER_PAIRS,), I32), pltpu.VMEM((D_MODEL,), F32),
                       pltpu.VMEM((PEER_TOPK, D_MODEL), F32), pltpu.VMEM((PEER_TOPK, D_MODEL), F32),
                       pltpu.VMEM((PEER_PAIRS,), F32), pltpu.SemaphoreType.DMA, pltpu.SemaphoreType.DMA],
        name="peer_u_sc",
    )(tab, expert, xn)


def _peer_coef_kernel(d_ref, gate_ref, c_ref):
    c_ref[...] = gate_ref[...] * _gelu_tanh(d_ref[...])


def _peer_coef(d_tp, gate_tp, tm=2048):
    t = d_tp.shape[0]
    tm = min(tm, t)
    return pl.pallas_call(
        _peer_coef_kernel,
        grid=(t // tm,),
        in_specs=[_tok_spec(tm), _tok_spec(tm)],
        out_specs=_tok_spec(tm),
        out_shape=jax.ShapeDtypeStruct((t, PEER_PAIRS), F32),
        compiler_params=_cparams(("parallel",)),
        name="peer_coef",
    )(d_tp, gate_tp)


PEER_V_TOKENS_PER_STEP = 8
PEER_V_KDIM = (PEER_PAIRS // 2) * 16


def _peer_v_kernel(row_ref, par_ref, coef_ref, h_ref, tab_ref, out_ref, cz_s, *, tb):
    half = PEER_PAIRS // 2
    kdim = PEER_V_KDIM
    nt = PEER_V_TOKENS_PER_STEP
    pk = lax.broadcasted_iota(I32, (PEER_PAIRS, kdim), 0)
    qk = lax.shift_right_logical(lax.broadcasted_iota(I32, (PEER_PAIRS, kdim), 1), 4)
    hk = (lax.broadcasted_iota(I32, (tb, kdim), 1) & 1).astype(F32)
    coef = coef_ref[...].astype(BF16)
    par = par_ref[...].astype(BF16)
    for hf in range(2):
        expand = jnp.where(pk == qk + hf * half, 1.0, 0.0).astype(BF16)
        cz_s[hf] = jnp.where(hk == _dot(par, expand), _dot(coef, expand), 0.0)
    ks = lax.broadcasted_iota(I32, (8, kdim), 0)
    k8 = lax.broadcasted_iota(I32, (8, kdim), 1)
    own_sublane = lax.shift_right_logical(k8 & 15, 1) == ks

    def tok_body(tt, carry):
        rows8 = pl.ds(pl.multiple_of(tt * nt, nt), nt)
        hb = h_ref[rows8, :]
        tiles = []
        for j in range(nt):
            t = tt * nt + j
            pieces = []
            for q in range(half):
                wa = pltpu.bitcast(_load_words(tab_ref, row_ref, t, q), BF16)
                wb = pltpu.bitcast(_load_words(tab_ref, row_ref, t, half + q), BF16)
                pieces.append(jnp.concatenate([wa, wb], axis=1))
            g = jnp.concatenate(pieces, axis=0)
            ck = jnp.concatenate(
                [jnp.where(own_sublane, jnp.broadcast_to(cz_s[hf, pl.ds(t, 1), :], (8, kdim)), 0.0)
                 for hf in range(2)], axis=0).astype(BF16)
            o = _dot(ck, g)
            tiles.append(_token_tile(hb, j) + (o[0:8, 0:128] + o[8:16, 128:256]))
        out_ref[rows8, :] = _tiles_to_rows(tiles)
        return carry

    lax.fori_loop(0, tb // nt, tok_body, 0)


def _peer_v_sc(tab, expert, coef, h2, t0, t_sc):
    tw = t_sc // SC_WORKERS
    heads = PEER_PAIRS // PEER_TOPK
    chunks = D_MODEL // SC_LANES
    mesh = plsc.VectorSubcoreMesh(core_axis_name="c", subcore_axis_name="s")

    def body(tab_hbm, idx_hbm, coef_hbm, h_hbm, out_hbm, idx_v, c_v, acc_v, rows_a, rows_b, sem_a, sem_b):
        wid = lax.axis_index("s") * SC_CORES + lax.axis_index("c")
        lane = lax.iota(I32, SC_LANES)
        bufs = ((rows_a, sem_a), (rows_b, sem_b))

        def gather(h):
            buf, sem = bufs[h % 2]
            return pltpu.make_async_copy(tab_hbm.at[idx_v.at[pl.ds(h * PEER_TOPK, PEER_TOPK)]], buf, sem)

        def token(i, carry):
            t = t0 + wid * tw + i
            pltpu.sync_copy(idx_hbm.at[t], idx_v)
            gather(0).start()
            pltpu.sync_copy(coef_hbm.at[t], c_v)
            pltpu.sync_copy(h_hbm.at[t], acc_v)
            for h in range(heads):
                if h + 1 < heads:
                    gather(h + 1).start()
                gather(h).wait()
                rows_v = bufs[h % 2][0]
                cvec = c_v[pl.ds(h * PEER_TOPK, PEER_TOPK)]
                cs = [jnp.sum(jnp.where(lane == r, cvec, 0.0)) for r in range(PEER_TOPK)]

                def chunk(j, carry2):
                    cols = pl.ds(j * SC_LANES, SC_LANES)
                    a = acc_v[cols]
                    for r in range(PEER_TOPK):
                        a = a + cs[r] * rows_v[r, cols]
                    acc_v[cols] = a
                    return carry2

                lax.fori_loop(0, chunks, chunk, 0)
            pltpu.sync_copy(acc_v, out_hbm.at[t - t0])
            return carry

        lax.fori_loop(0, tw, token, 0)

    return pl.kernel(
        body, mesh=mesh,
        out_type=jax.ShapeDtypeStruct((t_sc, D_MODEL), F32),
        compiler_params=pltpu.CompilerParams(needs_layout_passes=False),
        scratch_types=[pltpu.VMEM((PEER_PAIRS,), I32), pltpu.VMEM((PEER_PAIRS,), F32), pltpu.VMEM((D_MODEL,), F32),
                       pltpu.VMEM((PEER_TOPK, D_MODEL), F32), pltpu.VMEM((PEER_TOPK, D_MODEL), F32),
                       pltpu.SemaphoreType.DMA, pltpu.SemaphoreType.DMA],
        name="peer_v_sc",
    )(tab, expert, coef, h2)


def _peer_v(rows, par, coef, h2, tab, t, tb=128):
    return pl.pallas_call(
        functools.partial(_peer_v_kernel, tb=tb),
        grid=(t // tb,),
        in_specs=[_tok_spec(tb, pltpu.SMEM), _tok_spec(tb), _tok_spec(tb),
                  pl.BlockSpec((tb, D_MODEL), lambda i: (i, 0)), _table_spec()],
        out_specs=pl.BlockSpec((tb, D_MODEL), lambda i: (i, 0)),
        out_shape=jax.ShapeDtypeStruct((t, D_MODEL), F32),
        scratch_shapes=[pltpu.VMEM((2, tb, PEER_V_KDIM), F32)],
        compiler_params=_cparams(("arbitrary",)),
        name="peer_v",
    )(rows, par, coef, h2, tab)


def _final_norm_kernel(x_ref, g_ref, o_ref):
    x = x_ref[...]
    ms = jnp.mean(x * x, axis=-1, keepdims=True)
    o_ref[...] = x * lax.rsqrt(ms + NORM_EPS) * g_ref[...]


def _final_norm(x2, gain, tm=1024):
    t = x2.shape[0]
    tm = min(tm, t)
    row = lambda i: (i, 0)
    return pl.pallas_call(
        _final_norm_kernel,
        grid=(t // tm,),
        in_specs=[pl.BlockSpec((tm, D_MODEL), row), pl.BlockSpec((1, D_MODEL), lambda i: (0, 0))],
        out_specs=pl.BlockSpec((tm, D_MODEL), row),
        out_shape=jax.ShapeDtypeStruct((t, D_MODEL), F32),
        compiler_params=_cparams(("parallel",)),
        name="final_norm",
    )(x2, gain)


def _pack_table(tab):
    bits = lax.bitcast_convert_type(tab.astype(BF16), jnp.uint16).astype(U32)
    return (bits[:TAB_ROWS] | (bits[TAB_ROWS:] << 16)).reshape(TAB_ROWS * 8, 128)


def _peer_start(h2, w, tab_u_all):
    t = h2.shape[0]
    xn, expert, rows, par, gate = _peer_route(h2, w["norm2"], w["wqt"], w["keys"])
    d = _peer_u_sc(tab_u_all, expert.T + w["expert_base"], xn, 0, t)
    return dict(h2=h2, rows=rows.T, par=par.T, gate=gate.T, d=d)


def _peer_finish(st, w):
    t = st["h2"].shape[0]
    coef = _peer_coef(st["d"], st["gate"])
    return _peer_v(st["rows"], st["par"], coef, st["h2"], w["tab_v"], t)


def _rep(x, n):
    return jnp.repeat(x, n, axis=-1)


def _inproj_weight(w_in_l):
    o = 0
    qkv = w_in_l[:, o:o + GDN_QKV_W]; o += GDN_QKV_W
    z = w_in_l[:, o:o + GDN_W]; o += GDN_W
    a = w_in_l[:, o:o + GDN_HEADS]; o += GDN_HEADS
    b = w_in_l[:, o:o + GDN_HEADS]; o += GDN_HEADS
    rest = w_in_l[:, o:]
    return jnp.concatenate([qkv, z, _rep(a, GDN_DK), _rep(b, GDN_DK), rest], axis=1).astype(BF16)


def _block_diag_gc(b_gnc):
    g = b_gnc.shape[0]
    eye = jnp.eye(g, dtype=b_gnc.dtype)
    t = jnp.swapaxes(b_gnc, 1, 2)
    return (t[:, :, None, :] * eye[:, None, :, None]).reshape(g * t.shape[1], g * t.shape[2])


def _layer_weights(p, layer):
    row = lambda x: x[None, :]
    flat = lambda x: x.reshape(1, S5_NS)
    return dict(
        lam_init=0.8 - 0.6 * math.exp(-0.3 * layer),
        norm1=row(p["norm1_g"][layer]), w_in=_inproj_weight(p["w_in"][layer]),
        conv_w=p["gdn_conv_w"][layer], a_log=row(_rep(p["gdn_a_log"][layer].astype(F32), GDN_DK)),
        dtb=row(_rep(p["gdn_dt_bias"][layer].astype(F32), GDN_DK)), gdn_ng=row(p["gdn_norm_g"][layer]),
        s5=(flat(p["s5_lambda_re"][layer]), flat(p["s5_lambda_im"][layer]),
            flat(_rep(p["s5_log_step"][layer][:, None], S5_STATE)),
            _block_diag_gc(p["s5_b_re"][layer]), _block_diag_gc(p["s5_b_im"][layer]),
            _block_diag_gc(p["s5_c_re"][layer]), _block_diag_gc(p["s5_c_im"][layer]),
            row(p["s5_d"][layer])),
        attn=tuple(row(p[k][layer]) for k in ("diff_lq1", "diff_lk1", "diff_lq2", "diff_lk2", "diff_norm_g")),
        glu_w=p["s5_glu_w"][layer].astype(BF16), glu_b=row(p["s5_glu_b"][layer]),
        w_out=p["w_out"][layer].astype(BF16),
        norm2=row(p["norm2_g"][layer]), wqt=p["peer_wq"][layer].T.astype(BF16),
        keys=p["peer_keys"][layer].reshape(2 * PEER_HEADS, N_KEYS, PEER_DHALF).astype(BF16),
        tab_v=_pack_table(p["peer_v"][layer]), expert_base=layer * N_EXPERTS)


def _mixers(h, w):
    bn, seq, _ = h.shape
    t = bn * seq
    gdn_in, us, qc, kc, vc = _inproj(h.reshape(t, D_MODEL), w["norm1"], w["w_in"])
    o_a = _gdn(gdn_in.reshape(bn, seq, INPROJ_GDN_W), w["conv_w"], w["a_log"], w["dtb"], w["gdn_ng"])
    ys = _s5(us.reshape(bn, seq // S5_SUB, S5_SUB * S5_WIDTH), *w["s5"])
    o_c = _attn(qc.reshape(bn, seq, DIFF_W), kc.reshape(bn, seq, DIFF_W), vc.reshape(bn, seq, DIFF_W),
                *w["attn"], w["lam_init"])
    return _outproj(h.reshape(t, D_MODEL), o_a.reshape(t, GDN_W), ys.reshape(t, S5_WIDTH),
                    o_c.reshape(t, DIFF_W), w["glu_w"], w["glu_b"], w["w_out"])


def kernel(x, norm1_g, w_in, gdn_conv_w, gdn_a_log, gdn_dt_bias, gdn_norm_g, s5_lambda_re, s5_lambda_im, s5_b_re, s5_b_im, s5_c_re, s5_c_im, s5_d, s5_log_step, s5_glu_w, s5_glu_b, diff_lq1, diff_lk1, diff_lq2, diff_lk2, diff_norm_g, w_out, norm2_g, peer_wq, peer_keys, peer_u, peer_v, final_g):
    p = dict(norm1_g=norm1_g, w_in=w_in, gdn_conv_w=gdn_conv_w, gdn_a_log=gdn_a_log, gdn_dt_bias=gdn_dt_bias,
             gdn_norm_g=gdn_norm_g, s5_lambda_re=s5_lambda_re, s5_lambda_im=s5_lambda_im, s5_b_re=s5_b_re,
             s5_b_im=s5_b_im, s5_c_re=s5_c_re, s5_c_im=s5_c_im, s5_d=s5_d, s5_log_step=s5_log_step,
             s5_glu_w=s5_glu_w, s5_glu_b=s5_glu_b, diff_lq1=diff_lq1, diff_lk1=diff_lk1, diff_lq2=diff_lq2,
             diff_lk2=diff_lk2, diff_norm_g=diff_norm_g, w_out=w_out, norm2_g=norm2_g, peer_wq=peer_wq,
             peer_keys=peer_keys, peer_u=peer_u, peer_v=peer_v, final_g=final_g)
    bn, seq, _ = x.shape
    weights = [_layer_weights(p, layer) for layer in range(DEPTH)]
    tab_u_all = peer_u.reshape(DEPTH * N_EXPERTS, D_MODEL)
    hs = [x[b:b + 1] for b in range(bn)]
    pending = None
    for layer in range(DEPTH):
        for b in range(bn):
            started = _peer_start(_mixers(hs[b], weights[layer]), weights[layer], tab_u_all)
            if pending is not None:
                pb, pst, pw = pending
                hs[pb] = _peer_finish(pst, pw).reshape(1, seq, D_MODEL)
            pending = (b, started, weights[layer])
    pb, pst, pw = pending
    hs[pb] = _peer_finish(pst, pw).reshape(1, seq, D_MODEL)
    h = jnp.concatenate(hs, axis=0)
    return _final_norm(h.reshape(-1, D_MODEL), final_g[None, :]).reshape(x.shape)
```

```python
import functools
import math

import jax
import jax.numpy as jnp
from jax import lax
from jax.experimental import pallas as pl
from jax.experimental.pallas import tpu as pltpu
from jax.experimental.pallas import tpu_sc as plsc

F32 = jnp.float32
BF16 = jnp.bfloat16
I32 = jnp.int32
U32 = jnp.uint32

D_MODEL = 1024
DEPTH = 2
GDN_HEADS = 4
GDN_DK = 64
GDN_CHUNK = 64
GDN_W = GDN_HEADS * GDN_DK
GDN_QKV_W = 3 * GDN_W
GDN_CONV = 4
S5_WIDTH = 256
S5_GROUPS = 16
S5_GROUP_CH = 16
S5_STATE = 64
S5_NS = S5_GROUPS * S5_STATE
S5_SUB = 16
DIFF_HEADS = 4
DIFF_DQK = 64
DIFF_DV = 128
DIFF_W = 512
PEER_HEADS = 8
PEER_DHALF = 64
N_KEYS = 128
N_EXPERTS = N_KEYS * N_KEYS
PEER_TOPK = 16
PEER_PAIRS = PEER_HEADS * PEER_TOPK
NORM_EPS = 1e-6
NEG_INF = float("-inf")

VMEM_LIMIT_BYTES = 56 * 1024 * 1024


def _cparams(sem, vmem=VMEM_LIMIT_BYTES):
    return pltpu.CompilerParams(dimension_semantics=sem, vmem_limit_bytes=vmem)


def _dot(a, b):
    return jnp.dot(a, b, preferred_element_type=F32)


def _dot_nt(a, b):
    return lax.dot_general(a, b, (((1,), (1,)), ((), ())), preferred_element_type=F32)


def _dot_tn(a, b):
    return lax.dot_general(a, b, (((0,), (0,)), ((), ())), preferred_element_type=F32)


def _split(x):
    hi = x.astype(BF16)
    lo = (x - hi.astype(F32)).astype(BF16)
    return hi, lo


def _dot_sel_r(x, sel):
    hi, lo = _split(x)
    return _dot(hi, sel) + _dot(lo, sel)


def _dot_sel_l(sel, x):
    hi, lo = _split(x)
    return _dot(sel, hi) + _dot(sel, lo)


def _mm3(a, b):
    ah, al = _split(a)
    bh, bl = _split(b)
    return _dot(ah, bh) + (_dot(ah, bl) + _dot(al, bh))


def _sigmoid(x):
    return 1.0 / (1.0 + jnp.exp(-x))


def _softplus(x):
    return jnp.maximum(x, 0.0) + jnp.log1p(jnp.exp(-jnp.abs(x)))


def _gelu_tanh(x):
    c = math.sqrt(2.0 / math.pi)
    return 0.5 * x * (1.0 + jnp.tanh(c * (x + 0.044715 * (x * x * x))))


INPROJ_GDN_W = GDN_QKV_W + 3 * GDN_W


def _inproj_kernel(x_ref, g_ref, w_ref, gdn_ref, us_ref, q_ref, k_ref, v_ref):
    x = x_ref[...]
    ms = jnp.mean(x * x, axis=-1, keepdims=True)
    xn = (x * lax.rsqrt(ms + NORM_EPS) * g_ref[...]).astype(BF16)
    o = INPROJ_GDN_W
    gdn_ref[...] = _dot(xn, w_ref[:, 0:o])
    us_ref[...] = _dot(xn, w_ref[:, o:o + S5_WIDTH])
    o += S5_WIDTH
    q_ref[...] = (_dot(xn, w_ref[:, o:o + DIFF_W]) * (DIFF_DQK ** -0.5)).astype(BF16)
    k_ref[...] = _dot(xn, w_ref[:, o + DIFF_W:o + 2 * DIFF_W]).astype(BF16)
    v_ref[...] = _dot(xn, w_ref[:, o + 2 * DIFF_W:o + 3 * DIFF_W]).astype(BF16)


def _inproj(x2, gain, w, tm=512):
    t = x2.shape[0]
    nw = w.shape[1]
    row = lambda i: (i, 0)
    fixed = lambda i: (0, 0)
    return pl.pallas_call(
        _inproj_kernel,
        grid=(t // tm,),
        in_specs=[pl.BlockSpec((tm, D_MODEL), row), pl.BlockSpec((1, D_MODEL), fixed),
                  pl.BlockSpec((D_MODEL, nw), fixed)],
        out_specs=[pl.BlockSpec((tm, INPROJ_GDN_W), row), pl.BlockSpec((tm, S5_WIDTH), row),
                   pl.BlockSpec((tm, DIFF_W), row), pl.BlockSpec((tm, DIFF_W), row),
                   pl.BlockSpec((tm, DIFF_W), row)],
        out_shape=[jax.ShapeDtypeStruct((t, INPROJ_GDN_W), F32), jax.ShapeDtypeStruct((t, S5_WIDTH), F32),
                   jax.ShapeDtypeStruct((t, DIFF_W), BF16), jax.ShapeDtypeStruct((t, DIFF_W), BF16),
                   jax.ShapeDtypeStruct((t, DIFF_W), BF16)],
        compiler_params=_cparams(("parallel",)),
        name="inproj",
    )(x2, gain, w)


def _gdn_kernel(blk_ref, convw_ref, alog_ref, dtb_ref, ng_ref, out_ref,
                s_ref, tail_ref, xp_ref, q_s, k_s, v_s, b_s, g_s, *, ct):
    c64 = GDN_CHUNK

    @pl.when(pl.program_id(1) == 0)
    def _():
        s_ref[...] = jnp.zeros_like(s_ref)
        tail_ref[...] = jnp.zeros_like(tail_ref)

    qkv = blk_ref[0, :, 0:GDN_QKV_W]
    xp_ref[0:8, :] = tail_ref[...]
    xp_ref[8:8 + ct, :] = qkv
    tail_ref[...] = qkv[ct - 8:ct, :]
    cw = convw_ref[...]
    y = cw[0:1, :] * xp_ref[5:5 + ct, :]
    for j in range(1, GDN_CONV):
        y = y + cw[j:j + 1, :] * xp_ref[5 + j:5 + j + ct, :]
    y = y * _sigmoid(y)

    ri = lax.broadcasted_iota(I32, (GDN_W, GDN_W), 0)
    ci = lax.broadcasted_iota(I32, (GDN_W, GDN_W), 1)
    head_ones = jnp.where((ri // c64) == (ci // c64), 1.0, 0.0).astype(BF16)

    q = y[:, 0:GDN_W]
    k = y[:, GDN_W:2 * GDN_W]
    q_s[...] = q * lax.rsqrt(_dot_sel_r(q * q, head_ones) + 1e-6) * (GDN_DK ** -0.5)
    k_s[...] = k * lax.rsqrt(_dot_sel_r(k * k, head_ones) + 1e-6)
    v_s[...] = y[:, 2 * GDN_W:3 * GDN_W]
    a_rep = blk_ref[0, :, GDN_QKV_W + GDN_W:GDN_QKV_W + 2 * GDN_W]
    b_rep = blk_ref[0, :, GDN_QKV_W + 2 * GDN_W:GDN_QKV_W + 3 * GDN_W]
    b_s[...] = _sigmoid(b_rep)
    g_raw = -jnp.exp(alog_ref[...]) * _softplus(a_rep + dtb_ref[...])
    rt = lax.broadcasted_iota(I32, (ct, ct), 0)
    ctk = lax.broadcasted_iota(I32, (ct, ct), 1)
    tri_bd = jnp.where(((rt // c64) == (ctk // c64)) & (ctk <= rt), 1.0, 0.0).astype(BF16)
    g_s[...] = _dot_sel_l(tri_bd, g_raw)

    r64 = lax.broadcasted_iota(I32, (c64, c64), 0)
    col64 = lax.broadcasted_iota(I32, (c64, c64), 1)
    incl = col64 <= r64
    strict = col64 < r64
    eye_b = col64 == r64
    eye_f = jnp.where(eye_b, 1.0, 0.0)
    ones64 = jnp.ones((c64, c64), BF16)
    ng = ng_ref[...]

    n_chunks = ct // c64
    heads = range(GDN_HEADS)
    chains = [(slice(c * c64, (c + 1) * c64), slice(h * c64, (h + 1) * c64))
              for c in range(n_chunks) for h in heads]
    each = lambda f, *ls: [f(*xs) for xs in zip(*ls)]
    gi = [g_s[r, l] for r, l in chains]
    gj = each(lambda g: _dot_sel_l(ones64, jnp.where(eye_b, g, 0.0)), gi)
    dec = each(lambda a, b: jnp.exp(jnp.where(incl, a - b, NEG_INF)), gi, gj)
    kh = [k_s[r, l] for r, l in chains]
    bi = [b_s[r, l] for r, l in chains]
    kb = each(lambda x: x.astype(BF16), kh)
    kk = each(_dot_nt, kb, kb)
    lm = each(lambda b, x, d: jnp.where(strict, b * x * d, 0.0), bi, kk, dec)
    tinv = each(lambda x: eye_f - x, lm)
    pw = lm
    for _ in range(5):
        pw = each(_mm3, pw, pw)
        tinv = each(lambda t, m: t + _mm3(t, m), tinv, pw)
    tb = each(lambda x: x.astype(BF16), tinv)
    eg = each(jnp.exp, gi)
    u = each(lambda t, rl, b: _dot(t, (v_s[rl[0], rl[1]] * b).astype(BF16)), tb, chains, bi)
    w = each(lambda t, k, b, e: _dot(t, (k * (b * e)).astype(BF16)).astype(BF16), tb, kh, bi, eg)
    qh = [q_s[r, l] for r, l in chains]
    qk = each(lambda q, k, d: jnp.where(incl, _dot_nt(q.astype(BF16), k) * d, 0.0).astype(BF16), qh, kb, dec)
    qg = each(lambda q, e: (q * e).astype(BF16), qh, eg)
    g_last = each(lambda g: g[c64 - 1:c64, :], gi)
    kg = each(lambda k, gl, g: (k * jnp.exp(gl - g)).astype(BF16), kh, g_last, gi)
    decay_last = each(jnp.exp, g_last)

    states = [s_ref[h] for h in heads]
    for c in range(n_chunks):
        ids = [c * GDN_HEADS + h for h in heads]
        sb = [s.astype(BF16) for s in states]
        vnb = [(u[n] - _dot(w[n], sb[h])).astype(BF16) for h, n in zip(heads, ids)]
        o = [_dot(qg[n], sb[h]) + _dot(qk[n], vnb[h]) for h, n in zip(heads, ids)]
        states = [states[h] * decay_last[n] + _dot_tn(kg[n], vnb[h]) for h, n in zip(heads, ids)]
        ms = [_dot_sel_r(x * x, ones64) * (1.0 / c64) for x in o]
        outs = [x * lax.rsqrt(m + NORM_EPS) * ng for x, m in zip(o, ms)]
        rows = slice(c * c64, (c + 1) * c64)
        z = blk_ref[0, rows, GDN_QKV_W:GDN_QKV_W + GDN_W]
        out_ref[0, rows, :] = (jnp.concatenate(outs, axis=1) * (z * _sigmoid(z))).astype(out_ref.dtype)
    for h in heads:
        s_ref[h] = states[h]


def _gdn(gdn_in, conv_w, a_log_rep, dtb_rep, ng, ct=256):
    bn, seq, _ = gdn_in.shape
    fixed = lambda b, l: (0, 0)
    return pl.pallas_call(
        functools.partial(_gdn_kernel, ct=ct),
        grid=(bn, seq // ct),
        in_specs=[pl.BlockSpec((1, ct, INPROJ_GDN_W), lambda b, l: (b, l, 0)),
                  pl.BlockSpec((GDN_CONV, GDN_QKV_W), fixed), pl.BlockSpec((1, GDN_W), fixed),
                  pl.BlockSpec((1, GDN_W), fixed), pl.BlockSpec((1, GDN_DK), fixed)],
        out_specs=pl.BlockSpec((1, ct, GDN_W), lambda b, l: (b, l, 0)),
        out_shape=jax.ShapeDtypeStruct((bn, seq, GDN_W), BF16),
        scratch_shapes=[pltpu.VMEM((GDN_HEADS, GDN_DK, GDN_DK), F32), pltpu.VMEM((8, GDN_QKV_W), F32),
                        pltpu.VMEM((ct + 8, GDN_QKV_W), F32)] + [pltpu.VMEM((ct, GDN_W), F32)] * 5,
        compiler_params=_cparams(("arbitrary", "arbitrary")),
        name="gdn",
    )(gdn_in, conv_w, a_log_rep, dtb_rep, ng)


def _s5_kernel(u_ref, lre_ref, lim_ref, lstep_ref, bre_ref, bim_ref, cre_ref, cim_ref, d_ref, y_ref,
               bmat, cmat, avec, carry, zr_s, zi_s, xr_s, xi_s, *, rb):
    ns = S5_NS
    cw = S5_WIDTH

    @pl.when(pl.program_id(1) == 0)
    def _():
        step = jnp.exp(lstep_ref[...])
        lr = lre_ref[...]
        li = lim_ref[...]
        mag = jnp.exp(lr * step)
        ar = mag * jnp.cos(li * step)
        ai = mag * jnp.sin(li * step)
        den = lr * lr + li * li
        mr = ((ar - 1.0) * lr + ai * li) / den
        mi = (ai * lr - (ar - 1.0) * li) / den
        bre = bre_ref[...]
        bim = bim_ref[...]
        bmat[:, 0:ns] = (mr * bre - mi * bim).astype(BF16)
        bmat[:, ns:2 * ns] = (mr * bim + mi * bre).astype(BF16)
        cmat[0:ns, :] = cre_ref[...].astype(BF16)
        cmat[ns:2 * ns, :] = (-cim_ref[...]).astype(BF16)
        avec[0:1, :] = ar
        avec[1:2, :] = ai
        pr, pi = ar, ai
        for _ in range(4):
            pr, pi = pr * pr - pi * pi, 2.0 * pr * pi
        avec[2:3, :] = pr
        avec[3:4, :] = pi
        carry[...] = jnp.zeros_like(carry)

    ar = avec[0:1, :]
    ai = avec[1:2, :]
    a16r = avec[2:3, :]
    a16i = avec[3:4, :]

    def inject(s):
        ub = u_ref[0, :, s * cw:(s + 1) * cw].astype(BF16)
        return _dot(ub, bmat[...])

    xr = jnp.zeros((rb, ns), F32)
    xi = jnp.zeros((rb, ns), F32)
    for s in range(S5_SUB):
        p = inject(s)
        xr, xi = ar * xr - ai * xi + p[:, 0:ns], ar * xi + ai * xr + p[:, ns:2 * ns]
    zr_s[...] = xr
    zi_s[...] = xi

    def row_step(kk, c):
        cr, ci_ = c
        xr_s[pl.ds(kk, 1), :] = cr
        xi_s[pl.ds(kk, 1), :] = ci_
        zr = zr_s[pl.ds(kk, 1), :]
        zi = zi_s[pl.ds(kk, 1), :]
        return (a16r * cr - a16i * ci_ + zr, a16r * ci_ + a16i * cr + zi)

    cr, ci_ = lax.fori_loop(0, rb, row_step, (carry[0:1, :], carry[1:2, :]))
    carry[0:1, :] = cr
    carry[1:2, :] = ci_

    xr = xr_s[...]
    xi = xi_s[...]
    dsk = d_ref[...]
    for s in range(S5_SUB):
        p = inject(s)
        xr, xi = ar * xr - ai * xi + p[:, 0:ns], ar * xi + ai * xr + p[:, ns:2 * ns]
        yv = _dot(xr.astype(BF16), cmat[0:ns, :]) + _dot(xi.astype(BF16), cmat[ns:2 * ns, :])
        yv = yv + dsk * u_ref[0, :, s * cw:(s + 1) * cw]
        y_ref[0, :, s * cw:(s + 1) * cw] = _gelu_tanh(yv).astype(y_ref.dtype)


def _s5(u_rows, lre, lim, lstep, bre_bd, bim_bd, cre_bd, cim_bd, dskip, rb=128):
    bn, nrows, rw = u_rows.shape
    fixed = lambda b, r: (0, 0)
    ns = S5_NS
    return pl.pallas_call(
        functools.partial(_s5_kernel, rb=rb),
        grid=(bn, nrows // rb),
        in_specs=[pl.BlockSpec((1, rb, rw), lambda b, r: (b, r, 0)),
                  pl.BlockSpec((1, ns), fixed), pl.BlockSpec((1, ns), fixed), pl.BlockSpec((1, ns), fixed),
                  pl.BlockSpec((S5_WIDTH, ns), fixed), pl.BlockSpec((S5_WIDTH, ns), fixed),
                  pl.BlockSpec((ns, S5_WIDTH), fixed), pl.BlockSpec((ns, S5_WIDTH), fixed),
                  pl.BlockSpec((1, S5_WIDTH), fixed)],
        out_specs=pl.BlockSpec((1, rb, rw), lambda b, r: (b, r, 0)),
        out_shape=jax.ShapeDtypeStruct((bn, nrows, rw), BF16),
        scratch_shapes=[pltpu.VMEM((S5_WIDTH, 2 * ns), BF16), pltpu.VMEM((2 * ns, S5_WIDTH), BF16),
                        pltpu.VMEM((8, ns), F32), pltpu.VMEM((8, ns), F32)]
                       + [pltpu.VMEM((rb, ns), F32)] * 4,
        compiler_params=_cparams(("arbitrary", "arbitrary")),
        name="s5",
    )(u_rows, lre, lim, lstep, bre_bd, bim_bd, cre_bd, cim_bd, dskip)


def _attn_kernel(q_ref, k_ref, v_ref, lq1_ref, lk1_ref, lq2_ref, lk2_ref, ng_ref, o_ref,
                 qs_s, m_s, l_s, acc_s, *, blk, lam_init):
    i = pl.program_id(2)
    q = q_ref[0]
    lane = lax.broadcasted_iota(I32, q.shape, 1)
    zero = jnp.zeros_like(q)
    qs_s[0:blk, :] = jnp.where(lane < DIFF_DQK, q, zero)
    qs_s[blk:2 * blk, :] = jnp.where(lane >= DIFF_DQK, q, zero)
    m_s[...] = jnp.full_like(m_s, NEG_INF)
    l_s[...] = jnp.zeros_like(l_s)
    acc_s[...] = jnp.zeros_like(acc_s)
    ones = jnp.ones((blk, DIFF_DV), BF16)

    def block_rows(j):
        return pl.ds(pl.multiple_of(j * blk, blk), blk)

    def scores(j):
        return _dot_nt(qs_s[...], k_ref[0, block_rows(j), :])

    def update(j, s):
        m_old = m_s[...]
        m_new = jnp.maximum(m_old, jnp.max(s, axis=-1, keepdims=True))
        p = jnp.exp(s - jnp.concatenate([m_new] * (blk // DIFF_DV), axis=1)).astype(BF16)
        alpha = jnp.exp(m_old - m_new)
        pv = _dot(p, jnp.concatenate([v_ref[0, block_rows(j), :], ones], axis=1))
        acc_s[...] = alpha * acc_s[...] + pv[:, 0:DIFF_DV]
        l_s[...] = alpha * l_s[...] + pv[:, DIFF_DV:2 * DIFF_DV]
        m_s[...] = m_new

    def body(j, s):
        s_next = scores(j + 1)
        update(j, s)
        return s_next

    s = lax.fori_loop(0, i, body, scores(0))
    row = lax.broadcasted_iota(I32, s.shape, 0) & (blk - 1)
    col = lax.broadcasted_iota(I32, s.shape, 1)
    update(i, jnp.where(col <= row, s, NEG_INF))
    lam = (jnp.exp(jnp.sum(lq1_ref[...] * lk1_ref[...], axis=-1, keepdims=True))
           - jnp.exp(jnp.sum(lq2_ref[...] * lk2_ref[...], axis=-1, keepdims=True)) + lam_init)
    o = acc_s[0:blk, :] / l_s[0:blk, :] - lam * (acc_s[blk:2 * blk, :] / l_s[blk:2 * blk, :])
    ms = jnp.mean(o * o, axis=-1, keepdims=True)
    o = o * lax.rsqrt(ms + NORM_EPS) * ng_ref[...] * (1.0 - lam_init)
    o_ref[0] = o.astype(o_ref.dtype)


def _attn(q, k, v, lq1, lk1, lq2, lk2, ng, lam_init, blk=512):
    bn, seq, _ = q.shape
    blk = min(blk, seq)
    fixed = lambda b, h, i: (0, 0)
    kv_spec = pl.BlockSpec((1, seq, DIFF_DV), lambda b, h, i: (b, 0, h))
    return pl.pallas_call(
        functools.partial(_attn_kernel, blk=blk, lam_init=lam_init),
        grid=(bn, DIFF_HEADS, seq // blk),
        in_specs=[pl.BlockSpec((1, blk, DIFF_DV), lambda b, h, i: (b, i, h)), kv_spec, kv_spec]
                 + [pl.BlockSpec((1, DIFF_DQK), fixed)] * 4 + [pl.BlockSpec((1, DIFF_DV), fixed)],
        out_specs=pl.BlockSpec((1, blk, DIFF_DV), lambda b, h, i: (b, i, h)),
        out_shape=jax.ShapeDtypeStruct((bn, seq, DIFF_W), BF16),
        scratch_shapes=[pltpu.VMEM((2 * blk, DIFF_DV), BF16), pltpu.VMEM((2 * blk, DIFF_DV), F32),
                        pltpu.VMEM((2 * blk, DIFF_DV), F32), pltpu.VMEM((2 * blk, DIFF_DV), F32)],
        compiler_params=_cparams(("parallel", "parallel", "arbitrary")),
        name="diff_attn",
    )(q, k, v, lq1, lk1, lq2, lk2, ng)


def _outproj_kernel(h_ref, oa_ref, ys_ref, oc_ref, gw_ref, gb_ref, wo_ref, out_ref):
    zg = _dot(ys_ref[...], gw_ref[...]) + gb_ref[...]
    ob = (zg[:, 0:S5_WIDTH] * _sigmoid(zg[:, S5_WIDTH:2 * S5_WIDTH])).astype(BF16)
    acc = _dot(oa_ref[...], wo_ref[0:GDN_W, :])
    acc = acc + _dot(ob, wo_ref[GDN_W:GDN_W + S5_WIDTH, :])
    acc = acc + _dot(oc_ref[...], wo_ref[GDN_W + S5_WIDTH:, :])
    out_ref[...] = h_ref[...] + acc


def _outproj(h2, oa, ys, oc, glu_w, glu_b, w_out, tm=512):
    t = h2.shape[0]
    row = lambda i: (i, 0)
    fixed = lambda i: (0, 0)
    return pl.pallas_call(
        _outproj_kernel,
        grid=(t // tm,),
        in_specs=[pl.BlockSpec((tm, D_MODEL), row), pl.BlockSpec((tm, GDN_W), row),
                  pl.BlockSpec((tm, S5_WIDTH), row), pl.BlockSpec((tm, DIFF_W), row),
                  pl.BlockSpec((S5_WIDTH, 2 * S5_WIDTH), fixed), pl.BlockSpec((1, 2 * S5_WIDTH), fixed),
                  pl.BlockSpec((D_MODEL, D_MODEL), fixed)],
        out_specs=pl.BlockSpec((tm, D_MODEL), row),
        out_shape=jax.ShapeDtypeStruct((t, D_MODEL), F32),
        compiler_params=_cparams(("parallel",)),
        name="outproj",
    )(h2, oa, ys, oc, glu_w, glu_b, w_out)


_BIG_ID = 1.0e9


def _top16(x, ids, payload):
    n = x.shape[1]
    r16 = lax.broadcasted_iota(I32, (PEER_TOPK, n), 0)
    vals = jnp.zeros((PEER_TOPK, n), F32)
    pays = jnp.zeros((PEER_TOPK, n), F32)
    for kk in range(PEER_TOPK):
        m = jnp.max(x, axis=0, keepdims=True)
        first = jnp.min(jnp.where(x == m, ids, _BIG_ID), axis=0, keepdims=True)
        hit = ids == first
        pay = first if payload is None else jnp.max(jnp.where(hit, payload, -1.0), axis=0, keepdims=True)
        x = jnp.where(hit, NEG_INF, x)
        vals = jnp.where(r16 == kk, m, vals)
        pays = jnp.where(r16 == kk, pay, pays)
    return vals, pays


def _peer_route_kernel(h_ref, g_ref, wqt_ref, keys_ref, xn_ref, exp_ref, row_ref, par_ref, gate_ref,
                       qt_s, sv_s, si_s, *, tm):
    x = h_ref[...]
    ms = jnp.mean(x * x, axis=-1, keepdims=True)
    xn = x * lax.rsqrt(ms + NORM_EPS) * g_ref[...]
    xn_ref[...] = xn
    qt_s[...] = _dot_nt(wqt_ref[...], xn.astype(BF16)).astype(BF16)

    key_id = lax.broadcasted_iota(I32, (N_KEYS, tm), 0).astype(F32)

    def half_body(hp, carry):
        r0 = pl.multiple_of(hp * PEER_DHALF, PEER_DHALF)
        s = _dot(keys_ref[hp], qt_s[pl.ds(r0, PEER_DHALF), :])
        vals, ids = _top16(s, key_id, None)
        sv_s[hp] = vals
        si_s[hp] = ids
        return carry

    lax.fori_loop(0, 2 * PEER_HEADS, half_body, 0)

    i8 = lax.broadcasted_iota(I32, (8, tm), 0).astype(F32)

    def head_body(hd, carry):
        a0 = sv_s[2 * hd]
        a1 = sv_s[2 * hd + 1]
        e0 = si_s[2 * hd] * float(N_KEYS)
        e1 = si_s[2 * hd + 1]
        cs, es, fs = [], [], []
        for i in range(8):
            cs.append(a0[i:i + 1, :] + a1[0:8, :])
            es.append(e0[i:i + 1, :] + e1[0:8, :])
            fs.append(i8 + float(i * PEER_TOPK))
        cs.append(a0[0:1, :] + a1[8:16, :])
        es.append(e0[0:1, :] + e1[8:16, :])
        fs.append(i8 + 8.0)
        cs.append(a0[8:16, :] + a1[0:1, :])
        es.append(e0[8:16, :] + e1[0:1, :])
        fs.append((i8 + 8.0) * float(PEER_TOPK))
        top_s, experts = _top16(jnp.concatenate(cs, axis=0), jnp.concatenate(fs, axis=0),
                                jnp.concatenate(es, axis=0))
        ex = jnp.exp(top_s - jnp.max(top_s, axis=0, keepdims=True))
        rows = pl.ds(pl.multiple_of(hd * PEER_TOPK, PEER_TOPK), PEER_TOPK)
        gate_ref[rows, :] = ex / jnp.sum(ex, axis=0, keepdims=True)
        expert = experts.astype(I32)
        exp_ref[rows, :] = expert
        row_ref[rows, :] = (expert & (TAB_ROWS - 1)) * 8
        par_ref[rows, :] = lax.shift_right_logical(expert, TAB_ROWS.bit_length() - 1).astype(F32)
        return carry

    lax.fori_loop(0, PEER_HEADS, head_body, 0)


def _peer_route(h2, gain, wqt, keys, tm=512):
    t = h2.shape[0]
    tm = min(tm, t)
    row = lambda i: (i, 0)
    col = lambda i: (0, i)
    return pl.pallas_call(
        functools.partial(_peer_route_kernel, tm=tm),
        grid=(t // tm,),
        in_specs=[pl.BlockSpec((tm, D_MODEL), row), pl.BlockSpec((1, D_MODEL), lambda i: (0, 0)),
                  pl.BlockSpec((D_MODEL, D_MODEL), lambda i: (0, 0)),
                  pl.BlockSpec((2 * PEER_HEADS, N_KEYS, PEER_DHALF), lambda i: (0, 0, 0))],
        out_specs=[pl.BlockSpec((tm, D_MODEL), row)] + [pl.BlockSpec((PEER_PAIRS, tm), col)] * 4,
        out_shape=[jax.ShapeDtypeStruct((t, D_MODEL), F32), jax.ShapeDtypeStruct((PEER_PAIRS, t), I32),
                   jax.ShapeDtypeStruct((PEER_PAIRS, t), I32), jax.ShapeDtypeStruct((PEER_PAIRS, t), F32),
                   jax.ShapeDtypeStruct((PEER_PAIRS, t), F32)],
        scratch_shapes=[pltpu.VMEM((D_MODEL, tm), BF16), pltpu.VMEM((2 * PEER_HEADS, PEER_TOPK, tm), F32),
                        pltpu.VMEM((2 * PEER_HEADS, PEER_TOPK, tm), F32)],
        compiler_params=_cparams(("parallel",)),
        name="peer_route",
    )(h2, gain, wqt, keys)


TAB_ROWS = N_EXPERTS // 2
HIGH_HALF = 0xFFFF0000


def _splat_rows(row):
    r = lax.broadcasted_iota(I32, (128, 128), 0)
    c = lax.broadcasted_iota(I32, (128, 128), 1)
    diag = jnp.where(r == c, jnp.broadcast_to(row, (128, 128)), 0.0).astype(BF16)
    return _dot(diag, jnp.ones((128, 128), BF16))


def _stage_shifts(par_ref, t, srep_s):
    shift = 16.0 - 16.0 * par_ref[pl.ds(t, 1), :]
    srep_s[...] = pltpu.bitcast(_splat_rows(shift).astype(I32), U32)


def _token_tile(block, tl):
    return jnp.concatenate([block[tl:tl + 1, s * 128:(s + 1) * 128] for s in range(8)], axis=0)


def _tiles_to_rows(tiles):
    return jnp.concatenate([jnp.concatenate([tile[s:s + 1, :] for tile in tiles], axis=0) for s in range(8)], axis=1)


def _load_words(tab_ref, row_ref, t, p):
    return tab_ref[pl.ds(pl.multiple_of(row_ref[t, p], 8), 8), :]


def _load_expert(tab_ref, row_ref, srep_s, t, p):
    w = lax.shift_left(_load_words(tab_ref, row_ref, t, p), jnp.broadcast_to(srep_s[p:p + 1, :], (8, 128)))
    return pltpu.bitcast(w & jnp.uint32(HIGH_HALF), F32)


_TREE8_SUBLANE = (3, 7, 1, 5, 2, 6, 0, 4)


def _tree8(v, upper, bit2, bit1):
    c = []
    for a, b in ((v[0], v[1]), (v[2], v[3]), (v[4], v[5]), (v[6], v[7])):
        c.append(jnp.where(upper, b + pltpu.roll(b, 4, 0), a + pltpu.roll(a, 4, 0)))
    e = []
    for c1, c2 in ((c[0], c[1]), (c[2], c[3])):
        e.append(jnp.where(bit2, c1 + pltpu.roll(c1, 2, 0), c2 + pltpu.roll(c2, 6, 0)))
    return jnp.where(bit1, e[0] + pltpu.roll(e[0], 1, 0), e[1] + pltpu.roll(e[1], 7, 0))


PEER_U_TOKENS_PER_STEP = 2


def _peer_u_kernel(row_ref, par_ref, x_ref, tab_ref, d_ref, r_s, srep_a, srep_b, *, tb):
    sub = lax.broadcasted_iota(I32, (8, 128), 0)
    lane = lax.broadcasted_iota(I32, (8, 128), 1)
    upper = sub >= 4
    bit2 = (sub & 2) != 0
    bit1 = (sub & 1) != 0
    groups = PEER_PAIRS // 8
    nt = PEER_U_TOKENS_PER_STEP
    steps = tb // nt

    def stage(tt, srep_s):
        for j in range(nt):
            _stage_shifts(par_ref, tt * nt + j, srep_s.at[j])

    def run(tt, srep_s, xb, first):
        for j in range(nt):
            t = tt * nt + j
            xv = _token_tile(xb, first + j)
            prod = lambda p: _load_expert(tab_ref, row_ref, srep_s.at[j], t, p) * xv
            for g in range(groups):
                r_s[t * groups + g] = _tree8([prod(g * 8 + _TREE8_SUBLANE[m]) for m in range(8)],
                                             upper, bit2, bit1)

    stage(0, srep_a)
    steps_per_body = 8 // nt

    def eight_tokens(k):
        xb = x_ref[pl.ds(pl.multiple_of(k * 8, 8), 8), :]
        for i in range(steps_per_body):
            tt = k * steps_per_body + i
            cur, nxt = (srep_a, srep_b) if i % 2 == 0 else (srep_b, srep_a)
            stage(jnp.minimum(tt + 1, steps - 1), nxt)
            run(tt, cur, xb, i * nt)

    def lane_sums(t8):
        dacc = jnp.zeros((8, 128), F32)
        for tl in range(8):
            for g in range(groups):
                d = jnp.sum(r_s[(t8 * 8 + tl) * groups + g], axis=1, keepdims=True)
                dacc = jnp.where(lane == tl * PEER_TOPK + g, d, dacc)
        d_ref[t8] = dacc

    eight_tokens(0)

    def body(k, carry):
        lane_sums(k - 1)
        eight_tokens(k)
        return carry

    lax.fori_loop(1, tb // 8, body, 0)
    lane_sums(tb // 8 - 1)


def _tok_spec(tb, space=None):
    return pl.BlockSpec((tb, PEER_PAIRS), lambda i: (i, 0), memory_space=space)


def _table_spec():
    return pl.BlockSpec((TAB_ROWS * 8, 128), lambda i: (0, 0), pipeline_mode=pl.Buffered(1))


def _peer_u(rows, par, x2, tab, t, tb=128):
    return pl.pallas_call(
        functools.partial(_peer_u_kernel, tb=tb),
        grid=(t // tb,),
        in_specs=[_tok_spec(tb, pltpu.SMEM), _tok_spec(tb), pl.BlockSpec((tb, D_MODEL), lambda i: (i, 0)),
                  _table_spec()],
        out_specs=pl.BlockSpec((tb // 8, 8, 128), lambda i: (i, 0, 0)),
        out_shape=jax.ShapeDtypeStruct((t // 8, 8, 128), F32),
        scratch_shapes=[pltpu.VMEM((tb * (PEER_PAIRS // 8), 8, 128), F32),
                        pltpu.VMEM((PEER_U_TOKENS_PER_STEP, 128, 128), U32),
                        pltpu.VMEM((PEER_U_TOKENS_PER_STEP, 128, 128), U32)],
        compiler_params=_cparams(("arbitrary",)),
        name="peer_u",
    )(rows, par, x2, tab)


SC_CORES = 2
SC_SUBCORES = 16
SC_LANES = 16
SC_WORKERS = SC_CORES * SC_SUBCORES
PEER_U_TC_TOKENS = 768


def _peer_u_sc(tab, expert, xn, t0, t_sc):
    tw = t_sc // SC_WORKERS
    heads = PEER_PAIRS // PEER_TOPK
    chunks = D_MODEL // SC_LANES
    mesh = plsc.VectorSubcoreMesh(core_axis_name="c", subcore_axis_name="s")

    def body(tab_hbm, idx_hbm, x_hbm, d_hbm, idx_v, x_v, rows_a, rows_b, d_v, sem_a, sem_b):
        wid = lax.axis_index("s") * SC_CORES + lax.axis_index("c")
        lane = lax.iota(I32, SC_LANES)
        bufs = ((rows_a, sem_a), (rows_b, sem_b))

        def gather(h):
            buf, sem = bufs[h % 2]
            return pltpu.make_async_copy(tab_hbm.at[idx_v.at[pl.ds(h * PEER_TOPK, PEER_TOPK)]], buf, sem)

        def token(i, carry):
            t = t0 + wid * tw + i
            pltpu.sync_copy(idx_hbm.at[t], idx_v)
            pltpu.sync_copy(x_hbm.at[t], x_v)
            gather(0).start()
            for h in range(heads):
                if h + 1 < heads:
                    gather(h + 1).start()
                gather(h).wait()
                rows_v = bufs[h % 2][0]

                def chunk(j, accs):
                    xj = x_v[pl.ds(j * SC_LANES, SC_LANES)]
                    return tuple(a + rows_v[r, pl.ds(j * SC_LANES, SC_LANES)] * xj for r, a in enumerate(accs))

                accs = lax.fori_loop(0, chunks, chunk,
                                     tuple(jnp.zeros((SC_LANES,), F32) for _ in range(PEER_TOPK)))
                out = jnp.zeros((SC_LANES,), F32)
                for r in range(PEER_TOPK):
                    out = jnp.where(lane == r, jnp.sum(accs[r]), out)
                d_v[pl.ds(h * PEER_TOPK, PEER_TOPK)] = out
            pltpu.sync_copy(d_v, d_hbm.at[t - t0])
            return carry

        lax.fori_loop(0, tw, token, 0)

    return pl.kernel(
        body, mesh=mesh,
        out_type=jax.ShapeDtypeStruct((t_sc, PEER_PAIRS), F32),
        compiler_params=pltpu.CompilerParams(needs_layout_passes=False),
        scratch_types=[pltpu.VMEM((PEER_PAIRS,), I32), pltpu.VMEM((D_MODEL,), F32),
                       pltpu.VMEM((PEER_TOPK, D_MODEL), F32), pltpu.VMEM((PEER_TOPK, D_MODEL), F32),
                       pltpu.VMEM((PEER_PAIRS,), F32), pltpu.SemaphoreType.DMA, pltpu.SemaphoreType.DMA],
        name="peer_u_sc",
    )(tab, expert, xn)


def _peer_coef_kernel(d_ref, gate_ref, c_ref):
    c_ref[...] = gate_ref[...] * _gelu_tanh(d_ref[...])


def _peer_coef(d_tp, gate_tp, tm=2048):
    t = d_tp.shape[0]
    tm = min(tm, t)
    return pl.pallas_call(
        _peer_coef_kernel,
        grid=(t // tm,),
        in_specs=[_tok_spec(tm), _tok_spec(tm)],
        out_specs=_tok_spec(tm),
        out_shape=jax.ShapeDtypeStruct((t, PEER_PAIRS), F32),
        compiler_params=_cparams(("parallel",)),
        name="peer_coef",
    )(d_tp, gate_tp)


PEER_V_TOKENS_PER_STEP = 8
PEER_V_KDIM = (PEER_PAIRS // 2) * 16


def _peer_v_kernel(row_ref, par_ref, coef_ref, h_ref, tab_ref, out_ref, cz_s, *, tb):
    half = PEER_PAIRS // 2
    kdim = PEER_V_KDIM
    nt = PEER_V_TOKENS_PER_STEP
    pk = lax.broadcasted_iota(I32, (PEER_PAIRS, kdim), 0)
    qk = lax.shift_right_logical(lax.broadcasted_iota(I32, (PEER_PAIRS, kdim), 1), 4)
    hk = (lax.broadcasted_iota(I32, (tb, kdim), 1) & 1).astype(F32)
    coef = coef_ref[...].astype(BF16)
    par = par_ref[...].astype(BF16)
    for hf in range(2):
        expand = jnp.where(pk == qk + hf * half, 1.0, 0.0).astype(BF16)
        cz_s[hf] = jnp.where(hk == _dot(par, expand), _dot(coef, expand), 0.0)
    ks = lax.broadcasted_iota(I32, (8, kdim), 0)
    k8 = lax.broadcasted_iota(I32, (8, kdim), 1)
    own_sublane = lax.shift_right_logical(k8 & 15, 1) == ks

    def tok_body(tt, carry):
        rows8 = pl.ds(pl.multiple_of(tt * nt, nt), nt)
        hb = h_ref[rows8, :]
        tiles = []
        for j in range(nt):
            t = tt * nt + j
            pieces = []
            for q in range(half):
                wa = pltpu.bitcast(_load_words(tab_ref, row_ref, t, q), BF16)
                wb = pltpu.bitcast(_load_words(tab_ref, row_ref, t, half + q), BF16)
                pieces.append(jnp.concatenate([wa, wb], axis=1))
            g = jnp.concatenate(pieces, axis=0)
            ck = jnp.concatenate(
                [jnp.where(own_sublane, jnp.broadcast_to(cz_s[hf, pl.ds(t, 1), :], (8, kdim)), 0.0)
                 for hf in range(2)], axis=0).astype(BF16)
            o = _dot(ck, g)
            tiles.append(_token_tile(hb, j) + (o[0:8, 0:128] + o[8:16, 128:256]))
        out_ref[rows8, :] = _tiles_to_rows(tiles)
        return carry

    lax.fori_loop(0, tb // nt, tok_body, 0)


def _peer_v_sc(tab, expert, coef, h2, t0, t_sc):
    tw = t_sc // SC_WORKERS
    heads = PEER_PAIRS // PEER_TOPK
    chunks = D_MODEL // SC_LANES
    mesh = plsc.VectorSubcoreMesh(core_axis_name="c", subcore_axis_name="s")

    def body(tab_hbm, idx_hbm, coef_hbm, h_hbm, out_hbm, idx_v, c_v, acc_v, rows_a, rows_b, sem_a, sem_b):
        wid = lax.axis_index("s") * SC_CORES + lax.axis_index("c")
        lane = lax.iota(I32, SC_LANES)
        bufs = ((rows_a, sem_a), (rows_b, sem_b))

        def gather(h):
            buf, sem = bufs[h % 2]
            return pltpu.make_async_copy(tab_hbm.at[idx_v.at[pl.ds(h * PEER_TOPK, PEER_TOPK)]], buf, sem)

        def token(i, carry):
            t = t0 + wid * tw + i
            pltpu.sync_copy(idx_hbm.at[t], idx_v)
            gather(0).start()
            pltpu.sync_copy(coef_hbm.at[t], c_v)
            pltpu.sync_copy(h_hbm.at[t], acc_v)
            for h in range(heads):
                if h + 1 < heads:
                    gather(h + 1).start()
                gather(h).wait()
                rows_v = bufs[h % 2][0]
                cvec = c_v[pl.ds(h * PEER_TOPK, PEER_TOPK)]
                cs = [jnp.sum(jnp.where(lane == r, cvec, 0.0)) for r in range(PEER_TOPK)]

                def chunk(j, carry2):
                    cols = pl.ds(j * SC_LANES, SC_LANES)
                    a = acc_v[cols]
                    for r in range(PEER_TOPK):
                        a = a + cs[r] * rows_v[r, cols]
                    acc_v[cols] = a
                    return carry2

                lax.fori_loop(0, chunks, chunk, 0)
            pltpu.sync_copy(acc_v, out_hbm.at[t - t0])
            return carry

        lax.fori_loop(0, tw, token, 0)

    return pl.kernel(
        body, mesh=mesh,
        out_type=jax.ShapeDtypeStruct((t_sc, D_MODEL), F32),
        compiler_params=pltpu.CompilerParams(needs_layout_passes=False),
        scratch_types=[pltpu.VMEM((PEER_PAIRS,), I32), pltpu.VMEM((PEER_PAIRS,), F32), pltpu.VMEM((D_MODEL,), F32),
                       pltpu.VMEM((PEER_TOPK, D_MODEL), F32), pltpu.VMEM((PEER_TOPK, D_MODEL), F32),
                       pltpu.SemaphoreType.DMA, pltpu.SemaphoreType.DMA],
        name="peer_v_sc",
    )(tab, expert, coef, h2)


def _peer_v(rows, par, coef, h2, tab, t, tb=128):
    return pl.pallas_call(
        functools.partial(_peer_v_kernel, tb=tb),
        grid=(t // tb,),
        in_specs=[_tok_spec(tb, pltpu.SMEM), _tok_spec(tb), _tok_spec(tb),
                  pl.BlockSpec((tb, D_MODEL), lambda i: (i, 0)), _table_spec()],
        out_specs=pl.BlockSpec((tb, D_MODEL), lambda i: (i, 0)),
        out_shape=jax.ShapeDtypeStruct((t, D_MODEL), F32),
        scratch_shapes=[pltpu.VMEM((2, tb, PEER_V_KDIM), F32)],
        compiler_params=_cparams(("arbitrary",)),
        name="peer_v",
    )(rows, par, coef, h2, tab)


def _final_norm_kernel(x_ref, g_ref, o_ref):
    x = x_ref[...]
    ms = jnp.mean(x * x, axis=-1, keepdims=True)
    o_ref[...] = x * lax.rsqrt(ms + NORM_EPS) * g_ref[...]


def _final_norm(x2, gain, tm=1024):
    t = x2.shape[0]
    tm = min(tm, t)
    row = lambda i: (i, 0)
    return pl.pallas_call(
        _final_norm_kernel,
        grid=(t // tm,),
        in_specs=[pl.BlockSpec((tm, D_MODEL), row), pl.BlockSpec((1, D_MODEL), lambda i: (0, 0))],
        out_specs=pl.BlockSpec((tm, D_MODEL), row),
        out_shape=jax.ShapeDtypeStruct((t, D_MODEL), F32),
        compiler_params=_cparams(("parallel",)),
        name="final_norm",
    )(x2, gain)


def _pack_table(tab):
    bits = lax.bitcast_convert_type(tab.astype(BF16), jnp.uint16).astype(U32)
    return (bits[:TAB_ROWS] | (bits[TAB_ROWS:] << 16)).reshape(TAB_ROWS * 8, 128)


def _peer_start(h2, w, tab_u_all):
    t = h2.shape[0]
    xn, expert, rows, par, gate = _peer_route(h2, w["norm2"], w["wqt"], w["keys"])
    rows, par = rows.T, par.T
    t_tc = min(PEER_U_TC_TOKENS, t)
    d_sc = _peer_u_sc(tab_u_all, expert.T + w["expert_base"], xn, t_tc, t - t_tc)
    d = _peer_u(rows, par, xn, w["tab_u"], t_tc)
    d_tc = d.reshape(t_tc // 8, 8, 8, PEER_TOPK).transpose(0, 2, 3, 1).reshape(t_tc, PEER_PAIRS)
    return dict(h2=h2, rows=rows, par=par, gate=gate.T, d=jnp.concatenate([d_tc, d_sc], axis=0))


def _peer_finish(st, w):
    t = st["h2"].shape[0]
    coef = _peer_coef(st["d"], st["gate"])
    return _peer_v(st["rows"], st["par"], coef, st["h2"], w["tab_v"], t)


def _rep(x, n):
    return jnp.repeat(x, n, axis=-1)


def _inproj_weight(w_in_l):
    o = 0
    qkv = w_in_l[:, o:o + GDN_QKV_W]; o += GDN_QKV_W
    z = w_in_l[:, o:o + GDN_W]; o += GDN_W
    a = w_in_l[:, o:o + GDN_HEADS]; o += GDN_HEADS
    b = w_in_l[:, o:o + GDN_HEADS]; o += GDN_HEADS
    rest = w_in_l[:, o:]
    return jnp.concatenate([qkv, z, _rep(a, GDN_DK), _rep(b, GDN_DK), rest], axis=1).astype(BF16)


def _block_diag_gc(b_gnc):
    g = b_gnc.shape[0]
    eye = jnp.eye(g, dtype=b_gnc.dtype)
    t = jnp.swapaxes(b_gnc, 1, 2)
    return (t[:, :, None, :] * eye[:, None, :, None]).reshape(g * t.shape[1], g * t.shape[2])


def _layer_weights(p, layer):
    row = lambda x: x[None, :]
    flat = lambda x: x.reshape(1, S5_NS)
    return dict(
        lam_init=0.8 - 0.6 * math.exp(-0.3 * layer),
        norm1=row(p["norm1_g"][layer]), w_in=_inproj_weight(p["w_in"][layer]),
        conv_w=p["gdn_conv_w"][layer], a_log=row(_rep(p["gdn_a_log"][layer].astype(F32), GDN_DK)),
        dtb=row(_rep(p["gdn_dt_bias"][layer].astype(F32), GDN_DK)), gdn_ng=row(p["gdn_norm_g"][layer]),
        s5=(flat(p["s5_lambda_re"][layer]), flat(p["s5_lambda_im"][layer]),
            flat(_rep(p["s5_log_step"][layer][:, None], S5_STATE)),
            _block_diag_gc(p["s5_b_re"][layer]), _block_diag_gc(p["s5_b_im"][layer]),
            _block_diag_gc(p["s5_c_re"][layer]), _block_diag_gc(p["s5_c_im"][layer]),
            row(p["s5_d"][layer])),
        attn=tuple(row(p[k][layer]) for k in ("diff_lq1", "diff_lk1", "diff_lq2", "diff_lk2", "diff_norm_g")),
        glu_w=p["s5_glu_w"][layer].astype(BF16), glu_b=row(p["s5_glu_b"][layer]),
        w_out=p["w_out"][layer].astype(BF16),
        norm2=row(p["norm2_g"][layer]), wqt=p["peer_wq"][layer].T.astype(BF16),
        keys=p["peer_keys"][layer].reshape(2 * PEER_HEADS, N_KEYS, PEER_DHALF).astype(BF16),
        tab_u=_pack_table(p["peer_u"][layer]), tab_v=_pack_table(p["peer_v"][layer]),
        expert_base=layer * N_EXPERTS)


def _mixers(h, w):
    bn, seq, _ = h.shape
    t = bn * seq
    gdn_in, us, qc, kc, vc = _inproj(h.reshape(t, D_MODEL), w["norm1"], w["w_in"])
    o_a = _gdn(gdn_in.reshape(bn, seq, INPROJ_GDN_W), w["conv_w"], w["a_log"], w["dtb"], w["gdn_ng"])
    ys = _s5(us.reshape(bn, seq // S5_SUB, S5_SUB * S5_WIDTH), *w["s5"])
    o_c = _attn(qc.reshape(bn, seq, DIFF_W), kc.reshape(bn, seq, DIFF_W), vc.reshape(bn, seq, DIFF_W),
                *w["attn"], w["lam_init"])
    return _outproj(h.reshape(t, D_MODEL), o_a.reshape(t, GDN_W), ys.reshape(t, S5_WIDTH),
                    o_c.reshape(t, DIFF_W), w["glu_w"], w["glu_b"], w["w_out"])


def kernel(x, norm1_g, w_in, gdn_conv_w, gdn_a_log, gdn_dt_bias, gdn_norm_g, s5_lambda_re, s5_lambda_im, s5_b_re, s5_b_im, s5_c_re, s5_c_im, s5_d, s5_log_step, s5_glu_w, s5_glu_b, diff_lq1, diff_lk1, diff_lq2, diff_lk2, diff_norm_g, w_out, norm2_g, peer_wq, peer_keys, peer_u, peer_v, final_g):
    p = dict(norm1_g=norm1_g, w_in=w_in, gdn_conv_w=gdn_conv_w, gdn_a_log=gdn_a_log, gdn_dt_bias=gdn_dt_bias,
             gdn_norm_g=gdn_norm_g, s5_lambda_re=s5_lambda_re, s5_lambda_im=s5_lambda_im, s5_b_re=s5_b_re,
             s5_b_im=s5_b_im, s5_c_re=s5_c_re, s5_c_im=s5_c_im, s5_d=s5_d, s5_log_step=s5_log_step,
             s5_glu_w=s5_glu_w, s5_glu_b=s5_glu_b, diff_lq1=diff_lq1, diff_lk1=diff_lk1, diff_lq2=diff_lq2,
             diff_lk2=diff_lk2, diff_norm_g=diff_norm_g, w_out=w_out, norm2_g=norm2_g, peer_wq=peer_wq,
             peer_keys=peer_keys, peer_u=peer_u, peer_v=peer_v, final_g=final_g)
    bn, seq, _ = x.shape
    weights = [_layer_weights(p, layer) for layer in range(DEPTH)]
    tab_u_all = peer_u.reshape(DEPTH * N_EXPERTS, D_MODEL)
    hs = [x[b:b + 1] for b in range(bn)]
    pending = None
    for layer in range(DEPTH):
        for b in range(bn):
            started = _peer_start(_mixers(hs[b], weights[layer]), weights[layer], tab_u_all)
            if pending is not None:
                pb, pst, pw = pending
                hs[pb] = _peer_finish(pst, pw).reshape(1, seq, D_MODEL)
            pending = (b, started, weights[layer])
    pb, pst, pw = pending
    hs[pb] = _peer_finish(pst, pw).reshape(1, seq, D_MODEL)
    h = jnp.concatenate(hs, axis=0)
    return _final_norm(h.reshape(-1, D_MODEL), final_g[None, :]).reshape(x.shape)
```

```python
import functools
import math

import jax
import jax.numpy as jnp
from jax import lax
from jax.experimental import pallas as pl
from jax.experimental.pallas import tpu as pltpu
from jax.experimental.pallas import tpu_sc as plsc

F32 = jnp.float32
BF16 = jnp.bfloat16
I32 = jnp.int32
U32 = jnp.uint32

D_MODEL = 1024
DEPTH = 2
GDN_HEADS = 4
GDN_DK = 64
GDN_CHUNK = 64
GDN_W = GDN_HEADS * GDN_DK
GDN_QKV_W = 3 * GDN_W
GDN_CONV = 4
S5_WIDTH = 256
S5_GROUPS = 16
S5_GROUP_CH = 16
S5_STATE = 64
S5_NS = S5_GROUPS * S5_STATE
S5_SUB = 16
DIFF_HEADS = 4
DIFF_DQK = 64
DIFF_DV = 128
DIFF_W = 512
PEER_HEADS = 8
PEER_DHALF = 64
N_KEYS = 128
N_EXPERTS = N_KEYS * N_KEYS
PEER_TOPK = 16
PEER_PAIRS = PEER_HEADS * PEER_TOPK
NORM_EPS = 1e-6
NEG_INF = float("-inf")

VMEM_LIMIT_BYTES = 56 * 1024 * 1024


def _cparams(sem, vmem=VMEM_LIMIT_BYTES):
    return pltpu.CompilerParams(dimension_semantics=sem, vmem_limit_bytes=vmem)


def _dot(a, b):
    return jnp.dot(a, b, preferred_element_type=F32)


def _dot_nt(a, b):
    return lax.dot_general(a, b, (((1,), (1,)), ((), ())), preferred_element_type=F32)


def _dot_tn(a, b):
    return lax.dot_general(a, b, (((0,), (0,)), ((), ())), preferred_element_type=F32)


def _split(x):
    hi = x.astype(BF16)
    lo = (x - hi.astype(F32)).astype(BF16)
    return hi, lo


def _dot_sel_r(x, sel):
    hi, lo = _split(x)
    return _dot(hi, sel) + _dot(lo, sel)


def _dot_sel_l(sel, x):
    hi, lo = _split(x)
    return _dot(sel, hi) + _dot(sel, lo)


def _mm3(a, b):
    ah, al = _split(a)
    bh, bl = _split(b)
    return _dot(ah, bh) + (_dot(ah, bl) + _dot(al, bh))


def _sigmoid(x):
    return 1.0 / (1.0 + jnp.exp(-x))


def _softplus(x):
    return jnp.maximum(x, 0.0) + jnp.log1p(jnp.exp(-jnp.abs(x)))


def _gelu_tanh(x):
    c = math.sqrt(2.0 / math.pi)
    return 0.5 * x * (1.0 + jnp.tanh(c * (x + 0.044715 * (x * x * x))))


INPROJ_GDN_W = GDN_QKV_W + 3 * GDN_W


def _inproj_kernel(x_ref, g_ref, w_ref, gdn_ref, us_ref, q_ref, k_ref, v_ref):
    x = x_ref[...]
    ms = jnp.mean(x * x, axis=-1, keepdims=True)
    xn = (x * lax.rsqrt(ms + NORM_EPS) * g_ref[...]).astype(BF16)
    o = INPROJ_GDN_W
    gdn_ref[...] = _dot(xn, w_ref[:, 0:o])
    us_ref[...] = _dot(xn, w_ref[:, o:o + S5_WIDTH])
    o += S5_WIDTH
    q_ref[...] = (_dot(xn, w_ref[:, o:o + DIFF_W]) * (DIFF_DQK ** -0.5)).astype(BF16)
    k_ref[...] = _dot(xn, w_ref[:, o + DIFF_W:o + 2 * DIFF_W]).astype(BF16)
    v_ref[...] = _dot(xn, w_ref[:, o + 2 * DIFF_W:o + 3 * DIFF_W]).astype(BF16)


def _inproj(x2, gain, w, tm=512):
    t = x2.shape[0]
    nw = w.shape[1]
    row = lambda i: (i, 0)
    fixed = lambda i: (0, 0)
    return pl.pallas_call(
        _inproj_kernel,
        grid=(t // tm,),
        in_specs=[pl.BlockSpec((tm, D_MODEL), row), pl.BlockSpec((1, D_MODEL), fixed),
                  pl.BlockSpec((D_MODEL, nw), fixed)],
        out_specs=[pl.BlockSpec((tm, INPROJ_GDN_W), row), pl.BlockSpec((tm, S5_WIDTH), row),
                   pl.BlockSpec((tm, DIFF_W), row), pl.BlockSpec((tm, DIFF_W), row),
                   pl.BlockSpec((tm, DIFF_W), row)],
        out_shape=[jax.ShapeDtypeStruct((t, INPROJ_GDN_W), F32), jax.ShapeDtypeStruct((t, S5_WIDTH), F32),
                   jax.ShapeDtypeStruct((t, DIFF_W), BF16), jax.ShapeDtypeStruct((t, DIFF_W), BF16),
                   jax.ShapeDtypeStruct((t, DIFF_W), BF16)],
        compiler_params=_cparams(("parallel",)),
        name="inproj",
    )(x2, gain, w)


def _gdn_kernel(blk_ref, convw_ref, alog_ref, dtb_ref, ng_ref, out_ref,
                s_ref, tail_ref, xp_ref, q_s, k_s, v_s, b_s, g_s, *, ct):
    c64 = GDN_CHUNK

    @pl.when(pl.program_id(1) == 0)
    def _():
        s_ref[...] = jnp.zeros_like(s_ref)
        tail_ref[...] = jnp.zeros_like(tail_ref)

    qkv = blk_ref[0, :, 0:GDN_QKV_W]
    xp_ref[0:8, :] = tail_ref[...]
    xp_ref[8:8 + ct, :] = qkv
    tail_ref[...] = qkv[ct - 8:ct, :]
    cw = convw_ref[...]
    y = cw[0:1, :] * xp_ref[5:5 + ct, :]
    for j in range(1, GDN_CONV):
        y = y + cw[j:j + 1, :] * xp_ref[5 + j:5 + j + ct, :]
    y = y * _sigmoid(y)

    ri = lax.broadcasted_iota(I32, (GDN_W, GDN_W), 0)
    ci = lax.broadcasted_iota(I32, (GDN_W, GDN_W), 1)
    head_ones = jnp.where((ri // c64) == (ci // c64), 1.0, 0.0).astype(BF16)

    q = y[:, 0:GDN_W]
    k = y[:, GDN_W:2 * GDN_W]
    q_s[...] = q * lax.rsqrt(_dot_sel_r(q * q, head_ones) + 1e-6) * (GDN_DK ** -0.5)
    k_s[...] = k * lax.rsqrt(_dot_sel_r(k * k, head_ones) + 1e-6)
    v_s[...] = y[:, 2 * GDN_W:3 * GDN_W]
    a_rep = blk_ref[0, :, GDN_QKV_W + GDN_W:GDN_QKV_W + 2 * GDN_W]
    b_rep = blk_ref[0, :, GDN_QKV_W + 2 * GDN_W:GDN_QKV_W + 3 * GDN_W]
    b_s[...] = _sigmoid(b_rep)
    g_raw = -jnp.exp(alog_ref[...]) * _softplus(a_rep + dtb_ref[...])
    rt = lax.broadcasted_iota(I32, (ct, ct), 0)
    ctk = lax.broadcasted_iota(I32, (ct, ct), 1)
    tri_bd = jnp.where(((rt // c64) == (ctk // c64)) & (ctk <= rt), 1.0, 0.0).astype(BF16)
    g_s[...] = _dot_sel_l(tri_bd, g_raw)

    r64 = lax.broadcasted_iota(I32, (c64, c64), 0)
    col64 = lax.broadcasted_iota(I32, (c64, c64), 1)
    incl = col64 <= r64
    strict = col64 < r64
    eye_b = col64 == r64
    eye_f = jnp.where(eye_b, 1.0, 0.0)
    ones64 = jnp.ones((c64, c64), BF16)
    ng = ng_ref[...]

    n_chunks = ct // c64
    heads = range(GDN_HEADS)
    chains = [(slice(c * c64, (c + 1) * c64), slice(h * c64, (h + 1) * c64))
              for c in range(n_chunks) for h in heads]
    each = lambda f, *ls: [f(*xs) for xs in zip(*ls)]
    gi = [g_s[r, l] for r, l in chains]
    gj = each(lambda g: _dot_sel_l(ones64, jnp.where(eye_b, g, 0.0)), gi)
    dec = each(lambda a, b: jnp.exp(jnp.where(incl, a - b, NEG_INF)), gi, gj)
    kh = [k_s[r, l] for r, l in chains]
    bi = [b_s[r, l] for r, l in chains]
    kb = each(lambda x: x.astype(BF16), kh)
    kk = each(_dot_nt, kb, kb)
    lm = each(lambda b, x, d: jnp.where(strict, b * x * d, 0.0), bi, kk, dec)
    tinv = each(lambda x: eye_f - x, lm)
    pw = lm
    for _ in range(5):
        pw = each(_mm3, pw, pw)
        tinv = each(lambda t, m: t + _mm3(t, m), tinv, pw)
    tb = each(lambda x: x.astype(BF16), tinv)
    eg = each(jnp.exp, gi)
    u = each(lambda t, rl, b: _dot(t, (v_s[rl[0], rl[1]] * b).astype(BF16)), tb, chains, bi)
    w = each(lambda t, k, b, e: _dot(t, (k * (b * e)).astype(BF16)).astype(BF16), tb, kh, bi, eg)
    qh = [q_s[r, l] for r, l in chains]
    qk = each(lambda q, k, d: jnp.where(incl, _dot_nt(q.astype(BF16), k) * d, 0.0).astype(BF16), qh, kb, dec)
    qg = each(lambda q, e: (q * e).astype(BF16), qh, eg)
    g_last = each(lambda g: g[c64 - 1:c64, :], gi)
    kg = each(lambda k, gl, g: (k * jnp.exp(gl - g)).astype(BF16), kh, g_last, gi)
    decay_last = each(jnp.exp, g_last)

    states = [s_ref[h] for h in heads]
    for c in range(n_chunks):
        ids = [c * GDN_HEADS + h for h in heads]
        sb = [s.astype(BF16) for s in states]
        vnb = [(u[n] - _dot(w[n], sb[h])).astype(BF16) for h, n in zip(heads, ids)]
        o = [_dot(qg[n], sb[h]) + _dot(qk[n], vnb[h]) for h, n in zip(heads, ids)]
        states = [states[h] * decay_last[n] + _dot_tn(kg[n], vnb[h]) for h, n in zip(heads, ids)]
        ms = [_dot_sel_r(x * x, ones64) * (1.0 / c64) for x in o]
        outs = [x * lax.rsqrt(m + NORM_EPS) * ng for x, m in zip(o, ms)]
        rows = slice(c * c64, (c + 1) * c64)
        z = blk_ref[0, rows, GDN_QKV_W:GDN_QKV_W + GDN_W]
        out_ref[0, rows, :] = (jnp.concatenate(outs, axis=1) * (z * _sigmoid(z))).astype(out_ref.dtype)
    for h in heads:
        s_ref[h] = states[h]


def _gdn(gdn_in, conv_w, a_log_rep, dtb_rep, ng, ct=256):
    bn, seq, _ = gdn_in.shape
    fixed = lambda b, l: (0, 0)
    return pl.pallas_call(
        functools.partial(_gdn_kernel, ct=ct),
        grid=(bn, seq // ct),
        in_specs=[pl.BlockSpec((1, ct, INPROJ_GDN_W), lambda b, l: (b, l, 0)),
                  pl.BlockSpec((GDN_CONV, GDN_QKV_W), fixed), pl.BlockSpec((1, GDN_W), fixed),
                  pl.BlockSpec((1, GDN_W), fixed), pl.BlockSpec((1, GDN_DK), fixed)],
        out_specs=pl.BlockSpec((1, ct, GDN_W), lambda b, l: (b, l, 0)),
        out_shape=jax.ShapeDtypeStruct((bn, seq, GDN_W), BF16),
        scratch_shapes=[pltpu.VMEM((GDN_HEADS, GDN_DK, GDN_DK), F32), pltpu.VMEM((8, GDN_QKV_W), F32),
                        pltpu.VMEM((ct + 8, GDN_QKV_W), F32)] + [pltpu.VMEM((ct, GDN_W), F32)] * 5,
        compiler_params=_cparams(("arbitrary", "arbitrary")),
        name="gdn",
    )(gdn_in, conv_w, a_log_rep, dtb_rep, ng)


def _s5_kernel(u_ref, lre_ref, lim_ref, lstep_ref, bre_ref, bim_ref, cre_ref, cim_ref, d_ref, y_ref,
               bmat, cmat, avec, carry, zr_s, zi_s, xr_s, xi_s, *, rb):
    ns = S5_NS
    cw = S5_WIDTH

    @pl.when(pl.program_id(1) == 0)
    def _():
        step = jnp.exp(lstep_ref[...])
        lr = lre_ref[...]
        li = lim_ref[...]
        mag = jnp.exp(lr * step)
        ar = mag * jnp.cos(li * step)
        ai = mag * jnp.sin(li * step)
        den = lr * lr + li * li
        mr = ((ar - 1.0) * lr + ai * li) / den
        mi = (ai * lr - (ar - 1.0) * li) / den
        bre = bre_ref[...]
        bim = bim_ref[...]
        bmat[:, 0:ns] = (mr * bre - mi * bim).astype(BF16)
        bmat[:, ns:2 * ns] = (mr * bim + mi * bre).astype(BF16)
        cmat[0:ns, :] = cre_ref[...].astype(BF16)
        cmat[ns:2 * ns, :] = (-cim_ref[...]).astype(BF16)
        avec[0:1, :] = ar
        avec[1:2, :] = ai
        pr, pi = ar, ai
        for _ in range(4):
            pr, pi = pr * pr - pi * pi, 2.0 * pr * pi
        avec[2:3, :] = pr
        avec[3:4, :] = pi
        carry[...] = jnp.zeros_like(carry)

    ar = avec[0:1, :]
    ai = avec[1:2, :]
    a16r = avec[2:3, :]
    a16i = avec[3:4, :]

    def inject(s):
        ub = u_ref[0, :, s * cw:(s + 1) * cw].astype(BF16)
        return _dot(ub, bmat[...])

    xr = jnp.zeros((rb, ns), F32)
    xi = jnp.zeros((rb, ns), F32)
    for s in range(S5_SUB):
        p = inject(s)
        xr, xi = ar * xr - ai * xi + p[:, 0:ns], ar * xi + ai * xr + p[:, ns:2 * ns]
    zr_s[...] = xr
    zi_s[...] = xi

    def row_step(kk, c):
        cr, ci_ = c
        xr_s[pl.ds(kk, 1), :] = cr
        xi_s[pl.ds(kk, 1), :] = ci_
        zr = zr_s[pl.ds(kk, 1), :]
        zi = zi_s[pl.ds(kk, 1), :]
        return (a16r * cr - a16i * ci_ + zr, a16r * ci_ + a16i * cr + zi)

    cr, ci_ = lax.fori_loop(0, rb, row_step, (carry[0:1, :], carry[1:2, :]))
    carry[0:1, :] = cr
    carry[1:2, :] = ci_

    xr = xr_s[...]
    xi = xi_s[...]
    dsk = d_ref[...]
    for s in range(S5_SUB):
        p = inject(s)
        xr, xi = ar * xr - ai * xi + p[:, 0:ns], ar * xi + ai * xr + p[:, ns:2 * ns]
        yv = _dot(xr.astype(BF16), cmat[0:ns, :]) + _dot(xi.astype(BF16), cmat[ns:2 * ns, :])
        yv = yv + dsk * u_ref[0, :, s * cw:(s + 1) * cw]
        y_ref[0, :, s * cw:(s + 1) * cw] = _gelu_tanh(yv).astype(y_ref.dtype)


def _s5(u_rows, lre, lim, lstep, bre_bd, bim_bd, cre_bd, cim_bd, dskip, rb=128):
    bn, nrows, rw = u_rows.shape
    fixed = lambda b, r: (0, 0)
    ns = S5_NS
    return pl.pallas_call(
        functools.partial(_s5_kernel, rb=rb),
        grid=(bn, nrows // rb),
        in_specs=[pl.BlockSpec((1, rb, rw), lambda b, r: (b, r, 0)),
                  pl.BlockSpec((1, ns), fixed), pl.BlockSpec((1, ns), fixed), pl.BlockSpec((1, ns), fixed),
                  pl.BlockSpec((S5_WIDTH, ns), fixed), pl.BlockSpec((S5_WIDTH, ns), fixed),
                  pl.BlockSpec((ns, S5_WIDTH), fixed), pl.BlockSpec((ns, S5_WIDTH), fixed),
                  pl.BlockSpec((1, S5_WIDTH), fixed)],
        out_specs=pl.BlockSpec((1, rb, rw), lambda b, r: (b, r, 0)),
        out_shape=jax.ShapeDtypeStruct((bn, nrows, rw), BF16),
        scratch_shapes=[pltpu.VMEM((S5_WIDTH, 2 * ns), BF16), pltpu.VMEM((2 * ns, S5_WIDTH), BF16),
                        pltpu.VMEM((8, ns), F32), pltpu.VMEM((8, ns), F32)]
                       + [pltpu.VMEM((rb, ns), F32)] * 4,
        compiler_params=_cparams(("arbitrary", "arbitrary")),
        name="s5",
    )(u_rows, lre, lim, lstep, bre_bd, bim_bd, cre_bd, cim_bd, dskip)


def _attn_kernel(q_ref, k_ref, v_ref, lq1_ref, lk1_ref, lq2_ref, lk2_ref, ng_ref, o_ref,
                 qs_s, m_s, l_s, acc_s, *, blk, lam_init):
    i = pl.program_id(2)
    q = q_ref[0]
    lane = lax.broadcasted_iota(I32, q.shape, 1)
    zero = jnp.zeros_like(q)
    qs_s[0:blk, :] = jnp.where(lane < DIFF_DQK, q, zero)
    qs_s[blk:2 * blk, :] = jnp.where(lane >= DIFF_DQK, q, zero)
    m_s[...] = jnp.full_like(m_s, NEG_INF)
    l_s[...] = jnp.zeros_like(l_s)
    acc_s[...] = jnp.zeros_like(acc_s)
    ones = jnp.ones((blk, DIFF_DV), BF16)

    def block_rows(j):
        return pl.ds(pl.multiple_of(j * blk, blk), blk)

    def scores(j):
        return _dot_nt(qs_s[...], k_ref[0, block_rows(j), :])

    def update(j, s):
        m_old = m_s[...]
        m_new = jnp.maximum(m_old, jnp.max(s, axis=-1, keepdims=True))
        p = jnp.exp(s - jnp.concatenate([m_new] * (blk // DIFF_DV), axis=1)).astype(BF16)
        alpha = jnp.exp(m_old - m_new)
        pv = _dot(p, jnp.concatenate([v_ref[0, block_rows(j), :], ones], axis=1))
        acc_s[...] = alpha * acc_s[...] + pv[:, 0:DIFF_DV]
        l_s[...] = alpha * l_s[...] + pv[:, DIFF_DV:2 * DIFF_DV]
        m_s[...] = m_new

    def body(j, s):
        s_next = scores(j + 1)
        update(j, s)
        return s_next

    s = lax.fori_loop(0, i, body, scores(0))
    row = lax.broadcasted_iota(I32, s.shape, 0) & (blk - 1)
    col = lax.broadcasted_iota(I32, s.shape, 1)
    update(i, jnp.where(col <= row, s, NEG_INF))
    lam = (jnp.exp(jnp.sum(lq1_ref[...] * lk1_ref[...], axis=-1, keepdims=True))
           - jnp.exp(jnp.sum(lq2_ref[...] * lk2_ref[...], axis=-1, keepdims=True)) + lam_init)
    o = acc_s[0:blk, :] / l_s[0:blk, :] - lam * (acc_s[blk:2 * blk, :] / l_s[blk:2 * blk, :])
    ms = jnp.mean(o * o, axis=-1, keepdims=True)
    o = o * lax.rsqrt(ms + NORM_EPS) * ng_ref[...] * (1.0 - lam_init)
    o_ref[0] = o.astype(o_ref.dtype)


def _attn(q, k, v, lq1, lk1, lq2, lk2, ng, lam_init, blk=512):
    bn, seq, _ = q.shape
    blk = min(blk, seq)
    fixed = lambda b, h, i: (0, 0)
    kv_spec = pl.BlockSpec((1, seq, DIFF_DV), lambda b, h, i: (b, 0, h))
    return pl.pallas_call(
        functools.partial(_attn_kernel, blk=blk, lam_init=lam_init),
        grid=(bn, DIFF_HEADS, seq // blk),
        in_specs=[pl.BlockSpec((1, blk, DIFF_DV), lambda b, h, i: (b, i, h)), kv_spec, kv_spec]
                 + [pl.BlockSpec((1, DIFF_DQK), fixed)] * 4 + [pl.BlockSpec((1, DIFF_DV), fixed)],
        out_specs=pl.BlockSpec((1, blk, DIFF_DV), lambda b, h, i: (b, i, h)),
        out_shape=jax.ShapeDtypeStruct((bn, seq, DIFF_W), BF16),
        scratch_shapes=[pltpu.VMEM((2 * blk, DIFF_DV), BF16), pltpu.VMEM((2 * blk, DIFF_DV), F32),
                        pltpu.VMEM((2 * blk, DIFF_DV), F32), pltpu.VMEM((2 * blk, DIFF_DV), F32)],
        compiler_params=_cparams(("parallel", "parallel", "arbitrary")),
        name="diff_attn",
    )(q, k, v, lq1, lk1, lq2, lk2, ng)


def _outproj_kernel(h_ref, oa_ref, ys_ref, oc_ref, gw_ref, gb_ref, wo_ref, out_ref):
    zg = _dot(ys_ref[...], gw_ref[...]) + gb_ref[...]
    ob = (zg[:, 0:S5_WIDTH] * _sigmoid(zg[:, S5_WIDTH:2 * S5_WIDTH])).astype(BF16)
    acc = _dot(oa_ref[...], wo_ref[0:GDN_W, :])
    acc = acc + _dot(ob, wo_ref[GDN_W:GDN_W + S5_WIDTH, :])
    acc = acc + _dot(oc_ref[...], wo_ref[GDN_W + S5_WIDTH:, :])
    out_ref[...] = h_ref[...] + acc


def _outproj(h2, oa, ys, oc, glu_w, glu_b, w_out, tm=512):
    t = h2.shape[0]
    row = lambda i: (i, 0)
    fixed = lambda i: (0, 0)
    return pl.pallas_call(
        _outproj_kernel,
        grid=(t // tm,),
        in_specs=[pl.BlockSpec((tm, D_MODEL), row), pl.BlockSpec((tm, GDN_W), row),
                  pl.BlockSpec((tm, S5_WIDTH), row), pl.BlockSpec((tm, DIFF_W), row),
                  pl.BlockSpec((S5_WIDTH, 2 * S5_WIDTH), fixed), pl.BlockSpec((1, 2 * S5_WIDTH), fixed),
                  pl.BlockSpec((D_MODEL, D_MODEL), fixed)],
        out_specs=pl.BlockSpec((tm, D_MODEL), row),
        out_shape=jax.ShapeDtypeStruct((t, D_MODEL), F32),
        compiler_params=_cparams(("parallel",)),
        name="outproj",
    )(h2, oa, ys, oc, glu_w, glu_b, w_out)


_BIG_ID = 1.0e9


def _top16(x, ids, payload):
    n = x.shape[1]
    r16 = lax.broadcasted_iota(I32, (PEER_TOPK, n), 0)
    vals = jnp.zeros((PEER_TOPK, n), F32)
    pays = jnp.zeros((PEER_TOPK, n), F32)
    for kk in range(PEER_TOPK):
        m = jnp.max(x, axis=0, keepdims=True)
        first = jnp.min(jnp.where(x == m, ids, _BIG_ID), axis=0, keepdims=True)
        hit = ids == first
        pay = first if payload is None else jnp.max(jnp.where(hit, payload, -1.0), axis=0, keepdims=True)
        x = jnp.where(hit, NEG_INF, x)
        vals = jnp.where(r16 == kk, m, vals)
        pays = jnp.where(r16 == kk, pay, pays)
    return vals, pays


def _peer_route_kernel(h_ref, g_ref, wqt_ref, keys_ref, xn_ref, exp_ref, row_ref, par_ref, gate_ref,
                       qt_s, sv_s, si_s, *, tm):
    x = h_ref[...]
    ms = jnp.mean(x * x, axis=-1, keepdims=True)
    xn = x * lax.rsqrt(ms + NORM_EPS) * g_ref[...]
    xn_ref[...] = xn
    qt_s[...] = _dot_nt(wqt_ref[...], xn.astype(BF16)).astype(BF16)

    key_id = lax.broadcasted_iota(I32, (N_KEYS, tm), 0).astype(F32)

    def half_body(hp, carry):
        r0 = pl.multiple_of(hp * PEER_DHALF, PEER_DHALF)
        s = _dot(keys_ref[hp], qt_s[pl.ds(r0, PEER_DHALF), :])
        vals, ids = _top16(s, key_id, None)
        sv_s[hp] = vals
        si_s[hp] = ids
        return carry

    lax.fori_loop(0, 2 * PEER_HEADS, half_body, 0)

    i8 = lax.broadcasted_iota(I32, (8, tm), 0).astype(F32)

    def head_body(hd, carry):
        a0 = sv_s[2 * hd]
        a1 = sv_s[2 * hd + 1]
        e0 = si_s[2 * hd] * float(N_KEYS)
        e1 = si_s[2 * hd + 1]
        cs, es, fs = [], [], []
        for i in range(8):
            cs.append(a0[i:i + 1, :] + a1[0:8, :])
            es.append(e0[i:i + 1, :] + e1[0:8, :])
            fs.append(i8 + float(i * PEER_TOPK))
        cs.append(a0[0:1, :] + a1[8:16, :])
        es.append(e0[0:1, :] + e1[8:16, :])
        fs.append(i8 + 8.0)
        cs.append(a0[8:16, :] + a1[0:1, :])
        es.append(e0[8:16, :] + e1[0:1, :])
        fs.append((i8 + 8.0) * float(PEER_TOPK))
        top_s, experts = _top16(jnp.concatenate(cs, axis=0), jnp.concatenate(fs, axis=0),
                                jnp.concatenate(es, axis=0))
        ex = jnp.exp(top_s - jnp.max(top_s, axis=0, keepdims=True))
        rows = pl.ds(pl.multiple_of(hd * PEER_TOPK, PEER_TOPK), PEER_TOPK)
        gate_ref[rows, :] = ex / jnp.sum(ex, axis=0, keepdims=True)
        expert = experts.astype(I32)
        exp_ref[rows, :] = expert
        row_ref[rows, :] = (expert & (TAB_ROWS - 1)) * 8
        par_ref[rows, :] = lax.shift_right_logical(expert, TAB_ROWS.bit_length() - 1).astype(F32)
        return carry

    lax.fori_loop(0, PEER_HEADS, head_body, 0)


def _peer_route(h2, gain, wqt, keys, tm=512):
    t = h2.shape[0]
    tm = min(tm, t)
    row = lambda i: (i, 0)
    col = lambda i: (0, i)
    return pl.pallas_call(
        functools.partial(_peer_route_kernel, tm=tm),
        grid=(t // tm,),
        in_specs=[pl.BlockSpec((tm, D_MODEL), row), pl.BlockSpec((1, D_MODEL), lambda i: (0, 0)),
                  pl.BlockSpec((D_MODEL, D_MODEL), lambda i: (0, 0)),
                  pl.BlockSpec((2 * PEER_HEADS, N_KEYS, PEER_DHALF), lambda i: (0, 0, 0))],
        out_specs=[pl.BlockSpec((tm, D_MODEL), row)] + [pl.BlockSpec((PEER_PAIRS, tm), col)] * 4,
        out_shape=[jax.ShapeDtypeStruct((t, D_MODEL), F32), jax.ShapeDtypeStruct((PEER_PAIRS, t), I32),
                   jax.ShapeDtypeStruct((PEER_PAIRS, t), I32), jax.ShapeDtypeStruct((PEER_PAIRS, t), F32),
                   jax.ShapeDtypeStruct((PEER_PAIRS, t), F32)],
        scratch_shapes=[pltpu.VMEM((D_MODEL, tm), BF16), pltpu.VMEM((2 * PEER_HEADS, PEER_TOPK, tm), F32),
                        pltpu.VMEM((2 * PEER_HEADS, PEER_TOPK, tm), F32)],
        compiler_params=_cparams(("parallel",)),
        name="peer_route",
    )(h2, gain, wqt, keys)


TAB_ROWS = N_EXPERTS // 2


def _token_tile(block, tl):
    return jnp.concatenate([block[tl:tl + 1, s * 128:(s + 1) * 128] for s in range(8)], axis=0)


def _tiles_to_rows(tiles):
    return jnp.concatenate([jnp.concatenate([tile[s:s + 1, :] for tile in tiles], axis=0) for s in range(8)], axis=1)


def _load_words(tab_ref, row_ref, t, p):
    return tab_ref[pl.ds(pl.multiple_of(row_ref[t, p], 8), 8), :]


def _tok_spec(tb, space=None):
    return pl.BlockSpec((tb, PEER_PAIRS), lambda i: (i, 0), memory_space=space)


def _table_spec():
    return pl.BlockSpec((TAB_ROWS * 8, 128), lambda i: (0, 0), pipeline_mode=pl.Buffered(1))


SC_CORES = 2
SC_SUBCORES = 16
SC_LANES = 16
SC_WORKERS = SC_CORES * SC_SUBCORES


def _peer_u_sc(tab, expert, xn, t0, t_sc):
    tw = t_sc // SC_WORKERS
    heads = PEER_PAIRS // PEER_TOPK
    chunks = D_MODEL // SC_LANES
    mesh = plsc.VectorSubcoreMesh(core_axis_name="c", subcore_axis_name="s")

    def body(tab_hbm, idx_hbm, x_hbm, d_hbm, idx_a, idx_b, x_a, x_b, rows_a, rows_b, d_v,
             sem_ra, sem_rb, sem_ia, sem_ib, sem_xa, sem_xb):
        wid = lax.axis_index("s") * SC_CORES + lax.axis_index("c")
        lane = lax.iota(I32, SC_LANES)
        first = t0 + wid * tw
        last = first + tw - 1
        tok_bufs = ((idx_a, x_a, sem_ia, sem_xa), (idx_b, x_b, sem_ib, sem_xb))
        row_bufs = ((rows_a, sem_ra), (rows_b, sem_rb))

        def fetch(t, k):
            idx_v, x_v, sem_i, sem_x = tok_bufs[k]
            return (pltpu.make_async_copy(idx_hbm.at[t], idx_v, sem_i),
                    pltpu.make_async_copy(x_hbm.at[t], x_v, sem_x))

        def gather(k, h):
            buf, sem = row_bufs[h % 2]
            return pltpu.make_async_copy(tab_hbm.at[tok_bufs[k][0].at[pl.ds(h * PEER_TOPK, PEER_TOPK)]], buf, sem)

        for c in fetch(first, 0):
            c.start()
        for c in fetch(first, 0):
            c.wait()
        gather(0, 0).start()

        def two_tokens(i2, carry):
            for k in range(2):
                t = first + 2 * i2 + k
                x_v = tok_bufs[k][1]
                nxt = fetch(jnp.minimum(t + 1, last), 1 - k)
                for c in nxt:
                    c.start()
                for h in range(heads):
                    if h + 1 < heads:
                        gather(k, h + 1).start()
                    else:
                        for c in nxt:
                            c.wait()
                        gather(1 - k, 0).start()
                    gather(k, h).wait()
                    rows_v = row_bufs[h % 2][0]

                    def chunk(j, accs):
                        xj = x_v[pl.ds(j * SC_LANES, SC_LANES)]
                        return tuple(a + rows_v[r, pl.ds(j * SC_LANES, SC_LANES)] * xj for r, a in enumerate(accs))

                    accs = lax.fori_loop(0, chunks, chunk,
                                         tuple(jnp.zeros((SC_LANES,), F32) for _ in range(PEER_TOPK)))
                    out = jnp.zeros((SC_LANES,), F32)
                    for r in range(PEER_TOPK):
                        out = jnp.where(lane == r, jnp.sum(accs[r]), out)
                    d_v[pl.ds(h * PEER_TOPK, PEER_TOPK)] = out
                pltpu.sync_copy(d_v, d_hbm.at[t - t0])
            return carry

        lax.fori_loop(0, tw // 2, two_tokens, 0)
        gather(0, 0).wait()

    return pl.kernel(
        body, mesh=mesh,
        out_type=jax.ShapeDtypeStruct((t_sc, PEER_PAIRS), F32),
        compiler_params=pltpu.CompilerParams(needs_layout_passes=False),
        scratch_types=[pltpu.VMEM((PEER_PAIRS,), I32), pltpu.VMEM((PEER_PAIRS,), I32),
                       pltpu.VMEM((D_MODEL,), F32), pltpu.VMEM((D_MODEL,), F32),
                       pltpu.VMEM((PEER_TOPK, D_MODEL), F32), pltpu.VMEM((PEER_TOPK, D_MODEL), F32),
                       pltpu.VMEM((PEER_PAIRS,), F32)] + [pltpu.SemaphoreType.DMA] * 6,
        name="peer_u_sc",
    )(tab, expert, xn)


def _peer_coef_kernel(d_ref, gate_ref, c_ref):
    c_ref[...] = gate_ref[...] * _gelu_tanh(d_ref[...])


def _peer_coef(d_tp, gate_tp, tm=2048):
    t = d_tp.shape[0]
    tm = min(tm, t)
    return pl.pallas_call(
        _peer_coef_kernel,
        grid=(t // tm,),
        in_specs=[_tok_spec(tm), _tok_spec(tm)],
        out_specs=_tok_spec(tm),
        out_shape=jax.ShapeDtypeStruct((t, PEER_PAIRS), F32),
        compiler_params=_cparams(("parallel",)),
        name="peer_coef",
    )(d_tp, gate_tp)


PEER_V_TOKENS_PER_STEP = 8
PEER_V_KDIM = (PEER_PAIRS // 2) * 16


def _peer_v_kernel(row_ref, par_ref, coef_ref, h_ref, tab_ref, out_ref, cz_s, *, tb):
    half = PEER_PAIRS // 2
    kdim = PEER_V_KDIM
    nt = PEER_V_TOKENS_PER_STEP
    pk = lax.broadcasted_iota(I32, (PEER_PAIRS, kdim), 0)
    qk = lax.shift_right_logical(lax.broadcasted_iota(I32, (PEER_PAIRS, kdim), 1), 4)
    hk = (lax.broadcasted_iota(I32, (tb, kdim), 1) & 1).astype(F32)
    coef = coef_ref[...].astype(BF16)
    par = par_ref[...].astype(BF16)
    for hf in range(2):
        expand = jnp.where(pk == qk + hf * half, 1.0, 0.0).astype(BF16)
        cz_s[hf] = jnp.where(hk == _dot(par, expand), _dot(coef, expand), 0.0)
    ks = lax.broadcasted_iota(I32, (8, kdim), 0)
    k8 = lax.broadcasted_iota(I32, (8, kdim), 1)
    own_sublane = lax.shift_right_logical(k8 & 15, 1) == ks

    def tok_body(tt, carry):
        rows8 = pl.ds(pl.multiple_of(tt * nt, nt), nt)
        hb = h_ref[rows8, :]
        tiles = []
        for j in range(nt):
            t = tt * nt + j
            pieces = []
            for q in range(half):
                wa = pltpu.bitcast(_load_words(tab_ref, row_ref, t, q), BF16)
                wb = pltpu.bitcast(_load_words(tab_ref, row_ref, t, half + q), BF16)
                pieces.append(jnp.concatenate([wa, wb], axis=1))
            g = jnp.concatenate(pieces, axis=0)
            ck = jnp.concatenate(
                [jnp.where(own_sublane, jnp.broadcast_to(cz_s[hf, pl.ds(t, 1), :], (8, kdim)), 0.0)
                 for hf in range(2)], axis=0).astype(BF16)
            o = _dot(ck, g)
            tiles.append(_token_tile(hb, j) + (o[0:8, 0:128] + o[8:16, 128:256]))
        out_ref[rows8, :] = _tiles_to_rows(tiles)
        return carry

    lax.fori_loop(0, tb // nt, tok_body, 0)


def _peer_v(rows, par, coef, h2, tab, t, tb=128):
    return pl.pallas_call(
        functools.partial(_peer_v_kernel, tb=tb),
        grid=(t // tb,),
        in_specs=[_tok_spec(tb, pltpu.SMEM), _tok_spec(tb), _tok_spec(tb),
                  pl.BlockSpec((tb, D_MODEL), lambda i: (i, 0)), _table_spec()],
        out_specs=pl.BlockSpec((tb, D_MODEL), lambda i: (i, 0)),
        out_shape=jax.ShapeDtypeStruct((t, D_MODEL), F32),
        scratch_shapes=[pltpu.VMEM((2, tb, PEER_V_KDIM), F32)],
        compiler_params=_cparams(("arbitrary",)),
        name="peer_v",
    )(rows, par, coef, h2, tab)


def _final_norm_kernel(x_ref, g_ref, o_ref):
    x = x_ref[...]
    ms = jnp.mean(x * x, axis=-1, keepdims=True)
    o_ref[...] = x * lax.rsqrt(ms + NORM_EPS) * g_ref[...]


def _final_norm(x2, gain, tm=1024):
    t = x2.shape[0]
    tm = min(tm, t)
    row = lambda i: (i, 0)
    return pl.pallas_call(
        _final_norm_kernel,
        grid=(t // tm,),
        in_specs=[pl.BlockSpec((tm, D_MODEL), row), pl.BlockSpec((1, D_MODEL), lambda i: (0, 0))],
        out_specs=pl.BlockSpec((tm, D_MODEL), row),
        out_shape=jax.ShapeDtypeStruct((t, D_MODEL), F32),
        compiler_params=_cparams(("parallel",)),
        name="final_norm",
    )(x2, gain)


def _pack_table(tab):
    bits = lax.bitcast_convert_type(tab.astype(BF16), jnp.uint16).astype(U32)
    return (bits[:TAB_ROWS] | (bits[TAB_ROWS:] << 16)).reshape(TAB_ROWS * 8, 128)


def _peer_start(h2, w, tab_u_all):
    t = h2.shape[0]
    xn, expert, rows, par, gate = _peer_route(h2, w["norm2"], w["wqt"], w["keys"])
    d = _peer_u_sc(tab_u_all, expert.T + w["expert_base"], xn, 0, t)
    return dict(h2=h2, rows=rows.T, par=par.T, gate=gate.T, d=d)


def _peer_finish(st, w):
    t = st["h2"].shape[0]
    coef = _peer_coef(st["d"], st["gate"])
    return _peer_v(st["rows"], st["par"], coef, st["h2"], w["tab_v"], t)


def _rep(x, n):
    return jnp.repeat(x, n, axis=-1)


def _inproj_weight(w_in_l):
    o = 0
    qkv = w_in_l[:, o:o + GDN_QKV_W]; o += GDN_QKV_W
    z = w_in_l[:, o:o + GDN_W]; o += GDN_W
    a = w_in_l[:, o:o + GDN_HEADS]; o += GDN_HEADS
    b = w_in_l[:, o:o + GDN_HEADS]; o += GDN_HEADS
    rest = w_in_l[:, o:]
    return jnp.concatenate([qkv, z, _rep(a, GDN_DK), _rep(b, GDN_DK), rest], axis=1).astype(BF16)


def _block_diag_gc(b_gnc):
    g = b_gnc.shape[0]
    eye = jnp.eye(g, dtype=b_gnc.dtype)
    t = jnp.swapaxes(b_gnc, 1, 2)
    return (t[:, :, None, :] * eye[:, None, :, None]).reshape(g * t.shape[1], g * t.shape[2])


def _layer_weights(p, layer):
    row = lambda x: x[None, :]
    flat = lambda x: x.reshape(1, S5_NS)
    return dict(
        lam_init=0.8 - 0.6 * math.exp(-0.3 * layer),
        norm1=row(p["norm1_g"][layer]), w_in=_inproj_weight(p["w_in"][layer]),
        conv_w=p["gdn_conv_w"][layer], a_log=row(_rep(p["gdn_a_log"][layer].astype(F32), GDN_DK)),
        dtb=row(_rep(p["gdn_dt_bias"][layer].astype(F32), GDN_DK)), gdn_ng=row(p["gdn_norm_g"][layer]),
        s5=(flat(p["s5_lambda_re"][layer]), flat(p["s5_lambda_im"][layer]),
            flat(_rep(p["s5_log_step"][layer][:, None], S5_STATE)),
            _block_diag_gc(p["s5_b_re"][layer]), _block_diag_gc(p["s5_b_im"][layer]),
            _block_diag_gc(p["s5_c_re"][layer]), _block_diag_gc(p["s5_c_im"][layer]),
            row(p["s5_d"][layer])),
        attn=tuple(row(p[k][layer]) for k in ("diff_lq1", "diff_lk1", "diff_lq2", "diff_lk2", "diff_norm_g")),
        glu_w=p["s5_glu_w"][layer].astype(BF16), glu_b=row(p["s5_glu_b"][layer]),
        w_out=p["w_out"][layer].astype(BF16),
        norm2=row(p["norm2_g"][layer]), wqt=p["peer_wq"][layer].T.astype(BF16),
        keys=p["peer_keys"][layer].reshape(2 * PEER_HEADS, N_KEYS, PEER_DHALF).astype(BF16),
        tab_v=_pack_table(p["peer_v"][layer]), expert_base=layer * N_EXPERTS)


def _mixers(h, w):
    bn, seq, _ = h.shape
    t = bn * seq
    gdn_in, us, qc, kc, vc = _inproj(h.reshape(t, D_MODEL), w["norm1"], w["w_in"])
    o_a = _gdn(gdn_in.reshape(bn, seq, INPROJ_GDN_W), w["conv_w"], w["a_log"], w["dtb"], w["gdn_ng"])
    ys = _s5(us.reshape(bn, seq // S5_SUB, S5_SUB * S5_WIDTH), *w["s5"])
    o_c = _attn(qc.reshape(bn, seq, DIFF_W), kc.reshape(bn, seq, DIFF_W), vc.reshape(bn, seq, DIFF_W),
                *w["attn"], w["lam_init"])
    return _outproj(h.reshape(t, D_MODEL), o_a.reshape(t, GDN_W), ys.reshape(t, S5_WIDTH),
                    o_c.reshape(t, DIFF_W), w["glu_w"], w["glu_b"], w["w_out"])


def kernel(x, norm1_g, w_in, gdn_conv_w, gdn_a_log, gdn_dt_bias, gdn_norm_g, s5_lambda_re, s5_lambda_im, s5_b_re, s5_b_im, s5_c_re, s5_c_im, s5_d, s5_log_step, s5_glu_w, s5_glu_b, diff_lq1, diff_lk1, diff_lq2, diff_lk2, diff_norm_g, w_out, norm2_g, peer_wq, peer_keys, peer_u, peer_v, final_g):
    p = dict(norm1_g=norm1_g, w_in=w_in, gdn_conv_w=gdn_conv_w, gdn_a_log=gdn_a_log, gdn_dt_bias=gdn_dt_bias,
             gdn_norm_g=gdn_norm_g, s5_lambda_re=s5_lambda_re, s5_lambda_im=s5_lambda_im, s5_b_re=s5_b_re,
             s5_b_im=s5_b_im, s5_c_re=s5_c_re, s5_c_im=s5_c_im, s5_d=s5_d, s5_log_step=s5_log_step,
             s5_glu_w=s5_glu_w, s5_glu_b=s5_glu_b, diff_lq1=diff_lq1, diff_lk1=diff_lk1, diff_lq2=diff_lq2,
             diff_lk2=diff_lk2, diff_norm_g=diff_norm_g, w_out=w_out, norm2_g=norm2_g, peer_wq=peer_wq,
             peer_keys=peer_keys, peer_u=peer_u, peer_v=peer_v, final_g=final_g)
    bn, seq, _ = x.shape
    weights = [_layer_weights(p, layer) for layer in range(DEPTH)]
    tab_u_all = peer_u.reshape(DEPTH * N_EXPERTS, D_MODEL)
    hs = [x[b:b + 1] for b in range(bn)]
    pending = None
    for layer in range(DEPTH):
        for b in range(bn):
            started = _peer_start(_mixers(hs[b], weights[layer]), weights[layer], tab_u_all)
            if pending is not None:
                pb, pst, pw = pending
                hs[pb] = _peer_finish(pst, pw).reshape(1, seq, D_MODEL)
            pending = (b, started, weights[layer])
    pb, pst, pw = pending
    hs[pb] = _peer_finish(pst, pw).reshape(1, seq, D_MODEL)
    h = jnp.concatenate(hs, axis=0)
    return _final_norm(h.reshape(-1, D_MODEL), final_g[None, :]).reshape(x.shape)
```

```python
import functools
import math

import jax
import jax.numpy as jnp
from jax import lax
from jax.experimental import pallas as pl
from jax.experimental.pallas import tpu as pltpu
from jax.experimental.pallas import tpu_sc as plsc

F32 = jnp.float32
BF16 = jnp.bfloat16
I32 = jnp.int32
U32 = jnp.uint32

D_MODEL = 1024
DEPTH = 2
GDN_HEADS = 4
GDN_DK = 64
GDN_CHUNK = 64
GDN_W = GDN_HEADS * GDN_DK
GDN_QKV_W = 3 * GDN_W
GDN_CONV = 4
S5_WIDTH = 256
S5_GROUPS = 16
S5_GROUP_CH = 16
S5_STATE = 64
S5_NS = S5_GROUPS * S5_STATE
S5_SUB = 16
DIFF_HEADS = 4
DIFF_DQK = 64
DIFF_DV = 128
DIFF_W = 512
PEER_HEADS = 8
PEER_DHALF = 64
N_KEYS = 128
N_EXPERTS = N_KEYS * N_KEYS
PEER_TOPK = 16
PEER_PAIRS = PEER_HEADS * PEER_TOPK
NORM_EPS = 1e-6
NEG_INF = float("-inf")

VMEM_LIMIT_BYTES = 56 * 1024 * 1024


def _cparams(sem, vmem=VMEM_LIMIT_BYTES):
    return pltpu.CompilerParams(dimension_semantics=sem, vmem_limit_bytes=vmem)


def _dot(a, b):
    return jnp.dot(a, b, preferred_element_type=F32)


def _dot_nt(a, b):
    return lax.dot_general(a, b, (((1,), (1,)), ((), ())), preferred_element_type=F32)


def _dot_tn(a, b):
    return lax.dot_general(a, b, (((0,), (0,)), ((), ())), preferred_element_type=F32)


def _split(x):
    hi = x.astype(BF16)
    lo = (x - hi.astype(F32)).astype(BF16)
    return hi, lo


def _dot_sel_r(x, sel):
    hi, lo = _split(x)
    return _dot(hi, sel) + _dot(lo, sel)


def _dot_sel_l(sel, x):
    hi, lo = _split(x)
    return _dot(sel, hi) + _dot(sel, lo)


def _mm3(a, b):
    ah, al = _split(a)
    bh, bl = _split(b)
    return _dot(ah, bh) + (_dot(ah, bl) + _dot(al, bh))


def _sigmoid(x):
    return 1.0 / (1.0 + jnp.exp(-x))


def _softplus(x):
    return jnp.maximum(x, 0.0) + jnp.log1p(jnp.exp(-jnp.abs(x)))


def _gelu_tanh(x):
    c = math.sqrt(2.0 / math.pi)
    return 0.5 * x * (1.0 + jnp.tanh(c * (x + 0.044715 * (x * x * x))))


INPROJ_GDN_W = GDN_QKV_W + 3 * GDN_W


def _inproj_kernel(x_ref, g_ref, w_ref, gdn_ref, us_ref, q_ref, k_ref, v_ref):
    x = x_ref[...]
    ms = jnp.mean(x * x, axis=-1, keepdims=True)
    xn = (x * lax.rsqrt(ms + NORM_EPS) * g_ref[...]).astype(BF16)
    o = INPROJ_GDN_W
    gdn_ref[...] = _dot(xn, w_ref[:, 0:o])
    us_ref[...] = _dot(xn, w_ref[:, o:o + S5_WIDTH])
    o += S5_WIDTH
    q_ref[...] = (_dot(xn, w_ref[:, o:o + DIFF_W]) * (DIFF_DQK ** -0.5)).astype(BF16)
    k_ref[...] = _dot(xn, w_ref[:, o + DIFF_W:o + 2 * DIFF_W]).astype(BF16)
    v_ref[...] = _dot(xn, w_ref[:, o + 2 * DIFF_W:o + 3 * DIFF_W]).astype(BF16)


def _inproj(x2, gain, w, tm=512):
    t = x2.shape[0]
    nw = w.shape[1]
    row = lambda i: (i, 0)
    fixed = lambda i: (0, 0)
    return pl.pallas_call(
        _inproj_kernel,
        grid=(t // tm,),
        in_specs=[pl.BlockSpec((tm, D_MODEL), row), pl.BlockSpec((1, D_MODEL), fixed),
                  pl.BlockSpec((D_MODEL, nw), fixed)],
        out_specs=[pl.BlockSpec((tm, INPROJ_GDN_W), row), pl.BlockSpec((tm, S5_WIDTH), row),
                   pl.BlockSpec((tm, DIFF_W), row), pl.BlockSpec((tm, DIFF_W), row),
                   pl.BlockSpec((tm, DIFF_W), row)],
        out_shape=[jax.ShapeDtypeStruct((t, INPROJ_GDN_W), F32), jax.ShapeDtypeStruct((t, S5_WIDTH), F32),
                   jax.ShapeDtypeStruct((t, DIFF_W), BF16), jax.ShapeDtypeStruct((t, DIFF_W), BF16),
                   jax.ShapeDtypeStruct((t, DIFF_W), BF16)],
        compiler_params=_cparams(("parallel",)),
        name="inproj",
    )(x2, gain, w)


def _gdn_kernel(blk_ref, convw_ref, alog_ref, dtb_ref, ng_ref, out_ref,
                s_ref, tail_ref, xp_ref, q_s, k_s, v_s, b_s, g_s, *, ct):
    c64 = GDN_CHUNK

    @pl.when(pl.program_id(1) == 0)
    def _():
        s_ref[...] = jnp.zeros_like(s_ref)
        tail_ref[...] = jnp.zeros_like(tail_ref)

    qkv = blk_ref[0, :, 0:GDN_QKV_W]
    xp_ref[0:8, :] = tail_ref[...]
    xp_ref[8:8 + ct, :] = qkv
    tail_ref[...] = qkv[ct - 8:ct, :]
    cw = convw_ref[...]
    y = cw[0:1, :] * xp_ref[5:5 + ct, :]
    for j in range(1, GDN_CONV):
        y = y + cw[j:j + 1, :] * xp_ref[5 + j:5 + j + ct, :]
    y = y * _sigmoid(y)

    ri = lax.broadcasted_iota(I32, (GDN_W, GDN_W), 0)
    ci = lax.broadcasted_iota(I32, (GDN_W, GDN_W), 1)
    head_ones = jnp.where((ri // c64) == (ci // c64), 1.0, 0.0).astype(BF16)

    q = y[:, 0:GDN_W]
    k = y[:, GDN_W:2 * GDN_W]
    q_s[...] = q * lax.rsqrt(_dot_sel_r(q * q, head_ones) + 1e-6) * (GDN_DK ** -0.5)
    k_s[...] = k * lax.rsqrt(_dot_sel_r(k * k, head_ones) + 1e-6)
    v_s[...] = y[:, 2 * GDN_W:3 * GDN_W]
    a_rep = blk_ref[0, :, GDN_QKV_W + GDN_W:GDN_QKV_W + 2 * GDN_W]
    b_rep = blk_ref[0, :, GDN_QKV_W + 2 * GDN_W:GDN_QKV_W + 3 * GDN_W]
    b_s[...] = _sigmoid(b_rep)
    g_raw = -jnp.exp(alog_ref[...]) * _softplus(a_rep + dtb_ref[...])
    rt = lax.broadcasted_iota(I32, (ct, ct), 0)
    ctk = lax.broadcasted_iota(I32, (ct, ct), 1)
    tri_bd = jnp.where(((rt // c64) == (ctk // c64)) & (ctk <= rt), 1.0, 0.0).astype(BF16)
    g_s[...] = _dot_sel_l(tri_bd, g_raw)

    r64 = lax.broadcasted_iota(I32, (c64, c64), 0)
    col64 = lax.broadcasted_iota(I32, (c64, c64), 1)
    incl = col64 <= r64
    strict = col64 < r64
    eye_b = col64 == r64
    eye_f = jnp.where(eye_b, 1.0, 0.0)
    ones64 = jnp.ones((c64, c64), BF16)
    ng = ng_ref[...]

    n_chunks = ct // c64
    heads = range(GDN_HEADS)
    chains = [(slice(c * c64, (c + 1) * c64), slice(h * c64, (h + 1) * c64))
              for c in range(n_chunks) for h in heads]
    each = lambda f, *ls: [f(*xs) for xs in zip(*ls)]
    gi = [g_s[r, l] for r, l in chains]
    gj = each(lambda g: _dot_sel_l(ones64, jnp.where(eye_b, g, 0.0)), gi)
    dec = each(lambda a, b: jnp.exp(jnp.where(incl, a - b, NEG_INF)), gi, gj)
    kh = [k_s[r, l] for r, l in chains]
    bi = [b_s[r, l] for r, l in chains]
    kb = each(lambda x: x.astype(BF16), kh)
    kk = each(_dot_nt, kb, kb)
    lm = each(lambda b, x, d: jnp.where(strict, b * x * d, 0.0), bi, kk, dec)
    tinv = each(lambda x: eye_f - x, lm)
    pw = lm
    for _ in range(5):
        pw = each(_mm3, pw, pw)
        tinv = each(lambda t, m: t + _mm3(t, m), tinv, pw)
    tb = each(lambda x: x.astype(BF16), tinv)
    eg = each(jnp.exp, gi)
    u = each(lambda t, rl, b: _dot(t, (v_s[rl[0], rl[1]] * b).astype(BF16)), tb, chains, bi)
    w = each(lambda t, k, b, e: _dot(t, (k * (b * e)).astype(BF16)).astype(BF16), tb, kh, bi, eg)
    qh = [q_s[r, l] for r, l in chains]
    qk = each(lambda q, k, d: jnp.where(incl, _dot_nt(q.astype(BF16), k) * d, 0.0).astype(BF16), qh, kb, dec)
    qg = each(lambda q, e: (q * e).astype(BF16), qh, eg)
    g_last = each(lambda g: g[c64 - 1:c64, :], gi)
    kg = each(lambda k, gl, g: (k * jnp.exp(gl - g)).astype(BF16), kh, g_last, gi)
    decay_last = each(jnp.exp, g_last)

    states = [s_ref[h] for h in heads]
    for c in range(n_chunks):
        ids = [c * GDN_HEADS + h for h in heads]
        sb = [s.astype(BF16) for s in states]
        vnb = [(u[n] - _dot(w[n], sb[h])).astype(BF16) for h, n in zip(heads, ids)]
        o = [_dot(qg[n], sb[h]) + _dot(qk[n], vnb[h]) for h, n in zip(heads, ids)]
        states = [states[h] * decay_last[n] + _dot_tn(kg[n], vnb[h]) for h, n in zip(heads, ids)]
        ms = [_dot_sel_r(x * x, ones64) * (1.0 / c64) for x in o]
        outs = [x * lax.rsqrt(m + NORM_EPS) * ng for x, m in zip(o, ms)]
        rows = slice(c * c64, (c + 1) * c64)
        z = blk_ref[0, rows, GDN_QKV_W:GDN_QKV_W + GDN_W]
        out_ref[0, rows, :] = (jnp.concatenate(outs, axis=1) * (z * _sigmoid(z))).astype(out_ref.dtype)
    for h in heads:
        s_ref[h] = states[h]


def _gdn(gdn_in, conv_w, a_log_rep, dtb_rep, ng, ct=256):
    bn, seq, _ = gdn_in.shape
    fixed = lambda b, l: (0, 0)
    return pl.pallas_call(
        functools.partial(_gdn_kernel, ct=ct),
        grid=(bn, seq // ct),
        in_specs=[pl.BlockSpec((1, ct, INPROJ_GDN_W), lambda b, l: (b, l, 0)),
                  pl.BlockSpec((GDN_CONV, GDN_QKV_W), fixed), pl.BlockSpec((1, GDN_W), fixed),
                  pl.BlockSpec((1, GDN_W), fixed), pl.BlockSpec((1, GDN_DK), fixed)],
        out_specs=pl.BlockSpec((1, ct, GDN_W), lambda b, l: (b, l, 0)),
        out_shape=jax.ShapeDtypeStruct((bn, seq, GDN_W), BF16),
        scratch_shapes=[pltpu.VMEM((GDN_HEADS, GDN_DK, GDN_DK), F32), pltpu.VMEM((8, GDN_QKV_W), F32),
                        pltpu.VMEM((ct + 8, GDN_QKV_W), F32)] + [pltpu.VMEM((ct, GDN_W), F32)] * 5,
        compiler_params=_cparams(("arbitrary", "arbitrary")),
        name="gdn",
    )(gdn_in, conv_w, a_log_rep, dtb_rep, ng)


def _s5_kernel(u_ref, lre_ref, lim_ref, lstep_ref, bre_ref, bim_ref, cre_ref, cim_ref, d_ref, y_ref,
               bmat, cmat, avec, carry, zr_s, zi_s, xr_s, xi_s, *, rb):
    ns = S5_NS
    cw = S5_WIDTH

    @pl.when(pl.program_id(1) == 0)
    def _():
        step = jnp.exp(lstep_ref[...])
        lr = lre_ref[...]
        li = lim_ref[...]
        mag = jnp.exp(lr * step)
        ar = mag * jnp.cos(li * step)
        ai = mag * jnp.sin(li * step)
        den = lr * lr + li * li
        mr = ((ar - 1.0) * lr + ai * li) / den
        mi = (ai * lr - (ar - 1.0) * li) / den
        bre = bre_ref[...]
        bim = bim_ref[...]
        bmat[:, 0:ns] = (mr * bre - mi * bim).astype(BF16)
        bmat[:, ns:2 * ns] = (mr * bim + mi * bre).astype(BF16)
        cmat[0:ns, :] = cre_ref[...].astype(BF16)
        cmat[ns:2 * ns, :] = (-cim_ref[...]).astype(BF16)
        avec[0:1, :] = ar
        avec[1:2, :] = ai
        pr, pi = ar, ai
        for _ in range(4):
            pr, pi = pr * pr - pi * pi, 2.0 * pr * pi
        avec[2:3, :] = pr
        avec[3:4, :] = pi
        carry[...] = jnp.zeros_like(carry)

    ar = avec[0:1, :]
    ai = avec[1:2, :]
    a16r = avec[2:3, :]
    a16i = avec[3:4, :]

    def inject(s):
        ub = u_ref[0, :, s * cw:(s + 1) * cw].astype(BF16)
        return _dot(ub, bmat[...])

    xr = jnp.zeros((rb, ns), F32)
    xi = jnp.zeros((rb, ns), F32)
    for s in range(S5_SUB):
        p = inject(s)
        xr, xi = ar * xr - ai * xi + p[:, 0:ns], ar * xi + ai * xr + p[:, ns:2 * ns]
    zr_s[...] = xr
    zi_s[...] = xi

    def row_step(kk, c):
        cr, ci_ = c
        xr_s[pl.ds(kk, 1), :] = cr
        xi_s[pl.ds(kk, 1), :] = ci_
        zr = zr_s[pl.ds(kk, 1), :]
        zi = zi_s[pl.ds(kk, 1), :]
        return (a16r * cr - a16i * ci_ + zr, a16r * ci_ + a16i * cr + zi)

    cr, ci_ = lax.fori_loop(0, rb, row_step, (carry[0:1, :], carry[1:2, :]))
    carry[0:1, :] = cr
    carry[1:2, :] = ci_

    xr = xr_s[...]
    xi = xi_s[...]
    dsk = d_ref[...]
    for s in range(S5_SUB):
        p = inject(s)
        xr, xi = ar * xr - ai * xi + p[:, 0:ns], ar * xi + ai * xr + p[:, ns:2 * ns]
        yv = _dot(xr.astype(BF16), cmat[0:ns, :]) + _dot(xi.astype(BF16), cmat[ns:2 * ns, :])
        yv = yv + dsk * u_ref[0, :, s * cw:(s + 1) * cw]
        y_ref[0, :, s * cw:(s + 1) * cw] = _gelu_tanh(yv).astype(y_ref.dtype)


def _s5(u_rows, lre, lim, lstep, bre_bd, bim_bd, cre_bd, cim_bd, dskip, rb=128):
    bn, nrows, rw = u_rows.shape
    fixed = lambda b, r: (0, 0)
    ns = S5_NS
    return pl.pallas_call(
        functools.partial(_s5_kernel, rb=rb),
        grid=(bn, nrows // rb),
        in_specs=[pl.BlockSpec((1, rb, rw), lambda b, r: (b, r, 0)),
                  pl.BlockSpec((1, ns), fixed), pl.BlockSpec((1, ns), fixed), pl.BlockSpec((1, ns), fixed),
                  pl.BlockSpec((S5_WIDTH, ns), fixed), pl.BlockSpec((S5_WIDTH, ns), fixed),
                  pl.BlockSpec((ns, S5_WIDTH), fixed), pl.BlockSpec((ns, S5_WIDTH), fixed),
                  pl.BlockSpec((1, S5_WIDTH), fixed)],
        out_specs=pl.BlockSpec((1, rb, rw), lambda b, r: (b, r, 0)),
        out_shape=jax.ShapeDtypeStruct((bn, nrows, rw), BF16),
        scratch_shapes=[pltpu.VMEM((S5_WIDTH, 2 * ns), BF16), pltpu.VMEM((2 * ns, S5_WIDTH), BF16),
                        pltpu.VMEM((8, ns), F32), pltpu.VMEM((8, ns), F32)]
                       + [pltpu.VMEM((rb, ns), F32)] * 4,
        compiler_params=_cparams(("arbitrary", "arbitrary")),
        name="s5",
    )(u_rows, lre, lim, lstep, bre_bd, bim_bd, cre_bd, cim_bd, dskip)


def _attn_kernel(q_ref, k_ref, v_ref, lq1_ref, lk1_ref, lq2_ref, lk2_ref, ng_ref, o_ref,
                 qs_s, m_s, l_s, acc_s, *, blk, lam_init):
    i = pl.program_id(2)
    q = q_ref[0]
    lane = lax.broadcasted_iota(I32, q.shape, 1)
    zero = jnp.zeros_like(q)
    qs_s[0:blk, :] = jnp.where(lane < DIFF_DQK, q, zero)
    qs_s[blk:2 * blk, :] = jnp.where(lane >= DIFF_DQK, q, zero)
    m_s[...] = jnp.full_like(m_s, NEG_INF)
    l_s[...] = jnp.zeros_like(l_s)
    acc_s[...] = jnp.zeros_like(acc_s)
    ones = jnp.ones((blk, DIFF_DV), BF16)

    def block_rows(j):
        return pl.ds(pl.multiple_of(j * blk, blk), blk)

    def scores(j):
        return _dot_nt(qs_s[...], k_ref[0, block_rows(j), :])

    def update(j, s):
        m_old = m_s[...]
        m_new = jnp.maximum(m_old, jnp.max(s, axis=-1, keepdims=True))
        p = jnp.exp(s - jnp.concatenate([m_new] * (blk // DIFF_DV), axis=1)).astype(BF16)
        alpha = jnp.exp(m_old - m_new)
        pv = _dot(p, jnp.concatenate([v_ref[0, block_rows(j), :], ones], axis=1))
        acc_s[...] = alpha * acc_s[...] + pv[:, 0:DIFF_DV]
        l_s[...] = alpha * l_s[...] + pv[:, DIFF_DV:2 * DIFF_DV]
        m_s[...] = m_new

    def body(j, s):
        s_next = scores(j + 1)
        update(j, s)
        return s_next

    s = lax.fori_loop(0, i, body, scores(0))
    row = lax.broadcasted_iota(I32, s.shape, 0) & (blk - 1)
    col = lax.broadcasted_iota(I32, s.shape, 1)
    update(i, jnp.where(col <= row, s, NEG_INF))
    lam = (jnp.exp(jnp.sum(lq1_ref[...] * lk1_ref[...], axis=-1, keepdims=True))
           - jnp.exp(jnp.sum(lq2_ref[...] * lk2_ref[...], axis=-1, keepdims=True)) + lam_init)
    o = acc_s[0:blk, :] / l_s[0:blk, :] - lam * (acc_s[blk:2 * blk, :] / l_s[blk:2 * blk, :])
    ms = jnp.mean(o * o, axis=-1, keepdims=True)
    o = o * lax.rsqrt(ms + NORM_EPS) * ng_ref[...] * (1.0 - lam_init)
    o_ref[0] = o.astype(o_ref.dtype)


def _attn(q, k, v, lq1, lk1, lq2, lk2, ng, lam_init, blk=512):
    bn, seq, _ = q.shape
    blk = min(blk, seq)
    fixed = lambda b, h, i: (0, 0)
    kv_spec = pl.BlockSpec((1, seq, DIFF_DV), lambda b, h, i: (b, 0, h))
    return pl.pallas_call(
        functools.partial(_attn_kernel, blk=blk, lam_init=lam_init),
        grid=(bn, DIFF_HEADS, seq // blk),
        in_specs=[pl.BlockSpec((1, blk, DIFF_DV), lambda b, h, i: (b, i, h)), kv_spec, kv_spec]
                 + [pl.BlockSpec((1, DIFF_DQK), fixed)] * 4 + [pl.BlockSpec((1, DIFF_DV), fixed)],
        out_specs=pl.BlockSpec((1, blk, DIFF_DV), lambda b, h, i: (b, i, h)),
        out_shape=jax.ShapeDtypeStruct((bn, seq, DIFF_W), BF16),
        scratch_shapes=[pltpu.VMEM((2 * blk, DIFF_DV), BF16), pltpu.VMEM((2 * blk, DIFF_DV), F32),
                        pltpu.VMEM((2 * blk, DIFF_DV), F32), pltpu.VMEM((2 * blk, DIFF_DV), F32)],
        compiler_params=_cparams(("parallel", "parallel", "arbitrary")),
        name="diff_attn",
    )(q, k, v, lq1, lk1, lq2, lk2, ng)


def _outproj_kernel(h_ref, oa_ref, ys_ref, oc_ref, gw_ref, gb_ref, wo_ref, out_ref):
    zg = _dot(ys_ref[...], gw_ref[...]) + gb_ref[...]
    ob = (zg[:, 0:S5_WIDTH] * _sigmoid(zg[:, S5_WIDTH:2 * S5_WIDTH])).astype(BF16)
    acc = _dot(oa_ref[...], wo_ref[0:GDN_W, :])
    acc = acc + _dot(ob, wo_ref[GDN_W:GDN_W + S5_WIDTH, :])
    acc = acc + _dot(oc_ref[...], wo_ref[GDN_W + S5_WIDTH:, :])
    out_ref[...] = h_ref[...] + acc


def _outproj(h2, oa, ys, oc, glu_w, glu_b, w_out, tm=512):
    t = h2.shape[0]
    row = lambda i: (i, 0)
    fixed = lambda i: (0, 0)
    return pl.pallas_call(
        _outproj_kernel,
        grid=(t // tm,),
        in_specs=[pl.BlockSpec((tm, D_MODEL), row), pl.BlockSpec((tm, GDN_W), row),
                  pl.BlockSpec((tm, S5_WIDTH), row), pl.BlockSpec((tm, DIFF_W), row),
                  pl.BlockSpec((S5_WIDTH, 2 * S5_WIDTH), fixed), pl.BlockSpec((1, 2 * S5_WIDTH), fixed),
                  pl.BlockSpec((D_MODEL, D_MODEL), fixed)],
        out_specs=pl.BlockSpec((tm, D_MODEL), row),
        out_shape=jax.ShapeDtypeStruct((t, D_MODEL), F32),
        compiler_params=_cparams(("parallel",)),
        name="outproj",
    )(h2, oa, ys, oc, glu_w, glu_b, w_out)


_BIG_ID = 1.0e9


def _top16(x, ids, payload):
    n = x.shape[1]
    r16 = lax.broadcasted_iota(I32, (PEER_TOPK, n), 0)
    vals = jnp.zeros((PEER_TOPK, n), F32)
    pays = jnp.zeros((PEER_TOPK, n), F32)
    for kk in range(PEER_TOPK):
        m = jnp.max(x, axis=0, keepdims=True)
        first = jnp.min(jnp.where(x == m, ids, _BIG_ID), axis=0, keepdims=True)
        hit = ids == first
        pay = first if payload is None else jnp.max(jnp.where(hit, payload, -1.0), axis=0, keepdims=True)
        x = jnp.where(hit, NEG_INF, x)
        vals = jnp.where(r16 == kk, m, vals)
        pays = jnp.where(r16 == kk, pay, pays)
    return vals, pays


def _peer_route_kernel(h_ref, g_ref, wqt_ref, keys_ref, xn_ref, exp_ref, row_ref, par_ref, gate_ref,
                       qt_s, sv_s, si_s, gate_s, exp_s, *, tm):
    x = h_ref[...]
    ms = jnp.mean(x * x, axis=-1, keepdims=True)
    xn = x * lax.rsqrt(ms + NORM_EPS) * g_ref[...]
    xn_ref[...] = xn
    qt_s[...] = _dot_nt(wqt_ref[...], xn.astype(BF16)).astype(BF16)

    key_id = lax.broadcasted_iota(I32, (N_KEYS, tm), 0).astype(F32)

    def half_body(hp, carry):
        r0 = pl.multiple_of(hp * PEER_DHALF, PEER_DHALF)
        s = _dot(keys_ref[hp], qt_s[pl.ds(r0, PEER_DHALF), :])
        vals, ids = _top16(s, key_id, None)
        sv_s[hp] = vals
        si_s[hp] = ids
        return carry

    lax.fori_loop(0, 2 * PEER_HEADS, half_body, 0)

    i8 = lax.broadcasted_iota(I32, (8, tm), 0).astype(F32)

    def head_body(hd, carry):
        a0 = sv_s[2 * hd]
        a1 = sv_s[2 * hd + 1]
        e0 = si_s[2 * hd] * float(N_KEYS)
        e1 = si_s[2 * hd + 1]
        cs, es, fs = [], [], []
        for i in range(8):
            cs.append(a0[i:i + 1, :] + a1[0:8, :])
            es.append(e0[i:i + 1, :] + e1[0:8, :])
            fs.append(i8 + float(i * PEER_TOPK))
        cs.append(a0[0:1, :] + a1[8:16, :])
        es.append(e0[0:1, :] + e1[8:16, :])
        fs.append(i8 + 8.0)
        cs.append(a0[8:16, :] + a1[0:1, :])
        es.append(e0[8:16, :] + e1[0:1, :])
        fs.append((i8 + 8.0) * float(PEER_TOPK))
        top_s, experts = _top16(jnp.concatenate(cs, axis=0), jnp.concatenate(fs, axis=0),
                                jnp.concatenate(es, axis=0))
        ex = jnp.exp(top_s - jnp.max(top_s, axis=0, keepdims=True))
        rows = pl.ds(pl.multiple_of(hd * PEER_TOPK, PEER_TOPK), PEER_TOPK)
        gate_s[rows, :] = ex / jnp.sum(ex, axis=0, keepdims=True)
        exp_s[rows, :] = experts
        return carry

    lax.fori_loop(0, PEER_HEADS, head_body, 0)
    gate_ref[...] = gate_s[...].T
    expert = exp_s[...].T.astype(I32)
    exp_ref[...] = expert
    row_ref[...] = (expert & (TAB_ROWS - 1)) * 8
    par_ref[...] = lax.shift_right_logical(expert, TAB_ROWS.bit_length() - 1).astype(F32)


def _peer_route(h2, gain, wqt, keys, tm=512):
    t = h2.shape[0]
    tm = min(tm, t)
    row = lambda i: (i, 0)
    return pl.pallas_call(
        functools.partial(_peer_route_kernel, tm=tm),
        grid=(t // tm,),
        in_specs=[pl.BlockSpec((tm, D_MODEL), row), pl.BlockSpec((1, D_MODEL), lambda i: (0, 0)),
                  pl.BlockSpec((D_MODEL, D_MODEL), lambda i: (0, 0)),
                  pl.BlockSpec((2 * PEER_HEADS, N_KEYS, PEER_DHALF), lambda i: (0, 0, 0))],
        out_specs=[pl.BlockSpec((tm, D_MODEL), row)] + [pl.BlockSpec((tm, PEER_PAIRS), row)] * 4,
        out_shape=[jax.ShapeDtypeStruct((t, D_MODEL), F32), jax.ShapeDtypeStruct((t, PEER_PAIRS), I32),
                   jax.ShapeDtypeStruct((t, PEER_PAIRS), I32), jax.ShapeDtypeStruct((t, PEER_PAIRS), F32),
                   jax.ShapeDtypeStruct((t, PEER_PAIRS), F32)],
        scratch_shapes=[pltpu.VMEM((D_MODEL, tm), BF16), pltpu.VMEM((2 * PEER_HEADS, PEER_TOPK, tm), F32),
                        pltpu.VMEM((2 * PEER_HEADS, PEER_TOPK, tm), F32),
                        pltpu.VMEM((PEER_PAIRS, tm), F32), pltpu.VMEM((PEER_PAIRS, tm), F32)],
        compiler_params=_cparams(("parallel",)),
        name="peer_route",
    )(h2, gain, wqt, keys)


TAB_ROWS = N_EXPERTS // 2


def _token_tile(block, tl):
    return jnp.concatenate([block[tl:tl + 1, s * 128:(s + 1) * 128] for s in range(8)], axis=0)


def _tiles_to_rows(tiles):
    return jnp.concatenate([jnp.concatenate([tile[s:s + 1, :] for tile in tiles], axis=0) for s in range(8)], axis=1)


def _load_words(tab_ref, row_ref, t, p):
    return tab_ref[pl.ds(pl.multiple_of(row_ref[t, p], 8), 8), :]


def _tok_spec(tb, space=None):
    return pl.BlockSpec((tb, PEER_PAIRS), lambda i: (i, 0), memory_space=space)


def _table_spec():
    return pl.BlockSpec((TAB_ROWS * 8, 128), lambda i: (0, 0), pipeline_mode=pl.Buffered(1))


SC_CORES = 2
SC_SUBCORES = 16
SC_LANES = 16
SC_WORKERS = SC_CORES * SC_SUBCORES


def _peer_u_sc(tab, expert, xn, t0, t_sc):
    tw = t_sc // SC_WORKERS
    heads = PEER_PAIRS // PEER_TOPK
    chunks = D_MODEL // SC_LANES
    mesh = plsc.VectorSubcoreMesh(core_axis_name="c", subcore_axis_name="s")

    def body(tab_hbm, idx_hbm, x_hbm, d_hbm, idx_a, idx_b, x_a, x_b, rows_a, rows_b, d_v,
             sem_ra, sem_rb, sem_ia, sem_ib, sem_xa, sem_xb):
        wid = lax.axis_index("s") * SC_CORES + lax.axis_index("c")
        lane = lax.iota(I32, SC_LANES)
        first = t0 + wid * tw
        last = first + tw - 1
        tok_bufs = ((idx_a, x_a, sem_ia, sem_xa), (idx_b, x_b, sem_ib, sem_xb))
        row_bufs = ((rows_a, sem_ra), (rows_b, sem_rb))

        def fetch(t, k):
            idx_v, x_v, sem_i, sem_x = tok_bufs[k]
            return (pltpu.make_async_copy(idx_hbm.at[t], idx_v, sem_i),
                    pltpu.make_async_copy(x_hbm.at[t], x_v, sem_x))

        def gather(k, h):
            buf, sem = row_bufs[h % 2]
            return pltpu.make_async_copy(tab_hbm.at[tok_bufs[k][0].at[pl.ds(h * PEER_TOPK, PEER_TOPK)]], buf, sem)

        for c in fetch(first, 0):
            c.start()
        for c in fetch(first, 0):
            c.wait()
        gather(0, 0).start()

        def two_tokens(i2, carry):
            for k in range(2):
                t = first + 2 * i2 + k
                x_v = tok_bufs[k][1]
                nxt = fetch(jnp.minimum(t + 1, last), 1 - k)
                for c in nxt:
                    c.start()
                for h in range(heads):
                    if h + 1 < heads:
                        gather(k, h + 1).start()
                    else:
                        for c in nxt:
                            c.wait()
                        gather(1 - k, 0).start()
                    gather(k, h).wait()
                    rows_v = row_bufs[h % 2][0]

                    def chunk(j, accs):
                        xj = x_v[pl.ds(j * SC_LANES, SC_LANES)]
                        return tuple(a + rows_v[r, pl.ds(j * SC_LANES, SC_LANES)] * xj for r, a in enumerate(accs))

                    accs = lax.fori_loop(0, chunks, chunk,
                                         tuple(jnp.zeros((SC_LANES,), F32) for _ in range(PEER_TOPK)))
                    out = jnp.zeros((SC_LANES,), F32)
                    for r in range(PEER_TOPK):
                        out = jnp.where(lane == r, jnp.sum(accs[r]), out)
                    d_v[pl.ds(h * PEER_TOPK, PEER_TOPK)] = out
                pltpu.sync_copy(d_v, d_hbm.at[t - t0])
            return carry

        lax.fori_loop(0, tw // 2, two_tokens, 0)
        gather(0, 0).wait()

    return pl.kernel(
        body, mesh=mesh,
        out_type=jax.ShapeDtypeStruct((t_sc, PEER_PAIRS), F32),
        compiler_params=pltpu.CompilerParams(needs_layout_passes=False),
        scratch_types=[pltpu.VMEM((PEER_PAIRS,), I32), pltpu.VMEM((PEER_PAIRS,), I32),
                       pltpu.VMEM((D_MODEL,), F32), pltpu.VMEM((D_MODEL,), F32),
                       pltpu.VMEM((PEER_TOPK, D_MODEL), F32), pltpu.VMEM((PEER_TOPK, D_MODEL), F32),
                       pltpu.VMEM((PEER_PAIRS,), F32)] + [pltpu.SemaphoreType.DMA] * 6,
        name="peer_u_sc",
    )(tab, expert, xn)


def _peer_coef_kernel(d_ref, gate_ref, c_ref):
    c_ref[...] = gate_ref[...] * _gelu_tanh(d_ref[...])


def _peer_coef(d_tp, gate_tp, tm=2048):
    t = d_tp.shape[0]
    tm = min(tm, t)
    return pl.pallas_call(
        _peer_coef_kernel,
        grid=(t // tm,),
        in_specs=[_tok_spec(tm), _tok_spec(tm)],
        out_specs=_tok_spec(tm),
        out_shape=jax.ShapeDtypeStruct((t, PEER_PAIRS), F32),
        compiler_params=_cparams(("parallel",)),
        name="peer_coef",
    )(d_tp, gate_tp)


PEER_V_TOKENS_PER_STEP = 8
PEER_V_KDIM = (PEER_PAIRS // 2) * 16


def _peer_v_kernel(row_ref, par_ref, coef_ref, h_ref, tab_ref, out_ref, cz_s, *, tb):
    half = PEER_PAIRS // 2
    kdim = PEER_V_KDIM
    nt = PEER_V_TOKENS_PER_STEP
    pk = lax.broadcasted_iota(I32, (PEER_PAIRS, kdim), 0)
    qk = lax.shift_right_logical(lax.broadcasted_iota(I32, (PEER_PAIRS, kdim), 1), 4)
    hk = (lax.broadcasted_iota(I32, (tb, kdim), 1) & 1).astype(F32)
    coef = coef_ref[...].astype(BF16)
    par = par_ref[...].astype(BF16)
    for hf in range(2):
        expand = jnp.where(pk == qk + hf * half, 1.0, 0.0).astype(BF16)
        cz_s[hf] = jnp.where(hk == _dot(par, expand), _dot(coef, expand), 0.0)
    ks = lax.broadcasted_iota(I32, (8, kdim), 0)
    k8 = lax.broadcasted_iota(I32, (8, kdim), 1)
    own_sublane = lax.shift_right_logical(k8 & 15, 1) == ks

    def tok_body(tt, carry):
        rows8 = pl.ds(pl.multiple_of(tt * nt, nt), nt)
        hb = h_ref[rows8, :]
        tiles = []
        for j in range(nt):
            t = tt * nt + j
            pieces = []
            for q in range(half):
                wa = pltpu.bitcast(_load_words(tab_ref, row_ref, t, q), BF16)
                wb = pltpu.bitcast(_load_words(tab_ref, row_ref, t, half + q), BF16)
                pieces.append(jnp.concatenate([wa, wb], axis=1))
            g = jnp.concatenate(pieces, axis=0)
            ck = jnp.concatenate(
                [jnp.where(own_sublane, jnp.broadcast_to(cz_s[hf, pl.ds(t, 1), :], (8, kdim)), 0.0)
                 for hf in range(2)], axis=0).astype(BF16)
            o = _dot(ck, g)
            tiles.append(_token_tile(hb, j) + (o[0:8, 0:128] + o[8:16, 128:256]))
        out_ref[rows8, :] = _tiles_to_rows(tiles)
        return carry

    lax.fori_loop(0, tb // nt, tok_body, 0)


def _peer_v(rows, par, coef, h2, tab, t, tb=128):
    return pl.pallas_call(
        functools.partial(_peer_v_kernel, tb=tb),
        grid=(t // tb,),
        in_specs=[_tok_spec(tb, pltpu.SMEM), _tok_spec(tb), _tok_spec(tb),
                  pl.BlockSpec((tb, D_MODEL), lambda i: (i, 0)), _table_spec()],
        out_specs=pl.BlockSpec((tb, D_MODEL), lambda i: (i, 0)),
        out_shape=jax.ShapeDtypeStruct((t, D_MODEL), F32),
        scratch_shapes=[pltpu.VMEM((2, tb, PEER_V_KDIM), F32)],
        compiler_params=_cparams(("arbitrary",)),
        name="peer_v",
    )(rows, par, coef, h2, tab)


def _final_norm_kernel(x_ref, g_ref, o_ref):
    x = x_ref[...]
    ms = jnp.mean(x * x, axis=-1, keepdims=True)
    o_ref[...] = x * lax.rsqrt(ms + NORM_EPS) * g_ref[...]


def _final_norm(x2, gain, tm=1024):
    t = x2.shape[0]
    tm = min(tm, t)
    row = lambda i: (i, 0)
    return pl.pallas_call(
        _final_norm_kernel,
        grid=(t // tm,),
        in_specs=[pl.BlockSpec((tm, D_MODEL), row), pl.BlockSpec((1, D_MODEL), lambda i: (0, 0))],
        out_specs=pl.BlockSpec((tm, D_MODEL), row),
        out_shape=jax.ShapeDtypeStruct((t, D_MODEL), F32),
        compiler_params=_cparams(("parallel",)),
        name="final_norm",
    )(x2, gain)


def _pack_table(tab):
    bits = lax.bitcast_convert_type(tab.astype(BF16), jnp.uint16).astype(U32)
    return (bits[:TAB_ROWS] | (bits[TAB_ROWS:] << 16)).reshape(TAB_ROWS * 8, 128)


def _peer_start(h2, w, tab_u_all):
    t = h2.shape[0]
    xn, expert, rows, par, gate = _peer_route(h2, w["norm2"], w["wqt"], w["keys"])
    d = _peer_u_sc(tab_u_all, expert + w["expert_base"], xn, 0, t)
    return dict(h2=h2, rows=rows, par=par, gate=gate, d=d)


def _peer_finish(st, w):
    t = st["h2"].shape[0]
    coef = _peer_coef(st["d"], st["gate"])
    return _peer_v(st["rows"], st["par"], coef, st["h2"], w["tab_v"], t)


def _rep(x, n):
    return jnp.repeat(x, n, axis=-1)


def _inproj_weight(w_in_l):
    o = 0
    qkv = w_in_l[:, o:o + GDN_QKV_W]; o += GDN_QKV_W
    z = w_in_l[:, o:o + GDN_W]; o += GDN_W
    a = w_in_l[:, o:o + GDN_HEADS]; o += GDN_HEADS
    b = w_in_l[:, o:o + GDN_HEADS]; o += GDN_HEADS
    rest = w_in_l[:, o:]
    return jnp.concatenate([qkv, z, _rep(a, GDN_DK), _rep(b, GDN_DK), rest], axis=1).astype(BF16)


def _block_diag_gc(b_gnc):
    g = b_gnc.shape[0]
    eye = jnp.eye(g, dtype=b_gnc.dtype)
    t = jnp.swapaxes(b_gnc, 1, 2)
    return (t[:, :, None, :] * eye[:, None, :, None]).reshape(g * t.shape[1], g * t.shape[2])


def _layer_weights(p, layer):
    row = lambda x: x[None, :]
    flat = lambda x: x.reshape(1, S5_NS)
    return dict(
        lam_init=0.8 - 0.6 * math.exp(-0.3 * layer),
        norm1=row(p["norm1_g"][layer]), w_in=_inproj_weight(p["w_in"][layer]),
        conv_w=p["gdn_conv_w"][layer], a_log=row(_rep(p["gdn_a_log"][layer].astype(F32), GDN_DK)),
        dtb=row(_rep(p["gdn_dt_bias"][layer].astype(F32), GDN_DK)), gdn_ng=row(p["gdn_norm_g"][layer]),
        s5=(flat(p["s5_lambda_re"][layer]), flat(p["s5_lambda_im"][layer]),
            flat(_rep(p["s5_log_step"][layer][:, None], S5_STATE)),
            _block_diag_gc(p["s5_b_re"][layer]), _block_diag_gc(p["s5_b_im"][layer]),
            _block_diag_gc(p["s5_c_re"][layer]), _block_diag_gc(p["s5_c_im"][layer]),
            row(p["s5_d"][layer])),
        attn=tuple(row(p[k][layer]) for k in ("diff_lq1", "diff_lk1", "diff_lq2", "diff_lk2", "diff_norm_g")),
        glu_w=p["s5_glu_w"][layer].astype(BF16), glu_b=row(p["s5_glu_b"][layer]),
        w_out=p["w_out"][layer].astype(BF16),
        norm2=row(p["norm2_g"][layer]), wqt=p["peer_wq"][layer].T.astype(BF16),
        keys=p["peer_keys"][layer].reshape(2 * PEER_HEADS, N_KEYS, PEER_DHALF).astype(BF16),
        tab_v=_pack_table(p["peer_v"][layer]), expert_base=layer * N_EXPERTS)


def _mixers(h, w):
    bn, seq, _ = h.shape
    t = bn * seq
    gdn_in, us, qc, kc, vc = _inproj(h.reshape(t, D_MODEL), w["norm1"], w["w_in"])
    o_a = _gdn(gdn_in.reshape(bn, seq, INPROJ_GDN_W), w["conv_w"], w["a_log"], w["dtb"], w["gdn_ng"])
    ys = _s5(us.reshape(bn, seq // S5_SUB, S5_SUB * S5_WIDTH), *w["s5"])
    o_c = _attn(qc.reshape(bn, seq, DIFF_W), kc.reshape(bn, seq, DIFF_W), vc.reshape(bn, seq, DIFF_W),
                *w["attn"], w["lam_init"])
    return _outproj(h.reshape(t, D_MODEL), o_a.reshape(t, GDN_W), ys.reshape(t, S5_WIDTH),
                    o_c.reshape(t, DIFF_W), w["glu_w"], w["glu_b"], w["w_out"])


def kernel(x, norm1_g, w_in, gdn_conv_w, gdn_a_log, gdn_dt_bias, gdn_norm_g, s5_lambda_re, s5_lambda_im, s5_b_re, s5_b_im, s5_c_re, s5_c_im, s5_d, s5_log_step, s5_glu_w, s5_glu_b, diff_lq1, diff_lk1, diff_lq2, diff_lk2, diff_norm_g, w_out, norm2_g, peer_wq, peer_keys, peer_u, peer_v, final_g):
    p = dict(norm1_g=norm1_g, w_in=w_in, gdn_conv_w=gdn_conv_w, gdn_a_log=gdn_a_log, gdn_dt_bias=gdn_dt_bias,
             gdn_norm_g=gdn_norm_g, s5_lambda_re=s5_lambda_re, s5_lambda_im=s5_lambda_im, s5_b_re=s5_b_re,
             s5_b_im=s5_b_im, s5_c_re=s5_c_re, s5_c_im=s5_c_im, s5_d=s5_d, s5_log_step=s5_log_step,
             s5_glu_w=s5_glu_w, s5_glu_b=s5_glu_b, diff_lq1=diff_lq1, diff_lk1=diff_lk1, diff_lq2=diff_lq2,
             diff_lk2=diff_lk2, diff_norm_g=diff_norm_g, w_out=w_out, norm2_g=norm2_g, peer_wq=peer_wq,
             peer_keys=peer_keys, peer_u=peer_u, peer_v=peer_v, final_g=final_g)
    bn, seq, _ = x.shape
    weights = [_layer_weights(p, layer) for layer in range(DEPTH)]
    tab_u_all = peer_u.reshape(DEPTH * N_EXPERTS, D_MODEL)
    hs = [x[b:b + 1] for b in range(bn)]
    pending = None
    for layer in range(DEPTH):
        for b in range(bn):
            started = _peer_start(_mixers(hs[b], weights[layer]), weights[layer], tab_u_all)
            if pending is not None:
                pb, pst, pw = pending
                hs[pb] = _peer_finish(pst, pw).reshape(1, seq, D_MODEL)
            pending = (b, started, weights[layer])
    pb, pst, pw = pending
    hs[pb] = _peer_finish(pst, pw).reshape(1, seq, D_MODEL)
    h = jnp.concatenate(hs, axis=0)
    return _final_norm(h.reshape(-1, D_MODEL), final_g[None, :]).reshape(x.shape)
```

```python
import functools
import math

import jax
import jax.numpy as jnp
from jax import lax
from jax.experimental import pallas as pl
from jax.experimental.pallas import tpu as pltpu
from jax.experimental.pallas import tpu_sc as plsc

F32 = jnp.float32
BF16 = jnp.bfloat16
I32 = jnp.int32
U32 = jnp.uint32

D_MODEL = 1024
DEPTH = 2
GDN_HEADS = 4
GDN_DK = 64
GDN_CHUNK = 64
GDN_W = GDN_HEADS * GDN_DK
GDN_QKV_W = 3 * GDN_W
GDN_CONV = 4
S5_WIDTH = 256
S5_GROUPS = 16
S5_GROUP_CH = 16
S5_STATE = 64
S5_NS = S5_GROUPS * S5_STATE
S5_SUB = 16
DIFF_HEADS = 4
DIFF_DQK = 64
DIFF_DV = 128
DIFF_W = 512
PEER_HEADS = 8
PEER_DHALF = 64
N_KEYS = 128
N_EXPERTS = N_KEYS * N_KEYS
PEER_TOPK = 16
PEER_PAIRS = PEER_HEADS * PEER_TOPK
NORM_EPS = 1e-6
NEG_INF = float("-inf")

VMEM_LIMIT_BYTES = 56 * 1024 * 1024


def _cparams(sem, vmem=VMEM_LIMIT_BYTES):
    return pltpu.CompilerParams(dimension_semantics=sem, vmem_limit_bytes=vmem)


def _dot(a, b):
    return jnp.dot(a, b, preferred_element_type=F32)


def _dot_nt(a, b):
    return lax.dot_general(a, b, (((1,), (1,)), ((), ())), preferred_element_type=F32)


def _dot_tn(a, b):
    return lax.dot_general(a, b, (((0,), (0,)), ((), ())), preferred_element_type=F32)


def _split(x):
    hi = x.astype(BF16)
    lo = (x - hi.astype(F32)).astype(BF16)
    return hi, lo


def _dot_sel_r(x, sel):
    hi, lo = _split(x)
    return _dot(hi, sel) + _dot(lo, sel)


def _dot_sel_l(sel, x):
    hi, lo = _split(x)
    return _dot(sel, hi) + _dot(sel, lo)


def _mm3(a, b):
    ah, al = _split(a)
    bh, bl = _split(b)
    return _dot(ah, bh) + (_dot(ah, bl) + _dot(al, bh))


def _sigmoid(x):
    return 1.0 / (1.0 + jnp.exp(-x))


def _softplus(x):
    return jnp.maximum(x, 0.0) + jnp.log1p(jnp.exp(-jnp.abs(x)))


def _gelu_tanh(x):
    c = math.sqrt(2.0 / math.pi)
    return 0.5 * x * (1.0 + jnp.tanh(c * (x + 0.044715 * (x * x * x))))


INPROJ_GDN_W = GDN_QKV_W + 3 * GDN_W


def _inproj_kernel(x_ref, g_ref, w_ref, gdn_ref, us_ref, q_ref, k_ref, v_ref):
    x = x_ref[...]
    ms = jnp.mean(x * x, axis=-1, keepdims=True)
    xn = (x * lax.rsqrt(ms + NORM_EPS) * g_ref[...]).astype(BF16)
    o = INPROJ_GDN_W
    gdn_ref[...] = _dot(xn, w_ref[:, 0:o])
    us_ref[...] = _dot(xn, w_ref[:, o:o + S5_WIDTH])
    o += S5_WIDTH
    q_ref[...] = (_dot(xn, w_ref[:, o:o + DIFF_W]) * (DIFF_DQK ** -0.5)).astype(BF16)
    k_ref[...] = _dot(xn, w_ref[:, o + DIFF_W:o + 2 * DIFF_W]).astype(BF16)
    v_ref[...] = _dot(xn, w_ref[:, o + 2 * DIFF_W:o + 3 * DIFF_W]).astype(BF16)


def _inproj(x2, gain, w, tm=512):
    t = x2.shape[0]
    nw = w.shape[1]
    row = lambda i: (i, 0)
    fixed = lambda i: (0, 0)
    return pl.pallas_call(
        _inproj_kernel,
        grid=(t // tm,),
        in_specs=[pl.BlockSpec((tm, D_MODEL), row), pl.BlockSpec((1, D_MODEL), fixed),
                  pl.BlockSpec((D_MODEL, nw), fixed)],
        out_specs=[pl.BlockSpec((tm, INPROJ_GDN_W), row), pl.BlockSpec((tm, S5_WIDTH), row),
                   pl.BlockSpec((tm, DIFF_W), row), pl.BlockSpec((tm, DIFF_W), row),
                   pl.BlockSpec((tm, DIFF_W), row)],
        out_shape=[jax.ShapeDtypeStruct((t, INPROJ_GDN_W), F32), jax.ShapeDtypeStruct((t, S5_WIDTH), F32),
                   jax.ShapeDtypeStruct((t, DIFF_W), BF16), jax.ShapeDtypeStruct((t, DIFF_W), BF16),
                   jax.ShapeDtypeStruct((t, DIFF_W), BF16)],
        compiler_params=_cparams(("parallel",)),
        name="inproj",
    )(x2, gain, w)


def _gdn_kernel(blk_ref, convw_ref, alog_ref, dtb_ref, ng_ref, out_ref,
                s_ref, tail_ref, xp_ref, q_s, k_s, v_s, b_s, g_s, *, ct):
    c64 = GDN_CHUNK

    @pl.when(pl.program_id(1) == 0)
    def _():
        s_ref[...] = jnp.zeros_like(s_ref)
        tail_ref[...] = jnp.zeros_like(tail_ref)

    qkv = blk_ref[0, :, 0:GDN_QKV_W]
    xp_ref[0:8, :] = tail_ref[...]
    xp_ref[8:8 + ct, :] = qkv
    tail_ref[...] = qkv[ct - 8:ct, :]
    cw = convw_ref[...]
    y = cw[0:1, :] * xp_ref[5:5 + ct, :]
    for j in range(1, GDN_CONV):
        y = y + cw[j:j + 1, :] * xp_ref[5 + j:5 + j + ct, :]
    y = y * _sigmoid(y)

    ri = lax.broadcasted_iota(I32, (GDN_W, GDN_W), 0)
    ci = lax.broadcasted_iota(I32, (GDN_W, GDN_W), 1)
    head_ones = jnp.where((ri // c64) == (ci // c64), 1.0, 0.0).astype(BF16)

    q = y[:, 0:GDN_W]
    k = y[:, GDN_W:2 * GDN_W]
    q_s[...] = q * lax.rsqrt(_dot_sel_r(q * q, head_ones) + 1e-6) * (GDN_DK ** -0.5)
    k_s[...] = k * lax.rsqrt(_dot_sel_r(k * k, head_ones) + 1e-6)
    v_s[...] = y[:, 2 * GDN_W:3 * GDN_W]
    a_rep = blk_ref[0, :, GDN_QKV_W + GDN_W:GDN_QKV_W + 2 * GDN_W]
    b_rep = blk_ref[0, :, GDN_QKV_W + 2 * GDN_W:GDN_QKV_W + 3 * GDN_W]
    b_s[...] = _sigmoid(b_rep)
    g_raw = -jnp.exp(alog_ref[...]) * _softplus(a_rep + dtb_ref[...])
    rt = lax.broadcasted_iota(I32, (ct, ct), 0)
    ctk = lax.broadcasted_iota(I32, (ct, ct), 1)
    tri_bd = jnp.where(((rt // c64) == (ctk // c64)) & (ctk <= rt), 1.0, 0.0).astype(BF16)
    g_s[...] = _dot_sel_l(tri_bd, g_raw)

    r64 = lax.broadcasted_iota(I32, (c64, c64), 0)
    col64 = lax.broadcasted_iota(I32, (c64, c64), 1)
    incl = col64 <= r64
    strict = col64 < r64
    eye_b = col64 == r64
    eye_f = jnp.where(eye_b, 1.0, 0.0)
    ones64 = jnp.ones((c64, c64), BF16)
    ng = ng_ref[...]

    n_chunks = ct // c64
    heads = range(GDN_HEADS)
    chains = [(slice(c * c64, (c + 1) * c64), slice(h * c64, (h + 1) * c64))
              for c in range(n_chunks) for h in heads]
    each = lambda f, *ls: [f(*xs) for xs in zip(*ls)]
    gi = [g_s[r, l] for r, l in chains]
    gj = each(lambda g: _dot_sel_l(ones64, jnp.where(eye_b, g, 0.0)), gi)
    dec = each(lambda a, b: jnp.exp(jnp.where(incl, a - b, NEG_INF)), gi, gj)
    kh = [k_s[r, l] for r, l in chains]
    bi = [b_s[r, l] for r, l in chains]
    kb = each(lambda x: x.astype(BF16), kh)
    kk = each(_dot_nt, kb, kb)
    lm = each(lambda b, x, d: jnp.where(strict, b * x * d, 0.0), bi, kk, dec)
    tinv = each(lambda x: eye_f - x, lm)
    pw = lm
    for _ in range(5):
        pw = each(_mm3, pw, pw)
        tinv = each(lambda t, m: t + _mm3(t, m), tinv, pw)
    tb = each(lambda x: x.astype(BF16), tinv)
    eg = each(jnp.exp, gi)
    u = each(lambda t, rl, b: _dot(t, (v_s[rl[0], rl[1]] * b).astype(BF16)), tb, chains, bi)
    w = each(lambda t, k, b, e: _dot(t, (k * (b * e)).astype(BF16)).astype(BF16), tb, kh, bi, eg)
    qh = [q_s[r, l] for r, l in chains]
    qk = each(lambda q, k, d: jnp.where(incl, _dot_nt(q.astype(BF16), k) * d, 0.0).astype(BF16), qh, kb, dec)
    qg = each(lambda q, e: (q * e).astype(BF16), qh, eg)
    g_last = each(lambda g: g[c64 - 1:c64, :], gi)
    kg = each(lambda k, gl, g: (k * jnp.exp(gl - g)).astype(BF16), kh, g_last, gi)
    decay_last = each(jnp.exp, g_last)

    states = [s_ref[h] for h in heads]
    for c in range(n_chunks):
        ids = [c * GDN_HEADS + h for h in heads]
        sb = [s.astype(BF16) for s in states]
        vnb = [(u[n] - _dot(w[n], sb[h])).astype(BF16) for h, n in zip(heads, ids)]
        o = [_dot(qg[n], sb[h]) + _dot(qk[n], vnb[h]) for h, n in zip(heads, ids)]
        states = [states[h] * decay_last[n] + _dot_tn(kg[n], vnb[h]) for h, n in zip(heads, ids)]
        ms = [_dot_sel_r(x * x, ones64) * (1.0 / c64) for x in o]
        outs = [x * lax.rsqrt(m + NORM_EPS) * ng for x, m in zip(o, ms)]
        rows = slice(c * c64, (c + 1) * c64)
        z = blk_ref[0, rows, GDN_QKV_W:GDN_QKV_W + GDN_W]
        out_ref[0, rows, :] = (jnp.concatenate(outs, axis=1) * (z * _sigmoid(z))).astype(out_ref.dtype)
    for h in heads:
        s_ref[h] = states[h]


def _gdn(gdn_in, conv_w, a_log_rep, dtb_rep, ng, ct=256):
    bn, seq, _ = gdn_in.shape
    fixed = lambda b, l: (0, 0)
    return pl.pallas_call(
        functools.partial(_gdn_kernel, ct=ct),
        grid=(bn, seq // ct),
        in_specs=[pl.BlockSpec((1, ct, INPROJ_GDN_W), lambda b, l: (b, l, 0)),
                  pl.BlockSpec((GDN_CONV, GDN_QKV_W), fixed), pl.BlockSpec((1, GDN_W), fixed),
                  pl.BlockSpec((1, GDN_W), fixed), pl.BlockSpec((1, GDN_DK), fixed)],
        out_specs=pl.BlockSpec((1, ct, GDN_W), lambda b, l: (b, l, 0)),
        out_shape=jax.ShapeDtypeStruct((bn, seq, GDN_W), BF16),
        scratch_shapes=[pltpu.VMEM((GDN_HEADS, GDN_DK, GDN_DK), F32), pltpu.VMEM((8, GDN_QKV_W), F32),
                        pltpu.VMEM((ct + 8, GDN_QKV_W), F32)] + [pltpu.VMEM((ct, GDN_W), F32)] * 5,
        compiler_params=_cparams(("arbitrary", "arbitrary")),
        name="gdn",
    )(gdn_in, conv_w, a_log_rep, dtb_rep, ng)


def _s5_kernel(u_ref, lre_ref, lim_ref, lstep_ref, bre_ref, bim_ref, cre_ref, cim_ref, d_ref, y_ref,
               bmat, cmat, avec, carry, zr_s, zi_s, xr_s, xi_s, *, rb):
    ns = S5_NS
    cw = S5_WIDTH

    @pl.when(pl.program_id(1) == 0)
    def _():
        step = jnp.exp(lstep_ref[...])
        lr = lre_ref[...]
        li = lim_ref[...]
        mag = jnp.exp(lr * step)
        ar = mag * jnp.cos(li * step)
        ai = mag * jnp.sin(li * step)
        den = lr * lr + li * li
        mr = ((ar - 1.0) * lr + ai * li) / den
        mi = (ai * lr - (ar - 1.0) * li) / den
        bre = bre_ref[...]
        bim = bim_ref[...]
        bmat[:, 0:ns] = (mr * bre - mi * bim).astype(BF16)
        bmat[:, ns:2 * ns] = (mr * bim + mi * bre).astype(BF16)
        cmat[0:ns, :] = cre_ref[...].astype(BF16)
        cmat[ns:2 * ns, :] = (-cim_ref[...]).astype(BF16)
        avec[0:1, :] = ar
        avec[1:2, :] = ai
        pr, pi = ar, ai
        for _ in range(4):
            pr, pi = pr * pr - pi * pi, 2.0 * pr * pi
        avec[2:3, :] = pr
        avec[3:4, :] = pi
        carry[...] = jnp.zeros_like(carry)

    ar = avec[0:1, :]
    ai = avec[1:2, :]
    a16r = avec[2:3, :]
    a16i = avec[3:4, :]

    def inject(s):
        ub = u_ref[0, :, s * cw:(s + 1) * cw].astype(BF16)
        return _dot(ub, bmat[...])

    xr = jnp.zeros((rb, ns), F32)
    xi = jnp.zeros((rb, ns), F32)
    for s in range(S5_SUB):
        p = inject(s)
        xr, xi = ar * xr - ai * xi + p[:, 0:ns], ar * xi + ai * xr + p[:, ns:2 * ns]
    zr_s[...] = xr
    zi_s[...] = xi

    def row_step(kk, c):
        cr, ci_ = c
        xr_s[pl.ds(kk, 1), :] = cr
        xi_s[pl.ds(kk, 1), :] = ci_
        zr = zr_s[pl.ds(kk, 1), :]
        zi = zi_s[pl.ds(kk, 1), :]
        return (a16r * cr - a16i * ci_ + zr, a16r * ci_ + a16i * cr + zi)

    cr, ci_ = lax.fori_loop(0, rb, row_step, (carry[0:1, :], carry[1:2, :]))
    carry[0:1, :] = cr
    carry[1:2, :] = ci_

    xr = xr_s[...]
    xi = xi_s[...]
    dsk = d_ref[...]
    for s in range(S5_SUB):
        p = inject(s)
        xr, xi = ar * xr - ai * xi + p[:, 0:ns], ar * xi + ai * xr + p[:, ns:2 * ns]
        yv = _dot(xr.astype(BF16), cmat[0:ns, :]) + _dot(xi.astype(BF16), cmat[ns:2 * ns, :])
        yv = yv + dsk * u_ref[0, :, s * cw:(s + 1) * cw]
        y_ref[0, :, s * cw:(s + 1) * cw] = _gelu_tanh(yv).astype(y_ref.dtype)


def _s5(u_rows, lre, lim, lstep, bre_bd, bim_bd, cre_bd, cim_bd, dskip, rb=128):
    bn, nrows, rw = u_rows.shape
    fixed = lambda b, r: (0, 0)
    ns = S5_NS
    return pl.pallas_call(
        functools.partial(_s5_kernel, rb=rb),
        grid=(bn, nrows // rb),
        in_specs=[pl.BlockSpec((1, rb, rw), lambda b, r: (b, r, 0)),
                  pl.BlockSpec((1, ns), fixed), pl.BlockSpec((1, ns), fixed), pl.BlockSpec((1, ns), fixed),
                  pl.BlockSpec((S5_WIDTH, ns), fixed), pl.BlockSpec((S5_WIDTH, ns), fixed),
                  pl.BlockSpec((ns, S5_WIDTH), fixed), pl.BlockSpec((ns, S5_WIDTH), fixed),
                  pl.BlockSpec((1, S5_WIDTH), fixed)],
        out_specs=pl.BlockSpec((1, rb, rw), lambda b, r: (b, r, 0)),
        out_shape=jax.ShapeDtypeStruct((bn, nrows, rw), BF16),
        scratch_shapes=[pltpu.VMEM((S5_WIDTH, 2 * ns), BF16), pltpu.VMEM((2 * ns, S5_WIDTH), BF16),
                        pltpu.VMEM((8, ns), F32), pltpu.VMEM((8, ns), F32)]
                       + [pltpu.VMEM((rb, ns), F32)] * 4,
        compiler_params=_cparams(("arbitrary", "arbitrary")),
        name="s5",
    )(u_rows, lre, lim, lstep, bre_bd, bim_bd, cre_bd, cim_bd, dskip)


def _attn_kernel(q_ref, k_ref, v_ref, lq1_ref, lk1_ref, lq2_ref, lk2_ref, ng_ref, o_ref,
                 qs_s, m_s, l_s, acc_s, *, blk, lam_init):
    i = pl.program_id(2)
    q = q_ref[0]
    lane = lax.broadcasted_iota(I32, q.shape, 1)
    zero = jnp.zeros_like(q)
    qs_s[0:blk, :] = jnp.where(lane < DIFF_DQK, q, zero)
    qs_s[blk:2 * blk, :] = jnp.where(lane >= DIFF_DQK, q, zero)
    m_s[...] = jnp.full_like(m_s, NEG_INF)
    l_s[...] = jnp.zeros_like(l_s)
    acc_s[...] = jnp.zeros_like(acc_s)
    ones = jnp.ones((blk, DIFF_DV), BF16)

    def block_rows(j):
        return pl.ds(pl.multiple_of(j * blk, blk), blk)

    def scores(j):
        return _dot_nt(qs_s[...], k_ref[0, block_rows(j), :])

    def update(j, s):
        m_old = m_s[...]
        m_new = jnp.maximum(m_old, jnp.max(s, axis=-1, keepdims=True))
        p = jnp.exp(s - jnp.concatenate([m_new] * (blk // DIFF_DV), axis=1)).astype(BF16)
        alpha = jnp.exp(m_old - m_new)
        pv = _dot(p, jnp.concatenate([v_ref[0, block_rows(j), :], ones], axis=1))
        acc_s[...] = alpha * acc_s[...] + pv[:, 0:DIFF_DV]
        l_s[...] = alpha * l_s[...] + pv[:, DIFF_DV:2 * DIFF_DV]
        m_s[...] = m_new

    def body(j, s):
        s_next = scores(j + 1)
        update(j, s)
        return s_next

    s = lax.fori_loop(0, i, body, scores(0))
    row = lax.broadcasted_iota(I32, s.shape, 0) & (blk - 1)
    col = lax.broadcasted_iota(I32, s.shape, 1)
    update(i, jnp.where(col <= row, s, NEG_INF))
    lam = (jnp.exp(jnp.sum(lq1_ref[...] * lk1_ref[...], axis=-1, keepdims=True))
           - jnp.exp(jnp.sum(lq2_ref[...] * lk2_ref[...], axis=-1, keepdims=True)) + lam_init)
    o = acc_s[0:blk, :] / l_s[0:blk, :] - lam * (acc_s[blk:2 * blk, :] / l_s[blk:2 * blk, :])
    ms = jnp.mean(o * o, axis=-1, keepdims=True)
    o = o * lax.rsqrt(ms + NORM_EPS) * ng_ref[...] * (1.0 - lam_init)
    o_ref[0] = o.astype(o_ref.dtype)


def _attn(q, k, v, lq1, lk1, lq2, lk2, ng, lam_init, blk=512):
    bn, seq, _ = q.shape
    blk = min(blk, seq)
    fixed = lambda b, h, i: (0, 0)
    kv_spec = pl.BlockSpec((1, seq, DIFF_DV), lambda b, h, i: (b, 0, h))
    return pl.pallas_call(
        functools.partial(_attn_kernel, blk=blk, lam_init=lam_init),
        grid=(bn, DIFF_HEADS, seq // blk),
        in_specs=[pl.BlockSpec((1, blk, DIFF_DV), lambda b, h, i: (b, i, h)), kv_spec, kv_spec]
                 + [pl.BlockSpec((1, DIFF_DQK), fixed)] * 4 + [pl.BlockSpec((1, DIFF_DV), fixed)],
        out_specs=pl.BlockSpec((1, blk, DIFF_DV), lambda b, h, i: (b, i, h)),
        out_shape=jax.ShapeDtypeStruct((bn, seq, DIFF_W), BF16),
        scratch_shapes=[pltpu.VMEM((2 * blk, DIFF_DV), BF16), pltpu.VMEM((2 * blk, DIFF_DV), F32),
                        pltpu.VMEM((2 * blk, DIFF_DV), F32), pltpu.VMEM((2 * blk, DIFF_DV), F32)],
        compiler_params=_cparams(("parallel", "parallel", "arbitrary")),
        name="diff_attn",
    )(q, k, v, lq1, lk1, lq2, lk2, ng)


def _outproj_kernel(h_ref, oa_ref, ys_ref, oc_ref, gw_ref, gb_ref, wo_ref, out_ref):
    zg = _dot(ys_ref[...], gw_ref[...]) + gb_ref[...]
    ob = (zg[:, 0:S5_WIDTH] * _sigmoid(zg[:, S5_WIDTH:2 * S5_WIDTH])).astype(BF16)
    acc = _dot(oa_ref[...], wo_ref[0:GDN_W, :])
    acc = acc + _dot(ob, wo_ref[GDN_W:GDN_W + S5_WIDTH, :])
    acc = acc + _dot(oc_ref[...], wo_ref[GDN_W + S5_WIDTH:, :])
    out_ref[...] = h_ref[...] + acc


def _outproj(h2, oa, ys, oc, glu_w, glu_b, w_out, tm=512):
    t = h2.shape[0]
    row = lambda i: (i, 0)
    fixed = lambda i: (0, 0)
    return pl.pallas_call(
        _outproj_kernel,
        grid=(t // tm,),
        in_specs=[pl.BlockSpec((tm, D_MODEL), row), pl.BlockSpec((tm, GDN_W), row),
                  pl.BlockSpec((tm, S5_WIDTH), row), pl.BlockSpec((tm, DIFF_W), row),
                  pl.BlockSpec((S5_WIDTH, 2 * S5_WIDTH), fixed), pl.BlockSpec((1, 2 * S5_WIDTH), fixed),
                  pl.BlockSpec((D_MODEL, D_MODEL), fixed)],
        out_specs=pl.BlockSpec((tm, D_MODEL), row),
        out_shape=jax.ShapeDtypeStruct((t, D_MODEL), F32),
        compiler_params=_cparams(("parallel",)),
        name="outproj",
    )(h2, oa, ys, oc, glu_w, glu_b, w_out)


_BIG_ID = 1.0e9


def _top16(x, ids, payload):
    n = x.shape[1]
    r16 = lax.broadcasted_iota(I32, (PEER_TOPK, n), 0)
    vals = jnp.zeros((PEER_TOPK, n), F32)
    pays = jnp.zeros((PEER_TOPK, n), F32)
    for kk in range(PEER_TOPK):
        m = jnp.max(x, axis=0, keepdims=True)
        first = jnp.min(jnp.where(x == m, ids, _BIG_ID), axis=0, keepdims=True)
        hit = ids == first
        pay = first if payload is None else jnp.max(jnp.where(hit, payload, -1.0), axis=0, keepdims=True)
        x = jnp.where(hit, NEG_INF, x)
        vals = jnp.where(r16 == kk, m, vals)
        pays = jnp.where(r16 == kk, pay, pays)
    return vals, pays


def _peer_route_kernel(h_ref, g_ref, wqt_ref, keys_ref, xn_ref, exp_ref, row_ref, par_ref, gate_ref,
                       qt_s, sv_s, si_s, gate_s, exp_s, *, tm):
    x = h_ref[...]
    ms = jnp.mean(x * x, axis=-1, keepdims=True)
    xn = x * lax.rsqrt(ms + NORM_EPS) * g_ref[...]
    xn_ref[...] = xn
    qt_s[...] = _dot_nt(wqt_ref[...], xn.astype(BF16)).astype(BF16)

    key_id = lax.broadcasted_iota(I32, (N_KEYS, tm), 0).astype(F32)

    def half_body(hp, carry):
        r0 = pl.multiple_of(hp * PEER_DHALF, PEER_DHALF)
        s = _dot(keys_ref[hp], qt_s[pl.ds(r0, PEER_DHALF), :])
        vals, ids = _top16(s, key_id, None)
        sv_s[hp] = vals
        si_s[hp] = ids
        return carry

    lax.fori_loop(0, 2 * PEER_HEADS, half_body, 0)

    sub8 = lax.broadcasted_iota(I32, (8, tm), 0)
    i8 = sub8.astype(F32)
    low4 = sub8 < 4
    j4 = (sub8 & 3).astype(F32)

    def head_body(hd, carry):
        a0 = sv_s[2 * hd]
        a1 = sv_s[2 * hd + 1]
        e0 = si_s[2 * hd] * float(N_KEYS)
        e1 = si_s[2 * hd + 1]
        a1_44 = jnp.where(low4, a1[0:8, :], pltpu.roll(a1[0:8, :], 4, 0))
        e1_44 = jnp.where(low4, e1[0:8, :], pltpu.roll(e1[0:8, :], 4, 0))
        cs = [a0[0:1, :] + a1[0:8, :], a0[0:1, :] + a1[8:16, :]]
        es = [e0[0:1, :] + e1[0:8, :], e0[0:1, :] + e1[8:16, :]]
        fs = [i8, i8 + 8.0]
        for i in range(1, 4):
            cs.append(a0[i:i + 1, :] + a1[0:8, :])
            es.append(e0[i:i + 1, :] + e1[0:8, :])
            fs.append(i8 + float(i * PEER_TOPK))
        for i in (4, 6):
            cs.append(jnp.where(low4, a0[i:i + 1, :], a0[i + 1:i + 2, :]) + a1_44)
            es.append(jnp.where(low4, e0[i:i + 1, :], e0[i + 1:i + 2, :]) + e1_44)
            fs.append(jnp.where(low4, float(i * PEER_TOPK), float((i + 1) * PEER_TOPK)) + j4)
        cs.append(a0[8:16, :] + a1[0:1, :])
        es.append(e0[8:16, :] + e1[0:1, :])
        fs.append((i8 + 8.0) * float(PEER_TOPK))
        top_s, experts = _top16(jnp.concatenate(cs, axis=0), jnp.concatenate(fs, axis=0),
                                jnp.concatenate(es, axis=0))
        ex = jnp.exp(top_s - jnp.max(top_s, axis=0, keepdims=True))
        rows = pl.ds(pl.multiple_of(hd * PEER_TOPK, PEER_TOPK), PEER_TOPK)
        gate_s[rows, :] = ex / jnp.sum(ex, axis=0, keepdims=True)
        exp_s[rows, :] = experts
        return carry

    lax.fori_loop(0, PEER_HEADS, head_body, 0)
    gate_ref[...] = gate_s[...].T
    expert = exp_s[...].T.astype(I32)
    exp_ref[...] = expert
    row_ref[...] = (expert & (TAB_ROWS - 1)) * 8
    par_ref[...] = lax.shift_right_logical(expert, TAB_ROWS.bit_length() - 1).astype(F32)


def _peer_route(h2, gain, wqt, keys, tm=512):
    t = h2.shape[0]
    tm = min(tm, t)
    row = lambda i: (i, 0)
    return pl.pallas_call(
        functools.partial(_peer_route_kernel, tm=tm),
        grid=(t // tm,),
        in_specs=[pl.BlockSpec((tm, D_MODEL), row), pl.BlockSpec((1, D_MODEL), lambda i: (0, 0)),
                  pl.BlockSpec((D_MODEL, D_MODEL), lambda i: (0, 0)),
                  pl.BlockSpec((2 * PEER_HEADS, N_KEYS, PEER_DHALF), lambda i: (0, 0, 0))],
        out_specs=[pl.BlockSpec((tm, D_MODEL), row)] + [pl.BlockSpec((tm, PEER_PAIRS), row)] * 4,
        out_shape=[jax.ShapeDtypeStruct((t, D_MODEL), F32), jax.ShapeDtypeStruct((t, PEER_PAIRS), I32),
                   jax.ShapeDtypeStruct((t, PEER_PAIRS), I32), jax.ShapeDtypeStruct((t, PEER_PAIRS), F32),
                   jax.ShapeDtypeStruct((t, PEER_PAIRS), F32)],
        scratch_shapes=[pltpu.VMEM((D_MODEL, tm), BF16), pltpu.VMEM((2 * PEER_HEADS, PEER_TOPK, tm), F32),
                        pltpu.VMEM((2 * PEER_HEADS, PEER_TOPK, tm), F32),
                        pltpu.VMEM((PEER_PAIRS, tm), F32), pltpu.VMEM((PEER_PAIRS, tm), F32)],
        compiler_params=_cparams(("parallel",)),
        name="peer_route",
    )(h2, gain, wqt, keys)


TAB_ROWS = N_EXPERTS // 2


def _token_tile(block, tl):
    return jnp.concatenate([block[tl:tl + 1, s * 128:(s + 1) * 128] for s in range(8)], axis=0)


def _tiles_to_rows(tiles):
    return jnp.concatenate([jnp.concatenate([tile[s:s + 1, :] for tile in tiles], axis=0) for s in range(8)], axis=1)


def _load_words(tab_ref, row_ref, t, p):
    return tab_ref[pl.ds(pl.multiple_of(row_ref[t, p], 8), 8), :]


def _tok_spec(tb, space=None):
    return pl.BlockSpec((tb, PEER_PAIRS), lambda i: (i, 0), memory_space=space)


def _table_spec():
    return pl.BlockSpec((TAB_ROWS * 8, 128), lambda i: (0, 0), pipeline_mode=pl.Buffered(1))


SC_CORES = 2
SC_SUBCORES = 16
SC_LANES = 16
SC_WORKERS = SC_CORES * SC_SUBCORES


def _peer_u_sc(tab, expert, xn, t0, t_sc):
    tw = t_sc // SC_WORKERS
    heads = PEER_PAIRS // PEER_TOPK
    chunks = D_MODEL // SC_LANES
    mesh = plsc.VectorSubcoreMesh(core_axis_name="c", subcore_axis_name="s")

    def body(tab_hbm, idx_hbm, x_hbm, d_hbm, idx_a, idx_b, x_a, x_b, rows_a, rows_b, d_v,
             sem_ra, sem_rb, sem_ia, sem_ib, sem_xa, sem_xb):
        wid = lax.axis_index("s") * SC_CORES + lax.axis_index("c")
        lane = lax.iota(I32, SC_LANES)
        first = t0 + wid * tw
        last = first + tw - 1
        tok_bufs = ((idx_a, x_a, sem_ia, sem_xa), (idx_b, x_b, sem_ib, sem_xb))
        row_bufs = ((rows_a, sem_ra), (rows_b, sem_rb))

        def fetch(t, k):
            idx_v, x_v, sem_i, sem_x = tok_bufs[k]
            return (pltpu.make_async_copy(idx_hbm.at[t], idx_v, sem_i),
                    pltpu.make_async_copy(x_hbm.at[t], x_v, sem_x))

        def gather(k, h):
            buf, sem = row_bufs[h % 2]
            return pltpu.make_async_copy(tab_hbm.at[tok_bufs[k][0].at[pl.ds(h * PEER_TOPK, PEER_TOPK)]], buf, sem)

        for c in fetch(first, 0):
            c.start()
        for c in fetch(first, 0):
            c.wait()
        gather(0, 0).start()

        def two_tokens(i2, carry):
            for k in range(2):
                t = first + 2 * i2 + k
                x_v = tok_bufs[k][1]
                nxt = fetch(jnp.minimum(t + 1, last), 1 - k)
                for c in nxt:
                    c.start()
                for h in range(heads):
                    if h + 1 < heads:
                        gather(k, h + 1).start()
                    else:
                        for c in nxt:
                            c.wait()
                        gather(1 - k, 0).start()
                    gather(k, h).wait()
                    rows_v = row_bufs[h % 2][0]

                    def chunk(j, accs):
                        xj = x_v[pl.ds(j * SC_LANES, SC_LANES)]
                        return tuple(a + rows_v[r, pl.ds(j * SC_LANES, SC_LANES)] * xj for r, a in enumerate(accs))

                    accs = lax.fori_loop(0, chunks, chunk,
                                         tuple(jnp.zeros((SC_LANES,), F32) for _ in range(PEER_TOPK)))
                    out = jnp.zeros((SC_LANES,), F32)
                    for r in range(PEER_TOPK):
                        out = jnp.where(lane == r, jnp.sum(accs[r]), out)
                    d_v[pl.ds(h * PEER_TOPK, PEER_TOPK)] = out
                pltpu.sync_copy(d_v, d_hbm.at[t - t0])
            return carry

        lax.fori_loop(0, tw // 2, two_tokens, 0)
        gather(0, 0).wait()

    return pl.kernel(
        body, mesh=mesh,
        out_type=jax.ShapeDtypeStruct((t_sc, PEER_PAIRS), F32),
        compiler_params=pltpu.CompilerParams(needs_layout_passes=False),
        scratch_types=[pltpu.VMEM((PEER_PAIRS,), I32), pltpu.VMEM((PEER_PAIRS,), I32),
                       pltpu.VMEM((D_MODEL,), F32), pltpu.VMEM((D_MODEL,), F32),
                       pltpu.VMEM((PEER_TOPK, D_MODEL), F32), pltpu.VMEM((PEER_TOPK, D_MODEL), F32),
                       pltpu.VMEM((PEER_PAIRS,), F32)] + [pltpu.SemaphoreType.DMA] * 6,
        name="peer_u_sc",
    )(tab, expert, xn)


def _peer_coef_kernel(d_ref, gate_ref, c_ref):
    c_ref[...] = gate_ref[...] * _gelu_tanh(d_ref[...])


def _peer_coef(d_tp, gate_tp, tm=2048):
    t = d_tp.shape[0]
    tm = min(tm, t)
    return pl.pallas_call(
        _peer_coef_kernel,
        grid=(t // tm,),
        in_specs=[_tok_spec(tm), _tok_spec(tm)],
        out_specs=_tok_spec(tm),
        out_shape=jax.ShapeDtypeStruct((t, PEER_PAIRS), F32),
        compiler_params=_cparams(("parallel",)),
        name="peer_coef",
    )(d_tp, gate_tp)


PEER_V_TOKENS_PER_STEP = 8
PEER_V_KDIM = (PEER_PAIRS // 2) * 16


def _peer_v_kernel(row_ref, par_ref, coef_ref, h_ref, tab_ref, out_ref, cz_s, *, tb):
    half = PEER_PAIRS // 2
    kdim = PEER_V_KDIM
    nt = PEER_V_TOKENS_PER_STEP
    pk = lax.broadcasted_iota(I32, (PEER_PAIRS, kdim), 0)
    qk = lax.shift_right_logical(lax.broadcasted_iota(I32, (PEER_PAIRS, kdim), 1), 4)
    hk = (lax.broadcasted_iota(I32, (tb, kdim), 1) & 1).astype(F32)
    coef = coef_ref[...].astype(BF16)
    par = par_ref[...].astype(BF16)
    for hf in range(2):
        expand = jnp.where(pk == qk + hf * half, 1.0, 0.0).astype(BF16)
        cz_s[hf] = jnp.where(hk == _dot(par, expand), _dot(coef, expand), 0.0)
    ks = lax.broadcasted_iota(I32, (8, kdim), 0)
    k8 = lax.broadcasted_iota(I32, (8, kdim), 1)
    own_sublane = lax.shift_right_logical(k8 & 15, 1) == ks

    def tok_body(tt, carry):
        rows8 = pl.ds(pl.multiple_of(tt * nt, nt), nt)
        hb = h_ref[rows8, :]
        tiles = []
        for j in range(nt):
            t = tt * nt + j
            pieces = []
            for q in range(half):
                wa = pltpu.bitcast(_load_words(tab_ref, row_ref, t, q), BF16)
                wb = pltpu.bitcast(_load_words(tab_ref, row_ref, t, half + q), BF16)
                pieces.append(jnp.concatenate([wa, wb], axis=1))
            g = jnp.concatenate(pieces, axis=0)
            ck = jnp.concatenate(
                [jnp.where(own_sublane, jnp.broadcast_to(cz_s[hf, pl.ds(t, 1), :], (8, kdim)), 0.0)
                 for hf in range(2)], axis=0).astype(BF16)
            o = _dot(ck, g)
            tiles.append(_token_tile(hb, j) + (o[0:8, 0:128] + o[8:16, 128:256]))
        out_ref[rows8, :] = _tiles_to_rows(tiles)
        return carry

    lax.fori_loop(0, tb // nt, tok_body, 0)


def _peer_v(rows, par, coef, h2, tab, t, tb=128):
    return pl.pallas_call(
        functools.partial(_peer_v_kernel, tb=tb),
        grid=(t // tb,),
        in_specs=[_tok_spec(tb, pltpu.SMEM), _tok_spec(tb), _tok_spec(tb),
                  pl.BlockSpec((tb, D_MODEL), lambda i: (i, 0)), _table_spec()],
        out_specs=pl.BlockSpec((tb, D_MODEL), lambda i: (i, 0)),
        out_shape=jax.ShapeDtypeStruct((t, D_MODEL), F32),
        scratch_shapes=[pltpu.VMEM((2, tb, PEER_V_KDIM), F32)],
        compiler_params=_cparams(("arbitrary",)),
        name="peer_v",
    )(rows, par, coef, h2, tab)


def _final_norm_kernel(x_ref, g_ref, o_ref):
    x = x_ref[...]
    ms = jnp.mean(x * x, axis=-1, keepdims=True)
    o_ref[...] = x * lax.rsqrt(ms + NORM_EPS) * g_ref[...]


def _final_norm(x2, gain, tm=1024):
    t = x2.shape[0]
    tm = min(tm, t)
    row = lambda i: (i, 0)
    return pl.pallas_call(
        _final_norm_kernel,
        grid=(t // tm,),
        in_specs=[pl.BlockSpec((tm, D_MODEL), row), pl.BlockSpec((1, D_MODEL), lambda i: (0, 0))],
        out_specs=pl.BlockSpec((tm, D_MODEL), row),
        out_shape=jax.ShapeDtypeStruct((t, D_MODEL), F32),
        compiler_params=_cparams(("parallel",)),
        name="final_norm",
    )(x2, gain)


def _pack_table(tab):
    bits = lax.bitcast_convert_type(tab.astype(BF16), jnp.uint16).astype(U32)
    return (bits[:TAB_ROWS] | (bits[TAB_ROWS:] << 16)).reshape(TAB_ROWS * 8, 128)


def _peer_start(h2, w, tab_u_all):
    t = h2.shape[0]
    xn, expert, rows, par, gate = _peer_route(h2, w["norm2"], w["wqt"], w["keys"])
    d = _peer_u_sc(tab_u_all, expert + w["expert_base"], xn, 0, t)
    return dict(h2=h2, rows=rows, par=par, gate=gate, d=d)


def _peer_finish(st, w):
    t = st["h2"].shape[0]
    coef = _peer_coef(st["d"], st["gate"])
    return _peer_v(st["rows"], st["par"], coef, st["h2"], w["tab_v"], t)


def _rep(x, n):
    return jnp.repeat(x, n, axis=-1)


def _inproj_weight(w_in_l):
    o = 0
    qkv = w_in_l[:, o:o + GDN_QKV_W]; o += GDN_QKV_W
    z = w_in_l[:, o:o + GDN_W]; o += GDN_W
    a = w_in_l[:, o:o + GDN_HEADS]; o += GDN_HEADS
    b = w_in_l[:, o:o + GDN_HEADS]; o += GDN_HEADS
    rest = w_in_l[:, o:]
    return jnp.concatenate([qkv, z, _rep(a, GDN_DK), _rep(b, GDN_DK), rest], axis=1).astype(BF16)


def _block_diag_gc(b_gnc):
    g = b_gnc.shape[0]
    eye = jnp.eye(g, dtype=b_gnc.dtype)
    t = jnp.swapaxes(b_gnc, 1, 2)
    return (t[:, :, None, :] * eye[:, None, :, None]).reshape(g * t.shape[1], g * t.shape[2])


def _layer_weights(p, layer):
    row = lambda x: x[None, :]
    flat = lambda x: x.reshape(1, S5_NS)
    return dict(
        lam_init=0.8 - 0.6 * math.exp(-0.3 * layer),
        norm1=row(p["norm1_g"][layer]), w_in=_inproj_weight(p["w_in"][layer]),
        conv_w=p["gdn_conv_w"][layer], a_log=row(_rep(p["gdn_a_log"][layer].astype(F32), GDN_DK)),
        dtb=row(_rep(p["gdn_dt_bias"][layer].astype(F32), GDN_DK)), gdn_ng=row(p["gdn_norm_g"][layer]),
        s5=(flat(p["s5_lambda_re"][layer]), flat(p["s5_lambda_im"][layer]),
            flat(_rep(p["s5_log_step"][layer][:, None], S5_STATE)),
            _block_diag_gc(p["s5_b_re"][layer]), _block_diag_gc(p["s5_b_im"][layer]),
            _block_diag_gc(p["s5_c_re"][layer]), _block_diag_gc(p["s5_c_im"][layer]),
            row(p["s5_d"][layer])),
        attn=tuple(row(p[k][layer]) for k in ("diff_lq1", "diff_lk1", "diff_lq2", "diff_lk2", "diff_norm_g")),
        glu_w=p["s5_glu_w"][layer].astype(BF16), glu_b=row(p["s5_glu_b"][layer]),
        w_out=p["w_out"][layer].astype(BF16),
        norm2=row(p["norm2_g"][layer]), wqt=p["peer_wq"][layer].T.astype(BF16),
        keys=p["peer_keys"][layer].reshape(2 * PEER_HEADS, N_KEYS, PEER_DHALF).astype(BF16),
        tab_v=_pack_table(p["peer_v"][layer]), expert_base=layer * N_EXPERTS)


def _mixers(h, w):
    bn, seq, _ = h.shape
    t = bn * seq
    gdn_in, us, qc, kc, vc = _inproj(h.reshape(t, D_MODEL), w["norm1"], w["w_in"])
    o_a = _gdn(gdn_in.reshape(bn, seq, INPROJ_GDN_W), w["conv_w"], w["a_log"], w["dtb"], w["gdn_ng"])
    ys = _s5(us.reshape(bn, seq // S5_SUB, S5_SUB * S5_WIDTH), *w["s5"])
    o_c = _attn(qc.reshape(bn, seq, DIFF_W), kc.reshape(bn, seq, DIFF_W), vc.reshape(bn, seq, DIFF_W),
                *w["attn"], w["lam_init"])
    return _outproj(h.reshape(t, D_MODEL), o_a.reshape(t, GDN_W), ys.reshape(t, S5_WIDTH),
                    o_c.reshape(t, DIFF_W), w["glu_w"], w["glu_b"], w["w_out"])


def kernel(x, norm1_g, w_in, gdn_conv_w, gdn_a_log, gdn_dt_bias, gdn_norm_g, s5_lambda_re, s5_lambda_im, s5_b_re, s5_b_im, s5_c_re, s5_c_im, s5_d, s5_log_step, s5_glu_w, s5_glu_b, diff_lq1, diff_lk1, diff_lq2, diff_lk2, diff_norm_g, w_out, norm2_g, peer_wq, peer_keys, peer_u, peer_v, final_g):
    p = dict(norm1_g=norm1_g, w_in=w_in, gdn_conv_w=gdn_conv_w, gdn_a_log=gdn_a_log, gdn_dt_bias=gdn_dt_bias,
             gdn_norm_g=gdn_norm_g, s5_lambda_re=s5_lambda_re, s5_lambda_im=s5_lambda_im, s5_b_re=s5_b_re,
             s5_b_im=s5_b_im, s5_c_re=s5_c_re, s5_c_im=s5_c_im, s5_d=s5_d, s5_log_step=s5_log_step,
             s5_glu_w=s5_glu_w, s5_glu_b=s5_glu_b, diff_lq1=diff_lq1, diff_lk1=diff_lk1, diff_lq2=diff_lq2,
             diff_lk2=diff_lk2, diff_norm_g=diff_norm_g, w_out=w_out, norm2_g=norm2_g, peer_wq=peer_wq,
             peer_keys=peer_keys, peer_u=peer_u, peer_v=peer_v, final_g=final_g)
    bn, seq, _ = x.shape
    weights = [_layer_weights(p, layer) for layer in range(DEPTH)]
    tab_u_all = peer_u.reshape(DEPTH * N_EXPERTS, D_MODEL)
    hs = [x[b:b + 1] for b in range(bn)]
    pending = None
    for layer in range(DEPTH):
        for b in range(bn):
            started = _peer_start(_mixers(hs[b], weights[layer]), weights[layer], tab_u_all)
            if pending is not None:
                pb, pst, pw = pending
                hs[pb] = _peer_finish(pst, pw).reshape(1, seq, D_MODEL)
            pending = (b, started, weights[layer])
    pb, pst, pw = pending
    hs[pb] = _peer_finish(pst, pw).reshape(1, seq, D_MODEL)
    h = jnp.concatenate(hs, axis=0)
    return _final_norm(h.reshape(-1, D_MODEL), final_g[None, :]).reshape(x.shape)
```

```python
import functools
import math

import jax
import jax.numpy as jnp
from jax import lax
from jax.experimental import pallas as pl
from jax.experimental.pallas import tpu as pltpu
from jax.experimental.pallas import tpu_sc as plsc

F32 = jnp.float32
BF16 = jnp.bfloat16
I32 = jnp.int32
U32 = jnp.uint32

D_MODEL = 1024
DEPTH = 2
GDN_HEADS = 4
GDN_DK = 64
GDN_CHUNK = 64
GDN_W = GDN_HEADS * GDN_DK
GDN_QKV_W = 3 * GDN_W
GDN_CONV = 4
S5_WIDTH = 256
S5_GROUPS = 16
S5_GROUP_CH = 16
S5_STATE = 64
S5_NS = S5_GROUPS * S5_STATE
S5_SUB = 16
DIFF_HEADS = 4
DIFF_DQK = 64
DIFF_DV = 128
DIFF_W = 512
PEER_HEADS = 8
PEER_DHALF = 64
N_KEYS = 128
N_EXPERTS = N_KEYS * N_KEYS
PEER_TOPK = 16
PEER_PAIRS = PEER_HEADS * PEER_TOPK
NORM_EPS = 1e-6
NEG_INF = float("-inf")

VMEM_LIMIT_BYTES = 56 * 1024 * 1024


def _cparams(sem, vmem=VMEM_LIMIT_BYTES):
    return pltpu.CompilerParams(dimension_semantics=sem, vmem_limit_bytes=vmem)


def _dot(a, b):
    return jnp.dot(a, b, preferred_element_type=F32)


def _dot_nt(a, b):
    return lax.dot_general(a, b, (((1,), (1,)), ((), ())), preferred_element_type=F32)


def _dot_tn(a, b):
    return lax.dot_general(a, b, (((0,), (0,)), ((), ())), preferred_element_type=F32)


def _split(x):
    hi = x.astype(BF16)
    lo = (x - hi.astype(F32)).astype(BF16)
    return hi, lo


def _dot_sel_r(x, sel):
    hi, lo = _split(x)
    return _dot(hi, sel) + _dot(lo, sel)


def _dot_sel_l(sel, x):
    hi, lo = _split(x)
    return _dot(sel, hi) + _dot(sel, lo)


def _mm3(a, b):
    ah, al = _split(a)
    bh, bl = _split(b)
    return _dot(ah, bh) + (_dot(ah, bl) + _dot(al, bh))


def _sigmoid(x):
    return 1.0 / (1.0 + jnp.exp(-x))


def _softplus(x):
    return jnp.maximum(x, 0.0) + jnp.log1p(jnp.exp(-jnp.abs(x)))


def _gelu_tanh(x):
    c = math.sqrt(2.0 / math.pi)
    return 0.5 * x * (1.0 + jnp.tanh(c * (x + 0.044715 * (x * x * x))))


INPROJ_GDN_W = GDN_QKV_W + 3 * GDN_W


def _inproj_kernel(x_ref, g_ref, w_ref, gdn_ref, us_ref, q_ref, k_ref, v_ref):
    x = x_ref[...]
    ms = jnp.mean(x * x, axis=-1, keepdims=True)
    xn = (x * lax.rsqrt(ms + NORM_EPS) * g_ref[...]).astype(BF16)
    o = INPROJ_GDN_W
    gdn_ref[...] = _dot(xn, w_ref[:, 0:o])
    us_ref[...] = _dot(xn, w_ref[:, o:o + S5_WIDTH])
    o += S5_WIDTH
    q_ref[...] = (_dot(xn, w_ref[:, o:o + DIFF_W]) * (DIFF_DQK ** -0.5)).astype(BF16)
    k_ref[...] = _dot(xn, w_ref[:, o + DIFF_W:o + 2 * DIFF_W]).astype(BF16)
    v_ref[...] = _dot(xn, w_ref[:, o + 2 * DIFF_W:o + 3 * DIFF_W]).astype(BF16)


def _inproj(x2, gain, w, tm=512):
    t = x2.shape[0]
    nw = w.shape[1]
    row = lambda i: (i, 0)
    fixed = lambda i: (0, 0)
    return pl.pallas_call(
        _inproj_kernel,
        grid=(t // tm,),
        in_specs=[pl.BlockSpec((tm, D_MODEL), row), pl.BlockSpec((1, D_MODEL), fixed),
                  pl.BlockSpec((D_MODEL, nw), fixed)],
        out_specs=[pl.BlockSpec((tm, INPROJ_GDN_W), row), pl.BlockSpec((tm, S5_WIDTH), row),
                   pl.BlockSpec((tm, DIFF_W), row), pl.BlockSpec((tm, DIFF_W), row),
                   pl.BlockSpec((tm, DIFF_W), row)],
        out_shape=[jax.ShapeDtypeStruct((t, INPROJ_GDN_W), F32), jax.ShapeDtypeStruct((t, S5_WIDTH), F32),
                   jax.ShapeDtypeStruct((t, DIFF_W), BF16), jax.ShapeDtypeStruct((t, DIFF_W), BF16),
                   jax.ShapeDtypeStruct((t, DIFF_W), BF16)],
        compiler_params=_cparams(("parallel",)),
        name="inproj",
    )(x2, gain, w)


def _gdn_kernel(blk_ref, convw_ref, alog_ref, dtb_ref, ng_ref, out_ref,
                s_ref, tail_ref, xp_ref, q_s, k_s, v_s, b_s, g_s, *, ct):
    c64 = GDN_CHUNK

    @pl.when(pl.program_id(1) == 0)
    def _():
        s_ref[...] = jnp.zeros_like(s_ref)
        tail_ref[...] = jnp.zeros_like(tail_ref)

    qkv = blk_ref[0, :, 0:GDN_QKV_W]
    xp_ref[0:8, :] = tail_ref[...]
    xp_ref[8:8 + ct, :] = qkv
    tail_ref[...] = qkv[ct - 8:ct, :]
    cw = convw_ref[...]
    y = cw[0:1, :] * xp_ref[5:5 + ct, :]
    for j in range(1, GDN_CONV):
        y = y + cw[j:j + 1, :] * xp_ref[5 + j:5 + j + ct, :]
    y = y * _sigmoid(y)

    ri = lax.broadcasted_iota(I32, (GDN_W, GDN_W), 0)
    ci = lax.broadcasted_iota(I32, (GDN_W, GDN_W), 1)
    head_ones = jnp.where((ri // c64) == (ci // c64), 1.0, 0.0).astype(BF16)

    q = y[:, 0:GDN_W]
    k = y[:, GDN_W:2 * GDN_W]
    q_s[...] = q * lax.rsqrt(_dot_sel_r(q * q, head_ones) + 1e-6) * (GDN_DK ** -0.5)
    k_s[...] = k * lax.rsqrt(_dot_sel_r(k * k, head_ones) + 1e-6)
    v_s[...] = y[:, 2 * GDN_W:3 * GDN_W]
    a_rep = blk_ref[0, :, GDN_QKV_W + GDN_W:GDN_QKV_W + 2 * GDN_W]
    b_rep = blk_ref[0, :, GDN_QKV_W + 2 * GDN_W:GDN_QKV_W + 3 * GDN_W]
    b_s[...] = _sigmoid(b_rep)
    g_raw = -jnp.exp(alog_ref[...]) * _softplus(a_rep + dtb_ref[...])
    rt = lax.broadcasted_iota(I32, (ct, ct), 0)
    ctk = lax.broadcasted_iota(I32, (ct, ct), 1)
    tri_bd = jnp.where(((rt // c64) == (ctk // c64)) & (ctk <= rt), 1.0, 0.0).astype(BF16)
    g_s[...] = _dot_sel_l(tri_bd, g_raw)

    r64 = lax.broadcasted_iota(I32, (c64, c64), 0)
    col64 = lax.broadcasted_iota(I32, (c64, c64), 1)
    incl = col64 <= r64
    strict = col64 < r64
    eye_b = col64 == r64
    eye_f = jnp.where(eye_b, 1.0, 0.0)
    ones64 = jnp.ones((c64, c64), BF16)
    ng = ng_ref[...]

    n_chunks = ct // c64
    heads = range(GDN_HEADS)
    chains = [(slice(c * c64, (c + 1) * c64), slice(h * c64, (h + 1) * c64))
              for c in range(n_chunks) for h in heads]
    each = lambda f, *ls: [f(*xs) for xs in zip(*ls)]
    gi = [g_s[r, l] for r, l in chains]
    gj = each(lambda g: _dot_sel_l(ones64, jnp.where(eye_b, g, 0.0)), gi)
    dec = each(lambda a, b: jnp.exp(jnp.where(incl, a - b, NEG_INF)), gi, gj)
    kh = [k_s[r, l] for r, l in chains]
    bi = [b_s[r, l] for r, l in chains]
    kb = each(lambda x: x.astype(BF16), kh)
    kk = each(_dot_nt, kb, kb)
    lm = each(lambda b, x, d: jnp.where(strict, b * x * d, 0.0), bi, kk, dec)
    tinv = each(lambda x: eye_f - x, lm)
    pw = lm
    for _ in range(5):
        pw = each(_mm3, pw, pw)
        tinv = each(lambda t, m: t + _mm3(t, m), tinv, pw)
    tb = each(lambda x: x.astype(BF16), tinv)
    eg = each(jnp.exp, gi)
    u = each(lambda t, rl, b: _dot(t, (v_s[rl[0], rl[1]] * b).astype(BF16)), tb, chains, bi)
    w = each(lambda t, k, b, e: _dot(t, (k * (b * e)).astype(BF16)).astype(BF16), tb, kh, bi, eg)
    qh = [q_s[r, l] for r, l in chains]
    qk = each(lambda q, k, d: jnp.where(incl, _dot_nt(q.astype(BF16), k) * d, 0.0).astype(BF16), qh, kb, dec)
    qg = each(lambda q, e: (q * e).astype(BF16), qh, eg)
    g_last = each(lambda g: g[c64 - 1:c64, :], gi)
    kg = each(lambda k, gl, g: (k * jnp.exp(gl - g)).astype(BF16), kh, g_last, gi)
    decay_last = each(jnp.exp, g_last)

    states = [s_ref[h] for h in heads]
    for c in range(n_chunks):
        ids = [c * GDN_HEADS + h for h in heads]
        sb = [s.astype(BF16) for s in states]
        vnb = [(u[n] - _dot(w[n], sb[h])).astype(BF16) for h, n in zip(heads, ids)]
        o = [_dot(qg[n], sb[h]) + _dot(qk[n], vnb[h]) for h, n in zip(heads, ids)]
        states = [states[h] * decay_last[n] + _dot_tn(kg[n], vnb[h]) for h, n in zip(heads, ids)]
        ms = [_dot_sel_r(x * x, ones64) * (1.0 / c64) for x in o]
        outs = [x * lax.rsqrt(m + NORM_EPS) * ng for x, m in zip(o, ms)]
        rows = slice(c * c64, (c + 1) * c64)
        z = blk_ref[0, rows, GDN_QKV_W:GDN_QKV_W + GDN_W]
        out_ref[0, rows, :] = (jnp.concatenate(outs, axis=1) * (z * _sigmoid(z))).astype(out_ref.dtype)
    for h in heads:
        s_ref[h] = states[h]


def _gdn(gdn_in, conv_w, a_log_rep, dtb_rep, ng, ct=256):
    bn, seq, _ = gdn_in.shape
    fixed = lambda b, l: (0, 0)
    return pl.pallas_call(
        functools.partial(_gdn_kernel, ct=ct),
        grid=(bn, seq // ct),
        in_specs=[pl.BlockSpec((1, ct, INPROJ_GDN_W), lambda b, l: (b, l, 0)),
                  pl.BlockSpec((GDN_CONV, GDN_QKV_W), fixed), pl.BlockSpec((1, GDN_W), fixed),
                  pl.BlockSpec((1, GDN_W), fixed), pl.BlockSpec((1, GDN_DK), fixed)],
        out_specs=pl.BlockSpec((1, ct, GDN_W), lambda b, l: (b, l, 0)),
        out_shape=jax.ShapeDtypeStruct((bn, seq, GDN_W), BF16),
        scratch_shapes=[pltpu.VMEM((GDN_HEADS, GDN_DK, GDN_DK), F32), pltpu.VMEM((8, GDN_QKV_W), F32),
                        pltpu.VMEM((ct + 8, GDN_QKV_W), F32)] + [pltpu.VMEM((ct, GDN_W), F32)] * 5,
        compiler_params=_cparams(("arbitrary", "arbitrary")),
        name="gdn",
    )(gdn_in, conv_w, a_log_rep, dtb_rep, ng)


def _s5_kernel(u_ref, lre_ref, lim_ref, lstep_ref, bre_ref, bim_ref, cre_ref, cim_ref, d_ref, y_ref,
               bmat, cmat, avec, carry, zr_s, zi_s, xr_s, xi_s, *, rb):
    ns = S5_NS
    cw = S5_WIDTH

    @pl.when(pl.program_id(1) == 0)
    def _():
        step = jnp.exp(lstep_ref[...])
        lr = lre_ref[...]
        li = lim_ref[...]
        mag = jnp.exp(lr * step)
        ar = mag * jnp.cos(li * step)
        ai = mag * jnp.sin(li * step)
        den = lr * lr + li * li
        mr = ((ar - 1.0) * lr + ai * li) / den
        mi = (ai * lr - (ar - 1.0) * li) / den
        bre = bre_ref[...]
        bim = bim_ref[...]
        bmat[:, 0:ns] = (mr * bre - mi * bim).astype(BF16)
        bmat[:, ns:2 * ns] = (mr * bim + mi * bre).astype(BF16)
        cmat[0:ns, :] = cre_ref[...].astype(BF16)
        cmat[ns:2 * ns, :] = (-cim_ref[...]).astype(BF16)
        avec[0:1, :] = ar
        avec[1:2, :] = ai
        pr, pi = ar, ai
        for _ in range(4):
            pr, pi = pr * pr - pi * pi, 2.0 * pr * pi
        avec[2:3, :] = pr
        avec[3:4, :] = pi
        carry[...] = jnp.zeros_like(carry)

    ar = avec[0:1, :]
    ai = avec[1:2, :]
    a16r = avec[2:3, :]
    a16i = avec[3:4, :]

    def inject(s):
        ub = u_ref[0, :, s * cw:(s + 1) * cw].astype(BF16)
        return _dot(ub, bmat[...])

    xr = jnp.zeros((rb, ns), F32)
    xi = jnp.zeros((rb, ns), F32)
    for s in range(S5_SUB):
        p = inject(s)
        xr, xi = ar * xr - ai * xi + p[:, 0:ns], ar * xi + ai * xr + p[:, ns:2 * ns]
    zr_s[...] = xr
    zi_s[...] = xi

    def row_step(kk, c):
        cr, ci_ = c
        xr_s[pl.ds(kk, 1), :] = cr
        xi_s[pl.ds(kk, 1), :] = ci_
        zr = zr_s[pl.ds(kk, 1), :]
        zi = zi_s[pl.ds(kk, 1), :]
        return (a16r * cr - a16i * ci_ + zr, a16r * ci_ + a16i * cr + zi)

    cr, ci_ = lax.fori_loop(0, rb, row_step, (carry[0:1, :], carry[1:2, :]))
    carry[0:1, :] = cr
    carry[1:2, :] = ci_

    xr = xr_s[...]
    xi = xi_s[...]
    dsk = d_ref[...]
    for s in range(S5_SUB):
        p = inject(s)
        xr, xi = ar * xr - ai * xi + p[:, 0:ns], ar * xi + ai * xr + p[:, ns:2 * ns]
        yv = _dot(xr.astype(BF16), cmat[0:ns, :]) + _dot(xi.astype(BF16), cmat[ns:2 * ns, :])
        yv = yv + dsk * u_ref[0, :, s * cw:(s + 1) * cw]
        y_ref[0, :, s * cw:(s + 1) * cw] = _gelu_tanh(yv).astype(y_ref.dtype)


def _s5(u_rows, lre, lim, lstep, bre_bd, bim_bd, cre_bd, cim_bd, dskip, rb=128):
    bn, nrows, rw = u_rows.shape
    fixed = lambda b, r: (0, 0)
    ns = S5_NS
    return pl.pallas_call(
        functools.partial(_s5_kernel, rb=rb),
        grid=(bn, nrows // rb),
        in_specs=[pl.BlockSpec((1, rb, rw), lambda b, r: (b, r, 0)),
                  pl.BlockSpec((1, ns), fixed), pl.BlockSpec((1, ns), fixed), pl.BlockSpec((1, ns), fixed),
                  pl.BlockSpec((S5_WIDTH, ns), fixed), pl.BlockSpec((S5_WIDTH, ns), fixed),
                  pl.BlockSpec((ns, S5_WIDTH), fixed), pl.BlockSpec((ns, S5_WIDTH), fixed),
                  pl.BlockSpec((1, S5_WIDTH), fixed)],
        out_specs=pl.BlockSpec((1, rb, rw), lambda b, r: (b, r, 0)),
        out_shape=jax.ShapeDtypeStruct((bn, nrows, rw), BF16),
        scratch_shapes=[pltpu.VMEM((S5_WIDTH, 2 * ns), BF16), pltpu.VMEM((2 * ns, S5_WIDTH), BF16),
                        pltpu.VMEM((8, ns), F32), pltpu.VMEM((8, ns), F32)]
                       + [pltpu.VMEM((rb, ns), F32)] * 4,
        compiler_params=_cparams(("arbitrary", "arbitrary")),
        name="s5",
    )(u_rows, lre, lim, lstep, bre_bd, bim_bd, cre_bd, cim_bd, dskip)


def _attn_kernel(q_ref, k_ref, v_ref, lq1_ref, lk1_ref, lq2_ref, lk2_ref, ng_ref, o_ref,
                 qs_s, m_s, l_s, acc_s, *, blk, lam_init):
    i = pl.program_id(2)
    q = q_ref[0]
    lane = lax.broadcasted_iota(I32, q.shape, 1)
    zero = jnp.zeros_like(q)
    qs_s[0:blk, :] = jnp.where(lane < DIFF_DQK, q, zero)
    qs_s[blk:2 * blk, :] = jnp.where(lane >= DIFF_DQK, q, zero)
    m_s[...] = jnp.full_like(m_s, NEG_INF)
    l_s[...] = jnp.zeros_like(l_s)
    acc_s[...] = jnp.zeros_like(acc_s)
    ones = jnp.ones((blk, DIFF_DV), BF16)

    def block_rows(j):
        return pl.ds(pl.multiple_of(j * blk, blk), blk)

    def scores(j):
        return _dot_nt(qs_s[...], k_ref[0, block_rows(j), :])

    def update(j, s):
        m_old = m_s[...]
        m_new = jnp.maximum(m_old, jnp.max(s, axis=-1, keepdims=True))
        p = jnp.exp(s - jnp.concatenate([m_new] * (blk // DIFF_DV), axis=1)).astype(BF16)
        alpha = jnp.exp(m_old - m_new)
        pv = _dot(p, jnp.concatenate([v_ref[0, block_rows(j), :], ones], axis=1))
        acc_s[...] = alpha * acc_s[...] + pv[:, 0:DIFF_DV]
        l_s[...] = alpha * l_s[...] + pv[:, DIFF_DV:2 * DIFF_DV]
        m_s[...] = m_new

    def body(j, s):
        s_next = scores(j + 1)
        update(j, s)
        return s_next

    s = lax.fori_loop(0, i, body, scores(0))
    row = lax.broadcasted_iota(I32, s.shape, 0) & (blk - 1)
    col = lax.broadcasted_iota(I32, s.shape, 1)
    update(i, jnp.where(col <= row, s, NEG_INF))
    lam = (jnp.exp(jnp.sum(lq1_ref[...] * lk1_ref[...], axis=-1, keepdims=True))
           - jnp.exp(jnp.sum(lq2_ref[...] * lk2_ref[...], axis=-1, keepdims=True)) + lam_init)
    o = acc_s[0:blk, :] / l_s[0:blk, :] - lam * (acc_s[blk:2 * blk, :] / l_s[blk:2 * blk, :])
    ms = jnp.mean(o * o, axis=-1, keepdims=True)
    o = o * lax.rsqrt(ms + NORM_EPS) * ng_ref[...] * (1.0 - lam_init)
    o_ref[0] = o.astype(o_ref.dtype)


def _attn(q, k, v, lq1, lk1, lq2, lk2, ng, lam_init, blk=512):
    bn, seq, _ = q.shape
    blk = min(blk, seq)
    fixed = lambda b, h, i: (0, 0)
    kv_spec = pl.BlockSpec((1, seq, DIFF_DV), lambda b, h, i: (b, 0, h))
    return pl.pallas_call(
        functools.partial(_attn_kernel, blk=blk, lam_init=lam_init),
        grid=(bn, DIFF_HEADS, seq // blk),
        in_specs=[pl.BlockSpec((1, blk, DIFF_DV), lambda b, h, i: (b, i, h)), kv_spec, kv_spec]
                 + [pl.BlockSpec((1, DIFF_DQK), fixed)] * 4 + [pl.BlockSpec((1, DIFF_DV), fixed)],
        out_specs=pl.BlockSpec((1, blk, DIFF_DV), lambda b, h, i: (b, i, h)),
        out_shape=jax.ShapeDtypeStruct((bn, seq, DIFF_W), BF16),
        scratch_shapes=[pltpu.VMEM((2 * blk, DIFF_DV), BF16), pltpu.VMEM((2 * blk, DIFF_DV), F32),
                        pltpu.VMEM((2 * blk, DIFF_DV), F32), pltpu.VMEM((2 * blk, DIFF_DV), F32)],
        compiler_params=_cparams(("parallel", "parallel", "arbitrary")),
        name="diff_attn",
    )(q, k, v, lq1, lk1, lq2, lk2, ng)


def _outproj_kernel(h_ref, oa_ref, ys_ref, oc_ref, gw_ref, gb_ref, wo_ref, out_ref):
    zg = _dot(ys_ref[...], gw_ref[...]) + gb_ref[...]
    ob = (zg[:, 0:S5_WIDTH] * _sigmoid(zg[:, S5_WIDTH:2 * S5_WIDTH])).astype(BF16)
    acc = _dot(oa_ref[...], wo_ref[0:GDN_W, :])
    acc = acc + _dot(ob, wo_ref[GDN_W:GDN_W + S5_WIDTH, :])
    acc = acc + _dot(oc_ref[...], wo_ref[GDN_W + S5_WIDTH:, :])
    out_ref[...] = h_ref[...] + acc


def _outproj(h2, oa, ys, oc, glu_w, glu_b, w_out, tm=512):
    t = h2.shape[0]
    row = lambda i: (i, 0)
    fixed = lambda i: (0, 0)
    return pl.pallas_call(
        _outproj_kernel,
        grid=(t // tm,),
        in_specs=[pl.BlockSpec((tm, D_MODEL), row), pl.BlockSpec((tm, GDN_W), row),
                  pl.BlockSpec((tm, S5_WIDTH), row), pl.BlockSpec((tm, DIFF_W), row),
                  pl.BlockSpec((S5_WIDTH, 2 * S5_WIDTH), fixed), pl.BlockSpec((1, 2 * S5_WIDTH), fixed),
                  pl.BlockSpec((D_MODEL, D_MODEL), fixed)],
        out_specs=pl.BlockSpec((tm, D_MODEL), row),
        out_shape=jax.ShapeDtypeStruct((t, D_MODEL), F32),
        compiler_params=_cparams(("parallel",)),
        name="outproj",
    )(h2, oa, ys, oc, glu_w, glu_b, w_out)


_BIG_ID = 1.0e9


def _top16(x, ids, payload):
    n = x.shape[1]
    r16 = lax.broadcasted_iota(I32, (PEER_TOPK, n), 0)
    vals = jnp.zeros((PEER_TOPK, n), F32)
    pays = jnp.zeros((PEER_TOPK, n), F32)
    for kk in range(PEER_TOPK):
        m = jnp.max(x, axis=0, keepdims=True)
        first = jnp.min(jnp.where(x == m, ids, _BIG_ID), axis=0, keepdims=True)
        hit = ids == first
        pay = first if payload is None else jnp.max(jnp.where(hit, payload, -1.0), axis=0, keepdims=True)
        x = jnp.where(hit, NEG_INF, x)
        vals = jnp.where(r16 == kk, m, vals)
        pays = jnp.where(r16 == kk, pay, pays)
    return vals, pays


def _peer_route_kernel(h_ref, g_ref, wqt_ref, keys_ref, xn_ref, exp_ref, row_ref, par_ref, gate_ref,
                       qt_s, sv_s, si_s, gate_s, exp_s, *, tm):
    x = h_ref[...]
    ms = jnp.mean(x * x, axis=-1, keepdims=True)
    xn = x * lax.rsqrt(ms + NORM_EPS) * g_ref[...]
    xn_ref[...] = xn
    qt_s[...] = _dot_nt(wqt_ref[...], xn.astype(BF16)).astype(BF16)

    key_id = lax.broadcasted_iota(I32, (N_KEYS, tm), 0).astype(F32)

    def half_body(hp, carry):
        r0 = pl.multiple_of(hp * PEER_DHALF, PEER_DHALF)
        s = _dot(keys_ref[hp], qt_s[pl.ds(r0, PEER_DHALF), :])
        vals, ids = _top16(s, key_id, None)
        sv_s[hp] = vals
        si_s[hp] = ids
        return carry

    lax.fori_loop(0, 2 * PEER_HEADS, half_body, 0)

    sub8 = lax.broadcasted_iota(I32, (8, tm), 0)
    i8 = sub8.astype(F32)
    low4 = sub8 < 4
    j4 = (sub8 & 3).astype(F32)

    def head_body(hd, carry):
        a0 = sv_s[2 * hd]
        a1 = sv_s[2 * hd + 1]
        e0 = si_s[2 * hd] * float(N_KEYS)
        e1 = si_s[2 * hd + 1]
        a1_44 = jnp.where(low4, a1[0:8, :], pltpu.roll(a1[0:8, :], 4, 0))
        e1_44 = jnp.where(low4, e1[0:8, :], pltpu.roll(e1[0:8, :], 4, 0))
        cs = [a0[0:1, :] + a1[0:8, :], a0[0:1, :] + a1[8:16, :]]
        es = [e0[0:1, :] + e1[0:8, :], e0[0:1, :] + e1[8:16, :]]
        fs = [i8, i8 + 8.0]
        for i in range(1, 4):
            cs.append(a0[i:i + 1, :] + a1[0:8, :])
            es.append(e0[i:i + 1, :] + e1[0:8, :])
            fs.append(i8 + float(i * PEER_TOPK))
        for i in (4, 6):
            cs.append(jnp.where(low4, a0[i:i + 1, :], a0[i + 1:i + 2, :]) + a1_44)
            es.append(jnp.where(low4, e0[i:i + 1, :], e0[i + 1:i + 2, :]) + e1_44)
            fs.append(jnp.where(low4, float(i * PEER_TOPK), float((i + 1) * PEER_TOPK)) + j4)
        cs.append(a0[8:16, :] + a1[0:1, :])
        es.append(e0[8:16, :] + e1[0:1, :])
        fs.append((i8 + 8.0) * float(PEER_TOPK))
        top_s, experts = _top16(jnp.concatenate(cs, axis=0), jnp.concatenate(fs, axis=0),
                                jnp.concatenate(es, axis=0))
        ex = jnp.exp(top_s - jnp.max(top_s, axis=0, keepdims=True))
        rows = pl.ds(pl.multiple_of(hd * PEER_TOPK, PEER_TOPK), PEER_TOPK)
        gate_s[rows, :] = ex / jnp.sum(ex, axis=0, keepdims=True)
        exp_s[rows, :] = experts
        return carry

    lax.fori_loop(0, PEER_HEADS, head_body, 0)
    gate_ref[...] = gate_s[...].T
    expert = exp_s[...].T.astype(I32)
    exp_ref[...] = expert
    row_ref[...] = (expert & (TAB_ROWS - 1)) * 8
    par_ref[...] = lax.shift_right_logical(expert, TAB_ROWS.bit_length() - 1).astype(F32)


def _peer_route(h2, gain, wqt, keys, tm=1024):
    t = h2.shape[0]
    tm = min(tm, t)
    row = lambda i: (i, 0)
    return pl.pallas_call(
        functools.partial(_peer_route_kernel, tm=tm),
        grid=(t // tm,),
        in_specs=[pl.BlockSpec((tm, D_MODEL), row), pl.BlockSpec((1, D_MODEL), lambda i: (0, 0)),
                  pl.BlockSpec((D_MODEL, D_MODEL), lambda i: (0, 0)),
                  pl.BlockSpec((2 * PEER_HEADS, N_KEYS, PEER_DHALF), lambda i: (0, 0, 0))],
        out_specs=[pl.BlockSpec((tm, D_MODEL), row)] + [pl.BlockSpec((tm, PEER_PAIRS), row)] * 4,
        out_shape=[jax.ShapeDtypeStruct((t, D_MODEL), F32), jax.ShapeDtypeStruct((t, PEER_PAIRS), I32),
                   jax.ShapeDtypeStruct((t, PEER_PAIRS), I32), jax.ShapeDtypeStruct((t, PEER_PAIRS), F32),
                   jax.ShapeDtypeStruct((t, PEER_PAIRS), F32)],
        scratch_shapes=[pltpu.VMEM((D_MODEL, tm), BF16), pltpu.VMEM((2 * PEER_HEADS, PEER_TOPK, tm), F32),
                        pltpu.VMEM((2 * PEER_HEADS, PEER_TOPK, tm), F32),
                        pltpu.VMEM((PEER_PAIRS, tm), F32), pltpu.VMEM((PEER_PAIRS, tm), F32)],
        compiler_params=_cparams(("parallel",)),
        name="peer_route",
    )(h2, gain, wqt, keys)


TAB_ROWS = N_EXPERTS // 2


def _token_tile(block, tl):
    return jnp.concatenate([block[tl:tl + 1, s * 128:(s + 1) * 128] for s in range(8)], axis=0)


def _tiles_to_rows(tiles):
    return jnp.concatenate([jnp.concatenate([tile[s:s + 1, :] for tile in tiles], axis=0) for s in range(8)], axis=1)


def _load_words(tab_ref, row_ref, t, p):
    return tab_ref[pl.ds(pl.multiple_of(row_ref[t, p], 8), 8), :]


def _tok_spec(tb, space=None):
    return pl.BlockSpec((tb, PEER_PAIRS), lambda i: (i, 0), memory_space=space)


def _table_spec():
    return pl.BlockSpec((TAB_ROWS * 8, 128), lambda i: (0, 0), pipeline_mode=pl.Buffered(1))


SC_CORES = 2
SC_SUBCORES = 16
SC_LANES = 16
SC_WORKERS = SC_CORES * SC_SUBCORES


def _peer_u_sc(tab, expert, xn, t0, t_sc):
    tw = t_sc // SC_WORKERS
    heads = PEER_PAIRS // PEER_TOPK
    chunks = D_MODEL // SC_LANES
    mesh = plsc.VectorSubcoreMesh(core_axis_name="c", subcore_axis_name="s")

    def body(tab_hbm, idx_hbm, x_hbm, d_hbm, idx_a, idx_b, x_a, x_b, rows_a, rows_b, d_v,
             sem_ra, sem_rb, sem_ia, sem_ib, sem_xa, sem_xb):
        wid = lax.axis_index("s") * SC_CORES + lax.axis_index("c")
        lane = lax.iota(I32, SC_LANES)
        first = t0 + wid * tw
        last = first + tw - 1
        tok_bufs = ((idx_a, x_a, sem_ia, sem_xa), (idx_b, x_b, sem_ib, sem_xb))
        row_bufs = ((rows_a, sem_ra), (rows_b, sem_rb))

        def fetch(t, k):
            idx_v, x_v, sem_i, sem_x = tok_bufs[k]
            return (pltpu.make_async_copy(idx_hbm.at[t], idx_v, sem_i),
                    pltpu.make_async_copy(x_hbm.at[t], x_v, sem_x))

        def gather(k, h):
            buf, sem = row_bufs[h % 2]
            return pltpu.make_async_copy(tab_hbm.at[tok_bufs[k][0].at[pl.ds(h * PEER_TOPK, PEER_TOPK)]], buf, sem)

        for c in fetch(first, 0):
            c.start()
        for c in fetch(first, 0):
            c.wait()
        gather(0, 0).start()

        def two_tokens(i2, carry):
            for k in range(2):
                t = first + 2 * i2 + k
                x_v = tok_bufs[k][1]
                nxt = fetch(jnp.minimum(t + 1, last), 1 - k)
                for c in nxt:
                    c.start()
                for h in range(heads):
                    if h + 1 < heads:
                        gather(k, h + 1).start()
                    else:
                        for c in nxt:
                            c.wait()
                        gather(1 - k, 0).start()
                    gather(k, h).wait()
                    rows_v = row_bufs[h % 2][0]

                    def chunk(j, accs):
                        xj = x_v[pl.ds(j * SC_LANES, SC_LANES)]
                        return tuple(a + rows_v[r, pl.ds(j * SC_LANES, SC_LANES)] * xj for r, a in enumerate(accs))

                    accs = lax.fori_loop(0, chunks, chunk,
                                         tuple(jnp.zeros((SC_LANES,), F32) for _ in range(PEER_TOPK)))
                    out = jnp.zeros((SC_LANES,), F32)
                    for r in range(PEER_TOPK):
                        out = jnp.where(lane == r, jnp.sum(accs[r]), out)
                    d_v[pl.ds(h * PEER_TOPK, PEER_TOPK)] = out
                pltpu.sync_copy(d_v, d_hbm.at[t - t0])
            return carry

        lax.fori_loop(0, tw // 2, two_tokens, 0)
        gather(0, 0).wait()

    return pl.kernel(
        body, mesh=mesh,
        out_type=jax.ShapeDtypeStruct((t_sc, PEER_PAIRS), F32),
        compiler_params=pltpu.CompilerParams(needs_layout_passes=False),
        scratch_types=[pltpu.VMEM((PEER_PAIRS,), I32), pltpu.VMEM((PEER_PAIRS,), I32),
                       pltpu.VMEM((D_MODEL,), F32), pltpu.VMEM((D_MODEL,), F32),
                       pltpu.VMEM((PEER_TOPK, D_MODEL), F32), pltpu.VMEM((PEER_TOPK, D_MODEL), F32),
                       pltpu.VMEM((PEER_PAIRS,), F32)] + [pltpu.SemaphoreType.DMA] * 6,
        name="peer_u_sc",
    )(tab, expert, xn)


def _peer_coef_kernel(d_ref, gate_ref, c_ref):
    c_ref[...] = gate_ref[...] * _gelu_tanh(d_ref[...])


def _peer_coef(d_tp, gate_tp, tm=2048):
    t = d_tp.shape[0]
    tm = min(tm, t)
    return pl.pallas_call(
        _peer_coef_kernel,
        grid=(t // tm,),
        in_specs=[_tok_spec(tm), _tok_spec(tm)],
        out_specs=_tok_spec(tm),
        out_shape=jax.ShapeDtypeStruct((t, PEER_PAIRS), F32),
        compiler_params=_cparams(("parallel",)),
        name="peer_coef",
    )(d_tp, gate_tp)


PEER_V_TOKENS_PER_STEP = 8
PEER_V_KDIM = (PEER_PAIRS // 2) * 16


def _peer_v_kernel(row_ref, par_ref, coef_ref, h_ref, tab_ref, out_ref, cz_s, *, tb):
    half = PEER_PAIRS // 2
    kdim = PEER_V_KDIM
    nt = PEER_V_TOKENS_PER_STEP
    pk = lax.broadcasted_iota(I32, (PEER_PAIRS, kdim), 0)
    qk = lax.shift_right_logical(lax.broadcasted_iota(I32, (PEER_PAIRS, kdim), 1), 4)
    hk = (lax.broadcasted_iota(I32, (tb, kdim), 1) & 1).astype(F32)
    coef = coef_ref[...].astype(BF16)
    par = par_ref[...].astype(BF16)
    for hf in range(2):
        expand = jnp.where(pk == qk + hf * half, 1.0, 0.0).astype(BF16)
        cz_s[hf] = jnp.where(hk == _dot(par, expand), _dot(coef, expand), 0.0)
    ks = lax.broadcasted_iota(I32, (8, kdim), 0)
    k8 = lax.broadcasted_iota(I32, (8, kdim), 1)
    own_sublane = lax.shift_right_logical(k8 & 15, 1) == ks

    def tok_body(tt, carry):
        rows8 = pl.ds(pl.multiple_of(tt * nt, nt), nt)
        hb = h_ref[rows8, :]
        tiles = []
        for j in range(nt):
            t = tt * nt + j
            pieces = []
            for q in range(half):
                wa = pltpu.bitcast(_load_words(tab_ref, row_ref, t, q), BF16)
                wb = pltpu.bitcast(_load_words(tab_ref, row_ref, t, half + q), BF16)
                pieces.append(jnp.concatenate([wa, wb], axis=1))
            g = jnp.concatenate(pieces, axis=0)
            ck = jnp.concatenate(
                [jnp.where(own_sublane, jnp.broadcast_to(cz_s[hf, pl.ds(t, 1), :], (8, kdim)), 0.0)
                 for hf in range(2)], axis=0).astype(BF16)
            o = _dot(ck, g)
            tiles.append(_token_tile(hb, j) + (o[0:8, 0:128] + o[8:16, 128:256]))
        out_ref[rows8, :] = _tiles_to_rows(tiles)
        return carry

    lax.fori_loop(0, tb // nt, tok_body, 0)


def _peer_v(rows, par, coef, h2, tab, t, tb=128):
    return pl.pallas_call(
        functools.partial(_peer_v_kernel, tb=tb),
        grid=(t // tb,),
        in_specs=[_tok_spec(tb, pltpu.SMEM), _tok_spec(tb), _tok_spec(tb),
                  pl.BlockSpec((tb, D_MODEL), lambda i: (i, 0)), _table_spec()],
        out_specs=pl.BlockSpec((tb, D_MODEL), lambda i: (i, 0)),
        out_shape=jax.ShapeDtypeStruct((t, D_MODEL), F32),
        scratch_shapes=[pltpu.VMEM((2, tb, PEER_V_KDIM), F32)],
        compiler_params=_cparams(("arbitrary",)),
        name="peer_v",
    )(rows, par, coef, h2, tab)


def _final_norm_kernel(x_ref, g_ref, o_ref):
    x = x_ref[...]
    ms = jnp.mean(x * x, axis=-1, keepdims=True)
    o_ref[...] = x * lax.rsqrt(ms + NORM_EPS) * g_ref[...]


def _final_norm(x2, gain, tm=1024):
    t = x2.shape[0]
    tm = min(tm, t)
    row = lambda i: (i, 0)
    return pl.pallas_call(
        _final_norm_kernel,
        grid=(t // tm,),
        in_specs=[pl.BlockSpec((tm, D_MODEL), row), pl.BlockSpec((1, D_MODEL), lambda i: (0, 0))],
        out_specs=pl.BlockSpec((tm, D_MODEL), row),
        out_shape=jax.ShapeDtypeStruct((t, D_MODEL), F32),
        compiler_params=_cparams(("parallel",)),
        name="final_norm",
    )(x2, gain)


def _pack_table(tab):
    bits = lax.bitcast_convert_type(tab.astype(BF16), jnp.uint16).astype(U32)
    return (bits[:TAB_ROWS] | (bits[TAB_ROWS:] << 16)).reshape(TAB_ROWS * 8, 128)


def _peer_start(h2, w, tab_u_all):
    t = h2.shape[0]
    xn, expert, rows, par, gate = _peer_route(h2, w["norm2"], w["wqt"], w["keys"])
    d = _peer_u_sc(tab_u_all, expert + w["expert_base"], xn, 0, t)
    return dict(h2=h2, rows=rows, par=par, gate=gate, d=d)


def _peer_finish(st, w):
    t = st["h2"].shape[0]
    coef = _peer_coef(st["d"], st["gate"])
    return _peer_v(st["rows"], st["par"], coef, st["h2"], w["tab_v"], t)


def _rep(x, n):
    return jnp.repeat(x, n, axis=-1)


def _inproj_weight(w_in_l):
    o = 0
    qkv = w_in_l[:, o:o + GDN_QKV_W]; o += GDN_QKV_W
    z = w_in_l[:, o:o + GDN_W]; o += GDN_W
    a = w_in_l[:, o:o + GDN_HEADS]; o += GDN_HEADS
    b = w_in_l[:, o:o + GDN_HEADS]; o += GDN_HEADS
    rest = w_in_l[:, o:]
    return jnp.concatenate([qkv, z, _rep(a, GDN_DK), _rep(b, GDN_DK), rest], axis=1).astype(BF16)


def _block_diag_gc(b_gnc):
    g = b_gnc.shape[0]
    eye = jnp.eye(g, dtype=b_gnc.dtype)
    t = jnp.swapaxes(b_gnc, 1, 2)
    return (t[:, :, None, :] * eye[:, None, :, None]).reshape(g * t.shape[1], g * t.shape[2])


def _layer_weights(p, layer):
    row = lambda x: x[None, :]
    flat = lambda x: x.reshape(1, S5_NS)
    return dict(
        lam_init=0.8 - 0.6 * math.exp(-0.3 * layer),
        norm1=row(p["norm1_g"][layer]), w_in=_inproj_weight(p["w_in"][layer]),
        conv_w=p["gdn_conv_w"][layer], a_log=row(_rep(p["gdn_a_log"][layer].astype(F32), GDN_DK)),
        dtb=row(_rep(p["gdn_dt_bias"][layer].astype(F32), GDN_DK)), gdn_ng=row(p["gdn_norm_g"][layer]),
        s5=(flat(p["s5_lambda_re"][layer]), flat(p["s5_lambda_im"][layer]),
            flat(_rep(p["s5_log_step"][layer][:, None], S5_STATE)),
            _block_diag_gc(p["s5_b_re"][layer]), _block_diag_gc(p["s5_b_im"][layer]),
            _block_diag_gc(p["s5_c_re"][layer]), _block_diag_gc(p["s5_c_im"][layer]),
            row(p["s5_d"][layer])),
        attn=tuple(row(p[k][layer]) for k in ("diff_lq1", "diff_lk1", "diff_lq2", "diff_lk2", "diff_norm_g")),
        glu_w=p["s5_glu_w"][layer].astype(BF16), glu_b=row(p["s5_glu_b"][layer]),
        w_out=p["w_out"][layer].astype(BF16),
        norm2=row(p["norm2_g"][layer]), wqt=p["peer_wq"][layer].T.astype(BF16),
        keys=p["peer_keys"][layer].reshape(2 * PEER_HEADS, N_KEYS, PEER_DHALF).astype(BF16),
        tab_v=_pack_table(p["peer_v"][layer]), expert_base=layer * N_EXPERTS)


def _mixers(h, w):
    bn, seq, _ = h.shape
    t = bn * seq
    gdn_in, us, qc, kc, vc = _inproj(h.reshape(t, D_MODEL), w["norm1"], w["w_in"])
    o_a = _gdn(gdn_in.reshape(bn, seq, INPROJ_GDN_W), w["conv_w"], w["a_log"], w["dtb"], w["gdn_ng"])
    ys = _s5(us.reshape(bn, seq // S5_SUB, S5_SUB * S5_WIDTH), *w["s5"])
    o_c = _attn(qc.reshape(bn, seq, DIFF_W), kc.reshape(bn, seq, DIFF_W), vc.reshape(bn, seq, DIFF_W),
                *w["attn"], w["lam_init"])
    return _outproj(h.reshape(t, D_MODEL), o_a.reshape(t, GDN_W), ys.reshape(t, S5_WIDTH),
                    o_c.reshape(t, DIFF_W), w["glu_w"], w["glu_b"], w["w_out"])


def kernel(x, norm1_g, w_in, gdn_conv_w, gdn_a_log, gdn_dt_bias, gdn_norm_g, s5_lambda_re, s5_lambda_im, s5_b_re, s5_b_im, s5_c_re, s5_c_im, s5_d, s5_log_step, s5_glu_w, s5_glu_b, diff_lq1, diff_lk1, diff_lq2, diff_lk2, diff_norm_g, w_out, norm2_g, peer_wq, peer_keys, peer_u, peer_v, final_g):
    p = dict(norm1_g=norm1_g, w_in=w_in, gdn_conv_w=gdn_conv_w, gdn_a_log=gdn_a_log, gdn_dt_bias=gdn_dt_bias,
             gdn_norm_g=gdn_norm_g, s5_lambda_re=s5_lambda_re, s5_lambda_im=s5_lambda_im, s5_b_re=s5_b_re,
             s5_b_im=s5_b_im, s5_c_re=s5_c_re, s5_c_im=s5_c_im, s5_d=s5_d, s5_log_step=s5_log_step,
             s5_glu_w=s5_glu_w, s5_glu_b=s5_glu_b, diff_lq1=diff_lq1, diff_lk1=diff_lk1, diff_lq2=diff_lq2,
             diff_lk2=diff_lk2, diff_norm_g=diff_norm_g, w_out=w_out, norm2_g=norm2_g, peer_wq=peer_wq,
             peer_keys=peer_keys, peer_u=peer_u, peer_v=peer_v, final_g=final_g)
    bn, seq, _ = x.shape
    weights = [_layer_weights(p, layer) for layer in range(DEPTH)]
    tab_u_all = peer_u.reshape(DEPTH * N_EXPERTS, D_MODEL)
    hs = [x[b:b + 1] for b in range(bn)]
    pending = None
    for layer in range(DEPTH):
        for b in range(bn):
            started = _peer_start(_mixers(hs[b], weights[layer]), weights[layer], tab_u_all)
            if pending is not None:
                pb, pst, pw = pending
                hs[pb] = _peer_finish(pst, pw).reshape(1, seq, D_MODEL)
            pending = (b, started, weights[layer])
    pb, pst, pw = pending
    hs[pb] = _peer_finish(pst, pw).reshape(1, seq, D_MODEL)
    h = jnp.concatenate(hs, axis=0)
    return _final_norm(h.reshape(-1, D_MODEL), final_g[None, :]).reshape(x.shape)
```
